```python
import math
import jax
import jax.numpy as jnp
from jax import lax
import numpy as np

D_MODEL = 1024
BATCH = 16
SEQ = 256
DEPTH = 2
DEC_BATCH = 8
DEC_SEQ = 1024
PAST_LEN = 512

GRID_W = 64
N_EVEN = (DEPTH + 1) // 2
N_ODD = DEPTH // 2
N_MOD = 6
EPS = 1e-6
H_A = 4
DH_A = 64
W_A = H_A * 2 * DH_A
Q_BLOCK = 128
ROPE_BASE = 10000.0
AXIS_FREQS = DH_A // 4
H_B = 4
DK_B = 128
DV_B = 128
W_B = H_B * DV_B
CONV_K = 5
DELTA_CHUNK = 64
IN_EVEN = 3 * W_A + 4 * W_B + 4 * H_B
W_C = D_MODEL
C_GROUPS = 4
C_CHUNK = 128
N_GROUPS = 4
E_PER_GROUP = 8
N_EXPERTS = N_GROUPS * E_PER_GROUP
TOP_K = 2
D_EXPERT = 256

kernel_name = "hybrid_diff_deltanet_gmlp_hmoe_prefix_step"


def rms_norm(x, g):
    xf = x.astype(jnp.float32)
    y = xf * lax.rsqrt(jnp.mean(xf * xf, axis=-1, keepdims=True) + EPS)
    return y.astype(x.dtype) * g


def layer_norm(x, g, b):
    xf = x.astype(jnp.float32)
    mu = jnp.mean(xf, axis=-1, keepdims=True)
    var = jnp.mean(jnp.square(xf - mu), axis=-1, keepdims=True)
    return ((xf - mu) * lax.rsqrt(var + EPS)).astype(x.dtype) * g + b


def l2_normalize(x):
    xf = x.astype(jnp.float32)
    return (xf * lax.rsqrt(jnp.sum(xf * xf, axis=-1, keepdims=True) + EPS)).astype(x.dtype)


def rope_2d_tables(n_tok):
    rows = n_tok // GRID_W
    r, col = jnp.meshgrid(jnp.arange(rows), jnp.arange(GRID_W), indexing="ij")
    pos = jnp.stack([r.reshape(-1), col.reshape(-1)], axis=-1).astype(jnp.float32)
    inv = ROPE_BASE ** (-jnp.arange(AXIS_FREQS, dtype=jnp.float32) / AXIS_FREQS)
    ang = pos[:, :, None] * inv
    return jnp.cos(ang), jnp.sin(ang)


def apply_rope_2d(x, cos, sin):
    xs = x.reshape(x.shape[:-1] + (2, 2, AXIS_FREQS))
    x1, x2 = xs[..., 0, :], xs[..., 1, :]
    c = cos[None, :, None, None].astype(x.dtype)
    s = sin[None, :, None, None].astype(x.dtype)
    out = jnp.stack([x1 * c - x2 * s, x1 * s + x2 * c], axis=-2)
    return out.reshape(x.shape)


def diff_attention(q, k, v, lam):
    b, tq, h, e = q.shape
    nblk = tq // Q_BLOCK
    qb = q.reshape(b, nblk, Q_BLOCK, h, e).transpose(1, 0, 2, 3, 4)
    k1, k2 = k[..., :DH_A], k[..., DH_A:]
    scale = DH_A ** -0.5

    def block(qi):
        s1 = jnp.einsum("bqhd,bkhd->bhqk", qi[..., :DH_A], k1).astype(jnp.float32) * scale
        s2 = jnp.einsum("bqhd,bkhd->bhqk", qi[..., DH_A:], k2).astype(jnp.float32) * scale
        p = jax.nn.softmax(s1, axis=-1) - lam * jax.nn.softmax(s2, axis=-1)
        return jnp.einsum("bhqk,bkhe->bqhe", p.astype(v.dtype), v)

    o = lax.map(block, qb)
    return o.transpose(1, 0, 2, 3, 4).reshape(b, tq, h, e)


def short_conv(x, w):
    pad = (CONV_K - 1) // 2
    return lax.conv_general_dilated(x, w[:, None, :].astype(x.dtype), window_strides=(1,),
                                    padding=[(pad, pad)], dimension_numbers=("NWC", "WIO", "NWC"),
                                    feature_group_count=x.shape[-1])


def gated_delta_chunked(q, k, v, beta, g, s0):
    out_dtype = v.dtype
    q, k, v, beta, g, s0 = (a.astype(jnp.float32) for a in (q, k, v, beta, g, s0))
    b, t, h, _ = q.shape
    n = t // DELTA_CHUNK

    def chunks(a):
        a = a.reshape((b, n, DELTA_CHUNK) + a.shape[2:])
        return jnp.moveaxis(a, (1, 3), (0, 2))

    qc, kc, vc, bc = chunks(q), chunks(k), chunks(v), chunks(beta)
    gc = jnp.cumsum(chunks(g), axis=-1)
    idx = jnp.arange(DELTA_CHUNK)
    incl = idx[:, None] >= idx[None, :]
    strict = idx[:, None] > idx[None, :]
    diff = gc[..., :, None] - gc[..., None, :]
    decay = jnp.where(incl, jnp.exp(jnp.where(incl, diff, 0.0)), 0.0)
    kb = kc * bc[..., None]
    lmat = jnp.where(strict, jnp.einsum("nbhid,nbhjd->nbhij", kb, kc) * decay, 0.0)
    eye = jnp.eye(DELTA_CHUNK, dtype=jnp.float32)
    amat = eye + lmat
    tinv = lax.linalg.triangular_solve(amat, jnp.broadcast_to(eye, amat.shape), left_side=True,
                                       lower=True, unit_diagonal=True)
    u = tinv @ (vc * bc[..., None])
    w = tinv @ (kb * jnp.exp(gc)[..., None])
    att = jnp.where(incl, jnp.einsum("nbhid,nbhjd->nbhij", qc, kc) * decay, 0.0)

    def step(s, inp):
        q_i, k_i, u_i, w_i, g_i, a_i = inp
        v_new = u_i - w_i @ s
        o_i = (q_i * jnp.exp(g_i)[..., None]) @ s + a_i @ v_new
        g_last = g_i[..., -1:]
        s = s * jnp.exp(g_last)[..., None] + jnp.einsum(
            "bhcd,bhce->bhde", k_i * jnp.exp(g_last - g_i)[..., None], v_new)
        return s, o_i

    s_fin, o = lax.scan(step, s0, (qc, kc, u, w, gc, att))
    o = jnp.transpose(o, (1, 0, 3, 2, 4)).reshape(b, t, h, -1)
    return o.astype(out_dtype), s_fin


def even_mixer(h, w_in, conv_w, a_log, dt_bias, delta_norm, lam_q, lam_k, subln, w_out, lam_init, ctx):
    b, t, _ = h.shape
    proj = h @ w_in
    qa, ka, va, qkv_b, gate_b, ab = jnp.split(
        proj, [W_A, 2 * W_A, 3 * W_A, 3 * W_A + 3 * W_B, 3 * W_A + 4 * W_B], axis=-1)
    qa = qa.reshape(b, t, H_A, 2 * DH_A)
    ka = ka.reshape(b, t, H_A, 2 * DH_A)
    va = va.reshape(b, t, H_A, 2 * DH_A)
    lam = (jnp.exp(jnp.sum(lam_q[0] * lam_k[0])) - jnp.exp(jnp.sum(lam_q[1] * lam_k[1]))
           + lam_init).astype(jnp.float32)
    if ctx is None:
        keys, vals = ka, va
        s_f0 = jnp.zeros((b, H_B, DK_B, DV_B), jnp.float32)
        s_b0 = jnp.zeros((b, H_B, DK_B, DV_B), jnp.float32)
    else:
        k_ctx, v_ctx, s_f0, s_b0 = ctx
        cos, sin = rope_2d_tables(t)
        qa = apply_rope_2d(qa.reshape(b, t, H_A, 2, DH_A), cos, sin).reshape(b, t, H_A, 2 * DH_A)
        ka = apply_rope_2d(ka.reshape(b, t, H_A, 2, DH_A), cos, sin).reshape(b, t, H_A, 2 * DH_A)
        keys = jnp.concatenate([ka, k_ctx.astype(ka.dtype)], axis=1)
        vals = jnp.concatenate([va, v_ctx.astype(va.dtype)], axis=1)
    oa = rms_norm(diff_attention(qa, keys, vals, lam), subln) * (1.0 - lam_init)

    qkv_b = jax.nn.silu(short_conv(qkv_b, conv_w))
    qb, kb, vb = jnp.split(qkv_b, 3, axis=-1)
    qb = l2_normalize(qb.reshape(b, t, H_B, DK_B)) * (DK_B ** -0.5)
    kb = l2_normalize(kb.reshape(b, t, H_B, DK_B))
    vb = vb.reshape(b, t, H_B, DV_B)
    ab = ab.reshape(b, t, 2, 2, H_B).astype(jnp.float32)
    log_decay = -jnp.exp(a_log)[None, None] * jax.nn.softplus(ab[:, :, :, 0] + dt_bias[None, None])
    beta = jax.nn.sigmoid(ab[:, :, :, 1])
    o_f, s_f = gated_delta_chunked(qb, kb, vb, beta[:, :, 0], log_decay[:, :, 0], s_f0)
    o_br, s_bk = gated_delta_chunked(jnp.flip(qb, 1), jnp.flip(kb, 1), jnp.flip(vb, 1),
                                     jnp.flip(beta[:, :, 1], 1), jnp.flip(log_decay[:, :, 1], 1), s_b0)
    ob = rms_norm(o_f + jnp.flip(o_br, 1), delta_norm) * jax.nn.silu(gate_b.reshape(b, t, H_B, DV_B))
    out = jnp.concatenate([oa.reshape(b, t, W_A), ob.reshape(b, t, W_B)], axis=-1) @ w_out
    return out, (ka, va, s_f, s_bk)


def odd_mixer(h, w_in, ln_g, ln_b, w_s, b_s, w_out):
    b, t, _ = h.shape
    z = jax.nn.gelu(h @ w_in)
    u, v = jnp.split(z, 2, axis=-1)
    v = layer_norm(v, ln_g, ln_b)
    vc = v.reshape(b, t // C_CHUNK, C_CHUNK, C_GROUPS, W_C // C_GROUPS)
    mixed = jnp.einsum("gpq,bnqgc->bnpgc", w_s, vc) + b_s.T[None, None, :, :, None]
    return (u * mixed.reshape(b, t, W_C)) @ w_out


def hier_moe(h, w_rg, b_rg, w_re, b_re, w_g, w_u, w_d):
    b, t, d = h.shape
    x = h.reshape(-1, d)
    n = x.shape[0]
    g_prob = jax.nn.softmax((x @ w_rg + b_rg).astype(jnp.float32), axis=-1)
    p_group, g_idx = lax.top_k(g_prob, 1)
    e_logits = (x @ w_re + b_re).astype(jnp.float32).reshape(n, N_GROUPS, E_PER_GROUP)
    e_in = jnp.take_along_axis(e_logits, jnp.broadcast_to(g_idx[:, :, None], (n, 1, E_PER_GROUP)), axis=1)[:, 0]
    top_p, top_i = lax.top_k(jax.nn.softmax(e_in, axis=-1), TOP_K)
    wts = top_p / jnp.sum(top_p, axis=-1, keepdims=True) * p_group
    ids = g_idx * E_PER_GROUP + top_i
    gates = jnp.sum(jax.nn.one_hot(ids, N_EXPERTS, dtype=jnp.float32) * wts[..., None], axis=1)
    hid = jax.nn.silu(jnp.einsum("nd,edf->nef", x, w_g)) * jnp.einsum("nd,edf->nef", x, w_u)
    y = jnp.einsum("nef,efd->nd", hid * gates[:, :, None].astype(x.dtype), w_d)
    return y.reshape(b, t, d)


def setup_inputs(seed: int = 0) -> dict:
    key = jax.random.key(seed)
    ks = iter(jax.random.split(key, 40))
    nrm = lambda shape, s: jax.random.normal(next(ks), shape, jnp.float32) * s
    gain = lambda shape: 1.0 + nrm(shape, 0.1)
    d = D_MODEL
    return {
        "x_prompt": nrm((BATCH, SEQ, d), 1.0),
        "x_sample": nrm((DEC_BATCH, DEC_SEQ, d), 1.0),
        "cache_k": nrm((DEC_BATCH, N_EVEN, PAST_LEN, H_A, 2 * DH_A), 1.0),
        "cache_v": nrm((DEC_BATCH, N_EVEN, PAST_LEN, H_A, 2 * DH_A), 1.0),
        "state_fwd": nrm((DEC_BATCH, N_EVEN, H_B, DK_B, DV_B), 0.1),
        "state_bwd": nrm((DEC_BATCH, N_EVEN, H_B, DK_B, DV_B), 0.1),
        "c": nrm((DEC_BATCH, d), 1.0),
        "c_ctx": nrm((d,), 1.0),
        "norm_mix": gain((DEPTH, d)),
        "norm_ffn": gain((DEPTH, d)),
        "w_mod": nrm((DEPTH, d, N_MOD * d), d ** -0.5),
        "b_mod": nrm((DEPTH, N_MOD * d), 0.02),
        "w_in_even": nrm((N_EVEN, d, IN_EVEN), d ** -0.5),
        "conv_w": nrm((N_EVEN, CONV_K, 3 * W_B), CONV_K ** -0.5),
        "a_log": jnp.log(jax.random.uniform(next(ks), (N_EVEN, 2, H_B), jnp.float32, 1.0, 16.0)),
        "dt_bias": jnp.log(jnp.expm1(jax.random.uniform(next(ks), (N_EVEN, 2, H_B), jnp.float32, 0.001, 0.1))),
        "delta_norm": gain((N_EVEN, DV_B)),
        "lam_q": nrm((N_EVEN, 2, DH_A), 0.1),
        "lam_k": nrm((N_EVEN, 2, DH_A), 0.1),
        "subln": gain((N_EVEN, 2 * DH_A)),
        "w_out_even": nrm((N_EVEN, W_A + W_B, d), (W_A + W_B) ** -0.5),
        "w_in_odd": nrm((N_ODD, d, 2 * W_C), d ** -0.5),
        "sgu_ln_g": gain((N_ODD, W_C)),
        "sgu_ln_b": nrm((N_ODD, W_C), 0.02),
        "w_spatial": nrm((N_ODD, C_GROUPS, C_CHUNK, C_CHUNK), C_CHUNK ** -0.5),
        "b_spatial": 1.0 + nrm((N_ODD, C_GROUPS, C_CHUNK), 0.1),
        "w_out_odd": nrm((N_ODD, W_C, d), W_C ** -0.5),
        "w_router_group": nrm((DEPTH, d, N_GROUPS), d ** -0.5),
        "b_router_group": nrm((DEPTH, N_GROUPS), 0.01),
        "w_router_expert": nrm((DEPTH, d, N_EXPERTS), d ** -0.5),
        "b_router_expert": nrm((DEPTH, N_EXPERTS), 0.01),
        "w_exp_gate": nrm((DEPTH, N_EXPERTS, d, D_EXPERT), d ** -0.5),
        "w_exp_up": nrm((DEPTH, N_EXPERTS, d, D_EXPERT), d ** -0.5),
        "w_exp_down": nrm((DEPTH, N_EXPERTS, D_EXPERT, d), D_EXPERT ** -0.5),
        "final_norm": gain((d,)),
    }


def reference(x_prompt, x_sample, cache_k, cache_v, state_fwd, state_bwd, c, c_ctx,
              norm_mix, norm_ffn, w_mod, b_mod, w_in_even, conv_w, a_log, dt_bias, delta_norm,
              lam_q, lam_k, subln, w_out_even, w_in_odd, sgu_ln_g, sgu_ln_b, w_spatial, b_spatial,
              w_out_odd, w_router_group, b_router_group, w_router_expert, b_router_expert,
              w_exp_gate, w_exp_up, w_exp_down, final_norm):
    def run(x, cvec, latent):
        states = []
        for l in range(DEPTH):
            mod = jax.nn.silu(cvec) @ w_mod[l] + b_mod[l]
            sh1, sc1, g1, sh2, sc2, g2 = jnp.split(mod[:, None, :], N_MOD, axis=-1)
            h = rms_norm(x, norm_mix[l]) * (1.0 + sc1) + sh1
            i = l // 2
            if l % 2 == 0:
                ctx = (cache_k[:, i], cache_v[:, i], state_fwd[:, i], state_bwd[:, i]) if latent else None
                out, st = even_mixer(h, w_in_even[i], conv_w[i], a_log[i], dt_bias[i], delta_norm[i],
                                     lam_q[i], lam_k[i], subln[i], w_out_even[i],
                                     0.8 - 0.6 * math.exp(-0.3 * l), ctx)
                states.append(st)
            else:
                out = odd_mixer(h, w_in_odd[i], sgu_ln_g[i], sgu_ln_b[i], w_spatial[i], b_spatial[i],
                                w_out_odd[i])
            x = x + g1 * out
            h = rms_norm(x, norm_ffn[l]) * (1.0 + sc2) + sh2
            x = x + g2 * hier_moe(h, w_router_group[l], b_router_group[l], w_router_expert[l],
                                  b_router_expert[l], w_exp_gate[l], w_exp_up[l], w_exp_down[l])
        return rms_norm(x, final_norm), states

    y_prompt, st_p = run(x_prompt, c_ctx[None, :], False)
    y_sample, _ = run(x_sample, c, True)
    new_cache_k = jnp.stack([s[0] for s in st_p], axis=1)
    new_cache_v = jnp.stack([s[1] for s in st_p], axis=1)
    new_state_fwd = jnp.stack([s[2] for s in st_p], axis=1)
    new_state_bwd = jnp.stack([s[3] for s in st_p], axis=1)
    return (y_prompt, y_sample, new_cache_k, new_cache_v, new_state_fwd, new_state_bwd)
```

```python
import functools
import math

import jax
import jax.numpy as jnp
from jax import lax
from jax.experimental import pallas as pl
from jax.experimental.pallas import tpu as pltpu

f32 = jnp.float32
bf16 = jnp.bfloat16
i32 = jnp.int32
HIGHEST = lax.Precision.HIGHEST

D = 1024
BATCH, SEQ = 16, 256
DEC_BATCH, DEC_SEQ, PAST = 8, 1024, 512
DEPTH = 2
GRID_W = 64
N_MOD = 6
EPS = 1e-6
H_A, DH_A = 4, 64
W_A = H_A * 2 * DH_A
ROPE_BASE = 10000.0
AXIS_FREQS = DH_A // 4
H_B, DK_B, DV_B = 4, 128, 128
W_B = H_B * DV_B
CONV_K = 5
CH = 64
IN_EVEN = 3 * W_A + 4 * W_B + 4 * H_B
IN_EVEN_PAD = 3712
W_C = D
C_GROUPS, C_CHUNK = 4, 128
N_GROUPS, E_PER_GROUP, D_EXPERT = 4, 8, 256
N_EXPERTS = N_GROUPS * E_PER_GROUP

N_P = BATCH * SEQ
N_S = DEC_BATCH * DEC_SEQ
N_TOK = N_P + N_S
TM = 256
NPT = N_P // TM
NT = N_TOK // TM
COND_ROWS = 16

TMX = 128
N_BUCKETS = 256
MAX_NONEMPTY = N_GROUPS * (E_PER_GROUP * (E_PER_GROUP - 1) // 2)
MAXT = N_TOK // TMX + MAX_NONEMPTY
P_ROWS = MAXT * TMX
Y_ROWS = N_TOK + 2 * TMX

VMEM_LIMIT = 56 * 1024 * 1024


def _cp(sem):
    return pltpu.CompilerParams(dimension_semantics=sem, vmem_limit_bytes=VMEM_LIMIT)


def _mod_row(i):
    return jnp.where(i < NPT, DEC_BATCH, (i - NPT) // (DEC_SEQ // TM))


def _rms(x, g):
    return x * lax.rsqrt(jnp.mean(x * x, axis=-1, keepdims=True) + EPS) * g


def _silu(x):
    return x * jax.nn.sigmoid(x)


def _dot(a, b):
    return jnp.dot(a.astype(bf16), b.astype(bf16), preferred_element_type=f32)


TN_MOD = 1536


def _mod_kernel(c_ref, w_ref, b_ref, o_ref):
    a = _silu(c_ref[...])
    o_ref[0] = jnp.dot(a, w_ref[0], precision=HIGHEST, preferred_element_type=f32) + b_ref[0]


def _modulation(cond, w_mod, b_mod):
    return pl.pallas_call(
        _mod_kernel,
        grid=(DEPTH, N_MOD * D // TN_MOD),
        in_specs=[
            pl.BlockSpec((COND_ROWS, D), lambda l, j: (0, 0)),
            pl.BlockSpec((1, D, TN_MOD), lambda l, j: (l, 0, j)),
            pl.BlockSpec((1, 1, TN_MOD), lambda l, j: (l, 0, j)),
        ],
        out_specs=pl.BlockSpec((1, COND_ROWS, TN_MOD), lambda l, j: (l, 0, j)),
        out_shape=jax.ShapeDtypeStruct((DEPTH, COND_ROWS, N_MOD * D), f32),
        compiler_params=_cp(("arbitrary", "arbitrary")),
        name="modulation",
    )(cond, w_mod, b_mod.reshape(DEPTH, 1, N_MOD * D))


def _even_in_kernel(xp_ref, xs_ref, mod_ref, nrm_ref, w_ref, cos_ref, sin_ref,
                    q_ref, k_ref, v_ref, qkvb_ref, gate_ref, ab_ref):
    i = pl.program_id(0)
    x = jnp.where(i < NPT, xp_ref[...], xs_ref[...])
    mod = mod_ref[0]
    h = _rms(x, nrm_ref[...]) * (1.0 + mod[:, D:2 * D]) + mod[:, 0:D]
    proj = _dot(h, w_ref[...])
    cos = cos_ref[...]
    sin = sin_ref[...]
    lane = lax.broadcasted_iota(i32, (TM, 2 * DH_A), 1)
    first = (lane % (2 * AXIS_FREQS)) < AXIS_FREQS

    def rope(xh):
        partner = jnp.where(first, pltpu.roll(xh, 2 * DH_A - AXIS_FREQS, 1), pltpu.roll(xh, AXIS_FREQS, 1))
        return xh * cos + partner * sin

    for hh in range(H_A):
        sl = slice(hh * 2 * DH_A, (hh + 1) * 2 * DH_A)
        q_ref[:, sl] = (rope(proj[:, sl]) * (DH_A ** -0.5)).astype(bf16)
        k_ref[:, sl] = rope(proj[:, W_A + hh * 2 * DH_A:W_A + (hh + 1) * 2 * DH_A])
    v_ref[...] = proj[:, 2 * W_A:3 * W_A]
    qkvb_ref[...] = proj[:, 3 * W_A:3 * W_A + 3 * W_B]
    gate_ref[...] = proj[:, 3 * W_A + 3 * W_B:3 * W_A + 4 * W_B]
    ab_ref[...] = proj[:, 3 * W_A + 4 * W_B:IN_EVEN_PAD]


def _even_in(xp, xs, mod_l, nrm, w_in_pad, cos_t, sin_t):
    tile = lambda w: pl.BlockSpec((TM, w), lambda i: (i, 0))
    return pl.pallas_call(
        _even_in_kernel,
        grid=(NT,),
        in_specs=[
            pl.BlockSpec((TM, D), lambda i: (jnp.minimum(i, NPT - 1), 0)),
            pl.BlockSpec((TM, D), lambda i: (jnp.maximum(i - NPT, 0), 0)),
            pl.BlockSpec((1, 1, N_MOD * D), lambda i: (_mod_row(i), 0, 0)),
            pl.BlockSpec((1, D), lambda i: (0, 0)),
            pl.BlockSpec((D, IN_EVEN_PAD), lambda i: (0, 0)),
            pl.BlockSpec((TM, 2 * DH_A), lambda i: (jnp.where(i < NPT, 0, 1 + (i - NPT) % (DEC_SEQ // TM)), 0)),
            pl.BlockSpec((TM, 2 * DH_A), lambda i: (jnp.where(i < NPT, 0, 1 + (i - NPT) % (DEC_SEQ // TM)), 0)),
        ],
        out_specs=[tile(W_A), tile(W_A), tile(W_A), tile(3 * W_B), tile(W_B), tile(128)],
        out_shape=[
            jax.ShapeDtypeStruct((N_TOK, W_A), bf16),
            jax.ShapeDtypeStruct((N_TOK, W_A), f32),
            jax.ShapeDtypeStruct((N_TOK, W_A), f32),
            jax.ShapeDtypeStruct((N_TOK, 3 * W_B), f32),
            jax.ShapeDtypeStruct((N_TOK, W_B), f32),
            jax.ShapeDtypeStruct((N_TOK, 128), f32),
        ],
        compiler_params=_cp(("arbitrary",)),
        name="even_in",
    )(xp, xs, mod_l, nrm, w_in_pad, cos_t, sin_t)


def _attn_kernel(*refs, lam_init, has_ctx):
    if has_ctx:
        q_ref, kn_ref, vn_ref, kc_ref, vc_ref, lq_ref, lk_ref, sub_ref, o_ref = refs
    else:
        q_ref, kn_ref, vn_ref, lq_ref, lk_ref, sub_ref, o_ref = refs
    tq = q_ref.shape[0]
    lq = lq_ref[...]
    lk = lk_ref[...]
    prod = lq * lk
    lam = (jnp.exp(jnp.sum(prod[0:1], axis=-1, keepdims=True))
           - jnp.exp(jnp.sum(prod[1:2], axis=-1, keepdims=True)) + lam_init)
    lane = lax.broadcasted_iota(i32, (tq, 2 * DH_A), 1)
    nt = (((1,), (1,)), ((), ()))
    for hh in range(H_A):
        sl = slice(hh * 2 * DH_A, (hh + 1) * 2 * DH_A)
        q = q_ref[:, sl]
        srcs = [(kn_ref[:, sl].astype(bf16), vn_ref[:, sl].astype(bf16))]
        if has_ctx:
            srcs.append((kc_ref[:, sl].astype(bf16), vc_ref[:, sl].astype(bf16)))
        outs = []
        for part in range(2):
            qm = jnp.where((lane < DH_A) if part == 0 else (lane >= DH_A), q, jnp.zeros_like(q))
            ss = [lax.dot_general(qm, k, nt, preferred_element_type=f32) for k, _ in srcs]
            m = ss[0].max(axis=-1, keepdims=True)
            for s in ss[1:]:
                m = jnp.maximum(m, s.max(axis=-1, keepdims=True))
            l = jnp.zeros((tq, 1), f32)
            o = jnp.zeros((tq, 2 * DH_A), f32)
            for s, (_, v) in zip(ss, srcs):
                e = jnp.exp(s - m)
                l = l + jnp.sum(e, axis=-1, keepdims=True)
                o = o + jnp.dot(e.astype(bf16), v, preferred_element_type=f32)
            outs.append(o / l)
        o = outs[0] - lam * outs[1]
        o_ref[:, sl] = (_rms(o, sub_ref[...]) * (1.0 - lam_init)).astype(bf16)


def _attention(q, ka, va, lam_q, lam_k, subln, lam_init, *, seq, tq, tok_off, ctx=None):
    nq = seq // tq
    nb = (N_P if ctx is None else N_S) // seq
    qo, ko = tok_off // tq, tok_off // seq
    in_specs = [
        pl.BlockSpec((tq, W_A), lambda b, j: (qo + b * nq + j, 0)),
        pl.BlockSpec((seq, W_A), lambda b, j: (ko + b, 0)),
        pl.BlockSpec((seq, W_A), lambda b, j: (ko + b, 0)),
    ]
    args = [q, ka, va]
    if ctx is not None:
        in_specs += [pl.BlockSpec((PAST, W_A), lambda b, j: (b, 0))] * 2
        args += list(ctx)
    in_specs += [pl.BlockSpec((2, DH_A), lambda b, j: (0, 0))] * 2 + [pl.BlockSpec((1, 2 * DH_A), lambda b, j: (0, 0))]
    args += [lam_q, lam_k, subln]
    return pl.pallas_call(
        functools.partial(_attn_kernel, lam_init=lam_init, has_ctx=ctx is not None),
        grid=(nb, nq),
        in_specs=in_specs,
        out_specs=pl.BlockSpec((tq, W_A), lambda b, j: (b * nq + j, 0)),
        out_shape=jax.ShapeDtypeStruct((nb * seq, W_A), bf16),
        compiler_params=_cp(("arbitrary", "arbitrary")),
        name="diff_attn_ctx" if ctx is not None else "diff_attn",
    )(*args)


def _softplus(x):
    return jnp.maximum(x, 0.0) + jnp.log1p(jnp.exp(-jnp.abs(x)))


def _tri_inv(lmat, row, col):
    b16 = (row // 16) == (col // 16)
    b32 = (row // 32) == (col // 32)
    eye = (row == col).astype(f32)
    l0 = jnp.where(b16, lmat, 0.0)
    x = eye - l0
    p = _dot(l0, l0)
    x = x + _dot(x, p)
    p = _dot(p, p)
    x = x + _dot(x, p)
    p = _dot(p, p)
    x = x + _dot(x, p)
    c1 = jnp.where(jnp.logical_and(b32, jnp.logical_not(b16)), lmat, 0.0)
    x = x - _dot(x, _dot(c1, x))
    c2 = jnp.where(b32, 0.0, lmat)
    x = x - _dot(x, _dot(c2, x))
    return x


def _delta_kernel(*refs, seq, has_init):
    if has_init:
        (qkv_ref, gate_ref, ab_ref, cw_ref, al_ref, dt_ref, dn_ref, sf0_ref, sb0_ref,
         ob_ref, sf_ref, sb_ref, xpad, xq, xk, xv, gacc, bacc, o_f, o_b, st) = refs
    else:
        (qkv_ref, gate_ref, ab_ref, cw_ref, al_ref, dt_ref, dn_ref,
         ob_ref, sf_ref, sb_ref, xpad, xq, xk, xv, gacc, bacc, o_f, o_b, st) = refs
    n = seq // CH
    win = CH + 16

    xpad[0:8, :] = jnp.zeros((8, 3 * W_B), f32)
    xpad[seq + 8:seq + 16, :] = jnp.zeros((8, 3 * W_B), f32)
    xpad[8:seq + 8, :] = qkv_ref[...]
    neg_a = -jnp.exp(al_ref[...])
    dtb = dt_ref[...]

    def pre(c, carry):
        r0 = pl.multiple_of(c * CH, CH)
        a = xpad[pl.ds(r0, win), :]
        y = jnp.zeros((CH, 3 * W_B), f32)
        for j in range(CONV_K):
            sh = ((CONV_K - 1) // 2 - j) % win
            y = y + pltpu.roll(a, sh, 0)[8:8 + CH] * cw_ref[j:j + 1, :]
        y = _silu(y)
        for hh in range(H_B):
            qh = y[:, hh * DK_B:(hh + 1) * DK_B]
            kh = y[:, W_B + hh * DK_B:W_B + (hh + 1) * DK_B]
            qn = qh * lax.rsqrt(jnp.sum(qh * qh, axis=-1, keepdims=True) + EPS) * (DK_B ** -0.5)
            kn = kh * lax.rsqrt(jnp.sum(kh * kh, axis=-1, keepdims=True) + EPS)
            xq[pl.ds(r0, CH), hh * DK_B:(hh + 1) * DK_B] = qn
            xk[pl.ds(r0, CH), hh * DK_B:(hh + 1) * DK_B] = kn
        xv[pl.ds(r0, CH), :] = y[:, 2 * W_B:3 * W_B]
        ab = ab_ref[pl.ds(r0, CH), :]
        gacc[pl.ds(r0, CH), :] = neg_a * _softplus(ab + dtb)
        bacc[pl.ds(r0, CH), :] = jax.nn.sigmoid(ab)
        return carry

    lax.fori_loop(0, n, pre, 0)

    for hh in range(H_B):
        if has_init:
            st[hh] = sf0_ref[0, hh]
            st[H_B + hh] = sb0_ref[0, hh]
        else:
            st[hh] = jnp.zeros((DK_B, DV_B), f32)
            st[H_B + hh] = jnp.zeros((DK_B, DV_B), f32)

    row = lax.broadcasted_iota(i32, (CH, CH), 0)
    col = lax.broadcasted_iota(i32, (CH, CH), 1)
    rowl = lax.broadcasted_iota(i32, (CH, 128), 0)

    def step(it, carry):
        for d in range(2):
            c = it if d == 0 else n - 1 - it
            r0 = pl.multiple_of(c * CH, CH)
            g = gacc[pl.ds(r0, CH), :]
            beta = bacc[pl.ds(r0, CH), :]
            gc = g
            for s in (1, 2, 4, 8, 16, 32):
                gc = gc + jnp.where(rowl >= s, pltpu.roll(gc, s, 0), 0.0)
            tot = gc[CH - 1:CH, :]
            if d == 1:
                gc = tot - gc + g
            gct = gc.T
            incl = (row >= col) if d == 0 else (row <= col)
            strict = (row > col) if d == 0 else (row < col)
            for hh in range(H_B):
                ln = d * 2 * H_B + hh
                gcol = gc[:, ln:ln + 1]
                grow = gct[ln:ln + 1, :]
                bcol = beta[:, ln + H_B:ln + H_B + 1]
                tcol = tot[:, ln:ln + 1]
                q = xq[pl.ds(r0, CH), hh * DK_B:(hh + 1) * DK_B]
                k = xk[pl.ds(r0, CH), hh * DK_B:(hh + 1) * DK_B]
                v = xv[pl.ds(r0, CH), hh * DV_B:(hh + 1) * DV_B]
                kt = k.T
                dec = jnp.where(incl, jnp.exp(jnp.where(incl, gcol - grow, 0.0)), 0.0)
                kb = k * bcol
                lmat = jnp.where(strict, _dot(kb, kt) * dec, 0.0)
                att = jnp.where(incl, _dot(q, kt) * dec, 0.0)
                tinv = _tri_inv(lmat, row, col)
                eg = jnp.exp(gcol)
                u = _dot(tinv, v * bcol)
                w = _dot(tinv, kb * eg)
                s_old = st[d * H_B + hh]
                v_new = u - _dot(w, s_old)
                o = _dot(q * eg, s_old) + _dot(att, v_new)
                s_new = s_old * jnp.exp(tcol) + _dot(kt * jnp.exp(tcol - grow), v_new)
                st[d * H_B + hh] = s_new
                if d == 0:
                    o_f[pl.ds(r0, CH), hh * DV_B:(hh + 1) * DV_B] = o
                else:
                    o_b[pl.ds(r0, CH), hh * DV_B:(hh + 1) * DV_B] = o
        return carry

    lax.fori_loop(0, n, step, 0)

    def post(c, carry):
        r0 = pl.multiple_of(c * CH, CH)
        o = o_f[pl.ds(r0, CH), :] + o_b[pl.ds(r0, CH), :]
        gt = gate_ref[pl.ds(r0, CH), :]
        for hh in range(H_B):
            sl = slice(hh * DV_B, (hh + 1) * DV_B)
            ob_ref[pl.ds(r0, CH), sl] = (_rms(o[:, sl], dn_ref[...]) * _silu(gt[:, sl])).astype(bf16)
        return carry

    lax.fori_loop(0, n, post, 0)
    for hh in range(H_B):
        sf_ref[0, hh] = st[hh]
        sb_ref[0, hh] = st[H_B + hh]


def _delta(qkvb, gate, ab, conv_pad, al_lane, dt_lane, dnorm, *, seq, tok_off, init=None):
    nb = (N_P if init is None else N_S) // seq
    bo = tok_off // seq
    rows = lambda w: pl.BlockSpec((seq, w), lambda b: (bo + b, 0))
    const = lambda shp: pl.BlockSpec(shp, lambda b: tuple(0 for _ in shp))
    st_spec = pl.BlockSpec((1, H_B, DK_B, DV_B), lambda b: (b, 0, 0, 0))
    in_specs = [rows(3 * W_B), rows(W_B), rows(128), const((8, 3 * W_B)), const((1, 128)), const((1, 128)),
                const((1, DV_B))]
    args = [qkvb, gate, ab, conv_pad, al_lane, dt_lane, dnorm]
    if init is not None:
        in_specs += [st_spec, st_spec]
        args += [init[0], init[1]]
    return pl.pallas_call(
        functools.partial(_delta_kernel, seq=seq, has_init=init is not None),
        grid=(nb,),
        in_specs=in_specs,
        out_specs=[pl.BlockSpec((seq, W_B), lambda b: (b, 0)), st_spec, st_spec],
        out_shape=[
            jax.ShapeDtypeStruct((nb * seq, W_B), bf16),
            jax.ShapeDtypeStruct((nb, H_B, DK_B, DV_B), f32),
            jax.ShapeDtypeStruct((nb, H_B, DK_B, DV_B), f32),
        ],
        scratch_shapes=[
            pltpu.VMEM((seq + 16, 3 * W_B), f32),
            pltpu.VMEM((seq, W_B), f32),
            pltpu.VMEM((seq, W_B), f32),
            pltpu.VMEM((seq, W_B), f32),
            pltpu.VMEM((seq, 128), f32),
            pltpu.VMEM((seq, 128), f32),
            pltpu.VMEM((seq, W_B), f32),
            pltpu.VMEM((seq, W_B), f32),
            pltpu.VMEM((2 * H_B, DK_B, DV_B), f32),
        ],
        compiler_params=_cp(("arbitrary",)),
        name="delta_ctx" if init is not None else "delta",
    )(*args)


def _router_tail(i, x1, mod, nrm_ref, wr_ref, br_ref, h2_ref, route_ref, cnt_ref):
    h2 = _rms(x1, nrm_ref[...]) * (1.0 + mod[:, 4 * D:5 * D]) + mod[:, 3 * D:4 * D]
    h2_ref[...] = h2
    logits = jnp.dot(h2, wr_ref[...], precision=HIGHEST, preferred_element_type=f32) + br_ref[...]
    lane = lax.broadcasted_iota(i32, (TM, 128), 1)
    lane_f = lane.astype(f32)
    neg = jnp.float32(-1e30)
    lg = jnp.where(lane < N_GROUPS, logits[:, 0:128], neg)
    mg = lg.max(axis=-1, keepdims=True)
    gidx = jnp.min(jnp.where(lg == mg, lane_f, 128.0), axis=-1, keepdims=True)
    pg = 1.0 / jnp.sum(jnp.exp(lg - mg), axis=-1, keepdims=True)
    le = jnp.where((lane // E_PER_GROUP).astype(f32) == gidx, logits[:, 128:256], neg)
    m1 = le.max(axis=-1, keepdims=True)
    i1 = jnp.min(jnp.where(le == m1, lane_f, 128.0), axis=-1, keepdims=True)
    le2 = jnp.where(lane_f == i1, neg, le)
    m2 = le2.max(axis=-1, keepdims=True)
    i2 = jnp.min(jnp.where(le2 == m2, lane_f, 128.0), axis=-1, keepdims=True)
    e2 = jnp.exp(m2 - m1)
    w1 = pg / (1.0 + e2)
    w2 = w1 * e2
    ilo = jnp.minimum(i1, i2)
    ihi = jnp.maximum(i1, i2)
    wlo = jnp.where(i1 < i2, w1, w2)
    whi = jnp.where(i1 < i2, w2, w1)
    bucket = gidx * (E_PER_GROUP * E_PER_GROUP) + (ilo - gidx * E_PER_GROUP) * E_PER_GROUP + (ihi - gidx * E_PER_GROUP)
    colb = lax.broadcasted_iota(i32, (TM, N_BUCKETS), 1).astype(f32)
    onehot = (colb == bucket).astype(f32)
    r = lax.broadcasted_iota(i32, (TM, TM), 0)
    c = lax.broadcasted_iota(i32, (TM, TM), 1)
    before = jnp.dot((r > c).astype(bf16), onehot.astype(bf16), preferred_element_type=f32)

    @pl.when(i == 0)
    def _():
        cnt_ref[...] = jnp.zeros((1, N_BUCKETS), f32)

    cnt = cnt_ref[...]
    rank = jnp.sum(onehot * (before + cnt), axis=-1, keepdims=True)
    cnt_ref[...] = cnt + jnp.sum(onehot, axis=0, keepdims=True)
    tok1 = (lax.broadcasted_iota(i32, (TM, 1), 0) + i * TM + 1).astype(f32)
    cols = [bucket, rank, wlo, whi, tok1]
    route = jnp.zeros((TM, 128), f32)
    for j, cv in enumerate(cols):
        route = jnp.where(lane == j, cv, route)
    route_ref[...] = route


_ROUTER_OUT_SPECS = [
    pl.BlockSpec((TM, D), lambda i: (i, 0)),
    pl.BlockSpec((TM, D), lambda i: (i, 0)),
    pl.BlockSpec((TM, 128), lambda i: (i, 0)),
    pl.BlockSpec((1, N_BUCKETS), lambda i: (0, 0)),
]
_ROUTER_OUT_SHAPES = [
    jax.ShapeDtypeStruct((N_TOK, D), f32),
    jax.ShapeDtypeStruct((N_TOK, D), f32),
    jax.ShapeDtypeStruct((N_TOK, 128), f32),
    jax.ShapeDtypeStruct((1, N_BUCKETS), f32),
]


def _router_in_specs():
    return [pl.BlockSpec((1, D), lambda i: (0, 0)),
            pl.BlockSpec((D, 256), lambda i: (0, 0)),
            pl.BlockSpec((1, 256), lambda i: (0, 0))]


def _even_out_kernel(xp_ref, xs_ref, oap_ref, oas_ref, obp_ref, obs_ref, mod_ref, w_ref, nrm_ref, wr_ref, br_ref,
                     x1_ref, h2_ref, route_ref, cnt_ref):
    i = pl.program_id(0)
    is_p = i < NPT
    x = jnp.where(is_p, xp_ref[...], xs_ref[...])
    oa = jnp.where(is_p, oap_ref[...], oas_ref[...])
    ob = jnp.where(is_p, obp_ref[...], obs_ref[...])
    mod = mod_ref[0]
    out = (jnp.dot(oa, w_ref[0:W_A, :], preferred_element_type=f32)
           + jnp.dot(ob, w_ref[W_A:W_A + W_B, :], preferred_element_type=f32))
    x1 = x + mod[:, 2 * D:3 * D] * out
    x1_ref[...] = x1
    _router_tail(i, x1, mod, nrm_ref, wr_ref, br_ref, h2_ref, route_ref, cnt_ref)


def _even_out(xp, xs, oa_p, oa_s, ob_p, ob_s, mod_l, w_out, nrm, wr, br):
    ptile = lambda w: pl.BlockSpec((TM, w), lambda i: (jnp.minimum(i, NPT - 1), 0))
    stile = lambda w: pl.BlockSpec((TM, w), lambda i: (jnp.maximum(i - NPT, 0), 0))
    return pl.pallas_call(
        _even_out_kernel,
        grid=(NT,),
        in_specs=[
            ptile(D), stile(D), ptile(W_A), stile(W_A), ptile(W_B), stile(W_B),
            pl.BlockSpec((1, 1, N_MOD * D), lambda i: (_mod_row(i), 0, 0)),
            pl.BlockSpec((W_A + W_B, D), lambda i: (0, 0)),
        ] + _router_in_specs(),
        out_specs=_ROUTER_OUT_SPECS,
        out_shape=_ROUTER_OUT_SHAPES,
        compiler_params=_cp(("arbitrary",)),
        name="even_out_router",
    )(xp, xs, oa_p, oa_s, ob_p, ob_s, mod_l, w_out, nrm, wr, br)


def _gelu_tanh(x):
    return 0.5 * x * (1.0 + jnp.tanh(math.sqrt(2.0 / math.pi) * (x + 0.044715 * (x * x * x))))


def _odd_kernel(x_ref, y_ref, modp_ref, mod_ref, nm_ref, win_ref, lng_ref, lnb_ref, ws_ref, bs_ref, wout_ref,
                nrm_ref, wr_ref, br_ref, x1_ref, h2_ref, route_ref, cnt_ref, gated):
    i = pl.program_id(0)
    modp = modp_ref[0]
    mod = mod_ref[0]
    x = x_ref[...] + modp[:, 5 * D:6 * D] * y_ref[...]
    h = _rms(x, nm_ref[...]) * (1.0 + mod[:, D:2 * D]) + mod[:, 0:D]
    z = _gelu_tanh(_dot(h, win_ref[...]))
    u = z[:, 0:W_C]
    v = z[:, W_C:2 * W_C]
    mu = jnp.mean(v, axis=-1, keepdims=True)
    vc = v - mu
    var = jnp.mean(vc * vc, axis=-1, keepdims=True)
    vn = (vc * lax.rsqrt(var + EPS) * lng_ref[...] + lnb_ref[...]).astype(bf16)
    gw = W_C // C_GROUPS
    for ck in range(TM // C_CHUNK):
        rs = slice(ck * C_CHUNK, (ck + 1) * C_CHUNK)
        for g in range(C_GROUPS):
            cs = slice(g * gw, (g + 1) * gw)
            mixed = jnp.dot(ws_ref[g], vn[rs, cs], preferred_element_type=f32) + bs_ref[g]
            gated[rs, cs] = (u[rs, cs] * mixed).astype(bf16)
    out = jnp.dot(gated[...], wout_ref[...], preferred_element_type=f32)
    x1 = x + mod[:, 2 * D:3 * D] * out
    x1_ref[...] = x1
    _router_tail(i, x1, mod, nrm_ref, wr_ref, br_ref, h2_ref, route_ref, cnt_ref)


def _odd_mixer(x, y, mod_prev, mod_l, nm, w_in, ln_g, ln_b, ws, bs, w_out, nrm, wr, br):
    tile = lambda w: pl.BlockSpec((TM, w), lambda i: (i, 0))
    modspec = pl.BlockSpec((1, 1, N_MOD * D), lambda i: (_mod_row(i), 0, 0))
    const = lambda shp: pl.BlockSpec(shp, lambda i: tuple(0 for _ in shp))
    return pl.pallas_call(
        _odd_kernel,
        grid=(NT,),
        in_specs=[tile(D), tile(D), modspec, modspec, const((1, D)), const((D, 2 * W_C)), const((1, W_C)),
                  const((1, W_C)), const((C_GROUPS, C_CHUNK, C_CHUNK)), const((C_GROUPS, C_CHUNK, W_C // C_GROUPS)),
                  const((W_C, D))] + _router_in_specs(),
        out_specs=_ROUTER_OUT_SPECS,
        out_shape=_ROUTER_OUT_SHAPES,
        scratch_shapes=[pltpu.VMEM((TM, W_C), bf16)],
        compiler_params=_cp(("arbitrary",)),
        name="odd_mixer_router",
    )(x, y, mod_prev, mod_l, nm, w_in, ln_g, ln_b, ws, bs, w_out, nrm, wr, br)


def _expert_kernel(tg_ref, tlo_ref, thi_ref, gcur_ref, gnext_ref, scur_ref, rs_ref, h2_hbm, wg_ref, wu_ref, wd_ref,
                   y_hbm, xbuf, obuf, sem_g, sem_s):
    i = pl.program_id(0)
    last = pl.num_programs(0) - 1
    slot = i % 2
    nslot = 1 - slot

    def row_gather(idx_ref, s, r):
        return pltpu.make_async_copy(h2_hbm.at[pl.ds(idx_ref[0, 0, r], 1), :], xbuf.at[s, pl.ds(r, 1), :], sem_g.at[s])

    def row_scatter(dst_row, s, r):
        return pltpu.make_async_copy(obuf.at[s, pl.ds(r, 1), :], y_hbm.at[pl.ds(dst_row, 1), :], sem_s.at[s])

    def tile_gather_done(s):
        return pltpu.make_async_copy(h2_hbm.at[pl.ds(0, TMX), :], xbuf.at[s], sem_g.at[s])

    def tile_scatter_done(s):
        return pltpu.make_async_copy(obuf.at[s], y_hbm.at[pl.ds(0, TMX), :], sem_s.at[s])

    @pl.when(i == 0)
    def _():
        obuf[...] = jnp.zeros(obuf.shape, f32)
        for r in range(TMX):
            row_gather(gcur_ref, 0, r).start()
        for s in range(2):
            for r in range(TMX):
                row_scatter(N_TOK + s * TMX + r, s, r).start()

    tile_gather_done(slot).wait()
    tile_scatter_done(slot).wait()
    for r in range(TMX):
        row_gather(gnext_ref, nslot, r).start()

    xb = xbuf[slot].astype(bf16)
    rs = rs_ref[...]
    acc = jnp.zeros((TMX, D), f32)
    for e_ref, wcol in ((tlo_ref, rs[:, 2:3]), (thi_ref, rs[:, 3:4])):
        e = e_ref[i]
        a = jnp.dot(xb, wg_ref[e], preferred_element_type=f32)
        b = jnp.dot(xb, wu_ref[e], preferred_element_type=f32)
        hid = (_silu(a) * b * wcol).astype(bf16)
        acc = acc + jnp.dot(hid, wd_ref[e], preferred_element_type=f32)
    obuf[slot] = acc
    for r in range(TMX):
        row_scatter(scur_ref[0, 0, r], slot, r).start()

    @pl.when(i == last)
    def _():
        tile_gather_done(nslot).wait()
        tile_scatter_done(slot).wait()
        tile_scatter_done(nslot).wait()


def _experts(layer, tile_g, tile_lo, tile_hi, gidx, sidx, route_sorted, h2, wg, wu, wd):
    idx_cur = pl.BlockSpec((1, 1, TMX), lambda i, *_: (i, 0, 0), memory_space=pltpu.SMEM)
    idx_next = pl.BlockSpec((1, 1, TMX), lambda i, *_: (jnp.minimum(i + 1, MAXT - 1), 0, 0), memory_space=pltpu.SMEM)
    wspec = lambda a, b: pl.BlockSpec((None, None, E_PER_GROUP, a, b), lambda i, tg, tlo, thi: (layer, tg[i], 0, 0, 0))
    grid_spec = pltpu.PrefetchScalarGridSpec(
        num_scalar_prefetch=3,
        grid=(MAXT,),
        in_specs=[
            idx_cur, idx_next, idx_cur,
            pl.BlockSpec((TMX, 128), lambda i, *_: (i, 0)),
            pl.BlockSpec(memory_space=pl.ANY),
            wspec(D, D_EXPERT), wspec(D, D_EXPERT), wspec(D_EXPERT, D),
        ],
        out_specs=pl.BlockSpec(memory_space=pl.ANY),
        scratch_shapes=[
            pltpu.VMEM((2, TMX, D), f32),
            pltpu.VMEM((2, TMX, D), f32),
            pltpu.SemaphoreType.DMA((2,)),
            pltpu.SemaphoreType.DMA((2,)),
        ],
    )
    return pl.pallas_call(
        _expert_kernel,
        grid_spec=grid_spec,
        out_shape=jax.ShapeDtypeStruct((Y_ROWS, D), f32),
        compiler_params=_cp(("arbitrary",)),
        name="experts",
    )(tile_g, tile_lo, tile_hi, gidx, gidx, sidx, route_sorted, h2, wg, wu, wd)


def _dispatch_plan(route, cnt):
    bucket = route[:, 0].astype(i32)
    rank = route[:, 1].astype(i32)
    counts = cnt[0].astype(i32)
    tiles_b = (counts + TMX - 1) // TMX
    tile_end = jnp.cumsum(tiles_b)
    tile_start = tile_end - tiles_b
    slot = tile_start[bucket] * TMX + rank
    route_sorted = jnp.zeros((P_ROWS, 128), f32).at[slot].set(route, unique_indices=True)
    tok1 = route_sorted[:, 4].astype(i32)
    valid = tok1 > 0
    prow = jnp.arange(P_ROWS, dtype=i32)
    dummy = N_TOK + ((prow // TMX) % 2) * TMX + prow % TMX
    gidx = jnp.where(valid, tok1 - 1, 0).reshape(MAXT, 1, TMX)
    sidx = jnp.where(valid, tok1 - 1, dummy).reshape(MAXT, 1, TMX)
    tile_bucket = jnp.minimum(jnp.searchsorted(tile_end, jnp.arange(MAXT, dtype=i32), side="right"),
                              N_BUCKETS - 1).astype(i32)
    tile_g = tile_bucket // (E_PER_GROUP * E_PER_GROUP)
    tile_lo = (tile_bucket // E_PER_GROUP) % E_PER_GROUP
    tile_hi = tile_bucket % E_PER_GROUP
    return tile_g, tile_lo, tile_hi, gidx, sidx, route_sorted


def _final_kernel(x_ref, y_ref, mod_ref, fn_ref, o_ref):
    mod = mod_ref[0]
    x = x_ref[...] + mod[:, 5 * D:6 * D] * y_ref[...]
    o_ref[...] = _rms(x, fn_ref[...])


def _final(x, y, mod_l, fnorm, *, n_tiles, tile_off):
    return pl.pallas_call(
        _final_kernel,
        grid=(n_tiles,),
        in_specs=[
            pl.BlockSpec((TM, D), lambda i: (tile_off + i, 0)),
            pl.BlockSpec((TM, D), lambda i: (tile_off + i, 0)),
            pl.BlockSpec((1, 1, N_MOD * D), lambda i: (_mod_row(tile_off + i), 0, 0)),
            pl.BlockSpec((1, D), lambda i: (0, 0)),
        ],
        out_specs=pl.BlockSpec((TM, D), lambda i: (i, 0)),
        out_shape=jax.ShapeDtypeStruct((n_tiles * TM, D), f32),
        compiler_params=_cp(("arbitrary",)),
        name="final_norm",
    )(x, y, mod_l, fnorm)


def _rope_tables():
    rows = DEC_SEQ // GRID_W
    r, c = jnp.meshgrid(jnp.arange(rows), jnp.arange(GRID_W), indexing="ij")
    pos = jnp.stack([r.reshape(-1), c.reshape(-1)], axis=-1).astype(f32)
    inv = ROPE_BASE ** (-jnp.arange(AXIS_FREQS, dtype=f32) / AXIS_FREQS)
    ang = pos[:, :, None] * inv
    cos, sin = jnp.cos(ang), jnp.sin(ang)
    cos_h = jnp.concatenate([cos, cos], axis=-1).reshape(DEC_SEQ, DH_A)
    sin_h = jnp.concatenate([-sin, sin], axis=-1).reshape(DEC_SEQ, DH_A)
    cos_t = jnp.concatenate([jnp.ones((TM, 2 * DH_A), f32), jnp.tile(cos_h, (1, 2))], axis=0)
    sin_t = jnp.concatenate([jnp.zeros((TM, 2 * DH_A), f32), jnp.tile(sin_h, (1, 2))], axis=0)
    return cos_t, sin_t


def _gate_lanes(p):
    out = jnp.zeros((128,), f32)
    for d in range(2):
        out = out.at[d * 2 * H_B:d * 2 * H_B + H_B].set(p[d])
    return out.reshape(1, 128)


def _router_weights(w_rg, b_rg, w_re, b_re):
    wr = jnp.zeros((D, 256), f32).at[:, 0:N_GROUPS].set(w_rg).at[:, 128:128 + N_EXPERTS].set(w_re)
    br = jnp.zeros((1, 256), f32).at[0, 0:N_GROUPS].set(b_rg).at[0, 128:128 + N_EXPERTS].set(b_re)
    return wr, br


def kernel(x_prompt, x_sample, cache_k, cache_v, state_fwd, state_bwd, c, c_ctx, norm_mix, norm_ffn, w_mod, b_mod,
           w_in_even, conv_w, a_log, dt_bias, delta_norm, lam_q, lam_k, subln, w_out_even, w_in_odd, sgu_ln_g,
           sgu_ln_b, w_spatial, b_spatial, w_out_odd, w_router_group, b_router_group, w_router_expert,
           b_router_expert, w_exp_gate, w_exp_up, w_exp_down, final_norm):
    xp = x_prompt.reshape(N_P, D)
    xs = x_sample.reshape(N_S, D)
    cond = jnp.concatenate([c, c_ctx[None, :], jnp.zeros((COND_ROWS - DEC_BATCH - 1, D), f32)], axis=0)
    mod = _modulation(cond, w_mod, b_mod)
    mod0 = mod[0].reshape(COND_ROWS, 1, N_MOD * D)
    mod1 = mod[1].reshape(COND_ROWS, 1, N_MOD * D)

    lam_init = 0.8 - 0.6 * math.exp(-0.3 * 0)
    w_in_pad = jnp.pad(w_in_even[0], ((0, 0), (0, IN_EVEN_PAD - IN_EVEN))).astype(bf16)
    cos_t, sin_t = _rope_tables()
    qa, ka, va, qkvb, gate_b, ab = _even_in(xp, xs, mod0, norm_mix[0:1], w_in_pad, cos_t, sin_t)

    ctx_k = cache_k[:, 0].reshape(DEC_BATCH * PAST, W_A)
    ctx_v = cache_v[:, 0].reshape(DEC_BATCH * PAST, W_A)
    attn = functools.partial(_attention, qa, ka, va, lam_q[0], lam_k[0], subln[0:1], lam_init)
    oa_p = attn(seq=SEQ, tq=SEQ, tok_off=0)
    oa_s = attn(seq=DEC_SEQ, tq=512, tok_off=N_P, ctx=(ctx_k, ctx_v))

    conv_pad = jnp.pad(conv_w[0], ((0, 8 - CONV_K), (0, 0)))
    dl = functools.partial(_delta, qkvb, gate_b, ab, conv_pad, _gate_lanes(a_log[0]), _gate_lanes(dt_bias[0]),
                           delta_norm[0:1])
    ob_p, s_f, s_b = dl(seq=SEQ, tok_off=0)
    ob_s, _, _ = dl(seq=DEC_SEQ, tok_off=N_P, init=(state_fwd[:, 0], state_bwd[:, 0]))

    wr0, br0 = _router_weights(w_router_group[0], b_router_group[0], w_router_expert[0], b_router_expert[0])
    x1, h2, route, cnt = _even_out(xp, xs, oa_p, oa_s, ob_p, ob_s, mod0, w_out_even[0].astype(bf16), norm_ffn[0:1], wr0, br0)

    wshape = (DEPTH, N_GROUPS, E_PER_GROUP)
    wg = w_exp_gate.astype(bf16).reshape(wshape + (D, D_EXPERT))
    wu = w_exp_up.astype(bf16).reshape(wshape + (D, D_EXPERT))
    wd = w_exp_down.astype(bf16).reshape(wshape + (D_EXPERT, D))
    y0 = _experts(0, *_dispatch_plan(route, cnt), h2, wg, wu, wd)

    bs = jnp.broadcast_to(b_spatial[0][:, :, None], (C_GROUPS, C_CHUNK, W_C // C_GROUPS))
    wr1, br1 = _router_weights(w_router_group[1], b_router_group[1], w_router_expert[1], b_router_expert[1])
    x2, h2, route, cnt = _odd_mixer(x1, y0, mod0, mod1, norm_mix[1:2], w_in_odd[0].astype(bf16), sgu_ln_g[0:1],
                                    sgu_ln_b[0:1], w_spatial[0].astype(bf16), bs, w_out_odd[0].astype(bf16),
                                    norm_ffn[1:2], wr1, br1)
    y1 = _experts(1, *_dispatch_plan(route, cnt), h2, wg, wu, wd)

    y_prompt = _final(x2, y1, mod1, final_norm[None, :], n_tiles=NPT, tile_off=0)
    y_sample = _final(x2, y1, mod1, final_norm[None, :], n_tiles=NT - NPT, tile_off=NPT)

    new_cache_k = ka[:N_P].reshape(BATCH, 1, SEQ, H_A, 2 * DH_A)
    new_cache_v = va[:N_P].reshape(BATCH, 1, SEQ, H_A, 2 * DH_A)
    return (y_prompt.reshape(BATCH, SEQ, D), y_sample.reshape(DEC_BATCH, DEC_SEQ, D), new_cache_k, new_cache_v,
            s_f[:, None], s_b[:, None])
```

```python
import functools
import math

import jax
import jax.numpy as jnp
from jax import lax
from jax.experimental import pallas as pl
from jax.experimental.pallas import tpu as pltpu

f32 = jnp.float32
bf16 = jnp.bfloat16
i32 = jnp.int32
HIGHEST = lax.Precision.HIGHEST

D = 1024
BATCH, SEQ = 16, 256
DEC_BATCH, DEC_SEQ, PAST = 8, 1024, 512
DEPTH = 2
GRID_W = 64
N_MOD = 6
EPS = 1e-6
H_A, DH_A = 4, 64
W_A = H_A * 2 * DH_A
ROPE_BASE = 10000.0
AXIS_FREQS = DH_A // 4
H_B, DK_B, DV_B = 4, 128, 128
W_B = H_B * DV_B
CONV_K = 5
CH = 64
IN_EVEN = 3 * W_A + 4 * W_B + 4 * H_B
IN_EVEN_PAD = 3712
W_C = D
C_GROUPS, C_CHUNK = 4, 128
N_GROUPS, E_PER_GROUP, D_EXPERT = 4, 8, 256
N_EXPERTS = N_GROUPS * E_PER_GROUP

N_P = BATCH * SEQ
N_S = DEC_BATCH * DEC_SEQ
N_TOK = N_P + N_S
TM = 256
NPT = N_P // TM
NT = N_TOK // TM
COND_ROWS = 16

TMX = 128
N_BUCKETS = 256
MAX_NONEMPTY = N_GROUPS * (E_PER_GROUP * (E_PER_GROUP - 1) // 2)
MAXT = N_TOK // TMX + MAX_NONEMPTY
P_ROWS = MAXT * TMX
Y_ROWS = N_TOK + 2 * TMX

VMEM_LIMIT = 56 * 1024 * 1024


def _cp(sem):
    return pltpu.CompilerParams(dimension_semantics=sem, vmem_limit_bytes=VMEM_LIMIT)


def _mod_row(i):
    return jnp.where(i < NPT, DEC_BATCH, (i - NPT) // (DEC_SEQ // TM))


def _rms(x, g):
    return x * lax.rsqrt(jnp.mean(x * x, axis=-1, keepdims=True) + EPS) * g


def _silu(x):
    return x * jax.nn.sigmoid(x)


def _dot(a, b):
    return jnp.dot(a.astype(bf16), b.astype(bf16), preferred_element_type=f32)


ROW_SUB = D // 128


def _store_rows(ref, base, x):
    rows = x.shape[0]
    for j in range(ROW_SUB):
        ref[pl.ds(base + j, rows, stride=ROW_SUB), :] = x[:, j * 128:(j + 1) * 128]


def _load_rows(ref, base, rows):
    return jnp.concatenate([ref[pl.ds(base + j, rows, stride=ROW_SUB), :] for j in range(ROW_SUB)], axis=1)


TN_MOD = 1536


def _mod_kernel(c_ref, w_ref, b_ref, o_ref):
    a = _silu(c_ref[...])
    o_ref[0] = jnp.dot(a, w_ref[0], precision=HIGHEST, preferred_element_type=f32) + b_ref[0]


def _modulation(cond, w_mod, b_mod):
    return pl.pallas_call(
        _mod_kernel,
        grid=(DEPTH, N_MOD * D // TN_MOD),
        in_specs=[
            pl.BlockSpec((COND_ROWS, D), lambda l, j: (0, 0)),
            pl.BlockSpec((1, D, TN_MOD), lambda l, j: (l, 0, j)),
            pl.BlockSpec((1, 1, TN_MOD), lambda l, j: (l, 0, j)),
        ],
        out_specs=pl.BlockSpec((1, COND_ROWS, TN_MOD), lambda l, j: (l, 0, j)),
        out_shape=jax.ShapeDtypeStruct((DEPTH, COND_ROWS, N_MOD * D), f32),
        compiler_params=_cp(("arbitrary", "arbitrary")),
        name="modulation",
    )(cond, w_mod, b_mod.reshape(DEPTH, 1, N_MOD * D))


def _even_in_kernel(xp_ref, xs_ref, mod_ref, nrm_ref, w_ref, cos_ref, sin_ref,
                    q_ref, k_ref, v_ref, qkvb_ref, gate_ref, ab_ref):
    i = pl.program_id(0)
    x = jnp.where(i < NPT, xp_ref[...], xs_ref[...])
    mod = mod_ref[0]
    h = _rms(x, nrm_ref[...]) * (1.0 + mod[:, D:2 * D]) + mod[:, 0:D]
    proj = _dot(h, w_ref[...])
    cos = cos_ref[...]
    sin = sin_ref[...]
    lane = lax.broadcasted_iota(i32, (TM, 2 * DH_A), 1)
    first = (lane % (2 * AXIS_FREQS)) < AXIS_FREQS

    def rope(xh):
        partner = jnp.where(first, pltpu.roll(xh, 2 * DH_A - AXIS_FREQS, 1), pltpu.roll(xh, AXIS_FREQS, 1))
        return xh * cos + partner * sin

    for hh in range(H_A):
        sl = slice(hh * 2 * DH_A, (hh + 1) * 2 * DH_A)
        q_ref[:, sl] = (rope(proj[:, sl]) * (DH_A ** -0.5)).astype(bf16)
        k_ref[:, sl] = rope(proj[:, W_A + hh * 2 * DH_A:W_A + (hh + 1) * 2 * DH_A])
    v_ref[...] = proj[:, 2 * W_A:3 * W_A]
    qkvb_ref[...] = proj[:, 3 * W_A:3 * W_A + 3 * W_B]
    gate_ref[...] = proj[:, 3 * W_A + 3 * W_B:3 * W_A + 4 * W_B]
    ab_ref[...] = proj[:, 3 * W_A + 4 * W_B:IN_EVEN_PAD]


def _even_in(xp, xs, mod_l, nrm, w_in_pad, cos_t, sin_t):
    tile = lambda w: pl.BlockSpec((TM, w), lambda i: (i, 0))
    return pl.pallas_call(
        _even_in_kernel,
        grid=(NT,),
        in_specs=[
            pl.BlockSpec((TM, D), lambda i: (jnp.minimum(i, NPT - 1), 0)),
            pl.BlockSpec((TM, D), lambda i: (jnp.maximum(i - NPT, 0), 0)),
            pl.BlockSpec((1, 1, N_MOD * D), lambda i: (_mod_row(i), 0, 0)),
            pl.BlockSpec((1, D), lambda i: (0, 0)),
            pl.BlockSpec((D, IN_EVEN_PAD), lambda i: (0, 0)),
            pl.BlockSpec((TM, 2 * DH_A), lambda i: (jnp.where(i < NPT, 0, 1 + (i - NPT) % (DEC_SEQ // TM)), 0)),
            pl.BlockSpec((TM, 2 * DH_A), lambda i: (jnp.where(i < NPT, 0, 1 + (i - NPT) % (DEC_SEQ // TM)), 0)),
        ],
        out_specs=[tile(W_A), tile(W_A), tile(W_A), tile(3 * W_B), tile(W_B), tile(128)],
        out_shape=[
            jax.ShapeDtypeStruct((N_TOK, W_A), bf16),
            jax.ShapeDtypeStruct((N_TOK, W_A), f32),
            jax.ShapeDtypeStruct((N_TOK, W_A), f32),
            jax.ShapeDtypeStruct((N_TOK, 3 * W_B), f32),
            jax.ShapeDtypeStruct((N_TOK, W_B), f32),
            jax.ShapeDtypeStruct((N_TOK, 128), f32),
        ],
        compiler_params=_cp(("arbitrary",)),
        name="even_in",
    )(xp, xs, mod_l, nrm, w_in_pad, cos_t, sin_t)


def _attn_kernel(*refs, lam_init, has_ctx):
    if has_ctx:
        q_ref, kn_ref, vn_ref, kc_ref, vc_ref, lq_ref, lk_ref, sub_ref, o_ref = refs
    else:
        q_ref, kn_ref, vn_ref, lq_ref, lk_ref, sub_ref, o_ref = refs
    tq = q_ref.shape[0]
    lq = lq_ref[...]
    lk = lk_ref[...]
    prod = lq * lk
    lam = (jnp.exp(jnp.sum(prod[0:1], axis=-1, keepdims=True))
           - jnp.exp(jnp.sum(prod[1:2], axis=-1, keepdims=True)) + lam_init)
    lane = lax.broadcasted_iota(i32, (tq, 2 * DH_A), 1)
    nt = (((1,), (1,)), ((), ()))
    for hh in range(H_A):
        sl = slice(hh * 2 * DH_A, (hh + 1) * 2 * DH_A)
        q = q_ref[:, sl]
        srcs = [(kn_ref[:, sl].astype(bf16), vn_ref[:, sl].astype(bf16))]
        if has_ctx:
            srcs.append((kc_ref[:, sl].astype(bf16), vc_ref[:, sl].astype(bf16)))
        outs = []
        for part in range(2):
            qm = jnp.where((lane < DH_A) if part == 0 else (lane >= DH_A), q, jnp.zeros_like(q))
            ss = [lax.dot_general(qm, k, nt, preferred_element_type=f32) for k, _ in srcs]
            m = ss[0].max(axis=-1, keepdims=True)
            for s in ss[1:]:
                m = jnp.maximum(m, s.max(axis=-1, keepdims=True))
            l = jnp.zeros((tq, 1), f32)
            o = jnp.zeros((tq, 2 * DH_A), f32)
            for s, (_, v) in zip(ss, srcs):
                e = jnp.exp(s - m)
                l = l + jnp.sum(e, axis=-1, keepdims=True)
                o = o + jnp.dot(e.astype(bf16), v, preferred_element_type=f32)
            outs.append(o / l)
        o = outs[0] - lam * outs[1]
        o_ref[:, sl] = (_rms(o, sub_ref[...]) * (1.0 - lam_init)).astype(bf16)


def _attention(q, ka, va, lam_q, lam_k, subln, lam_init, *, seq, tq, tok_off, ctx=None):
    nq = seq // tq
    nb = (N_P if ctx is None else N_S) // seq
    qo, ko = tok_off // tq, tok_off // seq
    in_specs = [
        pl.BlockSpec((tq, W_A), lambda b, j: (qo + b * nq + j, 0)),
        pl.BlockSpec((seq, W_A), lambda b, j: (ko + b, 0)),
        pl.BlockSpec((seq, W_A), lambda b, j: (ko + b, 0)),
    ]
    args = [q, ka, va]
    if ctx is not None:
        in_specs += [pl.BlockSpec((PAST, W_A), lambda b, j: (b, 0))] * 2
        args += list(ctx)
    in_specs += [pl.BlockSpec((2, DH_A), lambda b, j: (0, 0))] * 2 + [pl.BlockSpec((1, 2 * DH_A), lambda b, j: (0, 0))]
    args += [lam_q, lam_k, subln]
    return pl.pallas_call(
        functools.partial(_attn_kernel, lam_init=lam_init, has_ctx=ctx is not None),
        grid=(nb, nq),
        in_specs=in_specs,
        out_specs=pl.BlockSpec((tq, W_A), lambda b, j: (b * nq + j, 0)),
        out_shape=jax.ShapeDtypeStruct((nb * seq, W_A), bf16),
        compiler_params=_cp(("arbitrary", "arbitrary")),
        name="diff_attn_ctx" if ctx is not None else "diff_attn",
    )(*args)


def _softplus(x):
    return jnp.maximum(x, 0.0) + jnp.log1p(jnp.exp(-jnp.abs(x)))


def _tri_inv(lmats, row, col):
    b16 = (row // 16) == (col // 16)
    b32 = (row // 32) == (col // 32)
    eye = (row == col).astype(f32)
    n = len(lmats)
    l0 = [jnp.where(b16, l, 0.0) for l in lmats]
    x = [eye - a for a in l0]
    p = [_dot(a, a) for a in l0]
    for level in range(3):
        xp = [_dot(x[i], p[i]) for i in range(n)]
        if level < 2:
            p = [_dot(a, a) for a in p]
        x = [x[i] + xp[i] for i in range(n)]
    for off_diag in (jnp.logical_and(b32, jnp.logical_not(b16)), jnp.logical_not(b32)):
        t = [_dot(jnp.where(off_diag, lmats[i], 0.0), x[i]) for i in range(n)]
        t = [_dot(x[i], t[i]) for i in range(n)]
        x = [x[i] - t[i] for i in range(n)]
    return x


def _delta_kernel(*refs, seq, has_init):
    if has_init:
        (qkv_ref, gate_ref, ab_ref, cw_ref, al_ref, dt_ref, dn_ref, sf0_ref, sb0_ref,
         ob_ref, sf_ref, sb_ref, xpad, xq, xk, xv, gacc, bacc, o_f, o_b, st) = refs
    else:
        (qkv_ref, gate_ref, ab_ref, cw_ref, al_ref, dt_ref, dn_ref,
         ob_ref, sf_ref, sb_ref, xpad, xq, xk, xv, gacc, bacc, o_f, o_b, st) = refs
    n = seq // CH
    win = CH + 16

    xpad[0:8, :] = jnp.zeros((8, 3 * W_B), f32)
    xpad[seq + 8:seq + 16, :] = jnp.zeros((8, 3 * W_B), f32)
    xpad[8:seq + 8, :] = qkv_ref[...]
    neg_a = -jnp.exp(al_ref[...])
    dtb = dt_ref[...]

    def pre(c, carry):
        r0 = pl.multiple_of(c * CH, CH)
        a = xpad[pl.ds(r0, win), :]
        y = jnp.zeros((CH, 3 * W_B), f32)
        for j in range(CONV_K):
            sh = ((CONV_K - 1) // 2 - j) % win
            y = y + pltpu.roll(a, sh, 0)[8:8 + CH] * cw_ref[j:j + 1, :]
        y = _silu(y)
        for hh in range(H_B):
            qh = y[:, hh * DK_B:(hh + 1) * DK_B]
            kh = y[:, W_B + hh * DK_B:W_B + (hh + 1) * DK_B]
            qn = qh * lax.rsqrt(jnp.sum(qh * qh, axis=-1, keepdims=True) + EPS) * (DK_B ** -0.5)
            kn = kh * lax.rsqrt(jnp.sum(kh * kh, axis=-1, keepdims=True) + EPS)
            xq[pl.ds(r0, CH), hh * DK_B:(hh + 1) * DK_B] = qn
            xk[pl.ds(r0, CH), hh * DK_B:(hh + 1) * DK_B] = kn
        xv[pl.ds(r0, CH), :] = y[:, 2 * W_B:3 * W_B]
        ab = ab_ref[pl.ds(r0, CH), :]
        gacc[pl.ds(r0, CH), :] = neg_a * _softplus(ab + dtb)
        bacc[pl.ds(r0, CH), :] = jax.nn.sigmoid(ab)
        return carry

    lax.fori_loop(0, n, pre, 0)

    for hh in range(H_B):
        if has_init:
            st[hh] = sf0_ref[0, hh]
            st[H_B + hh] = sb0_ref[0, hh]
        else:
            st[hh] = jnp.zeros((DK_B, DV_B), f32)
            st[H_B + hh] = jnp.zeros((DK_B, DV_B), f32)

    row = lax.broadcasted_iota(i32, (CH, CH), 0)
    col = lax.broadcasted_iota(i32, (CH, CH), 1)
    rowl = lax.broadcasted_iota(i32, (CH, 128), 0)

    def step(it, carry):
        chains = []
        for d in range(2):
            c = it if d == 0 else n - 1 - it
            r0 = pl.multiple_of(c * CH, CH)
            g = gacc[pl.ds(r0, CH), :]
            beta = bacc[pl.ds(r0, CH), :]
            gc = g
            for s in (1, 2, 4, 8, 16, 32):
                gc = gc + jnp.where(rowl >= s, pltpu.roll(gc, s, 0), 0.0)
            tot = gc[CH - 1:CH, :]
            if d == 1:
                gc = tot - gc + g
            gct = gc.T
            incl = (row >= col) if d == 0 else (row <= col)
            strict = (row > col) if d == 0 else (row < col)
            for hh in range(H_B):
                ln = d * 2 * H_B + hh
                gcol = gc[:, ln:ln + 1]
                grow = gct[ln:ln + 1, :]
                k = xk[pl.ds(r0, CH), hh * DK_B:(hh + 1) * DK_B]
                bcol = beta[:, ln + H_B:ln + H_B + 1]
                chains.append(dict(
                    d=d, hh=hh, r0=r0, incl=incl, strict=strict, grow=grow, bcol=bcol, tcol=tot[:, ln:ln + 1],
                    q=xq[pl.ds(r0, CH), hh * DK_B:(hh + 1) * DK_B], kt=k.T, kb=k * bcol,
                    v=xv[pl.ds(r0, CH), hh * DV_B:(hh + 1) * DV_B], eg=jnp.exp(gcol),
                    dec=jnp.where(incl, jnp.exp(jnp.where(incl, gcol - grow, 0.0)), 0.0)))
        kk = [_dot(e["kb"], e["kt"]) for e in chains]
        qk = [_dot(e["q"], e["kt"]) for e in chains]
        lmat = [jnp.where(e["strict"], kk[i] * e["dec"], 0.0) for i, e in enumerate(chains)]
        att = [jnp.where(e["incl"], qk[i] * e["dec"], 0.0) for i, e in enumerate(chains)]
        tinv = _tri_inv(lmat, row, col)
        u = [_dot(tinv[i], e["v"] * e["bcol"]) for i, e in enumerate(chains)]
        w = [_dot(tinv[i], e["kb"] * e["eg"]) for i, e in enumerate(chains)]
        s_old = [st[e["d"] * H_B + e["hh"]] for e in chains]
        ws = [_dot(w[i], s_old[i]) for i in range(len(chains))]
        qs = [_dot(e["q"] * e["eg"], s_old[i]) for i, e in enumerate(chains)]
        v_new = [u[i] - ws[i] for i in range(len(chains))]
        av = [_dot(att[i], v_new[i]) for i in range(len(chains))]
        kv = [_dot(e["kt"] * jnp.exp(e["tcol"] - e["grow"]), v_new[i]) for i, e in enumerate(chains)]
        for i, e in enumerate(chains):
            st[e["d"] * H_B + e["hh"]] = s_old[i] * jnp.exp(e["tcol"]) + kv[i]
            cols = slice(e["hh"] * DV_B, (e["hh"] + 1) * DV_B)
            if e["d"] == 0:
                o_f[pl.ds(e["r0"], CH), cols] = qs[i] + av[i]
            else:
                o_b[pl.ds(e["r0"], CH), cols] = qs[i] + av[i]
        return carry

    lax.fori_loop(0, n, step, 0)

    def post(c, carry):
        r0 = pl.multiple_of(c * CH, CH)
        o = o_f[pl.ds(r0, CH), :] + o_b[pl.ds(r0, CH), :]
        gt = gate_ref[pl.ds(r0, CH), :]
        for hh in range(H_B):
            sl = slice(hh * DV_B, (hh + 1) * DV_B)
            ob_ref[pl.ds(r0, CH), sl] = (_rms(o[:, sl], dn_ref[...]) * _silu(gt[:, sl])).astype(bf16)
        return carry

    lax.fori_loop(0, n, post, 0)
    for hh in range(H_B):
        sf_ref[0, hh] = st[hh]
        sb_ref[0, hh] = st[H_B + hh]


def _delta(qkvb, gate, ab, conv_pad, al_lane, dt_lane, dnorm, *, seq, tok_off, init=None):
    nb = (N_P if init is None else N_S) // seq
    bo = tok_off // seq
    rows = lambda w: pl.BlockSpec((seq, w), lambda b: (bo + b, 0))
    const = lambda shp: pl.BlockSpec(shp, lambda b: tuple(0 for _ in shp))
    st_spec = pl.BlockSpec((1, H_B, DK_B, DV_B), lambda b: (b, 0, 0, 0))
    in_specs = [rows(3 * W_B), rows(W_B), rows(128), const((8, 3 * W_B)), const((1, 128)), const((1, 128)),
                const((1, DV_B))]
    args = [qkvb, gate, ab, conv_pad, al_lane, dt_lane, dnorm]
    if init is not None:
        in_specs += [st_spec, st_spec]
        args += [init[0], init[1]]
    return pl.pallas_call(
        functools.partial(_delta_kernel, seq=seq, has_init=init is not None),
        grid=(nb,),
        in_specs=in_specs,
        out_specs=[pl.BlockSpec((seq, W_B), lambda b: (b, 0)), st_spec, st_spec],
        out_shape=[
            jax.ShapeDtypeStruct((nb * seq, W_B), bf16),
            jax.ShapeDtypeStruct((nb, H_B, DK_B, DV_B), f32),
            jax.ShapeDtypeStruct((nb, H_B, DK_B, DV_B), f32),
        ],
        scratch_shapes=[
            pltpu.VMEM((seq + 16, 3 * W_B), f32),
            pltpu.VMEM((seq, W_B), f32),
            pltpu.VMEM((seq, W_B), f32),
            pltpu.VMEM((seq, W_B), f32),
            pltpu.VMEM((seq, 128), f32),
            pltpu.VMEM((seq, 128), f32),
            pltpu.VMEM((seq, W_B), f32),
            pltpu.VMEM((seq, W_B), f32),
            pltpu.VMEM((2 * H_B, DK_B, DV_B), f32),
        ],
        compiler_params=_cp(("arbitrary",)),
        name="delta_ctx" if init is not None else "delta",
    )(*args)


def _router_tail(i, x1, mod, nrm_ref, wr_ref, br_ref, h2_ref, route_ref, cnt_ref):
    h2 = _rms(x1, nrm_ref[...]) * (1.0 + mod[:, 4 * D:5 * D]) + mod[:, 3 * D:4 * D]
    _store_rows(h2_ref, 0, h2)
    logits = jnp.dot(h2, wr_ref[...], precision=HIGHEST, preferred_element_type=f32) + br_ref[...]
    lane = lax.broadcasted_iota(i32, (TM, 128), 1)
    lane_f = lane.astype(f32)
    neg = jnp.float32(-1e30)
    lg = jnp.where(lane < N_GROUPS, logits[:, 0:128], neg)
    mg = lg.max(axis=-1, keepdims=True)
    gidx = jnp.min(jnp.where(lg == mg, lane_f, 128.0), axis=-1, keepdims=True)
    pg = 1.0 / jnp.sum(jnp.exp(lg - mg), axis=-1, keepdims=True)
    le = jnp.where((lane // E_PER_GROUP).astype(f32) == gidx, logits[:, 128:256], neg)
    m1 = le.max(axis=-1, keepdims=True)
    i1 = jnp.min(jnp.where(le == m1, lane_f, 128.0), axis=-1, keepdims=True)
    le2 = jnp.where(lane_f == i1, neg, le)
    m2 = le2.max(axis=-1, keepdims=True)
    i2 = jnp.min(jnp.where(le2 == m2, lane_f, 128.0), axis=-1, keepdims=True)
    e2 = jnp.exp(m2 - m1)
    w1 = pg / (1.0 + e2)
    w2 = w1 * e2
    ilo = jnp.minimum(i1, i2)
    ihi = jnp.maximum(i1, i2)
    wlo = jnp.where(i1 < i2, w1, w2)
    whi = jnp.where(i1 < i2, w2, w1)
    bucket = gidx * (E_PER_GROUP * E_PER_GROUP) + (ilo - gidx * E_PER_GROUP) * E_PER_GROUP + (ihi - gidx * E_PER_GROUP)
    colb = lax.broadcasted_iota(i32, (TM, N_BUCKETS), 1).astype(f32)
    onehot = (colb == bucket).astype(f32)
    r = lax.broadcasted_iota(i32, (TM, TM), 0)
    c = lax.broadcasted_iota(i32, (TM, TM), 1)
    before = jnp.dot((r > c).astype(bf16), onehot.astype(bf16), preferred_element_type=f32)

    @pl.when(i == 0)
    def _():
        cnt_ref[...] = jnp.zeros((1, N_BUCKETS), f32)

    cnt = cnt_ref[...]
    rank = jnp.sum(onehot * (before + cnt), axis=-1, keepdims=True)
    cnt_ref[...] = cnt + jnp.sum(onehot, axis=0, keepdims=True)
    tok1 = (lax.broadcasted_iota(i32, (TM, 1), 0) + i * TM + 1).astype(f32)
    cols = [bucket, rank, wlo, whi, tok1]
    route = jnp.zeros((TM, 128), f32)
    for j, cv in enumerate(cols):
        route = jnp.where(lane == j, cv, route)
    route_ref[...] = route


_ROUTER_OUT_SPECS = [
    pl.BlockSpec((TM, D), lambda i: (i, 0)),
    pl.BlockSpec((TM * ROW_SUB, 128), lambda i: (i, 0)),
    pl.BlockSpec((TM, 128), lambda i: (i, 0)),
    pl.BlockSpec((1, N_BUCKETS), lambda i: (0, 0)),
]
_ROUTER_OUT_SHAPES = [
    jax.ShapeDtypeStruct((N_TOK, D), f32),
    jax.ShapeDtypeStruct((N_TOK * ROW_SUB, 128), f32),
    jax.ShapeDtypeStruct((N_TOK, 128), f32),
    jax.ShapeDtypeStruct((1, N_BUCKETS), f32),
]


def _router_in_specs():
    return [pl.BlockSpec((1, D), lambda i: (0, 0)),
            pl.BlockSpec((D, 256), lambda i: (0, 0)),
            pl.BlockSpec((1, 256), lambda i: (0, 0))]


def _even_out_kernel(xp_ref, xs_ref, oap_ref, oas_ref, obp_ref, obs_ref, mod_ref, w_ref, nrm_ref, wr_ref, br_ref,
                     x1_ref, h2_ref, route_ref, cnt_ref):
    i = pl.program_id(0)
    is_p = i < NPT
    x = jnp.where(is_p, xp_ref[...], xs_ref[...])
    oa = jnp.where(is_p, oap_ref[...], oas_ref[...])
    ob = jnp.where(is_p, obp_ref[...], obs_ref[...])
    mod = mod_ref[0]
    out = (jnp.dot(oa, w_ref[0:W_A, :], preferred_element_type=f32)
           + jnp.dot(ob, w_ref[W_A:W_A + W_B, :], preferred_element_type=f32))
    x1 = x + mod[:, 2 * D:3 * D] * out
    x1_ref[...] = x1
    _router_tail(i, x1, mod, nrm_ref, wr_ref, br_ref, h2_ref, route_ref, cnt_ref)


def _even_out(xp, xs, oa_p, oa_s, ob_p, ob_s, mod_l, w_out, nrm, wr, br):
    ptile = lambda w: pl.BlockSpec((TM, w), lambda i: (jnp.minimum(i, NPT - 1), 0))
    stile = lambda w: pl.BlockSpec((TM, w), lambda i: (jnp.maximum(i - NPT, 0), 0))
    return pl.pallas_call(
        _even_out_kernel,
        grid=(NT,),
        in_specs=[
            ptile(D), stile(D), ptile(W_A), stile(W_A), ptile(W_B), stile(W_B),
            pl.BlockSpec((1, 1, N_MOD * D), lambda i: (_mod_row(i), 0, 0)),
            pl.BlockSpec((W_A + W_B, D), lambda i: (0, 0)),
        ] + _router_in_specs(),
        out_specs=_ROUTER_OUT_SPECS,
        out_shape=_ROUTER_OUT_SHAPES,
        compiler_params=_cp(("arbitrary",)),
        name="even_out_router",
    )(xp, xs, oa_p, oa_s, ob_p, ob_s, mod_l, w_out, nrm, wr, br)


def _gelu_tanh(x):
    return 0.5 * x * (1.0 + jnp.tanh(math.sqrt(2.0 / math.pi) * (x + 0.044715 * (x * x * x))))


def _odd_kernel(x_ref, y_ref, modp_ref, mod_ref, nm_ref, win_ref, lng_ref, lnb_ref, ws_ref, bs_ref, wout_ref,
                nrm_ref, wr_ref, br_ref, x1_ref, h2_ref, route_ref, cnt_ref, gated):
    i = pl.program_id(0)
    modp = modp_ref[0]
    mod = mod_ref[0]
    x = x_ref[...] + modp[:, 5 * D:6 * D] * _load_rows(y_ref, 0, TM)
    h = _rms(x, nm_ref[...]) * (1.0 + mod[:, D:2 * D]) + mod[:, 0:D]
    z = _gelu_tanh(_dot(h, win_ref[...]))
    u = z[:, 0:W_C]
    v = z[:, W_C:2 * W_C]
    mu = jnp.mean(v, axis=-1, keepdims=True)
    vc = v - mu
    var = jnp.mean(vc * vc, axis=-1, keepdims=True)
    vn = (vc * lax.rsqrt(var + EPS) * lng_ref[...] + lnb_ref[...]).astype(bf16)
    gw = W_C // C_GROUPS
    for ck in range(TM // C_CHUNK):
        rs = slice(ck * C_CHUNK, (ck + 1) * C_CHUNK)
        for g in range(C_GROUPS):
            cs = slice(g * gw, (g + 1) * gw)
            mixed = jnp.dot(ws_ref[g], vn[rs, cs], preferred_element_type=f32) + bs_ref[g]
            gated[rs, cs] = (u[rs, cs] * mixed).astype(bf16)
    out = jnp.dot(gated[...], wout_ref[...], preferred_element_type=f32)
    x1 = x + mod[:, 2 * D:3 * D] * out
    x1_ref[...] = x1
    _router_tail(i, x1, mod, nrm_ref, wr_ref, br_ref, h2_ref, route_ref, cnt_ref)


def _odd_mixer(x, y, mod_prev, mod_l, nm, w_in, ln_g, ln_b, ws, bs, w_out, nrm, wr, br):
    tile = lambda w: pl.BlockSpec((TM, w), lambda i: (i, 0))
    modspec = pl.BlockSpec((1, 1, N_MOD * D), lambda i: (_mod_row(i), 0, 0))
    const = lambda shp: pl.BlockSpec(shp, lambda i: tuple(0 for _ in shp))
    return pl.pallas_call(
        _odd_kernel,
        grid=(NT,),
        in_specs=[tile(D), pl.BlockSpec((TM * ROW_SUB, 128), lambda i: (i, 0)), modspec, modspec, const((1, D)),
                  const((D, 2 * W_C)), const((1, W_C)),
                  const((1, W_C)), const((C_GROUPS, C_CHUNK, C_CHUNK)), const((C_GROUPS, C_CHUNK, W_C // C_GROUPS)),
                  const((W_C, D))] + _router_in_specs(),
        out_specs=_ROUTER_OUT_SPECS,
        out_shape=_ROUTER_OUT_SHAPES,
        scratch_shapes=[pltpu.VMEM((TM, W_C), bf16)],
        compiler_params=_cp(("arbitrary",)),
        name="odd_mixer_router",
    )(x, y, mod_prev, mod_l, nm, w_in, ln_g, ln_b, ws, bs, w_out, nrm, wr, br)


def _expert_kernel(nu_ref, tg_ref, tlo_ref, thi_ref, gcur_ref, gnext_ref, scur_ref, rs_ref, h2_hbm,
                   wg_ref, wu_ref, wd_ref, y_hbm, xbuf, obuf, sem_g, sem_s):
    i = pl.program_id(0)
    n_used = nu_ref[0]
    slot = i % 2
    nslot = 1 - slot
    tile_rows = TMX * ROW_SUB

    def aligned(x, m):
        return x if isinstance(x, int) else pl.multiple_of(x, m)

    def buf_rows(s, r):
        return pl.ds(aligned((s * TMX + r) * ROW_SUB, ROW_SUB), ROW_SUB)

    def hbm_rows(row_off):
        return pl.ds(aligned(row_off, ROW_SUB), ROW_SUB)

    def row_gather(idx_ref, s, r):
        return pltpu.make_async_copy(h2_hbm.at[hbm_rows(idx_ref[0, 0, r]), :], xbuf.at[buf_rows(s, r), :], sem_g.at[s])

    def row_scatter(dst_off, s, r):
        return pltpu.make_async_copy(obuf.at[buf_rows(s, r), :], y_hbm.at[hbm_rows(dst_off), :], sem_s.at[s])

    def tile_of(buf, s):
        return buf.at[pl.ds(aligned(s * tile_rows, tile_rows), tile_rows), :]

    def tile_gather_done(s):
        return pltpu.make_async_copy(h2_hbm.at[pl.ds(0, tile_rows), :], tile_of(xbuf, s), sem_g.at[s])

    def tile_scatter_done(s):
        return pltpu.make_async_copy(tile_of(obuf, s), y_hbm.at[pl.ds(0, tile_rows), :], sem_s.at[s])

    @pl.when(i == 0)
    def _():
        obuf[...] = jnp.zeros(obuf.shape, f32)
        for r in range(TMX):
            row_gather(gcur_ref, 0, r).start()
        for s in range(2):
            for r in range(TMX):
                row_scatter((N_TOK + s * TMX + r) * ROW_SUB, s, r).start()

    @pl.when(i < n_used)
    def _():
        tile_gather_done(slot).wait()
        tile_scatter_done(slot).wait()

        @pl.when(i + 1 < n_used)
        def _():
            for r in range(TMX):
                row_gather(gnext_ref, nslot, r).start()

        base = pl.multiple_of(slot * tile_rows, tile_rows)
        xb = _load_rows(xbuf, base, TMX).astype(bf16)
        rs = rs_ref[...]
        acc = jnp.zeros((TMX, D), f32)
        for e_ref, wcol in ((tlo_ref, rs[:, 2:3]), (thi_ref, rs[:, 3:4])):
            e = e_ref[i]
            a = jnp.dot(xb, wg_ref[e], preferred_element_type=f32)
            b = jnp.dot(xb, wu_ref[e], preferred_element_type=f32)
            hid = (_silu(a) * b * wcol).astype(bf16)
            acc = acc + jnp.dot(hid, wd_ref[e], preferred_element_type=f32)
        _store_rows(obuf, base, acc)
        for r in range(TMX):
            row_scatter(scur_ref[0, 0, r], slot, r).start()

        @pl.when(i == n_used - 1)
        def _():
            tile_scatter_done(slot).wait()
            tile_scatter_done(nslot).wait()


def _experts(layer, n_used, tile_g, tile_lo, tile_hi, gidx, sidx, route_sorted, h2, wg, wu, wd):
    idx_cur = pl.BlockSpec((1, 1, TMX), lambda i, *_: (i, 0, 0), memory_space=pltpu.SMEM)
    idx_next = pl.BlockSpec((1, 1, TMX), lambda i, *_: (jnp.minimum(i + 1, MAXT - 1), 0, 0), memory_space=pltpu.SMEM)
    wspec = lambda a, b: pl.BlockSpec((None, None, E_PER_GROUP, a, b),
                                      lambda i, nu, tg, tlo, thi: (layer, tg[i], 0, 0, 0))
    grid_spec = pltpu.PrefetchScalarGridSpec(
        num_scalar_prefetch=4,
        grid=(MAXT,),
        in_specs=[
            idx_cur, idx_next, idx_cur,
            pl.BlockSpec((TMX, 128), lambda i, *_: (i, 0)),
            pl.BlockSpec(memory_space=pl.ANY),
            wspec(D, D_EXPERT), wspec(D, D_EXPERT), wspec(D_EXPERT, D),
        ],
        out_specs=pl.BlockSpec(memory_space=pl.ANY),
        scratch_shapes=[
            pltpu.VMEM((2 * TMX * ROW_SUB, 128), f32),
            pltpu.VMEM((2 * TMX * ROW_SUB, 128), f32),
            pltpu.SemaphoreType.DMA((2,)),
            pltpu.SemaphoreType.DMA((2,)),
        ],
    )
    return pl.pallas_call(
        _expert_kernel,
        grid_spec=grid_spec,
        out_shape=jax.ShapeDtypeStruct((Y_ROWS * ROW_SUB, 128), f32),
        compiler_params=_cp(("arbitrary",)),
        name="experts",
    )(n_used, tile_g, tile_lo, tile_hi, gidx, gidx, sidx, route_sorted, h2, wg, wu, wd)


def _dispatch_plan(route, cnt):
    bucket = route[:, 0].astype(i32)
    rank = route[:, 1].astype(i32)
    counts = cnt[0].astype(i32)
    tiles_b = (counts + TMX - 1) // TMX
    tile_end = jnp.cumsum(tiles_b)
    tile_start = tile_end - tiles_b
    buckets = jnp.arange(N_BUCKETS, dtype=i32)
    start_of_token = jnp.sum(jnp.where(bucket[:, None] == buckets[None, :], tile_start[None, :], 0), axis=1)
    slot = start_of_token * TMX + rank
    route_sorted = jnp.zeros((P_ROWS, 128), f32).at[slot].set(route, unique_indices=True)
    tok1 = route_sorted[:, 4].astype(i32)
    valid = tok1 > 0
    prow = jnp.arange(P_ROWS, dtype=i32)
    dummy = N_TOK + ((prow // TMX) % 2) * TMX + prow % TMX
    gidx = (jnp.where(valid, tok1 - 1, 0) * ROW_SUB).reshape(MAXT, 1, TMX)
    sidx = (jnp.where(valid, tok1 - 1, dummy) * ROW_SUB).reshape(MAXT, 1, TMX)
    tiles = jnp.arange(MAXT, dtype=i32)
    tile_bucket = jnp.minimum(jnp.sum((tile_end[None, :] <= tiles[:, None]).astype(i32), axis=1), N_BUCKETS - 1)
    tile_g = tile_bucket // (E_PER_GROUP * E_PER_GROUP)
    tile_lo = (tile_bucket // E_PER_GROUP) % E_PER_GROUP
    tile_hi = tile_bucket % E_PER_GROUP
    return tile_end[N_BUCKETS - 1:], tile_g, tile_lo, tile_hi, gidx, sidx, route_sorted


def _final_kernel(x_ref, y_ref, mod_ref, fn_ref, o_ref):
    mod = mod_ref[0]
    x = x_ref[...] + mod[:, 5 * D:6 * D] * _load_rows(y_ref, 0, TM)
    o_ref[...] = _rms(x, fn_ref[...])


def _final(x, y, mod_l, fnorm, *, n_tiles, tile_off):
    return pl.pallas_call(
        _final_kernel,
        grid=(n_tiles,),
        in_specs=[
            pl.BlockSpec((TM, D), lambda i: (tile_off + i, 0)),
            pl.BlockSpec((TM * ROW_SUB, 128), lambda i: (tile_off + i, 0)),
            pl.BlockSpec((1, 1, N_MOD * D), lambda i: (_mod_row(tile_off + i), 0, 0)),
            pl.BlockSpec((1, D), lambda i: (0, 0)),
        ],
        out_specs=pl.BlockSpec((TM, D), lambda i: (i, 0)),
        out_shape=jax.ShapeDtypeStruct((n_tiles * TM, D), f32),
        compiler_params=_cp(("arbitrary",)),
        name="final_norm",
    )(x, y, mod_l, fnorm)


def _rope_tables():
    rows = DEC_SEQ // GRID_W
    r, c = jnp.meshgrid(jnp.arange(rows), jnp.arange(GRID_W), indexing="ij")
    pos = jnp.stack([r.reshape(-1), c.reshape(-1)], axis=-1).astype(f32)
    inv = ROPE_BASE ** (-jnp.arange(AXIS_FREQS, dtype=f32) / AXIS_FREQS)
    ang = pos[:, :, None] * inv
    cos, sin = jnp.cos(ang), jnp.sin(ang)
    cos_h = jnp.concatenate([cos, cos], axis=-1).reshape(DEC_SEQ, DH_A)
    sin_h = jnp.concatenate([-sin, sin], axis=-1).reshape(DEC_SEQ, DH_A)
    cos_t = jnp.concatenate([jnp.ones((TM, 2 * DH_A), f32), jnp.tile(cos_h, (1, 2))], axis=0)
    sin_t = jnp.concatenate([jnp.zeros((TM, 2 * DH_A), f32), jnp.tile(sin_h, (1, 2))], axis=0)
    return cos_t, sin_t


def _gate_lanes(p):
    out = jnp.zeros((128,), f32)
    for d in range(2):
        out = out.at[d * 2 * H_B:d * 2 * H_B + H_B].set(p[d])
    return out.reshape(1, 128)


def _router_weights(w_rg, b_rg, w_re, b_re):
    wr = jnp.zeros((D, 256), f32).at[:, 0:N_GROUPS].set(w_rg).at[:, 128:128 + N_EXPERTS].set(w_re)
    br = jnp.zeros((1, 256), f32).at[0, 0:N_GROUPS].set(b_rg).at[0, 128:128 + N_EXPERTS].set(b_re)
    return wr, br


def kernel(x_prompt, x_sample, cache_k, cache_v, state_fwd, state_bwd, c, c_ctx, norm_mix, norm_ffn, w_mod, b_mod,
           w_in_even, conv_w, a_log, dt_bias, delta_norm, lam_q, lam_k, subln, w_out_even, w_in_odd, sgu_ln_g,
           sgu_ln_b, w_spatial, b_spatial, w_out_odd, w_router_group, b_router_group, w_router_expert,
           b_router_expert, w_exp_gate, w_exp_up, w_exp_down, final_norm):
    xp = x_prompt.reshape(N_P, D)
    xs = x_sample.reshape(N_S, D)
    cond = jnp.concatenate([c, c_ctx[None, :], jnp.zeros((COND_ROWS - DEC_BATCH - 1, D), f32)], axis=0)
    mod = _modulation(cond, w_mod, b_mod)
    mod0 = mod[0].reshape(COND_ROWS, 1, N_MOD * D)
    mod1 = mod[1].reshape(COND_ROWS, 1, N_MOD * D)

    lam_init = 0.8 - 0.6 * math.exp(-0.3 * 0)
    w_in_pad = jnp.pad(w_in_even[0], ((0, 0), (0, IN_EVEN_PAD - IN_EVEN))).astype(bf16)
    cos_t, sin_t = _rope_tables()
    qa, ka, va, qkvb, gate_b, ab = _even_in(xp, xs, mod0, norm_mix[0:1], w_in_pad, cos_t, sin_t)

    ctx_k = cache_k[:, 0].reshape(DEC_BATCH * PAST, W_A)
    ctx_v = cache_v[:, 0].reshape(DEC_BATCH * PAST, W_A)
    attn = functools.partial(_attention, qa, ka, va, lam_q[0], lam_k[0], subln[0:1], lam_init)
    oa_p = attn(seq=SEQ, tq=SEQ, tok_off=0)
    oa_s = attn(seq=DEC_SEQ, tq=512, tok_off=N_P, ctx=(ctx_k, ctx_v))

    conv_pad = jnp.pad(conv_w[0], ((0, 8 - CONV_K), (0, 0)))
    dl = functools.partial(_delta, qkvb, gate_b, ab, conv_pad, _gate_lanes(a_log[0]), _gate_lanes(dt_bias[0]),
                           delta_norm[0:1])
    ob_p, s_f, s_b = dl(seq=SEQ, tok_off=0)
    ob_s, _, _ = dl(seq=DEC_SEQ, tok_off=N_P, init=(state_fwd[:, 0], state_bwd[:, 0]))

    wr0, br0 = _router_weights(w_router_group[0], b_router_group[0], w_router_expert[0], b_router_expert[0])
    x1, h2, route, cnt = _even_out(xp, xs, oa_p, oa_s, ob_p, ob_s, mod0, w_out_even[0].astype(bf16), norm_ffn[0:1], wr0, br0)

    wshape = (DEPTH, N_GROUPS, E_PER_GROUP)
    wg = w_exp_gate.astype(bf16).reshape(wshape + (D, D_EXPERT))
    wu = w_exp_up.astype(bf16).reshape(wshape + (D, D_EXPERT))
    wd = w_exp_down.astype(bf16).reshape(wshape + (D_EXPERT, D))
    y0 = _experts(0, *_dispatch_plan(route, cnt), h2, wg, wu, wd)

    bs = jnp.broadcast_to(b_spatial[0][:, :, None], (C_GROUPS, C_CHUNK, W_C // C_GROUPS))
    wr1, br1 = _router_weights(w_router_group[1], b_router_group[1], w_router_expert[1], b_router_expert[1])
    x2, h2, route, cnt = _odd_mixer(x1, y0, mod0, mod1, norm_mix[1:2], w_in_odd[0].astype(bf16), sgu_ln_g[0:1],
                                    sgu_ln_b[0:1], w_spatial[0].astype(bf16), bs, w_out_odd[0].astype(bf16),
                                    norm_ffn[1:2], wr1, br1)
    y1 = _experts(1, *_dispatch_plan(route, cnt), h2, wg, wu, wd)

    y_prompt = _final(x2, y1, mod1, final_norm[None, :], n_tiles=NPT, tile_off=0)
    y_sample = _final(x2, y1, mod1, final_norm[None, :], n_tiles=NT - NPT, tile_off=NPT)

    new_cache_k = ka[:N_P].reshape(BATCH, 1, SEQ, H_A, 2 * DH_A)
    new_cache_v = va[:N_P].reshape(BATCH, 1, SEQ, H_A, 2 * DH_A)
    return (y_prompt.reshape(BATCH, SEQ, D), y_sample.reshape(DEC_BATCH, DEC_SEQ, D), new_cache_k, new_cache_v,
            s_f[:, None], s_b[:, None])
```

```python
import functools
import math

import jax
import jax.numpy as jnp
from jax import lax
from jax.experimental import pallas as pl
from jax.experimental.pallas import tpu as pltpu

f32 = jnp.float32
bf16 = jnp.bfloat16
i32 = jnp.int32
HIGHEST = lax.Precision.HIGHEST

D = 1024
BATCH, SEQ = 16, 256
DEC_BATCH, DEC_SEQ, PAST = 8, 1024, 512
DEPTH = 2
GRID_W = 64
N_MOD = 6
EPS = 1e-6
H_A, DH_A = 4, 64
W_A = H_A * 2 * DH_A
ROPE_BASE = 10000.0
AXIS_FREQS = DH_A // 4
H_B, DK_B, DV_B = 4, 128, 128
W_B = H_B * DV_B
CONV_K = 5
CH = 64
IN_EVEN = 3 * W_A + 4 * W_B + 4 * H_B
IN_EVEN_PAD = 3712
W_C = D
C_GROUPS, C_CHUNK = 4, 128
N_GROUPS, E_PER_GROUP, D_EXPERT = 4, 8, 256
N_EXPERTS = N_GROUPS * E_PER_GROUP

N_P = BATCH * SEQ
N_S = DEC_BATCH * DEC_SEQ
N_TOK = N_P + N_S
TM = 256
NPT = N_P // TM
NT = N_TOK // TM
COND_ROWS = 16

VMEM_LIMIT = 56 * 1024 * 1024


def _cp(sem):
    return pltpu.CompilerParams(dimension_semantics=sem, vmem_limit_bytes=VMEM_LIMIT)


def _mod_row(i):
    return jnp.where(i < NPT, DEC_BATCH, (i - NPT) // (DEC_SEQ // TM))


def _rms(x, g):
    return x * lax.rsqrt(jnp.mean(x * x, axis=-1, keepdims=True) + EPS) * g


def _silu(x):
    return x * jax.nn.sigmoid(x)


def _dot(a, b):
    return jnp.dot(a.astype(bf16), b.astype(bf16), preferred_element_type=f32)


ROW_SUB = D // 128


def _store_rows(ref, base, x):
    rows = x.shape[0]
    for j in range(ROW_SUB):
        ref[pl.ds(base + j, rows, stride=ROW_SUB), :] = x[:, j * 128:(j + 1) * 128]


def _load_rows(ref, base, rows):
    return jnp.concatenate([ref[pl.ds(base + j, rows, stride=ROW_SUB), :] for j in range(ROW_SUB)], axis=1)


TN_MOD = 1536


def _mod_kernel(c_ref, w_ref, b_ref, o_ref):
    a = _silu(c_ref[...])
    o_ref[0] = jnp.dot(a, w_ref[0], precision=HIGHEST, preferred_element_type=f32) + b_ref[0]


def _modulation(cond, w_mod, b_mod):
    return pl.pallas_call(
        _mod_kernel,
        grid=(DEPTH, N_MOD * D // TN_MOD),
        in_specs=[
            pl.BlockSpec((COND_ROWS, D), lambda l, j: (0, 0)),
            pl.BlockSpec((1, D, TN_MOD), lambda l, j: (l, 0, j)),
            pl.BlockSpec((1, 1, TN_MOD), lambda l, j: (l, 0, j)),
        ],
        out_specs=pl.BlockSpec((1, COND_ROWS, TN_MOD), lambda l, j: (l, 0, j)),
        out_shape=jax.ShapeDtypeStruct((DEPTH, COND_ROWS, N_MOD * D), f32),
        compiler_params=_cp(("arbitrary", "arbitrary")),
        name="modulation",
    )(cond, w_mod, b_mod.reshape(DEPTH, 1, N_MOD * D))


def _even_in_kernel(xp_ref, xs_ref, mod_ref, nrm_ref, w_ref, cos_ref, sin_ref,
                    q_ref, k_ref, v_ref, qkvb_ref, gate_ref, ab_ref):
    i = pl.program_id(0)
    x = jnp.where(i < NPT, xp_ref[...], xs_ref[...])
    mod = mod_ref[0]
    h = _rms(x, nrm_ref[...]) * (1.0 + mod[:, D:2 * D]) + mod[:, 0:D]
    proj = _dot(h, w_ref[...])
    cos = cos_ref[...]
    sin = sin_ref[...]
    lane = lax.broadcasted_iota(i32, (TM, 2 * DH_A), 1)
    first = (lane % (2 * AXIS_FREQS)) < AXIS_FREQS

    def rope(xh):
        partner = jnp.where(first, pltpu.roll(xh, 2 * DH_A - AXIS_FREQS, 1), pltpu.roll(xh, AXIS_FREQS, 1))
        return xh * cos + partner * sin

    for hh in range(H_A):
        sl = slice(hh * 2 * DH_A, (hh + 1) * 2 * DH_A)
        q_ref[:, sl] = (rope(proj[:, sl]) * (DH_A ** -0.5)).astype(bf16)
        k_ref[:, sl] = rope(proj[:, W_A + hh * 2 * DH_A:W_A + (hh + 1) * 2 * DH_A])
    v_ref[...] = proj[:, 2 * W_A:3 * W_A]
    qkvb_ref[...] = proj[:, 3 * W_A:3 * W_A + 3 * W_B]
    gate_ref[...] = proj[:, 3 * W_A + 3 * W_B:3 * W_A + 4 * W_B]
    ab_ref[...] = proj[:, 3 * W_A + 4 * W_B:IN_EVEN_PAD]


def _even_in(xp, xs, mod_l, nrm, w_in_pad, cos_t, sin_t):
    tile = lambda w: pl.BlockSpec((TM, w), lambda i: (i, 0))
    return pl.pallas_call(
        _even_in_kernel,
        grid=(NT,),
        in_specs=[
            pl.BlockSpec((TM, D), lambda i: (jnp.minimum(i, NPT - 1), 0)),
            pl.BlockSpec((TM, D), lambda i: (jnp.maximum(i - NPT, 0), 0)),
            pl.BlockSpec((1, 1, N_MOD * D), lambda i: (_mod_row(i), 0, 0)),
            pl.BlockSpec((1, D), lambda i: (0, 0)),
            pl.BlockSpec((D, IN_EVEN_PAD), lambda i: (0, 0)),
            pl.BlockSpec((TM, 2 * DH_A), lambda i: (jnp.where(i < NPT, 0, 1 + (i - NPT) % (DEC_SEQ // TM)), 0)),
            pl.BlockSpec((TM, 2 * DH_A), lambda i: (jnp.where(i < NPT, 0, 1 + (i - NPT) % (DEC_SEQ // TM)), 0)),
        ],
        out_specs=[tile(W_A), tile(W_A), tile(W_A), tile(3 * W_B), tile(W_B), tile(128)],
        out_shape=[
            jax.ShapeDtypeStruct((N_TOK, W_A), bf16),
            jax.ShapeDtypeStruct((N_TOK, W_A), f32),
            jax.ShapeDtypeStruct((N_TOK, W_A), f32),
            jax.ShapeDtypeStruct((N_TOK, 3 * W_B), f32),
            jax.ShapeDtypeStruct((N_TOK, W_B), f32),
            jax.ShapeDtypeStruct((N_TOK, 128), f32),
        ],
        compiler_params=_cp(("arbitrary",)),
        name="even_in",
    )(xp, xs, mod_l, nrm, w_in_pad, cos_t, sin_t)


def _attn_kernel(*refs, lam_init, has_ctx):
    if has_ctx:
        q_ref, kn_ref, vn_ref, kc_ref, vc_ref, lq_ref, lk_ref, sub_ref, o_ref = refs
    else:
        q_ref, kn_ref, vn_ref, lq_ref, lk_ref, sub_ref, o_ref = refs
    tq = q_ref.shape[0]
    lq = lq_ref[...]
    lk = lk_ref[...]
    prod = lq * lk
    lam = (jnp.exp(jnp.sum(prod[0:1], axis=-1, keepdims=True))
           - jnp.exp(jnp.sum(prod[1:2], axis=-1, keepdims=True)) + lam_init)
    lane = lax.broadcasted_iota(i32, (tq, 2 * DH_A), 1)
    nt = (((1,), (1,)), ((), ()))
    for hh in range(H_A):
        sl = slice(hh * 2 * DH_A, (hh + 1) * 2 * DH_A)
        q = q_ref[:, sl]
        srcs = [(kn_ref[:, sl].astype(bf16), vn_ref[:, sl].astype(bf16))]
        if has_ctx:
            srcs.append((kc_ref[:, sl].astype(bf16), vc_ref[:, sl].astype(bf16)))
        outs = []
        for part in range(2):
            qm = jnp.where((lane < DH_A) if part == 0 else (lane >= DH_A), q, jnp.zeros_like(q))
            ss = [lax.dot_general(qm, k, nt, preferred_element_type=f32) for k, _ in srcs]
            m = ss[0].max(axis=-1, keepdims=True)
            for s in ss[1:]:
                m = jnp.maximum(m, s.max(axis=-1, keepdims=True))
            l = jnp.zeros((tq, 1), f32)
            o = jnp.zeros((tq, 2 * DH_A), f32)
            for s, (_, v) in zip(ss, srcs):
                e = jnp.exp(s - m)
                l = l + jnp.sum(e, axis=-1, keepdims=True)
                o = o + jnp.dot(e.astype(bf16), v, preferred_element_type=f32)
            outs.append(o / l)
        o = outs[0] - lam * outs[1]
        o_ref[:, sl] = (_rms(o, sub_ref[...]) * (1.0 - lam_init)).astype(bf16)


def _attention(q, ka, va, lam_q, lam_k, subln, lam_init, *, seq, tq, tok_off, ctx=None):
    nq = seq // tq
    nb = (N_P if ctx is None else N_S) // seq
    qo, ko = tok_off // tq, tok_off // seq
    in_specs = [
        pl.BlockSpec((tq, W_A), lambda b, j: (qo + b * nq + j, 0)),
        pl.BlockSpec((seq, W_A), lambda b, j: (ko + b, 0)),
        pl.BlockSpec((seq, W_A), lambda b, j: (ko + b, 0)),
    ]
    args = [q, ka, va]
    if ctx is not None:
        in_specs += [pl.BlockSpec((PAST, W_A), lambda b, j: (b, 0))] * 2
        args += list(ctx)
    in_specs += [pl.BlockSpec((2, DH_A), lambda b, j: (0, 0))] * 2 + [pl.BlockSpec((1, 2 * DH_A), lambda b, j: (0, 0))]
    args += [lam_q, lam_k, subln]
    return pl.pallas_call(
        functools.partial(_attn_kernel, lam_init=lam_init, has_ctx=ctx is not None),
        grid=(nb, nq),
        in_specs=in_specs,
        out_specs=pl.BlockSpec((tq, W_A), lambda b, j: (b * nq + j, 0)),
        out_shape=jax.ShapeDtypeStruct((nb * seq, W_A), bf16),
        compiler_params=_cp(("arbitrary", "arbitrary")),
        name="diff_attn_ctx" if ctx is not None else "diff_attn",
    )(*args)


def _softplus(x):
    return jnp.maximum(x, 0.0) + jnp.log1p(jnp.exp(-jnp.abs(x)))


def _tri_inv(lmats, row, col):
    b16 = (row // 16) == (col // 16)
    b32 = (row // 32) == (col // 32)
    eye = (row == col).astype(f32)
    n = len(lmats)
    l0 = [jnp.where(b16, l, 0.0) for l in lmats]
    x = [eye - a for a in l0]
    p = [_dot(a, a) for a in l0]
    for level in range(3):
        xp = [_dot(x[i], p[i]) for i in range(n)]
        if level < 2:
            p = [_dot(a, a) for a in p]
        x = [x[i] + xp[i] for i in range(n)]
    for off_diag in (jnp.logical_and(b32, jnp.logical_not(b16)), jnp.logical_not(b32)):
        t = [_dot(jnp.where(off_diag, lmats[i], 0.0), x[i]) for i in range(n)]
        t = [_dot(x[i], t[i]) for i in range(n)]
        x = [x[i] - t[i] for i in range(n)]
    return x


def _delta_kernel(*refs, seq, has_init):
    if has_init:
        (qkv_ref, gate_ref, ab_ref, cw_ref, al_ref, dt_ref, dn_ref, sf0_ref, sb0_ref,
         ob_ref, sf_ref, sb_ref, xpad, xq, xk, xv, gacc, bacc, o_f, o_b, st) = refs
    else:
        (qkv_ref, gate_ref, ab_ref, cw_ref, al_ref, dt_ref, dn_ref,
         ob_ref, sf_ref, sb_ref, xpad, xq, xk, xv, gacc, bacc, o_f, o_b, st) = refs
    n = seq // CH
    win = CH + 16

    xpad[0:8, :] = jnp.zeros((8, 3 * W_B), f32)
    xpad[seq + 8:seq + 16, :] = jnp.zeros((8, 3 * W_B), f32)
    xpad[8:seq + 8, :] = qkv_ref[...]
    neg_a = -jnp.exp(al_ref[...])
    dtb = dt_ref[...]

    def pre(c, carry):
        r0 = pl.multiple_of(c * CH, CH)
        a = xpad[pl.ds(r0, win), :]
        y = jnp.zeros((CH, 3 * W_B), f32)
        for j in range(CONV_K):
            sh = ((CONV_K - 1) // 2 - j) % win
            y = y + pltpu.roll(a, sh, 0)[8:8 + CH] * cw_ref[j:j + 1, :]
        y = _silu(y)
        for hh in range(H_B):
            qh = y[:, hh * DK_B:(hh + 1) * DK_B]
            kh = y[:, W_B + hh * DK_B:W_B + (hh + 1) * DK_B]
            qn = qh * lax.rsqrt(jnp.sum(qh * qh, axis=-1, keepdims=True) + EPS) * (DK_B ** -0.5)
            kn = kh * lax.rsqrt(jnp.sum(kh * kh, axis=-1, keepdims=True) + EPS)
            xq[pl.ds(r0, CH), hh * DK_B:(hh + 1) * DK_B] = qn
            xk[pl.ds(r0, CH), hh * DK_B:(hh + 1) * DK_B] = kn
        xv[pl.ds(r0, CH), :] = y[:, 2 * W_B:3 * W_B]
        ab = ab_ref[pl.ds(r0, CH), :]
        gacc[pl.ds(r0, CH), :] = neg_a * _softplus(ab + dtb)
        bacc[pl.ds(r0, CH), :] = jax.nn.sigmoid(ab)
        return carry

    lax.fori_loop(0, n, pre, 0)

    for hh in range(H_B):
        if has_init:
            st[hh] = sf0_ref[0, hh]
            st[H_B + hh] = sb0_ref[0, hh]
        else:
            st[hh] = jnp.zeros((DK_B, DV_B), f32)
            st[H_B + hh] = jnp.zeros((DK_B, DV_B), f32)

    row = lax.broadcasted_iota(i32, (CH, CH), 0)
    col = lax.broadcasted_iota(i32, (CH, CH), 1)
    rowl = lax.broadcasted_iota(i32, (CH, 128), 0)

    def step(it, carry):
        chains = []
        for d in range(2):
            c = it if d == 0 else n - 1 - it
            r0 = pl.multiple_of(c * CH, CH)
            g = gacc[pl.ds(r0, CH), :]
            beta = bacc[pl.ds(r0, CH), :]
            gc = g
            for s in (1, 2, 4, 8, 16, 32):
                gc = gc + jnp.where(rowl >= s, pltpu.roll(gc, s, 0), 0.0)
            tot = gc[CH - 1:CH, :]
            if d == 1:
                gc = tot - gc + g
            gct = gc.T
            incl = (row >= col) if d == 0 else (row <= col)
            strict = (row > col) if d == 0 else (row < col)
            for hh in range(H_B):
                ln = d * 2 * H_B + hh
                gcol = gc[:, ln:ln + 1]
                grow = gct[ln:ln + 1, :]
                k = xk[pl.ds(r0, CH), hh * DK_B:(hh + 1) * DK_B]
                bcol = beta[:, ln + H_B:ln + H_B + 1]
                chains.append(dict(
                    d=d, hh=hh, r0=r0, incl=incl, strict=strict, grow=grow, bcol=bcol, tcol=tot[:, ln:ln + 1],
                    q=xq[pl.ds(r0, CH), hh * DK_B:(hh + 1) * DK_B], kt=k.T, kb=k * bcol,
                    v=xv[pl.ds(r0, CH), hh * DV_B:(hh + 1) * DV_B], eg=jnp.exp(gcol),
                    dec=jnp.where(incl, jnp.exp(jnp.where(incl, gcol - grow, 0.0)), 0.0)))
        kk = [_dot(e["kb"], e["kt"]) for e in chains]
        qk = [_dot(e["q"], e["kt"]) for e in chains]
        lmat = [jnp.where(e["strict"], kk[i] * e["dec"], 0.0) for i, e in enumerate(chains)]
        att = [jnp.where(e["incl"], qk[i] * e["dec"], 0.0) for i, e in enumerate(chains)]
        tinv = _tri_inv(lmat, row, col)
        u = [_dot(tinv[i], e["v"] * e["bcol"]) for i, e in enumerate(chains)]
        w = [_dot(tinv[i], e["kb"] * e["eg"]) for i, e in enumerate(chains)]
        s_old = [st[e["d"] * H_B + e["hh"]] for e in chains]
        ws = [_dot(w[i], s_old[i]) for i in range(len(chains))]
        qs = [_dot(e["q"] * e["eg"], s_old[i]) for i, e in enumerate(chains)]
        v_new = [u[i] - ws[i] for i in range(len(chains))]
        av = [_dot(att[i], v_new[i]) for i in range(len(chains))]
        kv = [_dot(e["kt"] * jnp.exp(e["tcol"] - e["grow"]), v_new[i]) for i, e in enumerate(chains)]
        for i, e in enumerate(chains):
            st[e["d"] * H_B + e["hh"]] = s_old[i] * jnp.exp(e["tcol"]) + kv[i]
            cols = slice(e["hh"] * DV_B, (e["hh"] + 1) * DV_B)
            if e["d"] == 0:
                o_f[pl.ds(e["r0"], CH), cols] = qs[i] + av[i]
            else:
                o_b[pl.ds(e["r0"], CH), cols] = qs[i] + av[i]
        return carry

    lax.fori_loop(0, n, step, 0)

    def post(c, carry):
        r0 = pl.multiple_of(c * CH, CH)
        o = o_f[pl.ds(r0, CH), :] + o_b[pl.ds(r0, CH), :]
        gt = gate_ref[pl.ds(r0, CH), :]
        for hh in range(H_B):
            sl = slice(hh * DV_B, (hh + 1) * DV_B)
            ob_ref[pl.ds(r0, CH), sl] = (_rms(o[:, sl], dn_ref[...]) * _silu(gt[:, sl])).astype(bf16)
        return carry

    lax.fori_loop(0, n, post, 0)
    for hh in range(H_B):
        sf_ref[0, hh] = st[hh]
        sb_ref[0, hh] = st[H_B + hh]


def _delta(qkvb, gate, ab, conv_pad, al_lane, dt_lane, dnorm, *, seq, tok_off, init=None):
    nb = (N_P if init is None else N_S) // seq
    bo = tok_off // seq
    rows = lambda w: pl.BlockSpec((seq, w), lambda b: (bo + b, 0))
    const = lambda shp: pl.BlockSpec(shp, lambda b: tuple(0 for _ in shp))
    st_spec = pl.BlockSpec((1, H_B, DK_B, DV_B), lambda b: (b, 0, 0, 0))
    in_specs = [rows(3 * W_B), rows(W_B), rows(128), const((8, 3 * W_B)), const((1, 128)), const((1, 128)),
                const((1, DV_B))]
    args = [qkvb, gate, ab, conv_pad, al_lane, dt_lane, dnorm]
    if init is not None:
        in_specs += [st_spec, st_spec]
        args += [init[0], init[1]]
    return pl.pallas_call(
        functools.partial(_delta_kernel, seq=seq, has_init=init is not None),
        grid=(nb,),
        in_specs=in_specs,
        out_specs=[pl.BlockSpec((seq, W_B), lambda b: (b, 0)), st_spec, st_spec],
        out_shape=[
            jax.ShapeDtypeStruct((nb * seq, W_B), bf16),
            jax.ShapeDtypeStruct((nb, H_B, DK_B, DV_B), f32),
            jax.ShapeDtypeStruct((nb, H_B, DK_B, DV_B), f32),
        ],
        scratch_shapes=[
            pltpu.VMEM((seq + 16, 3 * W_B), f32),
            pltpu.VMEM((seq, W_B), f32),
            pltpu.VMEM((seq, W_B), f32),
            pltpu.VMEM((seq, W_B), f32),
            pltpu.VMEM((seq, 128), f32),
            pltpu.VMEM((seq, 128), f32),
            pltpu.VMEM((seq, W_B), f32),
            pltpu.VMEM((seq, W_B), f32),
            pltpu.VMEM((2 * H_B, DK_B, DV_B), f32),
        ],
        compiler_params=_cp(("arbitrary",)),
        name="delta_ctx" if init is not None else "delta",
    )(*args)


RG = N_TOK
TMG = 256
MAXT_G = N_TOK // TMG + N_GROUPS
TILE_ROWS = TM * ROW_SUB
RUN_BITS = tuple(range(TM.bit_length() - 1, -1, -1))


def _hi_lo(x):
    hi = x.astype(bf16)
    return hi, (x - hi.astype(f32)).astype(bf16)


def _run_pieces(n):
    return [(1 << b, (n >> (b + 1)) << (b + 1), ((n >> b) & 1) == 1) for b in RUN_BITS]


def _route_and_dispatch(i, x1, mod, nrm_ref, wr_ref, br_ref, route_ref, cnt_ref, hs_hbm, gs_hbm,
                        hbuf, gbuf, run, sem):
    h2 = _rms(x1, nrm_ref[...]) * (1.0 + mod[:, 4 * D:5 * D]) + mod[:, 3 * D:4 * D]
    h_hi, h_lo = _hi_lo(h2)
    w_hi, w_lo = _hi_lo(wr_ref[...])
    logits = (jnp.dot(h_hi, w_hi, preferred_element_type=f32) + jnp.dot(h_hi, w_lo, preferred_element_type=f32)
              + jnp.dot(h_lo, w_hi, preferred_element_type=f32)) + br_ref[...]
    lane = lax.broadcasted_iota(i32, (TM, 128), 1)
    lane_f = lane.astype(f32)
    neg = jnp.float32(-1e30)
    lg = jnp.where(lane < N_GROUPS, logits[:, 0:128], neg)
    mg = lg.max(axis=-1, keepdims=True)
    gidx = jnp.min(jnp.where(lg == mg, lane_f, 128.0), axis=-1, keepdims=True)
    pg = 1.0 / jnp.sum(jnp.exp(lg - mg), axis=-1, keepdims=True)
    le = jnp.where((lane // E_PER_GROUP).astype(f32) == gidx, logits[:, 128:256], neg)
    m1 = le.max(axis=-1, keepdims=True)
    i1 = jnp.min(jnp.where(le == m1, lane_f, 128.0), axis=-1, keepdims=True)
    le2 = jnp.where(lane_f == i1, neg, le)
    m2 = le2.max(axis=-1, keepdims=True)
    i2 = jnp.min(jnp.where(le2 == m2, lane_f, 128.0), axis=-1, keepdims=True)
    e2 = jnp.exp(m2 - m1)
    w1 = pg / (1.0 + e2)
    w2 = w1 * e2
    local = lane_f + gidx * E_PER_GROUP
    gates = jnp.where(local == i1, w1, jnp.where(local == i2, w2, 0.0))

    onehot = (lane_f == gidx).astype(f32)
    r = lax.broadcasted_iota(i32, (TM, TM), 0)
    c = lax.broadcasted_iota(i32, (TM, TM), 1)
    before = jnp.dot((r > c).astype(bf16), onehot.astype(bf16), preferred_element_type=f32)
    cnt = jnp.sum(onehot, axis=0, keepdims=True)
    pos = (jnp.sum(onehot * before, axis=-1, keepdims=True)
           + jnp.sum(jnp.where(lane_f < gidx, cnt, 0.0), axis=-1, keepdims=True))
    route_ref[...] = jnp.where(lane == 0, pos, 0.0)
    cnt_ref[0] = cnt

    perm_t = (c.astype(f32) == pos).astype(bf16)
    tn = (((0,), (0,)), ((), ()))
    h_sorted = lax.dot_general(perm_t, h_hi, tn, preferred_element_type=f32)
    g_hi, g_lo = _hi_lo(gates)
    g_sorted = (lax.dot_general(perm_t, g_hi, tn, preferred_element_type=f32)
                + lax.dot_general(perm_t, g_lo, tn, preferred_element_type=f32))

    slot = i % 2
    hbase = pl.multiple_of(slot * TILE_ROWS, TILE_ROWS)
    gbase = pl.multiple_of(slot * TM, TM)

    def tile_done(s):
        hb = pl.multiple_of(s * TILE_ROWS, TILE_ROWS)
        gb = pl.multiple_of(s * TM, TM)
        pltpu.make_async_copy(hbuf.at[pl.ds(hb, TILE_ROWS), :], hs_hbm.at[pl.ds(0, TILE_ROWS), :], sem.at[s]).wait()
        pltpu.make_async_copy(gbuf.at[pl.ds(gb, TM), :], gs_hbm.at[pl.ds(0, TM), :], sem.at[s]).wait()

    @pl.when(i == 0)
    def _():
        for g in range(N_GROUPS):
            run[g] = 0

    @pl.when(i >= 2)
    def _():
        tile_done(slot)

    _store_rows(hbuf, hbase, h_sorted)
    gbuf[pl.ds(gbase, TM), :] = g_sorted
    cnt_i = cnt.astype(i32)
    off = 0
    for g in range(N_GROUPS):
        n = cnt_i[0, g]
        dst = g * RG + run[g]
        for rows, o, pred in _run_pieces(n):
            @pl.when(pred)
            def _():
                pltpu.make_async_copy(
                    hbuf.at[pl.ds(pl.multiple_of(hbase + (off + o) * ROW_SUB, ROW_SUB), rows * ROW_SUB), :],
                    hs_hbm.at[pl.ds(pl.multiple_of((dst + o) * ROW_SUB, ROW_SUB), rows * ROW_SUB), :],
                    sem.at[slot]).start()
                pltpu.make_async_copy(gbuf.at[pl.ds(gbase + off + o, rows), :], gs_hbm.at[pl.ds(dst + o, rows), :],
                                      sem.at[slot]).start()
        run[g] = run[g] + n
        off = off + n

    @pl.when(i == pl.num_programs(0) - 1)
    def _():
        tile_done(slot)
        tile_done(1 - slot)
        hbuf[pl.ds(0, TMG * ROW_SUB), :] = jnp.zeros((TMG * ROW_SUB, 128), f32)
        gbuf[pl.ds(0, TMG), :] = jnp.zeros((TMG, 128), f32)
        for g in range(N_GROUPS):
            tot = run[g]
            pad = (-tot) & (TMG - 1)
            for rows, o, pred in _run_pieces(pad):
                @pl.when(pred)
                def _():
                    dst = g * RG + tot + o
                    ch = pltpu.make_async_copy(
                        hbuf.at[pl.ds(0, rows * ROW_SUB), :],
                        hs_hbm.at[pl.ds(pl.multiple_of(dst * ROW_SUB, ROW_SUB), rows * ROW_SUB), :], sem.at[0])
                    cg = pltpu.make_async_copy(gbuf.at[pl.ds(0, rows), :], gs_hbm.at[pl.ds(dst, rows), :], sem.at[1])
                    ch.start()
                    cg.start()
                    ch.wait()
                    cg.wait()


def _dispatch_out_specs():
    return [
        pl.BlockSpec((TM, D), lambda i, *_: (i, 0)),
        pl.BlockSpec((TM, 128), lambda i, *_: (i, 0)),
        pl.BlockSpec((1, 1, 128), lambda i, *_: (i, 0, 0)),
        pl.BlockSpec(memory_space=pl.ANY),
        pl.BlockSpec(memory_space=pl.ANY),
    ]


_DISPATCH_OUT_SHAPES = [
    jax.ShapeDtypeStruct((N_TOK, D), f32),
    jax.ShapeDtypeStruct((N_TOK, 128), f32),
    jax.ShapeDtypeStruct((NT, 1, 128), f32),
    jax.ShapeDtypeStruct((N_GROUPS * RG * ROW_SUB, 128), f32),
    jax.ShapeDtypeStruct((N_GROUPS * RG, 128), f32),
]
_DISPATCH_SCRATCH = [
    pltpu.VMEM((2 * TILE_ROWS, 128), f32),
    pltpu.VMEM((2 * TM, 128), f32),
    pltpu.SMEM((N_GROUPS,), i32),
    pltpu.SemaphoreType.DMA((2,)),
]


def _router_in_specs():
    return [pl.BlockSpec((1, D), lambda i, *_: (0, 0)),
            pl.BlockSpec((D, 256), lambda i, *_: (0, 0)),
            pl.BlockSpec((1, 256), lambda i, *_: (0, 0))]


def _fetch_runs(n_ref, s_ref, y_hbm, ybuf, sem, tile, slot):
    base = slot * TILE_ROWS
    off = 0
    for g in range(N_GROUPS):
        n = n_ref[tile * N_GROUPS + g]
        src = s_ref[tile * N_GROUPS + g]
        for rows, o, pred in _run_pieces(n):
            @pl.when(pred)
            def _():
                pltpu.make_async_copy(
                    y_hbm.at[pl.ds(pl.multiple_of((src + o) * ROW_SUB, ROW_SUB), rows * ROW_SUB), :],
                    ybuf.at[pl.ds(pl.multiple_of(base + (off + o) * ROW_SUB, ROW_SUB), rows * ROW_SUB), :],
                    sem.at[slot]).start()
        off = off + n


def _combine_tile(i, tile, n_tiles, n_ref, s_ref, route_ref, y_hbm, ybuf, sem):
    slot = i % 2

    @pl.when(i == 0)
    def _():
        _fetch_runs(n_ref, s_ref, y_hbm, ybuf, sem, tile, 0)

    @pl.when(i + 1 < n_tiles)
    def _():
        _fetch_runs(n_ref, s_ref, y_hbm, ybuf, sem, tile + 1, 1 - slot)

    base = pl.multiple_of(slot * TILE_ROWS, TILE_ROWS)
    pltpu.make_async_copy(y_hbm.at[pl.ds(0, TILE_ROWS), :], ybuf.at[pl.ds(base, TILE_ROWS), :], sem.at[slot]).wait()
    y_hi, y_lo = _hi_lo(_load_rows(ybuf, base, TM))
    pos = route_ref[:, 0:1]
    perm_t = (lax.broadcasted_iota(i32, (TM, TM), 1).astype(f32) == pos).astype(bf16)
    return jnp.dot(perm_t, y_hi, preferred_element_type=f32) + jnp.dot(perm_t, y_lo, preferred_element_type=f32)


_COMBINE_SCRATCH = [pltpu.VMEM((2 * TILE_ROWS, 128), f32), pltpu.SemaphoreType.DMA((2,))]


def _even_out_kernel(xp_ref, xs_ref, oap_ref, oas_ref, obp_ref, obs_ref, mod_ref, w_ref, nrm_ref, wr_ref, br_ref,
                     x1_ref, route_ref, cnt_ref, hs_hbm, gs_hbm, hbuf, gbuf, run, sem):
    i = pl.program_id(0)
    is_p = i < NPT
    x = jnp.where(is_p, xp_ref[...], xs_ref[...])
    oa = jnp.where(is_p, oap_ref[...], oas_ref[...])
    ob = jnp.where(is_p, obp_ref[...], obs_ref[...])
    mod = mod_ref[0]
    out = (jnp.dot(oa, w_ref[0:W_A, :], preferred_element_type=f32)
           + jnp.dot(ob, w_ref[W_A:W_A + W_B, :], preferred_element_type=f32))
    x1 = x + mod[:, 2 * D:3 * D] * out
    x1_ref[...] = x1
    _route_and_dispatch(i, x1, mod, nrm_ref, wr_ref, br_ref, route_ref, cnt_ref, hs_hbm, gs_hbm, hbuf, gbuf, run, sem)


def _even_out(xp, xs, oa_p, oa_s, ob_p, ob_s, mod_l, w_out, nrm, wr, br):
    ptile = lambda w: pl.BlockSpec((TM, w), lambda i: (jnp.minimum(i, NPT - 1), 0))
    stile = lambda w: pl.BlockSpec((TM, w), lambda i: (jnp.maximum(i - NPT, 0), 0))
    return pl.pallas_call(
        _even_out_kernel,
        grid=(NT,),
        in_specs=[
            ptile(D), stile(D), ptile(W_A), stile(W_A), ptile(W_B), stile(W_B),
            pl.BlockSpec((1, 1, N_MOD * D), lambda i: (_mod_row(i), 0, 0)),
            pl.BlockSpec((W_A + W_B, D), lambda i: (0, 0)),
        ] + _router_in_specs(),
        out_specs=_dispatch_out_specs(),
        out_shape=_DISPATCH_OUT_SHAPES,
        scratch_shapes=_DISPATCH_SCRATCH,
        compiler_params=_cp(("arbitrary",)),
        name="even_out_router",
    )(xp, xs, oa_p, oa_s, ob_p, ob_s, mod_l, w_out, nrm, wr, br)


def _gelu_tanh(x):
    return 0.5 * x * (1.0 + jnp.tanh(math.sqrt(2.0 / math.pi) * (x + 0.044715 * (x * x * x))))


def _odd_kernel(n_ref, s_ref, x_ref, proute_ref, y_hbm, modp_ref, mod_ref, nm_ref, win_ref, lng_ref, lnb_ref,
                ws_ref, bs_ref, wout_ref, nrm_ref, wr_ref, br_ref,
                x1_ref, route_ref, cnt_ref, hs_hbm, gs_hbm, gated, ybuf, ysem, hbuf, gbuf, run, sem):
    i = pl.program_id(0)
    modp = modp_ref[0]
    mod = mod_ref[0]
    y = _combine_tile(i, i, NT, n_ref, s_ref, proute_ref, y_hbm, ybuf, ysem)
    x = x_ref[...] + modp[:, 5 * D:6 * D] * y
    h = _rms(x, nm_ref[...]) * (1.0 + mod[:, D:2 * D]) + mod[:, 0:D]
    z = _gelu_tanh(_dot(h, win_ref[...]))
    u = z[:, 0:W_C]
    v = z[:, W_C:2 * W_C]
    mu = jnp.mean(v, axis=-1, keepdims=True)
    vc = v - mu
    var = jnp.mean(vc * vc, axis=-1, keepdims=True)
    vn = (vc * lax.rsqrt(var + EPS) * lng_ref[...] + lnb_ref[...]).astype(bf16)
    gw = W_C // C_GROUPS
    for ck in range(TM // C_CHUNK):
        rs = slice(ck * C_CHUNK, (ck + 1) * C_CHUNK)
        for g in range(C_GROUPS):
            cs = slice(g * gw, (g + 1) * gw)
            mixed = jnp.dot(ws_ref[g], vn[rs, cs], preferred_element_type=f32) + bs_ref[g]
            gated[rs, cs] = (u[rs, cs] * mixed).astype(bf16)
    out = jnp.dot(gated[...], wout_ref[...], preferred_element_type=f32)
    x1 = x + mod[:, 2 * D:3 * D] * out
    x1_ref[...] = x1
    _route_and_dispatch(i, x1, mod, nrm_ref, wr_ref, br_ref, route_ref, cnt_ref, hs_hbm, gs_hbm, hbuf, gbuf, run, sem)


def _odd_mixer(run_n, run_src, x, prev_route, y, mod_prev, mod_l, nm, w_in, ln_g, ln_b, ws, bs, w_out, nrm, wr, br):
    tile = lambda w: pl.BlockSpec((TM, w), lambda i, *_: (i, 0))
    modspec = pl.BlockSpec((1, 1, N_MOD * D), lambda i, *_: (_mod_row(i), 0, 0))
    const = lambda shp: pl.BlockSpec(shp, lambda i, *_: tuple(0 for _ in shp))
    grid_spec = pltpu.PrefetchScalarGridSpec(
        num_scalar_prefetch=2,
        grid=(NT,),
        in_specs=[tile(D), tile(128), pl.BlockSpec(memory_space=pl.ANY), modspec, modspec, const((1, D)),
                  const((D, 2 * W_C)), const((1, W_C)), const((1, W_C)), const((C_GROUPS, C_CHUNK, C_CHUNK)),
                  const((C_GROUPS, C_CHUNK, W_C // C_GROUPS)), const((W_C, D))] + _router_in_specs(),
        out_specs=_dispatch_out_specs(),
        scratch_shapes=[pltpu.VMEM((TM, W_C), bf16)] + _COMBINE_SCRATCH + _DISPATCH_SCRATCH,
    )
    return pl.pallas_call(
        _odd_kernel,
        grid_spec=grid_spec,
        out_shape=_DISPATCH_OUT_SHAPES,
        compiler_params=_cp(("arbitrary",)),
        name="odd_mixer_router",
    )(run_n, run_src, x, prev_route, y, mod_prev, mod_l, nm, w_in, ln_g, ln_b, ws, bs, w_out, nrm, wr, br)


def _expert_kernel(nu_ref, tg_ref, tb_ref, hs_ref, gs_ref, wg_ref, wu_ref, wd_ref, y_ref):
    t = pl.program_id(0)

    @pl.when(t < nu_ref[0])
    def _():
        xb = _load_rows(hs_ref, 0, TMG).astype(bf16)
        gates = gs_ref[...]
        acc = jnp.zeros((TMG, D), f32)
        for e in range(E_PER_GROUP):
            a = jnp.dot(xb, wg_ref[e], preferred_element_type=f32)
            b = jnp.dot(xb, wu_ref[e], preferred_element_type=f32)
            hid = (_silu(a) * b * gates[:, e:e + 1]).astype(bf16)
            acc = acc + jnp.dot(hid, wd_ref[e], preferred_element_type=f32)
        _store_rows(y_ref, 0, acc)


def _experts(layer, n_used, tile_g, tile_blk, hs, gs, wg, wu, wd):
    wspec = lambda a, b: pl.BlockSpec((None, None, E_PER_GROUP, a, b), lambda t, nu, tg, tb: (layer, tg[t], 0, 0, 0))
    grid_spec = pltpu.PrefetchScalarGridSpec(
        num_scalar_prefetch=3,
        grid=(MAXT_G,),
        in_specs=[
            pl.BlockSpec((TMG * ROW_SUB, 128), lambda t, nu, tg, tb: (tb[t], 0)),
            pl.BlockSpec((TMG, 128), lambda t, nu, tg, tb: (tb[t], 0)),
            wspec(D, D_EXPERT), wspec(D, D_EXPERT), wspec(D_EXPERT, D),
        ],
        out_specs=pl.BlockSpec((TMG * ROW_SUB, 128), lambda t, nu, tg, tb: (tb[t], 0)),
    )
    return pl.pallas_call(
        _expert_kernel,
        grid_spec=grid_spec,
        out_shape=jax.ShapeDtypeStruct((N_GROUPS * RG * ROW_SUB, 128), f32),
        compiler_params=_cp(("arbitrary",)),
        name="experts",
    )(n_used, tile_g, tile_blk, hs, gs, wg, wu, wd)


def _dispatch_plan(cnt_tiles):
    cnt = cnt_tiles[:, 0, 0:N_GROUPS].astype(i32)
    ends = jnp.cumsum(cnt, axis=0)
    run_src = (ends - cnt + jnp.arange(N_GROUPS, dtype=i32)[None, :] * RG).reshape(-1)
    totals = ends[NT - 1]
    tiles_g = (totals + TMG - 1) // TMG
    tile_end = jnp.cumsum(tiles_g)
    n_used = tile_end[N_GROUPS - 1:]
    t = jnp.minimum(jnp.arange(MAXT_G, dtype=i32), n_used[0] - 1)
    tile_g = jnp.sum((tile_end[None, :] <= t[:, None]).astype(i32), axis=1)
    tile_start = tile_end - tiles_g
    first = jnp.sum(jnp.where(tile_g[:, None] == jnp.arange(N_GROUPS, dtype=i32)[None, :], tile_start[None, :], 0), axis=1)
    tile_blk = tile_g * (RG // TMG) + t - first
    return cnt.reshape(-1), run_src, n_used, tile_g, tile_blk


def _final_kernel(n_ref, s_ref, x_ref, route_ref, y_hbm, mod_ref, fn_ref, o_ref, ybuf, ysem, *, n_tiles, tile_off):
    i = pl.program_id(0)
    mod = mod_ref[0]
    y = _combine_tile(i, tile_off + i, n_tiles, n_ref, s_ref, route_ref, y_hbm, ybuf, ysem)
    x = x_ref[...] + mod[:, 5 * D:6 * D] * y
    o_ref[...] = _rms(x, fn_ref[...])


def _final(run_n, run_src, x, route, y, mod_l, fnorm, *, n_tiles, tile_off):
    grid_spec = pltpu.PrefetchScalarGridSpec(
        num_scalar_prefetch=2,
        grid=(n_tiles,),
        in_specs=[
            pl.BlockSpec((TM, D), lambda i, *_: (tile_off + i, 0)),
            pl.BlockSpec((TM, 128), lambda i, *_: (tile_off + i, 0)),
            pl.BlockSpec(memory_space=pl.ANY),
            pl.BlockSpec((1, 1, N_MOD * D), lambda i, *_: (_mod_row(tile_off + i), 0, 0)),
            pl.BlockSpec((1, D), lambda i, *_: (0, 0)),
        ],
        out_specs=pl.BlockSpec((TM, D), lambda i, *_: (i, 0)),
        scratch_shapes=_COMBINE_SCRATCH,
    )
    return pl.pallas_call(
        functools.partial(_final_kernel, n_tiles=n_tiles, tile_off=tile_off),
        grid_spec=grid_spec,
        out_shape=jax.ShapeDtypeStruct((n_tiles * TM, D), f32),
        compiler_params=_cp(("arbitrary",)),
        name="final_norm",
    )(run_n, run_src, x, route, y, mod_l, fnorm)


def _rope_tables():
    rows = DEC_SEQ // GRID_W
    r, c = jnp.meshgrid(jnp.arange(rows), jnp.arange(GRID_W), indexing="ij")
    pos = jnp.stack([r.reshape(-1), c.reshape(-1)], axis=-1).astype(f32)
    inv = ROPE_BASE ** (-jnp.arange(AXIS_FREQS, dtype=f32) / AXIS_FREQS)
    ang = pos[:, :, None] * inv
    cos, sin = jnp.cos(ang), jnp.sin(ang)
    cos_h = jnp.concatenate([cos, cos], axis=-1).reshape(DEC_SEQ, DH_A)
    sin_h = jnp.concatenate([-sin, sin], axis=-1).reshape(DEC_SEQ, DH_A)
    cos_t = jnp.concatenate([jnp.ones((TM, 2 * DH_A), f32), jnp.tile(cos_h, (1, 2))], axis=0)
    sin_t = jnp.concatenate([jnp.zeros((TM, 2 * DH_A), f32), jnp.tile(sin_h, (1, 2))], axis=0)
    return cos_t, sin_t


def _gate_lanes(p):
    out = jnp.zeros((128,), f32)
    for d in range(2):
        out = out.at[d * 2 * H_B:d * 2 * H_B + H_B].set(p[d])
    return out.reshape(1, 128)


def _router_weights(w_rg, b_rg, w_re, b_re):
    wr = jnp.zeros((D, 256), f32).at[:, 0:N_GROUPS].set(w_rg).at[:, 128:128 + N_EXPERTS].set(w_re)
    br = jnp.zeros((1, 256), f32).at[0, 0:N_GROUPS].set(b_rg).at[0, 128:128 + N_EXPERTS].set(b_re)
    return wr, br


def kernel(x_prompt, x_sample, cache_k, cache_v, state_fwd, state_bwd, c, c_ctx, norm_mix, norm_ffn, w_mod, b_mod,
           w_in_even, conv_w, a_log, dt_bias, delta_norm, lam_q, lam_k, subln, w_out_even, w_in_odd, sgu_ln_g,
           sgu_ln_b, w_spatial, b_spatial, w_out_odd, w_router_group, b_router_group, w_router_expert,
           b_router_expert, w_exp_gate, w_exp_up, w_exp_down, final_norm):
    xp = x_prompt.reshape(N_P, D)
    xs = x_sample.reshape(N_S, D)
    cond = jnp.concatenate([c, c_ctx[None, :], jnp.zeros((COND_ROWS - DEC_BATCH - 1, D), f32)], axis=0)
    mod = _modulation(cond, w_mod, b_mod)
    mod0 = mod[0].reshape(COND_ROWS, 1, N_MOD * D)
    mod1 = mod[1].reshape(COND_ROWS, 1, N_MOD * D)

    lam_init = 0.8 - 0.6 * math.exp(-0.3 * 0)
    w_in_pad = jnp.pad(w_in_even[0], ((0, 0), (0, IN_EVEN_PAD - IN_EVEN))).astype(bf16)
    cos_t, sin_t = _rope_tables()
    qa, ka, va, qkvb, gate_b, ab = _even_in(xp, xs, mod0, norm_mix[0:1], w_in_pad, cos_t, sin_t)

    ctx_k = cache_k[:, 0].reshape(DEC_BATCH * PAST, W_A)
    ctx_v = cache_v[:, 0].reshape(DEC_BATCH * PAST, W_A)
    attn = functools.partial(_attention, qa, ka, va, lam_q[0], lam_k[0], subln[0:1], lam_init)
    oa_p = attn(seq=SEQ, tq=SEQ, tok_off=0)
    oa_s = attn(seq=DEC_SEQ, tq=512, tok_off=N_P, ctx=(ctx_k, ctx_v))

    conv_pad = jnp.pad(conv_w[0], ((0, 8 - CONV_K), (0, 0)))
    dl = functools.partial(_delta, qkvb, gate_b, ab, conv_pad, _gate_lanes(a_log[0]), _gate_lanes(dt_bias[0]),
                           delta_norm[0:1])
    ob_p, s_f, s_b = dl(seq=SEQ, tok_off=0)
    ob_s, _, _ = dl(seq=DEC_SEQ, tok_off=N_P, init=(state_fwd[:, 0], state_bwd[:, 0]))

    wr0, br0 = _router_weights(w_router_group[0], b_router_group[0], w_router_expert[0], b_router_expert[0])
    x1, route0, cnt0, hs, gs = _even_out(xp, xs, oa_p, oa_s, ob_p, ob_s, mod0, w_out_even[0].astype(bf16),
                                         norm_ffn[0:1], wr0, br0)

    wshape = (DEPTH, N_GROUPS, E_PER_GROUP)
    wg = w_exp_gate.astype(bf16).reshape(wshape + (D, D_EXPERT))
    wu = w_exp_up.astype(bf16).reshape(wshape + (D, D_EXPERT))
    wd = w_exp_down.astype(bf16).reshape(wshape + (D_EXPERT, D))
    run_n0, run_src0, n_used, tile_g, tile_blk = _dispatch_plan(cnt0)
    y0 = _experts(0, n_used, tile_g, tile_blk, hs, gs, wg, wu, wd)

    bs = jnp.broadcast_to(b_spatial[0][:, :, None], (C_GROUPS, C_CHUNK, W_C // C_GROUPS))
    wr1, br1 = _router_weights(w_router_group[1], b_router_group[1], w_router_expert[1], b_router_expert[1])
    x2, route1, cnt1, hs, gs = _odd_mixer(run_n0, run_src0, x1, route0, y0, mod0, mod1, norm_mix[1:2],
                                          w_in_odd[0].astype(bf16), sgu_ln_g[0:1], sgu_ln_b[0:1],
                                          w_spatial[0].astype(bf16), bs, w_out_odd[0].astype(bf16), norm_ffn[1:2],
                                          wr1, br1)
    run_n1, run_src1, n_used, tile_g, tile_blk = _dispatch_plan(cnt1)
    y1 = _experts(1, n_used, tile_g, tile_blk, hs, gs, wg, wu, wd)

    fin = functools.partial(_final, run_n1, run_src1, x2, route1, y1, mod1, final_norm[None, :])
    y_prompt = fin(n_tiles=NPT, tile_off=0)
    y_sample = fin(n_tiles=NT - NPT, tile_off=NPT)

    new_cache_k = ka[:N_P].reshape(BATCH, 1, SEQ, H_A, 2 * DH_A)
    new_cache_v = va[:N_P].reshape(BATCH, 1, SEQ, H_A, 2 * DH_A)
    return (y_prompt.reshape(BATCH, SEQ, D), y_sample.reshape(DEC_BATCH, DEC_SEQ, D), new_cache_k, new_cache_v,
            s_f[:, None], s_b[:, None])
```

```python
import functools
import math

import jax
import jax.numpy as jnp
from jax import lax
from jax.experimental import pallas as pl
from jax.experimental.pallas import tpu as pltpu

f32 = jnp.float32
bf16 = jnp.bfloat16
i32 = jnp.int32
HIGHEST = lax.Precision.HIGHEST

D = 1024
BATCH, SEQ = 16, 256
DEC_BATCH, DEC_SEQ, PAST = 8, 1024, 512
DEPTH = 2
GRID_W = 64
N_MOD = 6
EPS = 1e-6
H_A, DH_A = 4, 64
W_A = H_A * 2 * DH_A
ROPE_BASE = 10000.0
AXIS_FREQS = DH_A // 4
H_B, DK_B, DV_B = 4, 128, 128
W_B = H_B * DV_B
CONV_K = 5
CH = 64
IN_EVEN = 3 * W_A + 4 * W_B + 4 * H_B
IN_EVEN_PAD = 3712
W_C = D
C_GROUPS, C_CHUNK = 4, 128
N_GROUPS, E_PER_GROUP, D_EXPERT = 4, 8, 256
N_EXPERTS = N_GROUPS * E_PER_GROUP

N_P = BATCH * SEQ
N_S = DEC_BATCH * DEC_SEQ
N_TOK = N_P + N_S
TM = 256
NPT = N_P // TM
NT = N_TOK // TM
COND_ROWS = 16

VMEM_LIMIT = 56 * 1024 * 1024
VMEM_LIMIT_EXPERTS = 62 * 1024 * 1024


def _cp(sem, vmem_limit=VMEM_LIMIT):
    return pltpu.CompilerParams(dimension_semantics=sem, vmem_limit_bytes=vmem_limit)


def _mod_row(i):
    return jnp.where(i < NPT, DEC_BATCH, (i - NPT) // (DEC_SEQ // TM))


def _rms(x, g):
    return x * lax.rsqrt(jnp.mean(x * x, axis=-1, keepdims=True) + EPS) * g


def _silu(x):
    return x * jax.nn.sigmoid(x)


def _dot(a, b):
    return jnp.dot(a.astype(bf16), b.astype(bf16), preferred_element_type=f32)


ROW_SUB = D // 128


def _store_rows(ref, base, x):
    rows = x.shape[0]
    for j in range(ROW_SUB):
        ref[pl.ds(base + j, rows, stride=ROW_SUB), :] = x[:, j * 128:(j + 1) * 128]


def _load_rows(ref, base, rows):
    return jnp.concatenate([ref[pl.ds(base + j, rows, stride=ROW_SUB), :] for j in range(ROW_SUB)], axis=1)


TN_MOD = 1536


def _mod_kernel(c_ref, w_ref, b_ref, o_ref):
    a = _silu(c_ref[...])
    o_ref[0] = jnp.dot(a, w_ref[0], precision=HIGHEST, preferred_element_type=f32) + b_ref[0]


def _modulation(cond, w_mod, b_mod):
    return pl.pallas_call(
        _mod_kernel,
        grid=(DEPTH, N_MOD * D // TN_MOD),
        in_specs=[
            pl.BlockSpec((COND_ROWS, D), lambda l, j: (0, 0)),
            pl.BlockSpec((1, D, TN_MOD), lambda l, j: (l, 0, j)),
            pl.BlockSpec((1, 1, TN_MOD), lambda l, j: (l, 0, j)),
        ],
        out_specs=pl.BlockSpec((1, COND_ROWS, TN_MOD), lambda l, j: (l, 0, j)),
        out_shape=jax.ShapeDtypeStruct((DEPTH, COND_ROWS, N_MOD * D), f32),
        compiler_params=_cp(("arbitrary", "arbitrary")),
        name="modulation",
    )(cond, w_mod, b_mod.reshape(DEPTH, 1, N_MOD * D))


def _even_in_kernel(xp_ref, xs_ref, mod_ref, nrm_ref, w_ref, cos_ref, sin_ref,
                    q_ref, k_ref, v_ref, qkvb_ref, gate_ref, ab_ref):
    i = pl.program_id(0)
    x = jnp.where(i < NPT, xp_ref[...], xs_ref[...])
    mod = mod_ref[0]
    h = _rms(x, nrm_ref[...]) * (1.0 + mod[:, D:2 * D]) + mod[:, 0:D]
    proj = _dot(h, w_ref[...])
    cos = cos_ref[...]
    sin = sin_ref[...]
    lane = lax.broadcasted_iota(i32, (TM, 2 * DH_A), 1)
    first = (lane % (2 * AXIS_FREQS)) < AXIS_FREQS

    def rope(xh):
        partner = jnp.where(first, pltpu.roll(xh, 2 * DH_A - AXIS_FREQS, 1), pltpu.roll(xh, AXIS_FREQS, 1))
        return xh * cos + partner * sin

    for hh in range(H_A):
        sl = slice(hh * 2 * DH_A, (hh + 1) * 2 * DH_A)
        q_ref[:, sl] = (rope(proj[:, sl]) * (DH_A ** -0.5)).astype(bf16)
        k_ref[:, sl] = rope(proj[:, W_A + hh * 2 * DH_A:W_A + (hh + 1) * 2 * DH_A])
    v_ref[...] = proj[:, 2 * W_A:3 * W_A]
    qkvb_ref[...] = proj[:, 3 * W_A:3 * W_A + 3 * W_B]
    gate_ref[...] = proj[:, 3 * W_A + 3 * W_B:3 * W_A + 4 * W_B]
    ab_ref[...] = proj[:, 3 * W_A + 4 * W_B:IN_EVEN_PAD]


def _even_in(xp, xs, mod_l, nrm, w_in_pad, cos_t, sin_t):
    tile = lambda w: pl.BlockSpec((TM, w), lambda i: (i, 0))
    return pl.pallas_call(
        _even_in_kernel,
        grid=(NT,),
        in_specs=[
            pl.BlockSpec((TM, D), lambda i: (jnp.minimum(i, NPT - 1), 0)),
            pl.BlockSpec((TM, D), lambda i: (jnp.maximum(i - NPT, 0), 0)),
            pl.BlockSpec((1, 1, N_MOD * D), lambda i: (_mod_row(i), 0, 0)),
            pl.BlockSpec((1, D), lambda i: (0, 0)),
            pl.BlockSpec((D, IN_EVEN_PAD), lambda i: (0, 0)),
            pl.BlockSpec((TM, 2 * DH_A), lambda i: (jnp.where(i < NPT, 0, 1 + (i - NPT) % (DEC_SEQ // TM)), 0)),
            pl.BlockSpec((TM, 2 * DH_A), lambda i: (jnp.where(i < NPT, 0, 1 + (i - NPT) % (DEC_SEQ // TM)), 0)),
        ],
        out_specs=[tile(W_A), tile(W_A), tile(W_A), tile(3 * W_B), tile(W_B), tile(128)],
        out_shape=[
            jax.ShapeDtypeStruct((N_TOK, W_A), bf16),
            jax.ShapeDtypeStruct((N_TOK, W_A), f32),
            jax.ShapeDtypeStruct((N_TOK, W_A), f32),
            jax.ShapeDtypeStruct((N_TOK, 3 * W_B), f32),
            jax.ShapeDtypeStruct((N_TOK, W_B), f32),
            jax.ShapeDtypeStruct((N_TOK, 128), f32),
        ],
        compiler_params=_cp(("arbitrary",)),
        name="even_in",
    )(xp, xs, mod_l, nrm, w_in_pad, cos_t, sin_t)


def _attn_kernel(*refs, lam_init, has_ctx):
    if has_ctx:
        q_ref, kn_ref, vn_ref, kc_ref, vc_ref, lq_ref, lk_ref, sub_ref, o_ref = refs
    else:
        q_ref, kn_ref, vn_ref, lq_ref, lk_ref, sub_ref, o_ref = refs
    tq = q_ref.shape[0]
    lq = lq_ref[...]
    lk = lk_ref[...]
    prod = lq * lk
    lam = (jnp.exp(jnp.sum(prod[0:1], axis=-1, keepdims=True))
           - jnp.exp(jnp.sum(prod[1:2], axis=-1, keepdims=True)) + lam_init)
    lane = lax.broadcasted_iota(i32, (tq, 2 * DH_A), 1)
    nt = (((1,), (1,)), ((), ()))
    for hh in range(H_A):
        sl = slice(hh * 2 * DH_A, (hh + 1) * 2 * DH_A)
        q = q_ref[:, sl]
        srcs = [(kn_ref[:, sl].astype(bf16), vn_ref[:, sl].astype(bf16))]
        if has_ctx:
            srcs.append((kc_ref[:, sl].astype(bf16), vc_ref[:, sl].astype(bf16)))
        outs = []
        for part in range(2):
            qm = jnp.where((lane < DH_A) if part == 0 else (lane >= DH_A), q, jnp.zeros_like(q))
            ss = [lax.dot_general(qm, k, nt, preferred_element_type=f32) for k, _ in srcs]
            m = ss[0].max(axis=-1, keepdims=True)
            for s in ss[1:]:
                m = jnp.maximum(m, s.max(axis=-1, keepdims=True))
            l = jnp.zeros((tq, 1), f32)
            o = jnp.zeros((tq, 2 * DH_A), f32)
            for s, (_, v) in zip(ss, srcs):
                e = jnp.exp(s - m)
                l = l + jnp.sum(e, axis=-1, keepdims=True)
                o = o + jnp.dot(e.astype(bf16), v, preferred_element_type=f32)
            outs.append(o / l)
        o = outs[0] - lam * outs[1]
        o_ref[:, sl] = (_rms(o, sub_ref[...]) * (1.0 - lam_init)).astype(bf16)


def _attention(q, ka, va, lam_q, lam_k, subln, lam_init, *, seq, tq, tok_off, ctx=None):
    nq = seq // tq
    nb = (N_P if ctx is None else N_S) // seq
    qo, ko = tok_off // tq, tok_off // seq
    in_specs = [
        pl.BlockSpec((tq, W_A), lambda b, j: (qo + b * nq + j, 0)),
        pl.BlockSpec((seq, W_A), lambda b, j: (ko + b, 0)),
        pl.BlockSpec((seq, W_A), lambda b, j: (ko + b, 0)),
    ]
    args = [q, ka, va]
    if ctx is not None:
        in_specs += [pl.BlockSpec((PAST, W_A), lambda b, j: (b, 0))] * 2
        args += list(ctx)
    in_specs += [pl.BlockSpec((2, DH_A), lambda b, j: (0, 0))] * 2 + [pl.BlockSpec((1, 2 * DH_A), lambda b, j: (0, 0))]
    args += [lam_q, lam_k, subln]
    return pl.pallas_call(
        functools.partial(_attn_kernel, lam_init=lam_init, has_ctx=ctx is not None),
        grid=(nb, nq),
        in_specs=in_specs,
        out_specs=pl.BlockSpec((tq, W_A), lambda b, j: (b * nq + j, 0)),
        out_shape=jax.ShapeDtypeStruct((nb * seq, W_A), bf16),
        compiler_params=_cp(("arbitrary", "arbitrary")),
        name="diff_attn_ctx" if ctx is not None else "diff_attn",
    )(*args)


def _softplus(x):
    return jnp.maximum(x, 0.0) + jnp.log1p(jnp.exp(-jnp.abs(x)))


RW = H_B * CH
CPI = 4


def _rep_rows(x, times):
    return jnp.concatenate([x] * times, axis=0)


def _tri_inv(lmats, bd, ii, jj):
    b16 = (ii // 16) == (jj // 16)
    b32 = (ii // 32) == (jj // 32)
    eye = (ii == jj).astype(f32)
    n = len(lmats)
    mm = lambda a, b: jnp.dot(a.astype(bf16), bd(b), preferred_element_type=f32)
    l0 = [jnp.where(b16, l, 0.0) for l in lmats]
    x = [eye - a for a in l0]
    p = [mm(a, a) for a in l0]
    for level in range(3):
        xp = [mm(x[i], p[i]) for i in range(n)]
        if level < 2:
            p = [mm(a, a) for a in p]
        x = [x[i] + xp[i] for i in range(n)]
    for off_diag in (jnp.logical_and(b32, jnp.logical_not(b16)), jnp.logical_not(b32)):
        t = [mm(jnp.where(off_diag, lmats[i], 0.0), x[i]) for i in range(n)]
        t = [mm(x[i], t[i]) for i in range(n)]
        x = [x[i] - t[i] for i in range(n)]
    return x


def _delta_kernel(*refs, seq, has_init):
    if has_init:
        (qkv_ref, gate_ref, ab_ref, cw_ref, al_ref, dt_ref, dn_ref, sf0_ref, sb0_ref,
         ob_ref, sf_ref, sb_ref, xpad, xq, xk, xv, gacc, bacc, o_f, o_b, st, pu, pw, pqe, pkd, patt, ptot) = refs
    else:
        (qkv_ref, gate_ref, ab_ref, cw_ref, al_ref, dt_ref, dn_ref,
         ob_ref, sf_ref, sb_ref, xpad, xq, xk, xv, gacc, bacc, o_f, o_b, st, pu, pw, pqe, pkd, patt, ptot) = refs
    n = seq // CH
    win = CH + 16

    xpad[0:8, :] = jnp.zeros((8, 3 * W_B), f32)
    xpad[seq + 8:seq + 16, :] = jnp.zeros((8, 3 * W_B), f32)
    xpad[8:seq + 8, :] = qkv_ref[...]
    neg_a = -jnp.exp(al_ref[...])
    dtb = dt_ref[...]

    def pre(c, carry):
        r0 = pl.multiple_of(c * CH, CH)
        a = xpad[pl.ds(r0, win), :]
        y = jnp.zeros((CH, 3 * W_B), f32)
        for j in range(CONV_K):
            sh = ((CONV_K - 1) // 2 - j) % win
            y = y + pltpu.roll(a, sh, 0)[8:8 + CH] * cw_ref[j:j + 1, :]
        y = _silu(y)
        for hh in range(H_B):
            qh = y[:, hh * DK_B:(hh + 1) * DK_B]
            kh = y[:, W_B + hh * DK_B:W_B + (hh + 1) * DK_B]
            qn = qh * lax.rsqrt(jnp.sum(qh * qh, axis=-1, keepdims=True) + EPS) * (DK_B ** -0.5)
            kn = kh * lax.rsqrt(jnp.sum(kh * kh, axis=-1, keepdims=True) + EPS)
            xq[pl.ds(r0, CH), hh * DK_B:(hh + 1) * DK_B] = qn
            xk[pl.ds(r0, CH), hh * DK_B:(hh + 1) * DK_B] = kn
        xv[pl.ds(r0, CH), :] = y[:, 2 * W_B:3 * W_B]
        ab = ab_ref[pl.ds(r0, CH), :]
        gacc[pl.ds(r0, CH), :] = neg_a * _softplus(ab + dtb)
        bacc[pl.ds(r0, CH), :] = jax.nn.sigmoid(ab)
        return carry

    lax.fori_loop(0, n, pre, 0)

    ii = lax.broadcasted_iota(i32, (CH, RW), 0)
    lane_r = lax.broadcasted_iota(i32, (CH, RW), 1)
    jj = lane_r % CH
    hb = lane_r // CH
    rowl = lax.broadcasted_iota(i32, (CH, 128), 0)
    bd_mask = (lax.broadcasted_iota(i32, (RW, RW), 0) // CH) == (lax.broadcasted_iota(i32, (RW, RW), 1) // CH)
    wide_mask = ((lax.broadcasted_iota(i32, (RW, W_B), 0) // CH)
                 == (lax.broadcasted_iota(i32, (RW, W_B), 1) // DK_B))
    pair_mask = ((lax.broadcasted_iota(i32, (2 * DK_B, 2 * DV_B), 0) // DK_B)
                 == (lax.broadcasted_iota(i32, (2 * DK_B, 2 * DV_B), 1) // DV_B))
    zero_b = jnp.zeros((), bf16)

    def bd(x):
        return jnp.where(bd_mask, _rep_rows(x.astype(bf16), H_B), zero_b)

    def bd_wide(x):
        return jnp.where(wide_mask, _rep_rows(x.astype(bf16), H_B), zero_b)

    def per_head(src, lanes):
        return jnp.concatenate([jnp.broadcast_to(src[:, ln:ln + 1], (src.shape[0], DK_B)) for ln in lanes], axis=1)

    nt = (((1,), (1,)), ((), ()))
    tn = (((0,), (0,)), ((), ()))

    def local(it, carry):
        chains = []
        for j in range(CPI):
            c = it * CPI + j
            r0 = pl.multiple_of(c * CH, CH)
            g = gacc[pl.ds(r0, CH), :]
            beta = bacc[pl.ds(r0, CH), :]
            pre_g = g
            for s in (1, 2, 4, 8, 16, 32):
                pre_g = pre_g + jnp.where(rowl >= s, pltpu.roll(pre_g, s, 0), 0.0)
            tot = pre_g[CH - 1:CH, :]
            q_all = xq[pl.ds(r0, CH), :]
            k_all = xk[pl.ds(r0, CH), :]
            v_all = xv[pl.ds(r0, CH), :]
            k_bd = bd_wide(k_all)
            for d in range(2):
                gc = pre_g if d == 0 else tot - pre_g + g
                lanes = [d * 2 * H_B + hh for hh in range(H_B)]
                gct = _rep_rows(gc, H_B).T
                g_col = jnp.zeros((CH, RW), f32)
                g_row = jnp.zeros((1, RW), f32)
                for hh, ln in enumerate(lanes):
                    g_col = jnp.where(hb == hh, gc[:, ln:ln + 1], g_col)
                    g_row = jnp.where(hb[0:1] == hh, gct[ln:ln + 1, :], g_row)
                g_wide = per_head(gc, lanes)
                b_wide = per_head(beta, [ln + H_B for ln in lanes])
                incl = (ii >= jj) if d == 0 else (ii <= jj)
                strict = (ii > jj) if d == 0 else (ii < jj)
                eg = jnp.exp(g_wide)
                kb = k_all * b_wide
                chains.append(dict(
                    ci=c * 2 + d, incl=incl, strict=strict, k_bd=k_bd, kb=kb, q=q_all, tot=tot,
                    dec=jnp.where(incl, jnp.exp(jnp.where(incl, g_col - g_row, 0.0)), 0.0),
                    vb=v_all * b_wide, kbe=kb * eg, qe=q_all * eg,
                    kd=k_all * jnp.exp(per_head(tot, lanes) - g_wide)))
        kk = [lax.dot_general(e["kb"].astype(bf16), e["k_bd"], nt, preferred_element_type=f32) for e in chains]
        qk = [lax.dot_general(e["q"].astype(bf16), e["k_bd"], nt, preferred_element_type=f32) for e in chains]
        lmat = [jnp.where(e["strict"], kk[i] * e["dec"], 0.0) for i, e in enumerate(chains)]
        tinv = _tri_inv(lmat, bd, ii, jj)
        u = [jnp.dot(tinv[i].astype(bf16), bd_wide(e["vb"]), preferred_element_type=f32) for i, e in enumerate(chains)]
        w = [jnp.dot(tinv[i].astype(bf16), bd_wide(e["kbe"]), preferred_element_type=f32) for i, e in enumerate(chains)]
        for i, e in enumerate(chains):
            ci = e["ci"]
            pu[ci] = u[i]
            pw[ci] = w[i].astype(bf16)
            pqe[ci] = e["qe"].astype(bf16)
            pkd[ci] = e["kd"].astype(bf16)
            patt[ci] = jnp.where(e["incl"], qk[i] * e["dec"], 0.0).astype(bf16)
            ptot[ci] = jnp.broadcast_to(e["tot"], (8, 128))
        return carry

    lax.fori_loop(0, n // CPI, local, 0)

    for hh in range(H_B):
        rows = slice(hh * DK_B, (hh + 1) * DK_B)
        if has_init:
            st[0, rows, :] = sf0_ref[0, hh]
            st[1, rows, :] = sb0_ref[0, hh]
        else:
            st[0, rows, :] = jnp.zeros((DK_B, DV_B), f32)
            st[1, rows, :] = jnp.zeros((DK_B, DV_B), f32)

    def scan(it, carry):
        chains = []
        for d in range(2):
            c = it if d == 0 else n - 1 - it
            ci = c * 2 + d
            s_old = st[d]
            s_b = s_old.astype(bf16)
            chains.append(dict(
                d=d, r0=pl.multiple_of(c * CH, CH), s_old=s_old, u=pu[ci], att=patt[ci], kd=pkd[ci], tot=ptot[ci],
                lhs=jnp.concatenate([pw[ci], pqe[ci]], axis=0),
                rhs=[jnp.where(pair_mask, jnp.concatenate([s_b[p * 2 * DK_B:(p + 1) * 2 * DK_B]] * 2, axis=1), zero_b)
                     for p in range(2)]))
        ws_qs = [jnp.concatenate(
            [jnp.dot(e["lhs"][:, p * 2 * DK_B:(p + 1) * 2 * DK_B], e["rhs"][p], preferred_element_type=f32)
             for p in range(2)], axis=1) for e in chains]
        v_new = [e["u"] - ws_qs[i][0:CH] for i, e in enumerate(chains)]
        av = [jnp.dot(e["att"], bd_wide(v_new[i]), preferred_element_type=f32) for i, e in enumerate(chains)]
        su = [lax.dot_general(jnp.where(wide_mask, _rep_rows(e["kd"], H_B), zero_b),
                              jnp.concatenate([v_new[i][:, hh * DV_B:(hh + 1) * DV_B] for hh in range(H_B)],
                                              axis=0).astype(bf16),
                              tn, preferred_element_type=f32) for i, e in enumerate(chains)]
        for i, e in enumerate(chains):
            d = e["d"]
            if d == 0:
                o_f[pl.ds(e["r0"], CH), :] = ws_qs[i][CH:2 * CH] + av[i]
            else:
                o_b[pl.ds(e["r0"], CH), :] = ws_qs[i][CH:2 * CH] + av[i]
            for hh in range(H_B):
                rows = slice(hh * DK_B, (hh + 1) * DK_B)
                ln = d * 2 * H_B + hh
                st[d, rows, :] = e["s_old"][rows] * jnp.exp(e["tot"][0:1, ln:ln + 1]) + su[i][rows]
        return carry

    lax.fori_loop(0, n, scan, 0)

    def post(c, carry):
        r0 = pl.multiple_of(c * CH, CH)
        o = o_f[pl.ds(r0, CH), :] + o_b[pl.ds(r0, CH), :]
        gt = gate_ref[pl.ds(r0, CH), :]
        for hh in range(H_B):
            sl = slice(hh * DV_B, (hh + 1) * DV_B)
            ob_ref[pl.ds(r0, CH), sl] = (_rms(o[:, sl], dn_ref[...]) * _silu(gt[:, sl])).astype(bf16)
        return carry

    lax.fori_loop(0, n, post, 0)
    for hh in range(H_B):
        sf_ref[0, hh] = st[0, hh * DK_B:(hh + 1) * DK_B, :]
        sb_ref[0, hh] = st[1, hh * DK_B:(hh + 1) * DK_B, :]


def _delta(qkvb, gate, ab, conv_pad, al_lane, dt_lane, dnorm, *, seq, tok_off, init=None):
    nb = (N_P if init is None else N_S) // seq
    bo = tok_off // seq
    rows = lambda w: pl.BlockSpec((seq, w), lambda b: (bo + b, 0))
    const = lambda shp: pl.BlockSpec(shp, lambda b: tuple(0 for _ in shp))
    st_spec = pl.BlockSpec((1, H_B, DK_B, DV_B), lambda b: (b, 0, 0, 0))
    in_specs = [rows(3 * W_B), rows(W_B), rows(128), const((8, 3 * W_B)), const((1, 128)), const((1, 128)),
                const((1, DV_B))]
    args = [qkvb, gate, ab, conv_pad, al_lane, dt_lane, dnorm]
    if init is not None:
        in_specs += [st_spec, st_spec]
        args += [init[0], init[1]]
    return pl.pallas_call(
        functools.partial(_delta_kernel, seq=seq, has_init=init is not None),
        grid=(nb,),
        in_specs=in_specs,
        out_specs=[pl.BlockSpec((seq, W_B), lambda b: (b, 0)), st_spec, st_spec],
        out_shape=[
            jax.ShapeDtypeStruct((nb * seq, W_B), bf16),
            jax.ShapeDtypeStruct((nb, H_B, DK_B, DV_B), f32),
            jax.ShapeDtypeStruct((nb, H_B, DK_B, DV_B), f32),
        ],
        scratch_shapes=[
            pltpu.VMEM((seq + 16, 3 * W_B), f32),
            pltpu.VMEM((seq, W_B), f32),
            pltpu.VMEM((seq, W_B), f32),
            pltpu.VMEM((seq, W_B), f32),
            pltpu.VMEM((seq, 128), f32),
            pltpu.VMEM((seq, 128), f32),
            pltpu.VMEM((seq, W_B), f32),
            pltpu.VMEM((seq, W_B), f32),
            pltpu.VMEM((2, H_B * DK_B, DV_B), f32),
            pltpu.VMEM((2 * seq // CH, CH, W_B), f32),
            pltpu.VMEM((2 * seq // CH, CH, W_B), bf16),
            pltpu.VMEM((2 * seq // CH, CH, W_B), bf16),
            pltpu.VMEM((2 * seq // CH, CH, W_B), bf16),
            pltpu.VMEM((2 * seq // CH, CH, RW), bf16),
            pltpu.VMEM((2 * seq // CH, 8, 128), f32),
        ],
        compiler_params=_cp(("arbitrary",)),
        name="delta_ctx" if init is not None else "delta",
    )(*args)


RG = N_TOK
TMG = 256
MAXT_G = N_TOK // TMG + N_GROUPS
TILE_ROWS = TM * ROW_SUB
RUN_BITS = tuple(range(TM.bit_length() - 1, -1, -1))


def _hi_lo(x):
    hi = x.astype(bf16)
    return hi, (x - hi.astype(f32)).astype(bf16)


def _run_pieces(n):
    return [(1 << b, (n >> (b + 1)) << (b + 1), ((n >> b) & 1) == 1) for b in RUN_BITS]


def _route_and_dispatch(i, x1, mod, nrm_ref, wr_ref, br_ref, route_ref, cnt_ref, hs_hbm, gs_hbm,
                        hbuf, gbuf, run, sem):
    h2 = _rms(x1, nrm_ref[...]) * (1.0 + mod[:, 4 * D:5 * D]) + mod[:, 3 * D:4 * D]
    h_hi, h_lo = _hi_lo(h2)
    w_hi, w_lo = wr_ref[0], wr_ref[1]
    logits = (jnp.dot(h_hi, w_hi, preferred_element_type=f32) + jnp.dot(h_hi, w_lo, preferred_element_type=f32)
              + jnp.dot(h_lo, w_hi, preferred_element_type=f32)) + br_ref[...]
    lane = lax.broadcasted_iota(i32, (TM, 128), 1)
    lane_f = lane.astype(f32)
    neg = jnp.float32(-1e30)
    lg = jnp.where(lane < N_GROUPS, logits[:, 0:128], neg)
    mg = lg.max(axis=-1, keepdims=True)
    gidx = jnp.min(jnp.where(lg == mg, lane_f, 128.0), axis=-1, keepdims=True)
    pg = 1.0 / jnp.sum(jnp.exp(lg - mg), axis=-1, keepdims=True)
    le = jnp.where((lane // E_PER_GROUP).astype(f32) == gidx, logits[:, 128:256], neg)
    m1 = le.max(axis=-1, keepdims=True)
    i1 = jnp.min(jnp.where(le == m1, lane_f, 128.0), axis=-1, keepdims=True)
    le2 = jnp.where(lane_f == i1, neg, le)
    m2 = le2.max(axis=-1, keepdims=True)
    i2 = jnp.min(jnp.where(le2 == m2, lane_f, 128.0), axis=-1, keepdims=True)
    e2 = jnp.exp(m2 - m1)
    w1 = pg / (1.0 + e2)
    w2 = w1 * e2
    local = lane_f + gidx * E_PER_GROUP
    gates = jnp.where(local == i1, w1, jnp.where(local == i2, w2, 0.0))

    onehot = (lane_f == gidx).astype(f32)
    r = lax.broadcasted_iota(i32, (TM, TM), 0)
    c = lax.broadcasted_iota(i32, (TM, TM), 1)
    before = jnp.dot((r > c).astype(bf16), onehot.astype(bf16), preferred_element_type=f32)
    cnt = jnp.sum(onehot, axis=0, keepdims=True)
    pos = (jnp.sum(onehot * before, axis=-1, keepdims=True)
           + jnp.sum(jnp.where(lane_f < gidx, cnt, 0.0), axis=-1, keepdims=True))
    route_ref[...] = jnp.where(lane == 0, pos, 0.0)
    cnt_ref[0] = cnt

    perm_t = (c.astype(f32) == pos).astype(bf16)
    tn = (((0,), (0,)), ((), ()))
    h_sorted = lax.dot_general(perm_t, h_hi, tn, preferred_element_type=f32)
    g_hi, g_lo = _hi_lo(gates)
    g_sorted = (lax.dot_general(perm_t, g_hi, tn, preferred_element_type=f32)
                + lax.dot_general(perm_t, g_lo, tn, preferred_element_type=f32))

    slot = i % 2
    hbase = pl.multiple_of(slot * TILE_ROWS, TILE_ROWS)
    gbase = pl.multiple_of(slot * TM, TM)

    def tile_done(s):
        hb = pl.multiple_of(s * TILE_ROWS, TILE_ROWS)
        gb = pl.multiple_of(s * TM, TM)
        pltpu.make_async_copy(hbuf.at[pl.ds(hb, TILE_ROWS), :], hs_hbm.at[pl.ds(0, TILE_ROWS), :], sem.at[s]).wait()
        pltpu.make_async_copy(gbuf.at[pl.ds(gb, TM), :], gs_hbm.at[pl.ds(0, TM), :], sem.at[s]).wait()

    @pl.when(i == 0)
    def _():
        for g in range(N_GROUPS):
            run[g] = 0

    @pl.when(i >= 2)
    def _():
        tile_done(slot)

    _store_rows(hbuf, hbase, h_sorted)
    gbuf[pl.ds(gbase, TM), :] = g_sorted
    cnt_i = cnt.astype(i32)
    off = 0
    for g in range(N_GROUPS):
        n = cnt_i[0, g]
        dst = g * RG + run[g]
        for rows, o, pred in _run_pieces(n):
            @pl.when(pred)
            def _():
                pltpu.make_async_copy(
                    hbuf.at[pl.ds(pl.multiple_of(hbase + (off + o) * ROW_SUB, ROW_SUB), rows * ROW_SUB), :],
                    hs_hbm.at[pl.ds(pl.multiple_of((dst + o) * ROW_SUB, ROW_SUB), rows * ROW_SUB), :],
                    sem.at[slot]).start()
                pltpu.make_async_copy(gbuf.at[pl.ds(gbase + off + o, rows), :], gs_hbm.at[pl.ds(dst + o, rows), :],
                                      sem.at[slot]).start()
        run[g] = run[g] + n
        off = off + n

    @pl.when(i == pl.num_programs(0) - 1)
    def _():
        tile_done(slot)
        tile_done(1 - slot)
        hbuf[pl.ds(0, TMG * ROW_SUB), :] = jnp.zeros((TMG * ROW_SUB, 128), f32)
        gbuf[pl.ds(0, TMG), :] = jnp.zeros((TMG, 128), f32)
        for g in range(N_GROUPS):
            tot = run[g]
            pad = (-tot) & (TMG - 1)
            for rows, o, pred in _run_pieces(pad):
                @pl.when(pred)
                def _():
                    dst = g * RG + tot + o
                    ch = pltpu.make_async_copy(
                        hbuf.at[pl.ds(0, rows * ROW_SUB), :],
                        hs_hbm.at[pl.ds(pl.multiple_of(dst * ROW_SUB, ROW_SUB), rows * ROW_SUB), :], sem.at[0])
                    cg = pltpu.make_async_copy(gbuf.at[pl.ds(0, rows), :], gs_hbm.at[pl.ds(dst, rows), :], sem.at[1])
                    ch.start()
                    cg.start()
                    ch.wait()
                    cg.wait()


def _dispatch_out_specs():
    return [
        pl.BlockSpec((TM, D), lambda i, *_: (i, 0)),
        pl.BlockSpec((TM, 128), lambda i, *_: (i, 0)),
        pl.BlockSpec((1, 1, 128), lambda i, *_: (i, 0, 0)),
        pl.BlockSpec(memory_space=pl.ANY),
        pl.BlockSpec(memory_space=pl.ANY),
    ]


_DISPATCH_OUT_SHAPES = [
    jax.ShapeDtypeStruct((N_TOK, D), f32),
    jax.ShapeDtypeStruct((N_TOK, 128), f32),
    jax.ShapeDtypeStruct((NT, 1, 128), f32),
    jax.ShapeDtypeStruct((N_GROUPS * RG * ROW_SUB, 128), f32),
    jax.ShapeDtypeStruct((N_GROUPS * RG, 128), f32),
]
_DISPATCH_SCRATCH = [
    pltpu.VMEM((2 * TILE_ROWS, 128), f32),
    pltpu.VMEM((2 * TM, 128), f32),
    pltpu.SMEM((N_GROUPS,), i32),
    pltpu.SemaphoreType.DMA((2,)),
]


def _router_in_specs():
    return [pl.BlockSpec((1, D), lambda i, *_: (0, 0)),
            pl.BlockSpec((2, D, 256), lambda i, *_: (0, 0, 0)),
            pl.BlockSpec((1, 256), lambda i, *_: (0, 0))]


def _fetch_runs(n_ref, s_ref, y_hbm, ybuf, sem, tile, slot):
    base = slot * TILE_ROWS
    off = 0
    for g in range(N_GROUPS):
        n = n_ref[tile * N_GROUPS + g]
        src = s_ref[tile * N_GROUPS + g]
        for rows, o, pred in _run_pieces(n):
            @pl.when(pred)
            def _():
                pltpu.make_async_copy(
                    y_hbm.at[pl.ds(pl.multiple_of((src + o) * ROW_SUB, ROW_SUB), rows * ROW_SUB), :],
                    ybuf.at[pl.ds(pl.multiple_of(base + (off + o) * ROW_SUB, ROW_SUB), rows * ROW_SUB), :],
                    sem.at[slot]).start()
        off = off + n


def _combine_tile(i, tile, n_tiles, n_ref, s_ref, route_ref, y_hbm, ybuf, sem):
    slot = i % 2

    @pl.when(i == 0)
    def _():
        _fetch_runs(n_ref, s_ref, y_hbm, ybuf, sem, tile, 0)

    @pl.when(i + 1 < n_tiles)
    def _():
        _fetch_runs(n_ref, s_ref, y_hbm, ybuf, sem, tile + 1, 1 - slot)

    base = pl.multiple_of(slot * TILE_ROWS, TILE_ROWS)
    pltpu.make_async_copy(y_hbm.at[pl.ds(0, TILE_ROWS), :], ybuf.at[pl.ds(base, TILE_ROWS), :], sem.at[slot]).wait()
    y_hi, y_lo = _hi_lo(_load_rows(ybuf, base, TM))
    pos = route_ref[:, 0:1]
    perm_t = (lax.broadcasted_iota(i32, (TM, TM), 1).astype(f32) == pos).astype(bf16)
    return jnp.dot(perm_t, y_hi, preferred_element_type=f32) + jnp.dot(perm_t, y_lo, preferred_element_type=f32)


_COMBINE_SCRATCH = [pltpu.VMEM((2 * TILE_ROWS, 128), f32), pltpu.SemaphoreType.DMA((2,))]


def _even_out_kernel(xp_ref, xs_ref, oap_ref, oas_ref, obp_ref, obs_ref, mod_ref, w_ref, nrm_ref, wr_ref, br_ref,
                     x1_ref, route_ref, cnt_ref, hs_hbm, gs_hbm, hbuf, gbuf, run, sem):
    i = pl.program_id(0)
    is_p = i < NPT
    x = jnp.where(is_p, xp_ref[...], xs_ref[...])
    oa = jnp.where(is_p, oap_ref[...], oas_ref[...])
    ob = jnp.where(is_p, obp_ref[...], obs_ref[...])
    mod = mod_ref[0]
    out = (jnp.dot(oa, w_ref[0:W_A, :], preferred_element_type=f32)
           + jnp.dot(ob, w_ref[W_A:W_A + W_B, :], preferred_element_type=f32))
    x1 = x + mod[:, 2 * D:3 * D] * out
    x1_ref[...] = x1
    _route_and_dispatch(i, x1, mod, nrm_ref, wr_ref, br_ref, route_ref, cnt_ref, hs_hbm, gs_hbm, hbuf, gbuf, run, sem)


def _even_out(xp, xs, oa_p, oa_s, ob_p, ob_s, mod_l, w_out, nrm, wr, br):
    ptile = lambda w: pl.BlockSpec((TM, w), lambda i: (jnp.minimum(i, NPT - 1), 0))
    stile = lambda w: pl.BlockSpec((TM, w), lambda i: (jnp.maximum(i - NPT, 0), 0))
    return pl.pallas_call(
        _even_out_kernel,
        grid=(NT,),
        in_specs=[
            ptile(D), stile(D), ptile(W_A), stile(W_A), ptile(W_B), stile(W_B),
            pl.BlockSpec((1, 1, N_MOD * D), lambda i: (_mod_row(i), 0, 0)),
            pl.BlockSpec((W_A + W_B, D), lambda i: (0, 0)),
        ] + _router_in_specs(),
        out_specs=_dispatch_out_specs(),
        out_shape=_DISPATCH_OUT_SHAPES,
        scratch_shapes=_DISPATCH_SCRATCH,
        compiler_params=_cp(("arbitrary",)),
        name="even_out_router",
    )(xp, xs, oa_p, oa_s, ob_p, ob_s, mod_l, w_out, nrm, wr, br)


def _gelu_tanh(x):
    return 0.5 * x * (1.0 + jnp.tanh(math.sqrt(2.0 / math.pi) * (x + 0.044715 * (x * x * x))))


def _odd_kernel(n_ref, s_ref, x_ref, proute_ref, y_hbm, modp_ref, mod_ref, nm_ref, win_ref, lng_ref, lnb_ref,
                ws_ref, bs_ref, wout_ref, nrm_ref, wr_ref, br_ref,
                x1_ref, route_ref, cnt_ref, hs_hbm, gs_hbm, gated, ybuf, ysem, hbuf, gbuf, run, sem):
    i = pl.program_id(0)
    modp = modp_ref[0]
    mod = mod_ref[0]
    y = _combine_tile(i, i, NT, n_ref, s_ref, proute_ref, y_hbm, ybuf, ysem)
    x = x_ref[...] + modp[:, 5 * D:6 * D] * y
    h = _rms(x, nm_ref[...]) * (1.0 + mod[:, D:2 * D]) + mod[:, 0:D]
    z = _gelu_tanh(_dot(h, win_ref[...]))
    u = z[:, 0:W_C]
    v = z[:, W_C:2 * W_C]
    mu = jnp.mean(v, axis=-1, keepdims=True)
    vc = v - mu
    var = jnp.mean(vc * vc, axis=-1, keepdims=True)
    vn = (vc * lax.rsqrt(var + EPS) * lng_ref[...] + lnb_ref[...]).astype(bf16)
    gw = W_C // C_GROUPS
    for ck in range(TM // C_CHUNK):
        rs = slice(ck * C_CHUNK, (ck + 1) * C_CHUNK)
        for g in range(C_GROUPS):
            cs = slice(g * gw, (g + 1) * gw)
            mixed = jnp.dot(ws_ref[g], vn[rs, cs], preferred_element_type=f32) + bs_ref[g]
            gated[rs, cs] = (u[rs, cs] * mixed).astype(bf16)
    out = jnp.dot(gated[...], wout_ref[...], preferred_element_type=f32)
    x1 = x + mod[:, 2 * D:3 * D] * out
    x1_ref[...] = x1
    _route_and_dispatch(i, x1, mod, nrm_ref, wr_ref, br_ref, route_ref, cnt_ref, hs_hbm, gs_hbm, hbuf, gbuf, run, sem)


def _odd_mixer(run_n, run_src, x, prev_route, y, mod_prev, mod_l, nm, w_in, ln_g, ln_b, ws, bs, w_out, nrm, wr, br):
    tile = lambda w: pl.BlockSpec((TM, w), lambda i, *_: (i, 0))
    modspec = pl.BlockSpec((1, 1, N_MOD * D), lambda i, *_: (_mod_row(i), 0, 0))
    const = lambda shp: pl.BlockSpec(shp, lambda i, *_: tuple(0 for _ in shp))
    grid_spec = pltpu.PrefetchScalarGridSpec(
        num_scalar_prefetch=2,
        grid=(NT,),
        in_specs=[tile(D), tile(128), pl.BlockSpec(memory_space=pl.ANY), modspec, modspec, const((1, D)),
                  const((D, 2 * W_C)), const((1, W_C)), const((1, W_C)), const((C_GROUPS, C_CHUNK, C_CHUNK)),
                  const((C_GROUPS, C_CHUNK, W_C // C_GROUPS)), const((W_C, D))] + _router_in_specs(),
        out_specs=_dispatch_out_specs(),
        scratch_shapes=[pltpu.VMEM((TM, W_C), bf16)] + _COMBINE_SCRATCH + _DISPATCH_SCRATCH,
    )
    return pl.pallas_call(
        _odd_kernel,
        grid_spec=grid_spec,
        out_shape=_DISPATCH_OUT_SHAPES,
        compiler_params=_cp(("arbitrary",)),
        name="odd_mixer_router",
    )(run_n, run_src, x, prev_route, y, mod_prev, mod_l, nm, w_in, ln_g, ln_b, ws, bs, w_out, nrm, wr, br)


def _expert_kernel(nu_ref, tg_ref, tb_ref, hs_ref, gs_ref, wg_ref, wu_ref, wd_ref, y_ref):
    t = pl.program_id(0)

    @pl.when(t < nu_ref[0])
    def _():
        xb = _load_rows(hs_ref, 0, TMG).astype(bf16)
        gates = gs_ref[...]
        acc = jnp.zeros((TMG, D), f32)
        for e in range(E_PER_GROUP):
            a = _dot(xb, wg_ref[e])
            b = _dot(xb, wu_ref[e])
            acc = acc + _dot(_silu(a) * b * gates[:, e:e + 1], wd_ref[e])
        _store_rows(y_ref, 0, acc)


def _experts(layer, n_used, tile_g, tile_blk, hs, gs, wg, wu, wd):
    wspec = lambda a, b: pl.BlockSpec((None, None, E_PER_GROUP, a, b), lambda t, nu, tg, tb: (layer, tg[t], 0, 0, 0))
    grid_spec = pltpu.PrefetchScalarGridSpec(
        num_scalar_prefetch=3,
        grid=(MAXT_G,),
        in_specs=[
            pl.BlockSpec((TMG * ROW_SUB, 128), lambda t, nu, tg, tb: (tb[t], 0)),
            pl.BlockSpec((TMG, 128), lambda t, nu, tg, tb: (tb[t], 0)),
            wspec(D, D_EXPERT), wspec(D, D_EXPERT), wspec(D_EXPERT, D),
        ],
        out_specs=pl.BlockSpec((TMG * ROW_SUB, 128), lambda t, nu, tg, tb: (tb[t], 0)),
    )
    return pl.pallas_call(
        _expert_kernel,
        grid_spec=grid_spec,
        out_shape=jax.ShapeDtypeStruct((N_GROUPS * RG * ROW_SUB, 128), f32),
        compiler_params=_cp(("arbitrary",), VMEM_LIMIT_EXPERTS),
        name="experts",
    )(n_used, tile_g, tile_blk, hs, gs, wg, wu, wd)


def _dispatch_plan(cnt_tiles):
    cnt = cnt_tiles[:, 0, 0:N_GROUPS].astype(i32)
    ends = jnp.cumsum(cnt, axis=0)
    run_src = (ends - cnt + jnp.arange(N_GROUPS, dtype=i32)[None, :] * RG).reshape(-1)
    totals = ends[NT - 1]
    tiles_g = (totals + TMG - 1) // TMG
    tile_end = jnp.cumsum(tiles_g)
    n_used = tile_end[N_GROUPS - 1:]
    t = jnp.minimum(jnp.arange(MAXT_G, dtype=i32), n_used[0] - 1)
    tile_g = jnp.sum((tile_end[None, :] <= t[:, None]).astype(i32), axis=1)
    tile_start = tile_end - tiles_g
    first = jnp.sum(jnp.where(tile_g[:, None] == jnp.arange(N_GROUPS, dtype=i32)[None, :], tile_start[None, :], 0), axis=1)
    tile_blk = tile_g * (RG // TMG) + t - first
    return cnt.reshape(-1), run_src, n_used, tile_g, tile_blk


def _final_kernel(n_ref, s_ref, x_ref, route_ref, y_hbm, mod_ref, fn_ref, o_ref, ybuf, ysem, *, n_tiles, tile_off):
    i = pl.program_id(0)
    mod = mod_ref[0]
    y = _combine_tile(i, tile_off + i, n_tiles, n_ref, s_ref, route_ref, y_hbm, ybuf, ysem)
    x = x_ref[...] + mod[:, 5 * D:6 * D] * y
    o_ref[...] = _rms(x, fn_ref[...])


def _final(run_n, run_src, x, route, y, mod_l, fnorm, *, n_tiles, tile_off):
    grid_spec = pltpu.PrefetchScalarGridSpec(
        num_scalar_prefetch=2,
        grid=(n_tiles,),
        in_specs=[
            pl.BlockSpec((TM, D), lambda i, *_: (tile_off + i, 0)),
            pl.BlockSpec((TM, 128), lambda i, *_: (tile_off + i, 0)),
            pl.BlockSpec(memory_space=pl.ANY),
            pl.BlockSpec((1, 1, N_MOD * D), lambda i, *_: (_mod_row(tile_off + i), 0, 0)),
            pl.BlockSpec((1, D), lambda i, *_: (0, 0)),
        ],
        out_specs=pl.BlockSpec((TM, D), lambda i, *_: (i, 0)),
        scratch_shapes=_COMBINE_SCRATCH,
    )
    return pl.pallas_call(
        functools.partial(_final_kernel, n_tiles=n_tiles, tile_off=tile_off),
        grid_spec=grid_spec,
        out_shape=jax.ShapeDtypeStruct((n_tiles * TM, D), f32),
        compiler_params=_cp(("arbitrary",)),
        name="final_norm",
    )(run_n, run_src, x, route, y, mod_l, fnorm)


def _rope_tables():
    rows = DEC_SEQ // GRID_W
    r, c = jnp.meshgrid(jnp.arange(rows), jnp.arange(GRID_W), indexing="ij")
    pos = jnp.stack([r.reshape(-1), c.reshape(-1)], axis=-1).astype(f32)
    inv = ROPE_BASE ** (-jnp.arange(AXIS_FREQS, dtype=f32) / AXIS_FREQS)
    ang = pos[:, :, None] * inv
    cos, sin = jnp.cos(ang), jnp.sin(ang)
    cos_h = jnp.concatenate([cos, cos], axis=-1).reshape(DEC_SEQ, DH_A)
    sin_h = jnp.concatenate([-sin, sin], axis=-1).reshape(DEC_SEQ, DH_A)
    cos_t = jnp.concatenate([jnp.ones((TM, 2 * DH_A), f32), jnp.tile(cos_h, (1, 2))], axis=0)
    sin_t = jnp.concatenate([jnp.zeros((TM, 2 * DH_A), f32), jnp.tile(sin_h, (1, 2))], axis=0)
    return cos_t, sin_t


def _gate_lanes(p):
    out = jnp.zeros((128,), f32)
    for d in range(2):
        out = out.at[d * 2 * H_B:d * 2 * H_B + H_B].set(p[d])
    return out.reshape(1, 128)


def _router_weights(w_rg, b_rg, w_re, b_re):
    wr = jnp.zeros((D, 256), f32).at[:, 0:N_GROUPS].set(w_rg).at[:, 128:128 + N_EXPERTS].set(w_re)
    br = jnp.zeros((1, 256), f32).at[0, 0:N_GROUPS].set(b_rg).at[0, 128:128 + N_EXPERTS].set(b_re)
    return jnp.stack(_hi_lo(wr)), br


def kernel(x_prompt, x_sample, cache_k, cache_v, state_fwd, state_bwd, c, c_ctx, norm_mix, norm_ffn, w_mod, b_mod,
           w_in_even, conv_w, a_log, dt_bias, delta_norm, lam_q, lam_k, subln, w_out_even, w_in_odd, sgu_ln_g,
           sgu_ln_b, w_spatial, b_spatial, w_out_odd, w_router_group, b_router_group, w_router_expert,
           b_router_expert, w_exp_gate, w_exp_up, w_exp_down, final_norm):
    xp = x_prompt.reshape(N_P, D)
    xs = x_sample.reshape(N_S, D)
    cond = jnp.concatenate([c, c_ctx[None, :], jnp.zeros((COND_ROWS - DEC_BATCH - 1, D), f32)], axis=0)
    mod = _modulation(cond, w_mod, b_mod)
    mod0 = mod[0].reshape(COND_ROWS, 1, N_MOD * D)
    mod1 = mod[1].reshape(COND_ROWS, 1, N_MOD * D)

    lam_init = 0.8 - 0.6 * math.exp(-0.3 * 0)
    w_in_pad = jnp.pad(w_in_even[0], ((0, 0), (0, IN_EVEN_PAD - IN_EVEN))).astype(bf16)
    cos_t, sin_t = _rope_tables()
    qa, ka, va, qkvb, gate_b, ab = _even_in(xp, xs, mod0, norm_mix[0:1], w_in_pad, cos_t, sin_t)

    ctx_k = cache_k[:, 0].reshape(DEC_BATCH * PAST, W_A)
    ctx_v = cache_v[:, 0].reshape(DEC_BATCH * PAST, W_A)
    attn = functools.partial(_attention, qa, ka, va, lam_q[0], lam_k[0], subln[0:1], lam_init)
    oa_p = attn(seq=SEQ, tq=SEQ, tok_off=0)
    oa_s = attn(seq=DEC_SEQ, tq=512, tok_off=N_P, ctx=(ctx_k, ctx_v))

    conv_pad = jnp.pad(conv_w[0], ((0, 8 - CONV_K), (0, 0)))
    dl = functools.partial(_delta, qkvb, gate_b, ab, conv_pad, _gate_lanes(a_log[0]), _gate_lanes(dt_bias[0]),
                           delta_norm[0:1])
    ob_p, s_f, s_b = dl(seq=SEQ, tok_off=0)
    ob_s, _, _ = dl(seq=DEC_SEQ, tok_off=N_P, init=(state_fwd[:, 0], state_bwd[:, 0]))

    wr0, br0 = _router_weights(w_router_group[0], b_router_group[0], w_router_expert[0], b_router_expert[0])
    x1, route0, cnt0, hs, gs = _even_out(xp, xs, oa_p, oa_s, ob_p, ob_s, mod0, w_out_even[0].astype(bf16),
                                         norm_ffn[0:1], wr0, br0)

    wshape = (DEPTH, N_GROUPS, E_PER_GROUP)
    wg = w_exp_gate.reshape(wshape + (D, D_EXPERT))
    wu = w_exp_up.reshape(wshape + (D, D_EXPERT))
    wd = w_exp_down.reshape(wshape + (D_EXPERT, D))
    run_n0, run_src0, n_used, tile_g, tile_blk = _dispatch_plan(cnt0)
    y0 = _experts(0, n_used, tile_g, tile_blk, hs, gs, wg, wu, wd)

    bs = jnp.broadcast_to(b_spatial[0][:, :, None], (C_GROUPS, C_CHUNK, W_C // C_GROUPS))
    wr1, br1 = _router_weights(w_router_group[1], b_router_group[1], w_router_expert[1], b_router_expert[1])
    x2, route1, cnt1, hs, gs = _odd_mixer(run_n0, run_src0, x1, route0, y0, mod0, mod1, norm_mix[1:2],
                                          w_in_odd[0].astype(bf16), sgu_ln_g[0:1], sgu_ln_b[0:1],
                                          w_spatial[0].astype(bf16), bs, w_out_odd[0].astype(bf16), norm_ffn[1:2],
                                          wr1, br1)
    run_n1, run_src1, n_used, tile_g, tile_blk = _dispatch_plan(cnt1)
    y1 = _experts(1, n_used, tile_g, tile_blk, hs, gs, wg, wu, wd)

    fin = functools.partial(_final, run_n1, run_src1, x2, route1, y1, mod1, final_norm[None, :])
    y_prompt = fin(n_tiles=NPT, tile_off=0)
    y_sample = fin(n_tiles=NT - NPT, tile_off=NPT)

    new_cache_k = ka[:N_P].reshape(BATCH, 1, SEQ, H_A, 2 * DH_A)
    new_cache_v = va[:N_P].reshape(BATCH, 1, SEQ, H_A, 2 * DH_A)
    return (y_prompt.reshape(BATCH, SEQ, D), y_sample.reshape(DEC_BATCH, DEC_SEQ, D), new_cache_k, new_cache_v,
            s_f[:, None], s_b[:, None])
```

```python
import functools
import math

import jax
import jax.numpy as jnp
from jax import lax
from jax.experimental import pallas as pl
from jax.experimental.pallas import tpu as pltpu

f32 = jnp.float32
bf16 = jnp.bfloat16
i32 = jnp.int32
HIGHEST = lax.Precision.HIGHEST

D = 1024
BATCH, SEQ = 16, 256
DEC_BATCH, DEC_SEQ, PAST = 8, 1024, 512
DEPTH = 2
GRID_W = 64
N_MOD = 6
EPS = 1e-6
H_A, DH_A = 4, 64
W_A = H_A * 2 * DH_A
ROPE_BASE = 10000.0
AXIS_FREQS = DH_A // 4
H_B, DK_B, DV_B = 4, 128, 128
W_B = H_B * DV_B
CONV_K = 5
CH = 64
IN_EVEN = 3 * W_A + 4 * W_B + 4 * H_B
IN_EVEN_PAD = 3712
W_C = D
C_GROUPS, C_CHUNK = 4, 128
N_GROUPS, E_PER_GROUP, D_EXPERT = 4, 8, 256
N_EXPERTS = N_GROUPS * E_PER_GROUP

N_P = BATCH * SEQ
N_S = DEC_BATCH * DEC_SEQ
N_TOK = N_P + N_S
TM = 256
NPT = N_P // TM
NT = N_TOK // TM
COND_ROWS = 16

VMEM_LIMIT = 56 * 1024 * 1024
VMEM_LIMIT_EXPERTS = 62 * 1024 * 1024


def _cp(sem, vmem_limit=VMEM_LIMIT):
    return pltpu.CompilerParams(dimension_semantics=sem, vmem_limit_bytes=vmem_limit)


def _mod_row(i):
    return jnp.where(i < NPT, DEC_BATCH, (i - NPT) // (DEC_SEQ // TM))


def _rms(x, g):
    return x * lax.rsqrt(jnp.mean(x * x, axis=-1, keepdims=True) + EPS) * g


def _silu(x):
    return x * jax.nn.sigmoid(x)


def _dot(a, b):
    return jnp.dot(a.astype(bf16), b.astype(bf16), preferred_element_type=f32)


ROW_SUB = D // 128


def _store_rows(ref, base, x):
    rows = x.shape[0]
    for j in range(ROW_SUB):
        ref[pl.ds(base + j, rows, stride=ROW_SUB), :] = x[:, j * 128:(j + 1) * 128]


def _load_rows(ref, base, rows):
    return jnp.concatenate([ref[pl.ds(base + j, rows, stride=ROW_SUB), :] for j in range(ROW_SUB)], axis=1)


TN_MOD = 1536


def _mod_kernel(c_ref, w_ref, b_ref, o_ref):
    a = _silu(c_ref[...])
    o_ref[0] = jnp.dot(a, w_ref[0], precision=HIGHEST, preferred_element_type=f32) + b_ref[0]


def _modulation(cond, w_mod, b_mod):
    return pl.pallas_call(
        _mod_kernel,
        grid=(DEPTH, N_MOD * D // TN_MOD),
        in_specs=[
            pl.BlockSpec((COND_ROWS, D), lambda l, j: (0, 0)),
            pl.BlockSpec((1, D, TN_MOD), lambda l, j: (l, 0, j)),
            pl.BlockSpec((1, 1, TN_MOD), lambda l, j: (l, 0, j)),
        ],
        out_specs=pl.BlockSpec((1, COND_ROWS, TN_MOD), lambda l, j: (l, 0, j)),
        out_shape=jax.ShapeDtypeStruct((DEPTH, COND_ROWS, N_MOD * D), f32),
        compiler_params=_cp(("arbitrary", "arbitrary")),
        name="modulation",
    )(cond, w_mod, b_mod.reshape(DEPTH, 1, N_MOD * D))


def _even_in_kernel(xp_ref, xs_ref, mod_ref, nrm_ref, w_ref, cos_ref, sin_ref,
                    q_ref, k_ref, v_ref, qkvb_ref, gate_ref, ab_ref):
    i = pl.program_id(0)
    x = jnp.where(i < NPT, xp_ref[...], xs_ref[...])
    mod = mod_ref[0]
    h = _rms(x, nrm_ref[...]) * (1.0 + mod[:, D:2 * D]) + mod[:, 0:D]
    proj = _dot(h, w_ref[...])
    cos = cos_ref[...]
    sin = sin_ref[...]
    lane = lax.broadcasted_iota(i32, (TM, 2 * DH_A), 1)
    first = (lane % (2 * AXIS_FREQS)) < AXIS_FREQS

    def rope(xh):
        partner = jnp.where(first, pltpu.roll(xh, 2 * DH_A - AXIS_FREQS, 1), pltpu.roll(xh, AXIS_FREQS, 1))
        return xh * cos + partner * sin

    for hh in range(H_A):
        sl = slice(hh * 2 * DH_A, (hh + 1) * 2 * DH_A)
        q_ref[:, sl] = (rope(proj[:, sl]) * (DH_A ** -0.5)).astype(bf16)
        k_ref[:, sl] = rope(proj[:, W_A + hh * 2 * DH_A:W_A + (hh + 1) * 2 * DH_A])
    v_ref[...] = proj[:, 2 * W_A:3 * W_A]
    qkvb_ref[...] = proj[:, 3 * W_A:3 * W_A + 3 * W_B]
    gate_ref[...] = proj[:, 3 * W_A + 3 * W_B:3 * W_A + 4 * W_B]
    ab_ref[...] = proj[:, 3 * W_A + 4 * W_B:IN_EVEN_PAD]


def _even_in(xp, xs, mod_l, nrm, w_in_pad, cos_t, sin_t):
    tile = lambda w: pl.BlockSpec((TM, w), lambda i: (i, 0))
    return pl.pallas_call(
        _even_in_kernel,
        grid=(NT,),
        in_specs=[
            pl.BlockSpec((TM, D), lambda i: (jnp.minimum(i, NPT - 1), 0)),
            pl.BlockSpec((TM, D), lambda i: (jnp.maximum(i - NPT, 0), 0)),
            pl.BlockSpec((1, 1, N_MOD * D), lambda i: (_mod_row(i), 0, 0)),
            pl.BlockSpec((1, D), lambda i: (0, 0)),
            pl.BlockSpec((D, IN_EVEN_PAD), lambda i: (0, 0)),
            pl.BlockSpec((TM, 2 * DH_A), lambda i: (jnp.where(i < NPT, 0, 1 + (i - NPT) % (DEC_SEQ // TM)), 0)),
            pl.BlockSpec((TM, 2 * DH_A), lambda i: (jnp.where(i < NPT, 0, 1 + (i - NPT) % (DEC_SEQ // TM)), 0)),
        ],
        out_specs=[tile(W_A), tile(W_A), tile(W_A), tile(3 * W_B), tile(W_B), tile(128)],
        out_shape=[
            jax.ShapeDtypeStruct((N_TOK, W_A), bf16),
            jax.ShapeDtypeStruct((N_TOK, W_A), f32),
            jax.ShapeDtypeStruct((N_TOK, W_A), f32),
            jax.ShapeDtypeStruct((N_TOK, 3 * W_B), f32),
            jax.ShapeDtypeStruct((N_TOK, W_B), f32),
            jax.ShapeDtypeStruct((N_TOK, 128), f32),
        ],
        compiler_params=_cp(("arbitrary",)),
        name="even_in",
    )(xp, xs, mod_l, nrm, w_in_pad, cos_t, sin_t)


def _attn_kernel(*refs, lam_init, has_ctx):
    if has_ctx:
        q_ref, kn_ref, vn_ref, kc_ref, vc_ref, lq_ref, lk_ref, sub_ref, o_ref = refs
    else:
        q_ref, kn_ref, vn_ref, lq_ref, lk_ref, sub_ref, o_ref = refs
    tq = q_ref.shape[0]
    lq = lq_ref[...]
    lk = lk_ref[...]
    prod = lq * lk
    lam = (jnp.exp(jnp.sum(prod[0:1], axis=-1, keepdims=True))
           - jnp.exp(jnp.sum(prod[1:2], axis=-1, keepdims=True)) + lam_init)
    lane = lax.broadcasted_iota(i32, (tq, 2 * DH_A), 1)
    nt = (((1,), (1,)), ((), ()))
    for hh in range(H_A):
        sl = slice(hh * 2 * DH_A, (hh + 1) * 2 * DH_A)
        q = q_ref[:, sl]
        srcs = [(kn_ref[:, sl].astype(bf16), vn_ref[:, sl].astype(bf16))]
        if has_ctx:
            srcs.append((kc_ref[:, sl].astype(bf16), vc_ref[:, sl].astype(bf16)))
        outs = []
        for part in range(2):
            qm = jnp.where((lane < DH_A) if part == 0 else (lane >= DH_A), q, jnp.zeros_like(q))
            ss = [lax.dot_general(qm, k, nt, preferred_element_type=f32) for k, _ in srcs]
            m = ss[0].max(axis=-1, keepdims=True)
            for s in ss[1:]:
                m = jnp.maximum(m, s.max(axis=-1, keepdims=True))
            l = jnp.zeros((tq, 1), f32)
            o = jnp.zeros((tq, 2 * DH_A), f32)
            for s, (_, v) in zip(ss, srcs):
                e = jnp.exp(s - m)
                l = l + jnp.sum(e, axis=-1, keepdims=True)
                o = o + jnp.dot(e.astype(bf16), v, preferred_element_type=f32)
            outs.append(o / l)
        o = outs[0] - lam * outs[1]
        o_ref[:, sl] = (_rms(o, sub_ref[...]) * (1.0 - lam_init)).astype(bf16)


def _attention(q, ka, va, lam_q, lam_k, subln, lam_init, *, seq, tq, tok_off, ctx=None):
    nq = seq // tq
    nb = (N_P if ctx is None else N_S) // seq
    qo, ko = tok_off // tq, tok_off // seq
    in_specs = [
        pl.BlockSpec((tq, W_A), lambda b, j: (qo + b * nq + j, 0)),
        pl.BlockSpec((seq, W_A), lambda b, j: (ko + b, 0)),
        pl.BlockSpec((seq, W_A), lambda b, j: (ko + b, 0)),
    ]
    args = [q, ka, va]
    if ctx is not None:
        in_specs += [pl.BlockSpec((PAST, W_A), lambda b, j: (b, 0))] * 2
        args += list(ctx)
    in_specs += [pl.BlockSpec((2, DH_A), lambda b, j: (0, 0))] * 2 + [pl.BlockSpec((1, 2 * DH_A), lambda b, j: (0, 0))]
    args += [lam_q, lam_k, subln]
    return pl.pallas_call(
        functools.partial(_attn_kernel, lam_init=lam_init, has_ctx=ctx is not None),
        grid=(nb, nq),
        in_specs=in_specs,
        out_specs=pl.BlockSpec((tq, W_A), lambda b, j: (b * nq + j, 0)),
        out_shape=jax.ShapeDtypeStruct((nb * seq, W_A), bf16),
        compiler_params=_cp(("arbitrary", "arbitrary")),
        name="diff_attn_ctx" if ctx is not None else "diff_attn",
    )(*args)


def _softplus(x):
    return jnp.maximum(x, 0.0) + jnp.log1p(jnp.exp(-jnp.abs(x)))


RW = H_B * CH
CPI = 4


def _rep_rows(x, times):
    return jnp.concatenate([x] * times, axis=0)


def _tri_inv(lmats, bd, ii, jj):
    b16 = (ii // 16) == (jj // 16)
    b32 = (ii // 32) == (jj // 32)
    eye = (ii == jj).astype(f32)
    n = len(lmats)
    mm = lambda a, b: jnp.dot(a.astype(bf16), bd(b), preferred_element_type=f32)
    l0 = [jnp.where(b16, l, 0.0) for l in lmats]
    x = [eye - a for a in l0]
    p = [mm(a, a) for a in l0]
    for level in range(3):
        xp = [mm(x[i], p[i]) for i in range(n)]
        if level < 2:
            p = [mm(a, a) for a in p]
        x = [x[i] + xp[i] for i in range(n)]
    for off_diag in (jnp.logical_and(b32, jnp.logical_not(b16)), jnp.logical_not(b32)):
        t = [mm(jnp.where(off_diag, lmats[i], 0.0), x[i]) for i in range(n)]
        t = [mm(x[i], t[i]) for i in range(n)]
        x = [x[i] - t[i] for i in range(n)]
    return x


def _delta_kernel(*refs, seq, has_init):
    if has_init:
        (qkv_ref, gate_ref, ab_ref, cw_ref, al_ref, dt_ref, dn_ref, sf0_ref, sb0_ref,
         ob_ref, sf_ref, sb_ref, xpad, xq, xk, xv, gacc, bacc, o_f, o_b, st, pu, pw, pqe, pkd, patt, ptot) = refs
    else:
        (qkv_ref, gate_ref, ab_ref, cw_ref, al_ref, dt_ref, dn_ref,
         ob_ref, sf_ref, sb_ref, xpad, xq, xk, xv, gacc, bacc, o_f, o_b, st, pu, pw, pqe, pkd, patt, ptot) = refs
    n = seq // CH
    win = CH + 16

    xpad[0:8, :] = jnp.zeros((8, 3 * W_B), f32)
    xpad[seq + 8:seq + 16, :] = jnp.zeros((8, 3 * W_B), f32)
    xpad[8:seq + 8, :] = qkv_ref[...]
    neg_a = -jnp.exp(al_ref[...])
    dtb = dt_ref[...]

    def pre(c, carry):
        r0 = pl.multiple_of(c * CH, CH)
        a = xpad[pl.ds(r0, win), :]
        y = jnp.zeros((CH, 3 * W_B), f32)
        for j in range(CONV_K):
            sh = ((CONV_K - 1) // 2 - j) % win
            y = y + pltpu.roll(a, sh, 0)[8:8 + CH] * cw_ref[j:j + 1, :]
        y = _silu(y)
        for hh in range(H_B):
            qh = y[:, hh * DK_B:(hh + 1) * DK_B]
            kh = y[:, W_B + hh * DK_B:W_B + (hh + 1) * DK_B]
            qn = qh * lax.rsqrt(jnp.sum(qh * qh, axis=-1, keepdims=True) + EPS) * (DK_B ** -0.5)
            kn = kh * lax.rsqrt(jnp.sum(kh * kh, axis=-1, keepdims=True) + EPS)
            xq[pl.ds(r0, CH), hh * DK_B:(hh + 1) * DK_B] = qn
            xk[pl.ds(r0, CH), hh * DK_B:(hh + 1) * DK_B] = kn
        xv[pl.ds(r0, CH), :] = y[:, 2 * W_B:3 * W_B]
        ab = ab_ref[pl.ds(r0, CH), :]
        gacc[pl.ds(r0, CH), :] = neg_a * _softplus(ab + dtb)
        bacc[pl.ds(r0, CH), :] = jax.nn.sigmoid(ab)
        return carry

    lax.fori_loop(0, n, pre, 0)

    ii = lax.broadcasted_iota(i32, (CH, RW), 0)
    lane_r = lax.broadcasted_iota(i32, (CH, RW), 1)
    jj = lane_r % CH
    hb = lane_r // CH
    rowl = lax.broadcasted_iota(i32, (CH, 128), 0)
    bd_mask = (lax.broadcasted_iota(i32, (RW, RW), 0) // CH) == (lax.broadcasted_iota(i32, (RW, RW), 1) // CH)
    wide_mask = ((lax.broadcasted_iota(i32, (RW, W_B), 0) // CH)
                 == (lax.broadcasted_iota(i32, (RW, W_B), 1) // DK_B))
    pair_mask = ((lax.broadcasted_iota(i32, (2 * DK_B, 2 * DV_B), 0) // DK_B)
                 == (lax.broadcasted_iota(i32, (2 * DK_B, 2 * DV_B), 1) // DV_B))
    zero_b = jnp.zeros((), bf16)

    def bd(x):
        return jnp.where(bd_mask, _rep_rows(x.astype(bf16), H_B), zero_b)

    def bd_wide(x):
        return jnp.where(wide_mask, _rep_rows(x.astype(bf16), H_B), zero_b)

    def per_head(src, lanes):
        return jnp.concatenate([jnp.broadcast_to(src[:, ln:ln + 1], (src.shape[0], DK_B)) for ln in lanes], axis=1)

    nt = (((1,), (1,)), ((), ()))
    tn = (((0,), (0,)), ((), ()))

    def local(it, carry):
        chains = []
        for j in range(CPI):
            c = it * CPI + j
            r0 = pl.multiple_of(c * CH, CH)
            g = gacc[pl.ds(r0, CH), :]
            beta = bacc[pl.ds(r0, CH), :]
            pre_g = g
            for s in (1, 2, 4, 8, 16, 32):
                pre_g = pre_g + jnp.where(rowl >= s, pltpu.roll(pre_g, s, 0), 0.0)
            tot = pre_g[CH - 1:CH, :]
            q_all = xq[pl.ds(r0, CH), :]
            k_all = xk[pl.ds(r0, CH), :]
            v_all = xv[pl.ds(r0, CH), :]
            k_bd = bd_wide(k_all)
            for d in range(2):
                gc = pre_g if d == 0 else tot - pre_g + g
                lanes = [d * 2 * H_B + hh for hh in range(H_B)]
                gct = _rep_rows(gc, H_B).T
                g_col = jnp.zeros((CH, RW), f32)
                g_row = jnp.zeros((1, RW), f32)
                for hh, ln in enumerate(lanes):
                    g_col = jnp.where(hb == hh, gc[:, ln:ln + 1], g_col)
                    g_row = jnp.where(hb[0:1] == hh, gct[ln:ln + 1, :], g_row)
                g_wide = per_head(gc, lanes)
                b_wide = per_head(beta, [ln + H_B for ln in lanes])
                incl = (ii >= jj) if d == 0 else (ii <= jj)
                strict = (ii > jj) if d == 0 else (ii < jj)
                eg = jnp.exp(g_wide)
                kb = k_all * b_wide
                chains.append(dict(
                    ci=c * 2 + d, incl=incl, strict=strict, k_bd=k_bd, kb=kb, q=q_all, tot=tot,
                    dec=jnp.where(incl, jnp.exp(jnp.where(incl, g_col - g_row, 0.0)), 0.0),
                    vb=v_all * b_wide, kbe=kb * eg, qe=q_all * eg,
                    kd=k_all * jnp.exp(per_head(tot, lanes) - g_wide)))
        kk = [lax.dot_general(e["kb"].astype(bf16), e["k_bd"], nt, preferred_element_type=f32) for e in chains]
        qk = [lax.dot_general(e["q"].astype(bf16), e["k_bd"], nt, preferred_element_type=f32) for e in chains]
        lmat = [jnp.where(e["strict"], kk[i] * e["dec"], 0.0) for i, e in enumerate(chains)]
        tinv = _tri_inv(lmat, bd, ii, jj)
        u = [jnp.dot(tinv[i].astype(bf16), bd_wide(e["vb"]), preferred_element_type=f32) for i, e in enumerate(chains)]
        w = [jnp.dot(tinv[i].astype(bf16), bd_wide(e["kbe"]), preferred_element_type=f32) for i, e in enumerate(chains)]
        for i, e in enumerate(chains):
            ci = e["ci"]
            pu[ci] = u[i]
            pw[ci] = w[i].astype(bf16)
            pqe[ci] = e["qe"].astype(bf16)
            pkd[ci] = e["kd"].astype(bf16)
            patt[ci] = jnp.where(e["incl"], qk[i] * e["dec"], 0.0).astype(bf16)
            ptot[ci] = jnp.broadcast_to(e["tot"], (8, 128))
        return carry

    lax.fori_loop(0, n // CPI, local, 0)

    for hh in range(H_B):
        rows = slice(hh * DK_B, (hh + 1) * DK_B)
        if has_init:
            st[0, rows, :] = sf0_ref[0, hh]
            st[1, rows, :] = sb0_ref[0, hh]
        else:
            st[0, rows, :] = jnp.zeros((DK_B, DV_B), f32)
            st[1, rows, :] = jnp.zeros((DK_B, DV_B), f32)

    def scan(it, carry):
        chains = []
        for d in range(2):
            c = it if d == 0 else n - 1 - it
            ci = c * 2 + d
            s_old = st[d]
            s_b = s_old.astype(bf16)
            chains.append(dict(
                d=d, r0=pl.multiple_of(c * CH, CH), s_old=s_old, u=pu[ci], att=patt[ci], kd=pkd[ci], tot=ptot[ci],
                lhs=jnp.concatenate([pw[ci], pqe[ci]], axis=0),
                rhs=[jnp.where(pair_mask, jnp.concatenate([s_b[p * 2 * DK_B:(p + 1) * 2 * DK_B]] * 2, axis=1), zero_b)
                     for p in range(2)]))
        ws_qs = [jnp.concatenate(
            [jnp.dot(e["lhs"][:, p * 2 * DK_B:(p + 1) * 2 * DK_B], e["rhs"][p], preferred_element_type=f32)
             for p in range(2)], axis=1) for e in chains]
        v_new = [e["u"] - ws_qs[i][0:CH] for i, e in enumerate(chains)]
        av = [jnp.dot(e["att"], bd_wide(v_new[i]), preferred_element_type=f32) for i, e in enumerate(chains)]
        su = [lax.dot_general(jnp.where(wide_mask, _rep_rows(e["kd"], H_B), zero_b),
                              jnp.concatenate([v_new[i][:, hh * DV_B:(hh + 1) * DV_B] for hh in range(H_B)],
                                              axis=0).astype(bf16),
                              tn, preferred_element_type=f32) for i, e in enumerate(chains)]
        for i, e in enumerate(chains):
            d = e["d"]
            if d == 0:
                o_f[pl.ds(e["r0"], CH), :] = ws_qs[i][CH:2 * CH] + av[i]
            else:
                o_b[pl.ds(e["r0"], CH), :] = ws_qs[i][CH:2 * CH] + av[i]
            for hh in range(H_B):
                rows = slice(hh * DK_B, (hh + 1) * DK_B)
                ln = d * 2 * H_B + hh
                st[d, rows, :] = e["s_old"][rows] * jnp.exp(e["tot"][0:1, ln:ln + 1]) + su[i][rows]
        return carry

    lax.fori_loop(0, n, scan, 0)

    def post(c, carry):
        r0 = pl.multiple_of(c * CH, CH)
        o = o_f[pl.ds(r0, CH), :] + o_b[pl.ds(r0, CH), :]
        gt = gate_ref[pl.ds(r0, CH), :]
        for hh in range(H_B):
            sl = slice(hh * DV_B, (hh + 1) * DV_B)
            ob_ref[pl.ds(r0, CH), sl] = (_rms(o[:, sl], dn_ref[...]) * _silu(gt[:, sl])).astype(bf16)
        return carry

    lax.fori_loop(0, n, post, 0)
    for hh in range(H_B):
        sf_ref[0, hh] = st[0, hh * DK_B:(hh + 1) * DK_B, :]
        sb_ref[0, hh] = st[1, hh * DK_B:(hh + 1) * DK_B, :]


def _delta(qkvb, gate, ab, conv_pad, al_lane, dt_lane, dnorm, *, seq, tok_off, init=None):
    nb = (N_P if init is None else N_S) // seq
    bo = tok_off // seq
    rows = lambda w: pl.BlockSpec((seq, w), lambda b: (bo + b, 0))
    const = lambda shp: pl.BlockSpec(shp, lambda b: tuple(0 for _ in shp))
    st_spec = pl.BlockSpec((1, H_B, DK_B, DV_B), lambda b: (b, 0, 0, 0))
    in_specs = [rows(3 * W_B), rows(W_B), rows(128), const((8, 3 * W_B)), const((1, 128)), const((1, 128)),
                const((1, DV_B))]
    args = [qkvb, gate, ab, conv_pad, al_lane, dt_lane, dnorm]
    if init is not None:
        in_specs += [st_spec, st_spec]
        args += [init[0], init[1]]
    return pl.pallas_call(
        functools.partial(_delta_kernel, seq=seq, has_init=init is not None),
        grid=(nb,),
        in_specs=in_specs,
        out_specs=[pl.BlockSpec((seq, W_B), lambda b: (b, 0)), st_spec, st_spec],
        out_shape=[
            jax.ShapeDtypeStruct((nb * seq, W_B), bf16),
            jax.ShapeDtypeStruct((nb, H_B, DK_B, DV_B), f32),
            jax.ShapeDtypeStruct((nb, H_B, DK_B, DV_B), f32),
        ],
        scratch_shapes=[
            pltpu.VMEM((seq + 16, 3 * W_B), f32),
            pltpu.VMEM((seq, W_B), f32),
            pltpu.VMEM((seq, W_B), f32),
            pltpu.VMEM((seq, W_B), f32),
            pltpu.VMEM((seq, 128), f32),
            pltpu.VMEM((seq, 128), f32),
            pltpu.VMEM((seq, W_B), f32),
            pltpu.VMEM((seq, W_B), f32),
            pltpu.VMEM((2, H_B * DK_B, DV_B), f32),
            pltpu.VMEM((2 * seq // CH, CH, W_B), f32),
            pltpu.VMEM((2 * seq // CH, CH, W_B), bf16),
            pltpu.VMEM((2 * seq // CH, CH, W_B), bf16),
            pltpu.VMEM((2 * seq // CH, CH, W_B), bf16),
            pltpu.VMEM((2 * seq // CH, CH, RW), bf16),
            pltpu.VMEM((2 * seq // CH, 8, 128), f32),
        ],
        compiler_params=_cp(("arbitrary",)),
        name="delta_ctx" if init is not None else "delta",
    )(*args)


RG = N_TOK
TMG = 256
MAXT_G = N_TOK // TMG + N_GROUPS
TILE_ROWS = TM * ROW_SUB
RUN_BITS = tuple(range(TM.bit_length() - 1, -1, -1))
SUB = 2
NS = NT // SUB
NPS = NPT // SUB


def _mod_row_step(i):
    return jnp.where(i < NPS, DEC_BATCH, (i - NPS) // (DEC_SEQ // (SUB * TM)))


def _lockstep(chains):
    live = list(chains)
    while live:
        nxt = []
        for ch in live:
            try:
                next(ch)
                nxt.append(ch)
            except StopIteration:
                pass
        live = nxt


def _hi_lo(x):
    hi = x.astype(bf16)
    return hi, (x - hi.astype(f32)).astype(bf16)


def _run_pieces(n):
    return [(1 << b, (n >> (b + 1)) << (b + 1), ((n >> b) & 1) == 1) for b in RUN_BITS]


def _route_chain(s, x1, mod, nrm_ref, wr_ref, br_ref, route_ref, cnt_ref, hbuf, gbuf, counts):
    rows = slice(s * TM, (s + 1) * TM)
    h2 = _rms(x1, nrm_ref[...]) * (1.0 + mod[:, 4 * D:5 * D]) + mod[:, 3 * D:4 * D]
    h_hi, h_lo = _hi_lo(h2)
    w_hi, w_lo = wr_ref[0], wr_ref[1]
    logits = (jnp.dot(h_hi, w_hi, preferred_element_type=f32) + jnp.dot(h_hi, w_lo, preferred_element_type=f32)
              + jnp.dot(h_lo, w_hi, preferred_element_type=f32)) + br_ref[...]
    yield
    lane = lax.broadcasted_iota(i32, (TM, 128), 1)
    lane_f = lane.astype(f32)
    neg = jnp.float32(-1e30)
    lg = jnp.where(lane < N_GROUPS, logits[:, 0:128], neg)
    mg = lg.max(axis=-1, keepdims=True)
    gidx = jnp.min(jnp.where(lg == mg, lane_f, 128.0), axis=-1, keepdims=True)
    pg = 1.0 / jnp.sum(jnp.exp(lg - mg), axis=-1, keepdims=True)
    le = jnp.where((lane // E_PER_GROUP).astype(f32) == gidx, logits[:, 128:256], neg)
    m1 = le.max(axis=-1, keepdims=True)
    i1 = jnp.min(jnp.where(le == m1, lane_f, 128.0), axis=-1, keepdims=True)
    le2 = jnp.where(lane_f == i1, neg, le)
    m2 = le2.max(axis=-1, keepdims=True)
    i2 = jnp.min(jnp.where(le2 == m2, lane_f, 128.0), axis=-1, keepdims=True)
    e2 = jnp.exp(m2 - m1)
    w1 = pg / (1.0 + e2)
    w2 = w1 * e2
    local = lane_f + gidx * E_PER_GROUP
    gates = jnp.where(local == i1, w1, jnp.where(local == i2, w2, 0.0))

    onehot = (lane_f == gidx).astype(f32)
    r = lax.broadcasted_iota(i32, (TM, TM), 0)
    c = lax.broadcasted_iota(i32, (TM, TM), 1)
    before = jnp.dot((r > c).astype(bf16), onehot.astype(bf16), preferred_element_type=f32)
    yield
    cnt = jnp.sum(onehot, axis=0, keepdims=True)
    pos = (jnp.sum(onehot * before, axis=-1, keepdims=True)
           + jnp.sum(jnp.where(lane_f < gidx, cnt, 0.0), axis=-1, keepdims=True))
    route_ref[rows, :] = jnp.where(lane == 0, pos, 0.0)
    cnt_ref[s] = cnt
    counts[s] = cnt.astype(i32)

    perm_t = (c.astype(f32) == pos).astype(bf16)
    tn = (((0,), (0,)), ((), ()))
    h_sorted = lax.dot_general(perm_t, h_hi, tn, preferred_element_type=f32)
    g_hi, g_lo = _hi_lo(gates)
    g_sorted = (lax.dot_general(perm_t, g_hi, tn, preferred_element_type=f32)
                + lax.dot_general(perm_t, g_lo, tn, preferred_element_type=f32))
    yield
    _store_rows(hbuf, s * TILE_ROWS, h_sorted)
    gbuf[s * TM:(s + 1) * TM, :] = g_sorted


def _dispatch_done(s, hs_hbm, gs_hbm, hbuf, gbuf, sem):
    pltpu.make_async_copy(hbuf.at[pl.ds(s * TILE_ROWS, TILE_ROWS), :], hs_hbm.at[pl.ds(0, TILE_ROWS), :],
                          sem.at[s]).wait()
    pltpu.make_async_copy(gbuf.at[pl.ds(s * TM, TM), :], gs_hbm.at[pl.ds(0, TM), :], sem.at[s]).wait()


def _dispatch_runs(s, cnt_i, hs_hbm, gs_hbm, hbuf, gbuf, run, sem):
    off = 0
    for g in range(N_GROUPS):
        n = cnt_i[0, g]
        dst = g * RG + run[g]
        for rows, o, pred in _run_pieces(n):
            @pl.when(pred)
            def _():
                pltpu.make_async_copy(
                    hbuf.at[pl.ds(pl.multiple_of(s * TILE_ROWS + (off + o) * ROW_SUB, ROW_SUB), rows * ROW_SUB), :],
                    hs_hbm.at[pl.ds(pl.multiple_of((dst + o) * ROW_SUB, ROW_SUB), rows * ROW_SUB), :],
                    sem.at[s]).start()
                pltpu.make_async_copy(gbuf.at[pl.ds(s * TM + off + o, rows), :], gs_hbm.at[pl.ds(dst + o, rows), :],
                                      sem.at[s]).start()
        run[g] = run[g] + n
        off = off + n


def _dispatch_tail_fill(hs_hbm, gs_hbm, hbuf, gbuf, run, sem):
    hbuf[pl.ds(0, TMG * ROW_SUB), :] = jnp.zeros((TMG * ROW_SUB, 128), f32)
    gbuf[pl.ds(0, TMG), :] = jnp.zeros((TMG, 128), f32)
    for g in range(N_GROUPS):
        tot = run[g]
        pad = (-tot) & (TMG - 1)
        for rows, o, pred in _run_pieces(pad):
            @pl.when(pred)
            def _():
                dst = g * RG + tot + o
                ch = pltpu.make_async_copy(
                    hbuf.at[pl.ds(0, rows * ROW_SUB), :],
                    hs_hbm.at[pl.ds(pl.multiple_of(dst * ROW_SUB, ROW_SUB), rows * ROW_SUB), :], sem.at[0])
                cg = pltpu.make_async_copy(gbuf.at[pl.ds(0, rows), :], gs_hbm.at[pl.ds(dst, rows), :], sem.at[1])
                ch.start()
                cg.start()
                ch.wait()
                cg.wait()


def _mixer_out_tail(i, chains, counts, hs_hbm, gs_hbm, hbuf, gbuf, run, sem):
    @pl.when(i == 0)
    def _():
        for g in range(N_GROUPS):
            run[g] = 0

    @pl.when(i >= 1)
    def _():
        for s in range(SUB):
            _dispatch_done(s, hs_hbm, gs_hbm, hbuf, gbuf, sem)

    _lockstep(chains)
    for s in range(SUB):
        _dispatch_runs(s, counts[s], hs_hbm, gs_hbm, hbuf, gbuf, run, sem)

    @pl.when(i == pl.num_programs(0) - 1)
    def _():
        for s in range(SUB):
            _dispatch_done(s, hs_hbm, gs_hbm, hbuf, gbuf, sem)
        _dispatch_tail_fill(hs_hbm, gs_hbm, hbuf, gbuf, run, sem)


def _dispatch_out_specs():
    return [
        pl.BlockSpec((SUB * TM, D), lambda i, *_: (i, 0)),
        pl.BlockSpec((SUB * TM, 128), lambda i, *_: (i, 0)),
        pl.BlockSpec((SUB, 1, 128), lambda i, *_: (i, 0, 0)),
        pl.BlockSpec(memory_space=pl.ANY),
        pl.BlockSpec(memory_space=pl.ANY),
    ]


_DISPATCH_OUT_SHAPES = [
    jax.ShapeDtypeStruct((N_TOK, D), f32),
    jax.ShapeDtypeStruct((N_TOK, 128), f32),
    jax.ShapeDtypeStruct((NT, 1, 128), f32),
    jax.ShapeDtypeStruct((N_GROUPS * RG * ROW_SUB, 128), f32),
    jax.ShapeDtypeStruct((N_GROUPS * RG, 128), f32),
]
_DISPATCH_SCRATCH = [
    pltpu.VMEM((SUB * TILE_ROWS, 128), f32),
    pltpu.VMEM((SUB * TM, 128), f32),
    pltpu.SMEM((N_GROUPS,), i32),
    pltpu.SemaphoreType.DMA((SUB,)),
]


def _router_in_specs():
    return [pl.BlockSpec((1, D), lambda i, *_: (0, 0)),
            pl.BlockSpec((2, D, 256), lambda i, *_: (0, 0, 0)),
            pl.BlockSpec((1, 256), lambda i, *_: (0, 0))]


def _fetch_runs(n_ref, s_ref, y_hbm, ybuf, sem, tile, slot):
    base = slot * TILE_ROWS
    off = 0
    for g in range(N_GROUPS):
        n = n_ref[tile * N_GROUPS + g]
        src = s_ref[tile * N_GROUPS + g]
        for rows, o, pred in _run_pieces(n):
            @pl.when(pred)
            def _():
                pltpu.make_async_copy(
                    y_hbm.at[pl.ds(pl.multiple_of((src + o) * ROW_SUB, ROW_SUB), rows * ROW_SUB), :],
                    ybuf.at[pl.ds(pl.multiple_of(base + (off + o) * ROW_SUB, ROW_SUB), rows * ROW_SUB), :],
                    sem.at[slot]).start()
        off = off + n


def _combine_fetch(i, n_steps, sub, tile0, n_ref, s_ref, y_hbm, ybuf, sem):
    half = (i % 2) * sub

    @pl.when(i == 0)
    def _():
        for s in range(sub):
            _fetch_runs(n_ref, s_ref, y_hbm, ybuf, sem, tile0 + s, s)

    @pl.when(i + 1 < n_steps)
    def _():
        for s in range(sub):
            _fetch_runs(n_ref, s_ref, y_hbm, ybuf, sem, tile0 + (i + 1) * sub + s, sub - half + s)

    bases = []
    for s in range(sub):
        base = pl.multiple_of((half + s) * TILE_ROWS, TILE_ROWS)
        pltpu.make_async_copy(y_hbm.at[pl.ds(0, TILE_ROWS), :], ybuf.at[pl.ds(base, TILE_ROWS), :],
                              sem.at[half + s]).wait()
        bases.append(base)
    return bases


def _unsort(ybuf, base, pos):
    y_hi, y_lo = _hi_lo(_load_rows(ybuf, base, TM))
    perm_t = (lax.broadcasted_iota(i32, (TM, TM), 1).astype(f32) == pos).astype(bf16)
    return jnp.dot(perm_t, y_hi, preferred_element_type=f32) + jnp.dot(perm_t, y_lo, preferred_element_type=f32)


def _combine_scratch(sub):
    return [pltpu.VMEM((2 * sub * TILE_ROWS, 128), f32), pltpu.SemaphoreType.DMA((2 * sub,))]


def _even_out_kernel(xp_ref, xs_ref, oap_ref, oas_ref, obp_ref, obs_ref, mod_ref, w_ref, nrm_ref, wr_ref, br_ref,
                     x1_ref, route_ref, cnt_ref, hs_hbm, gs_hbm, hbuf, gbuf, run, sem):
    i = pl.program_id(0)
    is_p = i < NPS
    mod = mod_ref[0]
    counts = [None] * SUB

    def chain(s):
        rows = slice(s * TM, (s + 1) * TM)
        x = jnp.where(is_p, xp_ref[rows, :], xs_ref[rows, :])
        oa = jnp.where(is_p, oap_ref[rows, :], oas_ref[rows, :])
        ob = jnp.where(is_p, obp_ref[rows, :], obs_ref[rows, :])
        out = (jnp.dot(oa, w_ref[0:W_A, :], preferred_element_type=f32)
               + jnp.dot(ob, w_ref[W_A:W_A + W_B, :], preferred_element_type=f32))
        yield
        x1 = x + mod[:, 2 * D:3 * D] * out
        x1_ref[rows, :] = x1
        yield from _route_chain(s, x1, mod, nrm_ref, wr_ref, br_ref, route_ref, cnt_ref, hbuf, gbuf, counts)

    _mixer_out_tail(i, [chain(s) for s in range(SUB)], counts, hs_hbm, gs_hbm, hbuf, gbuf, run, sem)


def _even_out(xp, xs, oa_p, oa_s, ob_p, ob_s, mod_l, w_out, nrm, wr, br):
    ptile = lambda w: pl.BlockSpec((SUB * TM, w), lambda i: (jnp.minimum(i, NPS - 1), 0))
    stile = lambda w: pl.BlockSpec((SUB * TM, w), lambda i: (jnp.maximum(i - NPS, 0), 0))
    return pl.pallas_call(
        _even_out_kernel,
        grid=(NS,),
        in_specs=[
            ptile(D), stile(D), ptile(W_A), stile(W_A), ptile(W_B), stile(W_B),
            pl.BlockSpec((1, 1, N_MOD * D), lambda i: (_mod_row_step(i), 0, 0)),
            pl.BlockSpec((W_A + W_B, D), lambda i: (0, 0)),
        ] + _router_in_specs(),
        out_specs=_dispatch_out_specs(),
        out_shape=_DISPATCH_OUT_SHAPES,
        scratch_shapes=_DISPATCH_SCRATCH,
        compiler_params=_cp(("arbitrary",)),
        name="even_out_router",
    )(xp, xs, oa_p, oa_s, ob_p, ob_s, mod_l, w_out, nrm, wr, br)


def _gelu_tanh(x):
    return 0.5 * x * (1.0 + jnp.tanh(math.sqrt(2.0 / math.pi) * (x + 0.044715 * (x * x * x))))


def _odd_kernel(n_ref, s_ref, x_ref, proute_ref, y_hbm, modp_ref, mod_ref, nm_ref, win_ref, lng_ref, lnb_ref,
                ws_ref, bs_ref, wout_ref, nrm_ref, wr_ref, br_ref,
                x1_ref, route_ref, cnt_ref, hs_hbm, gs_hbm, gated, ybuf, ysem, hbuf, gbuf, run, sem):
    i = pl.program_id(0)
    modp = modp_ref[0]
    mod = mod_ref[0]
    counts = [None] * SUB
    ybase = _combine_fetch(i, NS, SUB, 0, n_ref, s_ref, y_hbm, ybuf, ysem)
    gw = W_C // C_GROUPS

    def chain(s):
        rows = slice(s * TM, (s + 1) * TM)
        y = _unsort(ybuf, ybase[s], proute_ref[rows, 0:1])
        yield
        x = x_ref[rows, :] + modp[:, 5 * D:6 * D] * y
        h = _rms(x, nm_ref[...]) * (1.0 + mod[:, D:2 * D]) + mod[:, 0:D]
        z = _dot(h, win_ref[...])
        yield
        z = _gelu_tanh(z)
        u = z[:, 0:W_C]
        v = z[:, W_C:2 * W_C]
        mu = jnp.mean(v, axis=-1, keepdims=True)
        vc = v - mu
        var = jnp.mean(vc * vc, axis=-1, keepdims=True)
        vn = (vc * lax.rsqrt(var + EPS) * lng_ref[...] + lnb_ref[...]).astype(bf16)
        for ck in range(TM // C_CHUNK):
            rs = slice(ck * C_CHUNK, (ck + 1) * C_CHUNK)
            for g in range(C_GROUPS):
                cs = slice(g * gw, (g + 1) * gw)
                mixed = jnp.dot(ws_ref[g], vn[rs, cs], preferred_element_type=f32) + bs_ref[g]
                gated[s * TM + ck * C_CHUNK:s * TM + (ck + 1) * C_CHUNK, cs] = (u[rs, cs] * mixed).astype(bf16)
        yield
        out = jnp.dot(gated[rows, :], wout_ref[...], preferred_element_type=f32)
        yield
        x1 = x + mod[:, 2 * D:3 * D] * out
        x1_ref[rows, :] = x1
        yield from _route_chain(s, x1, mod, nrm_ref, wr_ref, br_ref, route_ref, cnt_ref, hbuf, gbuf, counts)

    _mixer_out_tail(i, [chain(s) for s in range(SUB)], counts, hs_hbm, gs_hbm, hbuf, gbuf, run, sem)


def _odd_mixer(run_n, run_src, x, prev_route, y, mod_prev, mod_l, nm, w_in, ln_g, ln_b, ws, bs, w_out, nrm, wr, br):
    tile = lambda w: pl.BlockSpec((SUB * TM, w), lambda i, *_: (i, 0))
    modspec = pl.BlockSpec((1, 1, N_MOD * D), lambda i, *_: (_mod_row_step(i), 0, 0))
    const = lambda shp: pl.BlockSpec(shp, lambda i, *_: tuple(0 for _ in shp))
    grid_spec = pltpu.PrefetchScalarGridSpec(
        num_scalar_prefetch=2,
        grid=(NS,),
        in_specs=[tile(D), tile(128), pl.BlockSpec(memory_space=pl.ANY), modspec, modspec, const((1, D)),
                  const((D, 2 * W_C)), const((1, W_C)), const((1, W_C)), const((C_GROUPS, C_CHUNK, C_CHUNK)),
                  const((C_GROUPS, C_CHUNK, W_C // C_GROUPS)), const((W_C, D))] + _router_in_specs(),
        out_specs=_dispatch_out_specs(),
        scratch_shapes=[pltpu.VMEM((SUB * TM, W_C), bf16)] + _combine_scratch(SUB) + _DISPATCH_SCRATCH,
    )
    return pl.pallas_call(
        _odd_kernel,
        grid_spec=grid_spec,
        out_shape=_DISPATCH_OUT_SHAPES,
        compiler_params=_cp(("arbitrary",)),
        name="odd_mixer_router",
    )(run_n, run_src, x, prev_route, y, mod_prev, mod_l, nm, w_in, ln_g, ln_b, ws, bs, w_out, nrm, wr, br)


def _expert_kernel(nu_ref, tg_ref, tb_ref, hs_ref, gs_ref, wg_ref, wu_ref, wd_ref, y_ref):
    t = pl.program_id(0)

    @pl.when(t < nu_ref[0])
    def _():
        xb = _load_rows(hs_ref, 0, TMG).astype(bf16)
        gates = gs_ref[...]
        acc = jnp.zeros((TMG, D), f32)
        gate_up = lambda e: (_dot(xb, wg_ref[e]), _dot(xb, wu_ref[e]))
        nxt = gate_up(0)
        for e in range(E_PER_GROUP):
            a, b = nxt
            if e + 1 < E_PER_GROUP:
                nxt = gate_up(e + 1)
            acc = acc + _dot(_silu(a) * b * gates[:, e:e + 1], wd_ref[e])
        _store_rows(y_ref, 0, acc)


def _experts(layer, n_used, tile_g, tile_blk, hs, gs, wg, wu, wd):
    wspec = lambda a, b: pl.BlockSpec((None, None, E_PER_GROUP, a, b), lambda t, nu, tg, tb: (layer, tg[t], 0, 0, 0))
    grid_spec = pltpu.PrefetchScalarGridSpec(
        num_scalar_prefetch=3,
        grid=(MAXT_G,),
        in_specs=[
            pl.BlockSpec((TMG * ROW_SUB, 128), lambda t, nu, tg, tb: (tb[t], 0)),
            pl.BlockSpec((TMG, 128), lambda t, nu, tg, tb: (tb[t], 0)),
            wspec(D, D_EXPERT), wspec(D, D_EXPERT), wspec(D_EXPERT, D),
        ],
        out_specs=pl.BlockSpec((TMG * ROW_SUB, 128), lambda t, nu, tg, tb: (tb[t], 0)),
    )
    return pl.pallas_call(
        _expert_kernel,
        grid_spec=grid_spec,
        out_shape=jax.ShapeDtypeStruct((N_GROUPS * RG * ROW_SUB, 128), f32),
        compiler_params=_cp(("arbitrary",), VMEM_LIMIT_EXPERTS),
        name="experts",
    )(n_used, tile_g, tile_blk, hs, gs, wg, wu, wd)


def _dispatch_plan(cnt_tiles):
    cnt = cnt_tiles[:, 0, 0:N_GROUPS].astype(i32)
    ends = jnp.cumsum(cnt, axis=0)
    run_src = (ends - cnt + jnp.arange(N_GROUPS, dtype=i32)[None, :] * RG).reshape(-1)
    totals = ends[NT - 1]
    tiles_g = (totals + TMG - 1) // TMG
    tile_end = jnp.cumsum(tiles_g)
    n_used = tile_end[N_GROUPS - 1:]
    t = jnp.minimum(jnp.arange(MAXT_G, dtype=i32), n_used[0] - 1)
    tile_g = jnp.sum((tile_end[None, :] <= t[:, None]).astype(i32), axis=1)
    tile_start = tile_end - tiles_g
    first = jnp.sum(jnp.where(tile_g[:, None] == jnp.arange(N_GROUPS, dtype=i32)[None, :], tile_start[None, :], 0), axis=1)
    tile_blk = tile_g * (RG // TMG) + t - first
    return cnt.reshape(-1), run_src, n_used, tile_g, tile_blk


def _final_kernel(n_ref, s_ref, x_ref, route_ref, y_hbm, mod_ref, fn_ref, o_ref, ybuf, ysem, *, n_tiles, tile_off):
    i = pl.program_id(0)
    mod = mod_ref[0]
    ybase = _combine_fetch(i, n_tiles, 1, tile_off, n_ref, s_ref, y_hbm, ybuf, ysem)
    x = x_ref[...] + mod[:, 5 * D:6 * D] * _unsort(ybuf, ybase[0], route_ref[:, 0:1])
    o_ref[...] = _rms(x, fn_ref[...])


def _final(run_n, run_src, x, route, y, mod_l, fnorm, *, n_tiles, tile_off):
    grid_spec = pltpu.PrefetchScalarGridSpec(
        num_scalar_prefetch=2,
        grid=(n_tiles,),
        in_specs=[
            pl.BlockSpec((TM, D), lambda i, *_: (tile_off + i, 0)),
            pl.BlockSpec((TM, 128), lambda i, *_: (tile_off + i, 0)),
            pl.BlockSpec(memory_space=pl.ANY),
            pl.BlockSpec((1, 1, N_MOD * D), lambda i, *_: (_mod_row(tile_off + i), 0, 0)),
            pl.BlockSpec((1, D), lambda i, *_: (0, 0)),
        ],
        out_specs=pl.BlockSpec((TM, D), lambda i, *_: (i, 0)),
        scratch_shapes=_combine_scratch(1),
    )
    return pl.pallas_call(
        functools.partial(_final_kernel, n_tiles=n_tiles, tile_off=tile_off),
        grid_spec=grid_spec,
        out_shape=jax.ShapeDtypeStruct((n_tiles * TM, D), f32),
        compiler_params=_cp(("arbitrary",)),
        name="final_norm",
    )(run_n, run_src, x, route, y, mod_l, fnorm)


def _rope_tables():
    rows = DEC_SEQ // GRID_W
    r, c = jnp.meshgrid(jnp.arange(rows), jnp.arange(GRID_W), indexing="ij")
    pos = jnp.stack([r.reshape(-1), c.reshape(-1)], axis=-1).astype(f32)
    inv = ROPE_BASE ** (-jnp.arange(AXIS_FREQS, dtype=f32) / AXIS_FREQS)
    ang = pos[:, :, None] * inv
    cos, sin = jnp.cos(ang), jnp.sin(ang)
    cos_h = jnp.concatenate([cos, cos], axis=-1).reshape(DEC_SEQ, DH_A)
    sin_h = jnp.concatenate([-sin, sin], axis=-1).reshape(DEC_SEQ, DH_A)
    cos_t = jnp.concatenate([jnp.ones((TM, 2 * DH_A), f32), jnp.tile(cos_h, (1, 2))], axis=0)
    sin_t = jnp.concatenate([jnp.zeros((TM, 2 * DH_A), f32), jnp.tile(sin_h, (1, 2))], axis=0)
    return cos_t, sin_t


def _gate_lanes(p):
    out = jnp.zeros((128,), f32)
    for d in range(2):
        out = out.at[d * 2 * H_B:d * 2 * H_B + H_B].set(p[d])
    return out.reshape(1, 128)


def _router_weights(w_rg, b_rg, w_re, b_re):
    wr = jnp.zeros((D, 256), f32).at[:, 0:N_GROUPS].set(w_rg).at[:, 128:128 + N_EXPERTS].set(w_re)
    br = jnp.zeros((1, 256), f32).at[0, 0:N_GROUPS].set(b_rg).at[0, 128:128 + N_EXPERTS].set(b_re)
    return jnp.stack(_hi_lo(wr)), br


def kernel(x_prompt, x_sample, cache_k, cache_v, state_fwd, state_bwd, c, c_ctx, norm_mix, norm_ffn, w_mod, b_mod,
           w_in_even, conv_w, a_log, dt_bias, delta_norm, lam_q, lam_k, subln, w_out_even, w_in_odd, sgu_ln_g,
           sgu_ln_b, w_spatial, b_spatial, w_out_odd, w_router_group, b_router_group, w_router_expert,
           b_router_expert, w_exp_gate, w_exp_up, w_exp_down, final_norm):
    xp = x_prompt.reshape(N_P, D)
    xs = x_sample.reshape(N_S, D)
    cond = jnp.concatenate([c, c_ctx[None, :], jnp.zeros((COND_ROWS - DEC_BATCH - 1, D), f32)], axis=0)
    mod = _modulation(cond, w_mod, b_mod)
    mod0 = mod[0].reshape(COND_ROWS, 1, N_MOD * D)
    mod1 = mod[1].reshape(COND_ROWS, 1, N_MOD * D)

    lam_init = 0.8 - 0.6 * math.exp(-0.3 * 0)
    w_in_pad = jnp.pad(w_in_even[0], ((0, 0), (0, IN_EVEN_PAD - IN_EVEN))).astype(bf16)
    cos_t, sin_t = _rope_tables()
    qa, ka, va, qkvb, gate_b, ab = _even_in(xp, xs, mod0, norm_mix[0:1], w_in_pad, cos_t, sin_t)

    ctx_k = cache_k[:, 0].reshape(DEC_BATCH * PAST, W_A)
    ctx_v = cache_v[:, 0].reshape(DEC_BATCH * PAST, W_A)
    attn = functools.partial(_attention, qa, ka, va, lam_q[0], lam_k[0], subln[0:1], lam_init)
    oa_p = attn(seq=SEQ, tq=SEQ, tok_off=0)
    oa_s = attn(seq=DEC_SEQ, tq=512, tok_off=N_P, ctx=(ctx_k, ctx_v))

    conv_pad = jnp.pad(conv_w[0], ((0, 8 - CONV_K), (0, 0)))
    dl = functools.partial(_delta, qkvb, gate_b, ab, conv_pad, _gate_lanes(a_log[0]), _gate_lanes(dt_bias[0]),
                           delta_norm[0:1])
    ob_p, s_f, s_b = dl(seq=SEQ, tok_off=0)
    ob_s, _, _ = dl(seq=DEC_SEQ, tok_off=N_P, init=(state_fwd[:, 0], state_bwd[:, 0]))

    wr0, br0 = _router_weights(w_router_group[0], b_router_group[0], w_router_expert[0], b_router_expert[0])
    x1, route0, cnt0, hs, gs = _even_out(xp, xs, oa_p, oa_s, ob_p, ob_s, mod0, w_out_even[0].astype(bf16),
                                         norm_ffn[0:1], wr0, br0)

    wshape = (DEPTH, N_GROUPS, E_PER_GROUP)
    wg = w_exp_gate.reshape(wshape + (D, D_EXPERT))
    wu = w_exp_up.reshape(wshape + (D, D_EXPERT))
    wd = w_exp_down.reshape(wshape + (D_EXPERT, D))
    run_n0, run_src0, n_used, tile_g, tile_blk = _dispatch_plan(cnt0)
    y0 = _experts(0, n_used, tile_g, tile_blk, hs, gs, wg, wu, wd)

    bs = jnp.broadcast_to(b_spatial[0][:, :, None], (C_GROUPS, C_CHUNK, W_C // C_GROUPS))
    wr1, br1 = _router_weights(w_router_group[1], b_router_group[1], w_router_expert[1], b_router_expert[1])
    x2, route1, cnt1, hs, gs = _odd_mixer(run_n0, run_src0, x1, route0, y0, mod0, mod1, norm_mix[1:2],
                                          w_in_odd[0].astype(bf16), sgu_ln_g[0:1], sgu_ln_b[0:1],
                                          w_spatial[0].astype(bf16), bs, w_out_odd[0].astype(bf16), norm_ffn[1:2],
                                          wr1, br1)
    run_n1, run_src1, n_used, tile_g, tile_blk = _dispatch_plan(cnt1)
    y1 = _experts(1, n_used, tile_g, tile_blk, hs, gs, wg, wu, wd)

    fin = functools.partial(_final, run_n1, run_src1, x2, route1, y1, mod1, final_norm[None, :])
    y_prompt = fin(n_tiles=NPT, tile_off=0)
    y_sample = fin(n_tiles=NT - NPT, tile_off=NPT)

    new_cache_k = ka[:N_P].reshape(BATCH, 1, SEQ, H_A, 2 * DH_A)
    new_cache_v = va[:N_P].reshape(BATCH, 1, SEQ, H_A, 2 * DH_A)
    return (y_prompt.reshape(BATCH, SEQ, D), y_sample.reshape(DEC_BATCH, DEC_SEQ, D), new_cache_k, new_cache_v,
            s_f[:, None], s_b[:, None])
```

```python
import functools
import math

import jax
import jax.numpy as jnp
from jax import lax
from jax.experimental import pallas as pl
from jax.experimental.pallas import tpu as pltpu

f32 = jnp.float32
bf16 = jnp.bfloat16
i32 = jnp.int32
HIGHEST = lax.Precision.HIGHEST

D = 1024
BATCH, SEQ = 16, 256
DEC_BATCH, DEC_SEQ, PAST = 8, 1024, 512
DEPTH = 2
GRID_W = 64
N_MOD = 6
EPS = 1e-6
H_A, DH_A = 4, 64
W_A = H_A * 2 * DH_A
ROPE_BASE = 10000.0
AXIS_FREQS = DH_A // 4
H_B, DK_B, DV_B = 4, 128, 128
W_B = H_B * DV_B
CONV_K = 5
CH = 64
IN_EVEN = 3 * W_A + 4 * W_B + 4 * H_B
IN_EVEN_PAD = 3712
W_C = D
C_GROUPS, C_CHUNK = 4, 128
N_GROUPS, E_PER_GROUP, D_EXPERT = 4, 8, 256
N_EXPERTS = N_GROUPS * E_PER_GROUP

N_P = BATCH * SEQ
N_S = DEC_BATCH * DEC_SEQ
N_TOK = N_P + N_S
TM = 256
NPT = N_P // TM
NT = N_TOK // TM
COND_ROWS = 16

VMEM_LIMIT = 56 * 1024 * 1024
VMEM_LIMIT_EXPERTS = 62 * 1024 * 1024


def _cp(sem, vmem_limit=VMEM_LIMIT):
    return pltpu.CompilerParams(dimension_semantics=sem, vmem_limit_bytes=vmem_limit)


SUB = 2
NS = NT // SUB
NPS = NPT // SUB


def _mod_row_step(i):
    return jnp.where(i < NPS, DEC_BATCH, (i - NPS) // (DEC_SEQ // (SUB * TM)))


def _lockstep(chains):
    live = list(chains)
    while live:
        nxt = []
        for ch in live:
            try:
                next(ch)
                nxt.append(ch)
            except StopIteration:
                pass
        live = nxt


def _rms(x, g):
    return x * lax.rsqrt(jnp.mean(x * x, axis=-1, keepdims=True) + EPS) * g


def _silu(x):
    return x * jax.nn.sigmoid(x)


def _dot(a, b):
    return jnp.dot(a.astype(bf16), b.astype(bf16), preferred_element_type=f32)


ROW_SUB = D // 128


def _store_rows(ref, base, x):
    rows = x.shape[0]
    for j in range(ROW_SUB):
        ref[pl.ds(base + j, rows, stride=ROW_SUB), :] = x[:, j * 128:(j + 1) * 128]


def _load_rows(ref, base, rows):
    return jnp.concatenate([ref[pl.ds(base + j, rows, stride=ROW_SUB), :] for j in range(ROW_SUB)], axis=1)


TN_MOD = 1536


def _mod_kernel(c_ref, w_ref, b_ref, o_ref):
    a = _silu(c_ref[...])
    o_ref[0] = jnp.dot(a, w_ref[0], precision=HIGHEST, preferred_element_type=f32) + b_ref[0]


def _modulation(cond, w_mod, b_mod):
    return pl.pallas_call(
        _mod_kernel,
        grid=(DEPTH, N_MOD * D // TN_MOD),
        in_specs=[
            pl.BlockSpec((COND_ROWS, D), lambda l, j: (0, 0)),
            pl.BlockSpec((1, D, TN_MOD), lambda l, j: (l, 0, j)),
            pl.BlockSpec((1, 1, TN_MOD), lambda l, j: (l, 0, j)),
        ],
        out_specs=pl.BlockSpec((1, COND_ROWS, TN_MOD), lambda l, j: (l, 0, j)),
        out_shape=jax.ShapeDtypeStruct((DEPTH, COND_ROWS, N_MOD * D), f32),
        compiler_params=_cp(("arbitrary", "arbitrary")),
        name="modulation",
    )(cond, w_mod, b_mod.reshape(DEPTH, 1, N_MOD * D))


def _even_in_kernel(xp_ref, xs_ref, mod_ref, nrm_ref, w_ref, cos_ref, sin_ref,
                    q_ref, k_ref, v_ref, qkvb_ref, gate_ref, ab_ref, kp_ref, vp_ref):
    i = pl.program_id(0)
    is_p = i < NPS
    mod = mod_ref[0]
    lane = lax.broadcasted_iota(i32, (TM, 2 * DH_A), 1)
    first = (lane % (2 * AXIS_FREQS)) < AXIS_FREQS

    def chain(s):
        rows = slice(s * TM, (s + 1) * TM)
        x = jnp.where(is_p, xp_ref[rows, :], xs_ref[rows, :])
        h = _rms(x, nrm_ref[...]) * (1.0 + mod[:, D:2 * D]) + mod[:, 0:D]
        proj = _dot(h, w_ref[...])
        yield
        cos = cos_ref[rows, :]
        sin = sin_ref[rows, :]

        def rope(xh):
            partner = jnp.where(first, pltpu.roll(xh, 2 * DH_A - AXIS_FREQS, 1), pltpu.roll(xh, AXIS_FREQS, 1))
            return xh * cos + partner * sin

        for hh in range(H_A):
            sl = slice(hh * 2 * DH_A, (hh + 1) * 2 * DH_A)
            q_ref[rows, sl] = (rope(proj[:, sl]) * (DH_A ** -0.5)).astype(bf16)
            k_ref[rows, sl] = rope(proj[:, W_A + hh * 2 * DH_A:W_A + (hh + 1) * 2 * DH_A])
        v_ref[rows, :] = proj[:, 2 * W_A:3 * W_A]
        qkvb_ref[rows, :] = proj[:, 3 * W_A:3 * W_A + 3 * W_B]
        gate_ref[rows, :] = proj[:, 3 * W_A + 3 * W_B:3 * W_A + 4 * W_B]
        ab_ref[rows, :] = proj[:, 3 * W_A + 4 * W_B:IN_EVEN_PAD]

    _lockstep([chain(s) for s in range(SUB)])

    @pl.when(is_p)
    def _():
        kp_ref[...] = k_ref[...]
        vp_ref[...] = v_ref[...]


def _even_in(xp, xs, mod_l, nrm, w_in_pad, cos_t, sin_t):
    tile = lambda w: pl.BlockSpec((SUB * TM, w), lambda i: (i, 0))
    ptile = lambda w: pl.BlockSpec((SUB * TM, w), lambda i: (jnp.minimum(i, NPS - 1), 0))
    rope_idx = lambda i: (jnp.where(i < NPS, 0, 1 + (i - NPS) % (DEC_SEQ // (SUB * TM))), 0)
    return pl.pallas_call(
        _even_in_kernel,
        grid=(NS,),
        in_specs=[
            ptile(D),
            pl.BlockSpec((SUB * TM, D), lambda i: (jnp.maximum(i - NPS, 0), 0)),
            pl.BlockSpec((1, 1, N_MOD * D), lambda i: (_mod_row_step(i), 0, 0)),
            pl.BlockSpec((1, D), lambda i: (0, 0)),
            pl.BlockSpec((D, IN_EVEN_PAD), lambda i: (0, 0)),
            pl.BlockSpec((SUB * TM, 2 * DH_A), rope_idx),
            pl.BlockSpec((SUB * TM, 2 * DH_A), rope_idx),
        ],
        out_specs=[tile(W_A), tile(W_A), tile(W_A), tile(3 * W_B), tile(W_B), tile(128), ptile(W_A), ptile(W_A)],
        out_shape=[
            jax.ShapeDtypeStruct((N_TOK, W_A), bf16),
            jax.ShapeDtypeStruct((N_TOK, W_A), f32),
            jax.ShapeDtypeStruct((N_TOK, W_A), f32),
            jax.ShapeDtypeStruct((N_TOK, 3 * W_B), f32),
            jax.ShapeDtypeStruct((N_TOK, W_B), f32),
            jax.ShapeDtypeStruct((N_TOK, 128), f32),
            jax.ShapeDtypeStruct((N_P, W_A), f32),
            jax.ShapeDtypeStruct((N_P, W_A), f32),
        ],
        compiler_params=_cp(("arbitrary",)),
        name="even_in",
    )(xp, xs, mod_l, nrm, w_in_pad, cos_t, sin_t)


def _attn_kernel(*refs, lam_init, has_ctx):
    if has_ctx:
        q_ref, kn_ref, vn_ref, kc_ref, vc_ref, lq_ref, lk_ref, sub_ref, o_ref = refs
    else:
        q_ref, kn_ref, vn_ref, lq_ref, lk_ref, sub_ref, o_ref = refs
    tq = q_ref.shape[0]
    lq = lq_ref[...]
    lk = lk_ref[...]
    prod = lq * lk
    lam = (jnp.exp(jnp.sum(prod[0:1], axis=-1, keepdims=True))
           - jnp.exp(jnp.sum(prod[1:2], axis=-1, keepdims=True)) + lam_init)
    lane = lax.broadcasted_iota(i32, (tq, 2 * DH_A), 1)
    nt = (((1,), (1,)), ((), ()))
    for hh in range(H_A):
        sl = slice(hh * 2 * DH_A, (hh + 1) * 2 * DH_A)
        q = q_ref[:, sl]
        srcs = [(kn_ref[:, sl].astype(bf16), vn_ref[:, sl].astype(bf16))]
        if has_ctx:
            srcs.append((kc_ref[:, sl].astype(bf16), vc_ref[:, sl].astype(bf16)))
        outs = []
        for part in range(2):
            qm = jnp.where((lane < DH_A) if part == 0 else (lane >= DH_A), q, jnp.zeros_like(q))
            ss = [lax.dot_general(qm, k, nt, preferred_element_type=f32) for k, _ in srcs]
            m = ss[0].max(axis=-1, keepdims=True)
            for s in ss[1:]:
                m = jnp.maximum(m, s.max(axis=-1, keepdims=True))
            l = jnp.zeros((tq, 1), f32)
            o = jnp.zeros((tq, 2 * DH_A), f32)
            for s, (_, v) in zip(ss, srcs):
                e = jnp.exp(s - m)
                l = l + jnp.sum(e, axis=-1, keepdims=True)
                o = o + jnp.dot(e.astype(bf16), v, preferred_element_type=f32)
            outs.append(o / l)
        o = outs[0] - lam * outs[1]
        o_ref[:, sl] = (_rms(o, sub_ref[...]) * (1.0 - lam_init)).astype(bf16)


def _attention(q, ka, va, lam_q, lam_k, subln, lam_init, *, seq, tq, tok_off, ctx=None):
    nq = seq // tq
    nb = (N_P if ctx is None else N_S) // seq
    qo, ko = tok_off // tq, tok_off // seq
    in_specs = [
        pl.BlockSpec((tq, W_A), lambda b, j: (qo + b * nq + j, 0)),
        pl.BlockSpec((seq, W_A), lambda b, j: (ko + b, 0)),
        pl.BlockSpec((seq, W_A), lambda b, j: (ko + b, 0)),
    ]
    args = [q, ka, va]
    if ctx is not None:
        in_specs += [pl.BlockSpec((PAST, W_A), lambda b, j: (b, 0))] * 2
        args += list(ctx)
    in_specs += [pl.BlockSpec((2, DH_A), lambda b, j: (0, 0))] * 2 + [pl.BlockSpec((1, 2 * DH_A), lambda b, j: (0, 0))]
    args += [lam_q, lam_k, subln]
    return pl.pallas_call(
        functools.partial(_attn_kernel, lam_init=lam_init, has_ctx=ctx is not None),
        grid=(nb, nq),
        in_specs=in_specs,
        out_specs=pl.BlockSpec((tq, W_A), lambda b, j: (b * nq + j, 0)),
        out_shape=jax.ShapeDtypeStruct((nb * seq, W_A), bf16),
        compiler_params=_cp(("arbitrary", "arbitrary")),
        name="diff_attn_ctx" if ctx is not None else "diff_attn",
    )(*args)


def _softplus(x):
    return jnp.maximum(x, 0.0) + jnp.log1p(jnp.exp(-jnp.abs(x)))


RW = H_B * CH
CPI = 4


def _rep_rows(x, times):
    return jnp.concatenate([x] * times, axis=0)


def _tri_inv(lmats, bd, ii, jj):
    b16 = (ii // 16) == (jj // 16)
    b32 = (ii // 32) == (jj // 32)
    eye = (ii == jj).astype(f32)
    n = len(lmats)
    mm = lambda a, b: jnp.dot(a.astype(bf16), bd(b), preferred_element_type=f32)
    l0 = [jnp.where(b16, l, 0.0) for l in lmats]
    x = [eye - a for a in l0]
    p = [mm(a, a) for a in l0]
    for level in range(3):
        xp = [mm(x[i], p[i]) for i in range(n)]
        if level < 2:
            p = [mm(a, a) for a in p]
        x = [x[i] + xp[i] for i in range(n)]
    for off_diag in (jnp.logical_and(b32, jnp.logical_not(b16)), jnp.logical_not(b32)):
        t = [mm(jnp.where(off_diag, lmats[i], 0.0), x[i]) for i in range(n)]
        t = [mm(x[i], t[i]) for i in range(n)]
        x = [x[i] - t[i] for i in range(n)]
    return x


def _delta_kernel(*refs, seq, has_init):
    if has_init:
        (qkv_ref, gate_ref, ab_ref, cw_ref, al_ref, dt_ref, dn_ref, sf0_ref, sb0_ref,
         ob_ref, sf_ref, sb_ref, xpad, xq, xk, xv, gacc, bacc, o_f, o_b, st, pu, pw, pqe, pkd, patt, ptot) = refs
    else:
        (qkv_ref, gate_ref, ab_ref, cw_ref, al_ref, dt_ref, dn_ref,
         ob_ref, sf_ref, sb_ref, xpad, xq, xk, xv, gacc, bacc, o_f, o_b, st, pu, pw, pqe, pkd, patt, ptot) = refs
    n = seq // CH
    win = CH + 16

    xpad[0:8, :] = jnp.zeros((8, 3 * W_B), f32)
    xpad[seq + 8:seq + 16, :] = jnp.zeros((8, 3 * W_B), f32)
    xpad[8:seq + 8, :] = qkv_ref[...]
    neg_a = -jnp.exp(al_ref[...])
    dtb = dt_ref[...]

    def pre(c, carry):
        r0 = pl.multiple_of(c * CH, CH)
        a = xpad[pl.ds(r0, win), :]
        y = jnp.zeros((CH, 3 * W_B), f32)
        for j in range(CONV_K):
            sh = ((CONV_K - 1) // 2 - j) % win
            y = y + pltpu.roll(a, sh, 0)[8:8 + CH] * cw_ref[j:j + 1, :]
        y = _silu(y)
        for hh in range(H_B):
            qh = y[:, hh * DK_B:(hh + 1) * DK_B]
            kh = y[:, W_B + hh * DK_B:W_B + (hh + 1) * DK_B]
            qn = qh * lax.rsqrt(jnp.sum(qh * qh, axis=-1, keepdims=True) + EPS) * (DK_B ** -0.5)
            kn = kh * lax.rsqrt(jnp.sum(kh * kh, axis=-1, keepdims=True) + EPS)
            xq[pl.ds(r0, CH), hh * DK_B:(hh + 1) * DK_B] = qn
            xk[pl.ds(r0, CH), hh * DK_B:(hh + 1) * DK_B] = kn
        xv[pl.ds(r0, CH), :] = y[:, 2 * W_B:3 * W_B]
        ab = ab_ref[pl.ds(r0, CH), :]
        gacc[pl.ds(r0, CH), :] = neg_a * _softplus(ab + dtb)
        bacc[pl.ds(r0, CH), :] = jax.nn.sigmoid(ab)
        return carry

    lax.fori_loop(0, n, pre, 0)

    ii = lax.broadcasted_iota(i32, (CH, RW), 0)
    lane_r = lax.broadcasted_iota(i32, (CH, RW), 1)
    jj = lane_r % CH
    hb = lane_r // CH
    rowl = lax.broadcasted_iota(i32, (CH, 128), 0)
    bd_mask = (lax.broadcasted_iota(i32, (RW, RW), 0) // CH) == (lax.broadcasted_iota(i32, (RW, RW), 1) // CH)
    wide_mask = ((lax.broadcasted_iota(i32, (RW, W_B), 0) // CH)
                 == (lax.broadcasted_iota(i32, (RW, W_B), 1) // DK_B))
    pair_mask = ((lax.broadcasted_iota(i32, (2 * DK_B, 2 * DV_B), 0) // DK_B)
                 == (lax.broadcasted_iota(i32, (2 * DK_B, 2 * DV_B), 1) // DV_B))
    zero_b = jnp.zeros((), bf16)

    def bd(x):
        return jnp.where(bd_mask, _rep_rows(x.astype(bf16), H_B), zero_b)

    def bd_wide(x):
        return jnp.where(wide_mask, _rep_rows(x.astype(bf16), H_B), zero_b)

    def per_head(src, lanes):
        return jnp.concatenate([jnp.broadcast_to(src[:, ln:ln + 1], (src.shape[0], DK_B)) for ln in lanes], axis=1)

    nt = (((1,), (1,)), ((), ()))
    tn = (((0,), (0,)), ((), ()))

    def local(it, carry):
        chains = []
        for j in range(CPI):
            c = it * CPI + j
            r0 = pl.multiple_of(c * CH, CH)
            g = gacc[pl.ds(r0, CH), :]
            beta = bacc[pl.ds(r0, CH), :]
            pre_g = g
            for s in (1, 2, 4, 8, 16, 32):
                pre_g = pre_g + jnp.where(rowl >= s, pltpu.roll(pre_g, s, 0), 0.0)
            tot = pre_g[CH - 1:CH, :]
            q_all = xq[pl.ds(r0, CH), :]
            k_all = xk[pl.ds(r0, CH), :]
            v_all = xv[pl.ds(r0, CH), :]
            k_bd = bd_wide(k_all)
            for d in range(2):
                gc = pre_g if d == 0 else tot - pre_g + g
                lanes = [d * 2 * H_B + hh for hh in range(H_B)]
                gct = _rep_rows(gc, H_B).T
                g_col = jnp.zeros((CH, RW), f32)
                g_row = jnp.zeros((1, RW), f32)
                for hh, ln in enumerate(lanes):
                    g_col = jnp.where(hb == hh, gc[:, ln:ln + 1], g_col)
                    g_row = jnp.where(hb[0:1] == hh, gct[ln:ln + 1, :], g_row)
                g_wide = per_head(gc, lanes)
                b_wide = per_head(beta, [ln + H_B for ln in lanes])
                incl = (ii >= jj) if d == 0 else (ii <= jj)
                strict = (ii > jj) if d == 0 else (ii < jj)
                eg = jnp.exp(g_wide)
                kb = k_all * b_wide
                chains.append(dict(
                    ci=c * 2 + d, incl=incl, strict=strict, k_bd=k_bd, kb=kb, q=q_all, tot=tot,
                    dec=jnp.where(incl, jnp.exp(jnp.where(incl, g_col - g_row, 0.0)), 0.0),
                    vb=v_all * b_wide, kbe=kb * eg, qe=q_all * eg,
                    kd=k_all * jnp.exp(per_head(tot, lanes) - g_wide)))
        kk = [lax.dot_general(e["kb"].astype(bf16), e["k_bd"], nt, preferred_element_type=f32) for e in chains]
        qk = [lax.dot_general(e["q"].astype(bf16), e["k_bd"], nt, preferred_element_type=f32) for e in chains]
        lmat = [jnp.where(e["strict"], kk[i] * e["dec"], 0.0) for i, e in enumerate(chains)]
        tinv = _tri_inv(lmat, bd, ii, jj)
        u = [jnp.dot(tinv[i].astype(bf16), bd_wide(e["vb"]), preferred_element_type=f32) for i, e in enumerate(chains)]
        w = [jnp.dot(tinv[i].astype(bf16), bd_wide(e["kbe"]), preferred_element_type=f32) for i, e in enumerate(chains)]
        for i, e in enumerate(chains):
            ci = e["ci"]
            pu[ci] = u[i]
            pw[ci] = w[i].astype(bf16)
            pqe[ci] = e["qe"].astype(bf16)
            pkd[ci] = e["kd"].astype(bf16)
            patt[ci] = jnp.where(e["incl"], qk[i] * e["dec"], 0.0).astype(bf16)
            ptot[ci] = jnp.broadcast_to(e["tot"], (8, 128))
        return carry

    lax.fori_loop(0, n // CPI, local, 0)

    for hh in range(H_B):
        rows = slice(hh * DK_B, (hh + 1) * DK_B)
        if has_init:
            st[0, rows, :] = sf0_ref[0, hh]
            st[1, rows, :] = sb0_ref[0, hh]
        else:
            st[0, rows, :] = jnp.zeros((DK_B, DV_B), f32)
            st[1, rows, :] = jnp.zeros((DK_B, DV_B), f32)

    def scan(it, carry):
        chains = []
        for d in range(2):
            c = it if d == 0 else n - 1 - it
            ci = c * 2 + d
            s_old = st[d]
            s_b = s_old.astype(bf16)
            chains.append(dict(
                d=d, r0=pl.multiple_of(c * CH, CH), s_old=s_old, u=pu[ci], att=patt[ci], kd=pkd[ci], tot=ptot[ci],
                lhs=jnp.concatenate([pw[ci], pqe[ci]], axis=0),
                rhs=[jnp.where(pair_mask, jnp.concatenate([s_b[p * 2 * DK_B:(p + 1) * 2 * DK_B]] * 2, axis=1), zero_b)
                     for p in range(2)]))
        ws_qs = [jnp.concatenate(
            [jnp.dot(e["lhs"][:, p * 2 * DK_B:(p + 1) * 2 * DK_B], e["rhs"][p], preferred_element_type=f32)
             for p in range(2)], axis=1) for e in chains]
        v_new = [e["u"] - ws_qs[i][0:CH] for i, e in enumerate(chains)]
        av = [jnp.dot(e["att"], bd_wide(v_new[i]), preferred_element_type=f32) for i, e in enumerate(chains)]
        su = [lax.dot_general(jnp.where(wide_mask, _rep_rows(e["kd"], H_B), zero_b),
                              jnp.concatenate([v_new[i][:, hh * DV_B:(hh + 1) * DV_B] for hh in range(H_B)],
                                              axis=0).astype(bf16),
                              tn, preferred_element_type=f32) for i, e in enumerate(chains)]
        for i, e in enumerate(chains):
            d = e["d"]
            if d == 0:
                o_f[pl.ds(e["r0"], CH), :] = ws_qs[i][CH:2 * CH] + av[i]
            else:
                o_b[pl.ds(e["r0"], CH), :] = ws_qs[i][CH:2 * CH] + av[i]
            for hh in range(H_B):
                rows = slice(hh * DK_B, (hh + 1) * DK_B)
                ln = d * 2 * H_B + hh
                st[d, rows, :] = e["s_old"][rows] * jnp.exp(e["tot"][0:1, ln:ln + 1]) + su[i][rows]
        return carry

    lax.fori_loop(0, n, scan, 0)

    def post(c, carry):
        r0 = pl.multiple_of(c * CH, CH)
        o = o_f[pl.ds(r0, CH), :] + o_b[pl.ds(r0, CH), :]
        gt = gate_ref[pl.ds(r0, CH), :]
        for hh in range(H_B):
            sl = slice(hh * DV_B, (hh + 1) * DV_B)
            ob_ref[pl.ds(r0, CH), sl] = (_rms(o[:, sl], dn_ref[...]) * _silu(gt[:, sl])).astype(bf16)
        return carry

    lax.fori_loop(0, n, post, 0)
    for hh in range(H_B):
        sf_ref[0, hh] = st[0, hh * DK_B:(hh + 1) * DK_B, :]
        sb_ref[0, hh] = st[1, hh * DK_B:(hh + 1) * DK_B, :]


def _delta(qkvb, gate, ab, conv_pad, al_lane, dt_lane, dnorm, *, seq, tok_off, init=None):
    nb = (N_P if init is None else N_S) // seq
    bo = tok_off // seq
    rows = lambda w: pl.BlockSpec((seq, w), lambda b: (bo + b, 0))
    const = lambda shp: pl.BlockSpec(shp, lambda b: tuple(0 for _ in shp))
    st_spec = pl.BlockSpec((1, H_B, DK_B, DV_B), lambda b: (b, 0, 0, 0))
    in_specs = [rows(3 * W_B), rows(W_B), rows(128), const((8, 3 * W_B)), const((1, 128)), const((1, 128)),
                const((1, DV_B))]
    args = [qkvb, gate, ab, conv_pad, al_lane, dt_lane, dnorm]
    if init is not None:
        in_specs += [st_spec, st_spec]
        args += [init[0], init[1]]
    return pl.pallas_call(
        functools.partial(_delta_kernel, seq=seq, has_init=init is not None),
        grid=(nb,),
        in_specs=in_specs,
        out_specs=[pl.BlockSpec((seq, W_B), lambda b: (b, 0)), st_spec, st_spec],
        out_shape=[
            jax.ShapeDtypeStruct((nb * seq, W_B), bf16),
            jax.ShapeDtypeStruct((nb, H_B, DK_B, DV_B), f32),
            jax.ShapeDtypeStruct((nb, H_B, DK_B, DV_B), f32),
        ],
        scratch_shapes=[
            pltpu.VMEM((seq + 16, 3 * W_B), f32),
            pltpu.VMEM((seq, W_B), f32),
            pltpu.VMEM((seq, W_B), f32),
            pltpu.VMEM((seq, W_B), f32),
            pltpu.VMEM((seq, 128), f32),
            pltpu.VMEM((seq, 128), f32),
            pltpu.VMEM((seq, W_B), f32),
            pltpu.VMEM((seq, W_B), f32),
            pltpu.VMEM((2, H_B * DK_B, DV_B), f32),
            pltpu.VMEM((2 * seq // CH, CH, W_B), f32),
            pltpu.VMEM((2 * seq // CH, CH, W_B), bf16),
            pltpu.VMEM((2 * seq // CH, CH, W_B), bf16),
            pltpu.VMEM((2 * seq // CH, CH, W_B), bf16),
            pltpu.VMEM((2 * seq // CH, CH, RW), bf16),
            pltpu.VMEM((2 * seq // CH, 8, 128), f32),
        ],
        compiler_params=_cp(("arbitrary",)),
        name="delta_ctx" if init is not None else "delta",
    )(*args)


RG = N_TOK
TMG = 256
MAXT_G = N_TOK // TMG + N_GROUPS
TILE_ROWS = TM * ROW_SUB
RUN_BITS = tuple(range(TM.bit_length() - 1, -1, -1))


def _hi_lo(x):
    hi = x.astype(bf16)
    return hi, (x - hi.astype(f32)).astype(bf16)


def _run_pieces(n):
    return [(1 << b, (n >> (b + 1)) << (b + 1), ((n >> b) & 1) == 1) for b in RUN_BITS]


def _route_chain(s, x1, mod, nrm_ref, wr_ref, br_ref, route_ref, cnt_ref, hbuf, gbuf, counts):
    rows = slice(s * TM, (s + 1) * TM)
    h2 = _rms(x1, nrm_ref[...]) * (1.0 + mod[:, 4 * D:5 * D]) + mod[:, 3 * D:4 * D]
    h_hi, h_lo = _hi_lo(h2)
    w_hi, w_lo = wr_ref[0], wr_ref[1]
    logits = (jnp.dot(h_hi, w_hi, preferred_element_type=f32) + jnp.dot(h_hi, w_lo, preferred_element_type=f32)
              + jnp.dot(h_lo, w_hi, preferred_element_type=f32)) + br_ref[...]
    yield
    lane = lax.broadcasted_iota(i32, (TM, 128), 1)
    lane_f = lane.astype(f32)
    neg = jnp.float32(-1e30)
    lg = jnp.where(lane < N_GROUPS, logits[:, 0:128], neg)
    mg = lg.max(axis=-1, keepdims=True)
    gidx = jnp.min(jnp.where(lg == mg, lane_f, 128.0), axis=-1, keepdims=True)
    pg = 1.0 / jnp.sum(jnp.exp(lg - mg), axis=-1, keepdims=True)
    le = jnp.where((lane // E_PER_GROUP).astype(f32) == gidx, logits[:, 128:256], neg)
    m1 = le.max(axis=-1, keepdims=True)
    i1 = jnp.min(jnp.where(le == m1, lane_f, 128.0), axis=-1, keepdims=True)
    le2 = jnp.where(lane_f == i1, neg, le)
    m2 = le2.max(axis=-1, keepdims=True)
    i2 = jnp.min(jnp.where(le2 == m2, lane_f, 128.0), axis=-1, keepdims=True)
    e2 = jnp.exp(m2 - m1)
    w1 = pg / (1.0 + e2)
    w2 = w1 * e2
    local = lane_f + gidx * E_PER_GROUP
    gates = jnp.where(local == i1, w1, jnp.where(local == i2, w2, 0.0))

    onehot = (lane_f == gidx).astype(f32)
    r = lax.broadcasted_iota(i32, (TM, TM), 0)
    c = lax.broadcasted_iota(i32, (TM, TM), 1)
    before = jnp.dot((r > c).astype(bf16), onehot.astype(bf16), preferred_element_type=f32)
    yield
    cnt = jnp.sum(onehot, axis=0, keepdims=True)
    pos = (jnp.sum(onehot * before, axis=-1, keepdims=True)
           + jnp.sum(jnp.where(lane_f < gidx, cnt, 0.0), axis=-1, keepdims=True))
    route_ref[rows, :] = jnp.where(lane == 0, pos, 0.0)
    cnt_ref[s] = cnt
    counts[s] = cnt.astype(i32)

    perm_t = (c.astype(f32) == pos).astype(bf16)
    tn = (((0,), (0,)), ((), ()))
    h_sorted = lax.dot_general(perm_t, h_hi, tn, preferred_element_type=f32)
    g_hi, g_lo = _hi_lo(gates)
    g_sorted = (lax.dot_general(perm_t, g_hi, tn, preferred_element_type=f32)
                + lax.dot_general(perm_t, g_lo, tn, preferred_element_type=f32))
    yield
    _store_rows(hbuf, s * TILE_ROWS, h_sorted)
    gbuf[s * TM:(s + 1) * TM, :] = g_sorted


def _dispatch_done(s, hs_hbm, gs_hbm, hbuf, gbuf, sem):
    pltpu.make_async_copy(hbuf.at[pl.ds(s * TILE_ROWS, TILE_ROWS), :], hs_hbm.at[pl.ds(0, TILE_ROWS), :],
                          sem.at[s]).wait()
    pltpu.make_async_copy(gbuf.at[pl.ds(s * TM, TM), :], gs_hbm.at[pl.ds(0, TM), :], sem.at[s]).wait()


def _dispatch_runs(s, cnt_i, hs_hbm, gs_hbm, hbuf, gbuf, run, sem):
    off = 0
    for g in range(N_GROUPS):
        n = cnt_i[0, g]
        dst = g * RG + run[g]
        for rows, o, pred in _run_pieces(n):
            @pl.when(pred)
            def _():
                pltpu.make_async_copy(
                    hbuf.at[pl.ds(pl.multiple_of(s * TILE_ROWS + (off + o) * ROW_SUB, ROW_SUB), rows * ROW_SUB), :],
                    hs_hbm.at[pl.ds(pl.multiple_of((dst + o) * ROW_SUB, ROW_SUB), rows * ROW_SUB), :],
                    sem.at[s]).start()
                pltpu.make_async_copy(gbuf.at[pl.ds(s * TM + off + o, rows), :], gs_hbm.at[pl.ds(dst + o, rows), :],
                                      sem.at[s]).start()
        run[g] = run[g] + n
        off = off + n


def _dispatch_tail_fill(hs_hbm, gs_hbm, hbuf, gbuf, run, sem):
    hbuf[pl.ds(0, TMG * ROW_SUB), :] = jnp.zeros((TMG * ROW_SUB, 128), f32)
    gbuf[pl.ds(0, TMG), :] = jnp.zeros((TMG, 128), f32)
    for g in range(N_GROUPS):
        tot = run[g]
        pad = (-tot) & (TMG - 1)
        for rows, o, pred in _run_pieces(pad):
            @pl.when(pred)
            def _():
                dst = g * RG + tot + o
                ch = pltpu.make_async_copy(
                    hbuf.at[pl.ds(0, rows * ROW_SUB), :],
                    hs_hbm.at[pl.ds(pl.multiple_of(dst * ROW_SUB, ROW_SUB), rows * ROW_SUB), :], sem.at[0])
                cg = pltpu.make_async_copy(gbuf.at[pl.ds(0, rows), :], gs_hbm.at[pl.ds(dst, rows), :], sem.at[1])
                ch.start()
                cg.start()
                ch.wait()
                cg.wait()


def _mixer_out_tail(i, chains, counts, hs_hbm, gs_hbm, hbuf, gbuf, run, sem):
    @pl.when(i == 0)
    def _():
        for g in range(N_GROUPS):
            run[g] = 0

    @pl.when(i >= 1)
    def _():
        for s in range(SUB):
            _dispatch_done(s, hs_hbm, gs_hbm, hbuf, gbuf, sem)

    _lockstep(chains)
    for s in range(SUB):
        _dispatch_runs(s, counts[s], hs_hbm, gs_hbm, hbuf, gbuf, run, sem)

    @pl.when(i == pl.num_programs(0) - 1)
    def _():
        for s in range(SUB):
            _dispatch_done(s, hs_hbm, gs_hbm, hbuf, gbuf, sem)
        _dispatch_tail_fill(hs_hbm, gs_hbm, hbuf, gbuf, run, sem)


def _dispatch_out_specs():
    return [
        pl.BlockSpec((SUB * TM, D), lambda i, *_: (i, 0)),
        pl.BlockSpec((SUB * TM, 128), lambda i, *_: (i, 0)),
        pl.BlockSpec((SUB, 1, 128), lambda i, *_: (i, 0, 0)),
        pl.BlockSpec(memory_space=pl.ANY),
        pl.BlockSpec(memory_space=pl.ANY),
    ]


_DISPATCH_OUT_SHAPES = [
    jax.ShapeDtypeStruct((N_TOK, D), f32),
    jax.ShapeDtypeStruct((N_TOK, 128), f32),
    jax.ShapeDtypeStruct((NT, 1, 128), f32),
    jax.ShapeDtypeStruct((N_GROUPS * RG * ROW_SUB, 128), f32),
    jax.ShapeDtypeStruct((N_GROUPS * RG, 128), f32),
]
_DISPATCH_SCRATCH = [
    pltpu.VMEM((SUB * TILE_ROWS, 128), f32),
    pltpu.VMEM((SUB * TM, 128), f32),
    pltpu.SMEM((N_GROUPS,), i32),
    pltpu.SemaphoreType.DMA((SUB,)),
]


def _router_in_specs():
    return [pl.BlockSpec((1, D), lambda i, *_: (0, 0)),
            pl.BlockSpec((2, D, 256), lambda i, *_: (0, 0, 0)),
            pl.BlockSpec((1, 256), lambda i, *_: (0, 0))]


def _fetch_runs(n_ref, s_ref, y_hbm, ybuf, sem, tile, slot):
    base = slot * TILE_ROWS
    off = 0
    for g in range(N_GROUPS):
        n = n_ref[tile * N_GROUPS + g]
        src = s_ref[tile * N_GROUPS + g]
        for rows, o, pred in _run_pieces(n):
            @pl.when(pred)
            def _():
                pltpu.make_async_copy(
                    y_hbm.at[pl.ds(pl.multiple_of((src + o) * ROW_SUB, ROW_SUB), rows * ROW_SUB), :],
                    ybuf.at[pl.ds(pl.multiple_of(base + (off + o) * ROW_SUB, ROW_SUB), rows * ROW_SUB), :],
                    sem.at[slot]).start()
        off = off + n


def _combine_fetch(i, n_steps, sub, tile0, n_ref, s_ref, y_hbm, ybuf, sem):
    half = (i % 2) * sub

    @pl.when(i == 0)
    def _():
        for s in range(sub):
            _fetch_runs(n_ref, s_ref, y_hbm, ybuf, sem, tile0 + s, s)

    @pl.when(i + 1 < n_steps)
    def _():
        for s in range(sub):
            _fetch_runs(n_ref, s_ref, y_hbm, ybuf, sem, tile0 + (i + 1) * sub + s, sub - half + s)

    bases = []
    for s in range(sub):
        base = pl.multiple_of((half + s) * TILE_ROWS, TILE_ROWS)
        pltpu.make_async_copy(y_hbm.at[pl.ds(0, TILE_ROWS), :], ybuf.at[pl.ds(base, TILE_ROWS), :],
                              sem.at[half + s]).wait()
        bases.append(base)
    return bases


def _unsort(ybuf, base, pos):
    y_hi, y_lo = _hi_lo(_load_rows(ybuf, base, TM))
    perm_t = (lax.broadcasted_iota(i32, (TM, TM), 1).astype(f32) == pos).astype(bf16)
    return jnp.dot(perm_t, y_hi, preferred_element_type=f32) + jnp.dot(perm_t, y_lo, preferred_element_type=f32)


def _combine_scratch(sub):
    return [pltpu.VMEM((2 * sub * TILE_ROWS, 128), f32), pltpu.SemaphoreType.DMA((2 * sub,))]


def _even_out_kernel(xp_ref, xs_ref, oap_ref, oas_ref, obp_ref, obs_ref, mod_ref, w_ref, nrm_ref, wr_ref, br_ref,
                     x1_ref, route_ref, cnt_ref, hs_hbm, gs_hbm, hbuf, gbuf, run, sem):
    i = pl.program_id(0)
    is_p = i < NPS
    mod = mod_ref[0]
    counts = [None] * SUB

    def chain(s):
        rows = slice(s * TM, (s + 1) * TM)
        x = jnp.where(is_p, xp_ref[rows, :], xs_ref[rows, :])
        oa = jnp.where(is_p, oap_ref[rows, :], oas_ref[rows, :])
        ob = jnp.where(is_p, obp_ref[rows, :], obs_ref[rows, :])
        out = (jnp.dot(oa, w_ref[0:W_A, :], preferred_element_type=f32)
               + jnp.dot(ob, w_ref[W_A:W_A + W_B, :], preferred_element_type=f32))
        yield
        x1 = x + mod[:, 2 * D:3 * D] * out
        x1_ref[rows, :] = x1
        yield from _route_chain(s, x1, mod, nrm_ref, wr_ref, br_ref, route_ref, cnt_ref, hbuf, gbuf, counts)

    _mixer_out_tail(i, [chain(s) for s in range(SUB)], counts, hs_hbm, gs_hbm, hbuf, gbuf, run, sem)


def _even_out(xp, xs, oa_p, oa_s, ob_p, ob_s, mod_l, w_out, nrm, wr, br):
    ptile = lambda w: pl.BlockSpec((SUB * TM, w), lambda i: (jnp.minimum(i, NPS - 1), 0))
    stile = lambda w: pl.BlockSpec((SUB * TM, w), lambda i: (jnp.maximum(i - NPS, 0), 0))
    return pl.pallas_call(
        _even_out_kernel,
        grid=(NS,),
        in_specs=[
            ptile(D), stile(D), ptile(W_A), stile(W_A), ptile(W_B), stile(W_B),
            pl.BlockSpec((1, 1, N_MOD * D), lambda i: (_mod_row_step(i), 0, 0)),
            pl.BlockSpec((W_A + W_B, D), lambda i: (0, 0)),
        ] + _router_in_specs(),
        out_specs=_dispatch_out_specs(),
        out_shape=_DISPATCH_OUT_SHAPES,
        scratch_shapes=_DISPATCH_SCRATCH,
        compiler_params=_cp(("arbitrary",)),
        name="even_out_router",
    )(xp, xs, oa_p, oa_s, ob_p, ob_s, mod_l, w_out, nrm, wr, br)


def _gelu_tanh(x):
    return 0.5 * x * (1.0 + jnp.tanh(math.sqrt(2.0 / math.pi) * (x + 0.044715 * (x * x * x))))


def _odd_kernel(n_ref, s_ref, x_ref, proute_ref, y_hbm, modp_ref, mod_ref, nm_ref, win_ref, lng_ref, lnb_ref,
                ws_ref, bs_ref, wout_ref, nrm_ref, wr_ref, br_ref,
                x1_ref, route_ref, cnt_ref, hs_hbm, gs_hbm, gated, ybuf, ysem, hbuf, gbuf, run, sem):
    i = pl.program_id(0)
    modp = modp_ref[0]
    mod = mod_ref[0]
    counts = [None] * SUB
    ybase = _combine_fetch(i, NS, SUB, 0, n_ref, s_ref, y_hbm, ybuf, ysem)
    gw = W_C // C_GROUPS

    def chain(s):
        rows = slice(s * TM, (s + 1) * TM)
        y = _unsort(ybuf, ybase[s], proute_ref[rows, 0:1])
        yield
        x = x_ref[rows, :] + modp[:, 5 * D:6 * D] * y
        h = _rms(x, nm_ref[...]) * (1.0 + mod[:, D:2 * D]) + mod[:, 0:D]
        z = _dot(h, win_ref[...])
        yield
        z = _gelu_tanh(z)
        u = z[:, 0:W_C]
        v = z[:, W_C:2 * W_C]
        mu = jnp.mean(v, axis=-1, keepdims=True)
        vc = v - mu
        var = jnp.mean(vc * vc, axis=-1, keepdims=True)
        vn = (vc * lax.rsqrt(var + EPS) * lng_ref[...] + lnb_ref[...]).astype(bf16)
        for ck in range(TM // C_CHUNK):
            rs = slice(ck * C_CHUNK, (ck + 1) * C_CHUNK)
            for g in range(C_GROUPS):
                cs = slice(g * gw, (g + 1) * gw)
                mixed = jnp.dot(ws_ref[g], vn[rs, cs], preferred_element_type=f32) + bs_ref[g]
                gated[s * TM + ck * C_CHUNK:s * TM + (ck + 1) * C_CHUNK, cs] = (u[rs, cs] * mixed).astype(bf16)
        yield
        out = jnp.dot(gated[rows, :], wout_ref[...], preferred_element_type=f32)
        yield
        x1 = x + mod[:, 2 * D:3 * D] * out
        x1_ref[rows, :] = x1
        yield from _route_chain(s, x1, mod, nrm_ref, wr_ref, br_ref, route_ref, cnt_ref, hbuf, gbuf, counts)

    _mixer_out_tail(i, [chain(s) for s in range(SUB)], counts, hs_hbm, gs_hbm, hbuf, gbuf, run, sem)


def _odd_mixer(run_n, run_src, x, prev_route, y, mod_prev, mod_l, nm, w_in, ln_g, ln_b, ws, bs, w_out, nrm, wr, br):
    tile = lambda w: pl.BlockSpec((SUB * TM, w), lambda i, *_: (i, 0))
    modspec = pl.BlockSpec((1, 1, N_MOD * D), lambda i, *_: (_mod_row_step(i), 0, 0))
    const = lambda shp: pl.BlockSpec(shp, lambda i, *_: tuple(0 for _ in shp))
    grid_spec = pltpu.PrefetchScalarGridSpec(
        num_scalar_prefetch=2,
        grid=(NS,),
        in_specs=[tile(D), tile(128), pl.BlockSpec(memory_space=pl.ANY), modspec, modspec, const((1, D)),
                  const((D, 2 * W_C)), const((1, W_C)), const((1, W_C)), const((C_GROUPS, C_CHUNK, C_CHUNK)),
                  const((C_GROUPS, C_CHUNK, W_C // C_GROUPS)), const((W_C, D))] + _router_in_specs(),
        out_specs=_dispatch_out_specs(),
        scratch_shapes=[pltpu.VMEM((SUB * TM, W_C), bf16)] + _combine_scratch(SUB) + _DISPATCH_SCRATCH,
    )
    return pl.pallas_call(
        _odd_kernel,
        grid_spec=grid_spec,
        out_shape=_DISPATCH_OUT_SHAPES,
        compiler_params=_cp(("arbitrary",)),
        name="odd_mixer_router",
    )(run_n, run_src, x, prev_route, y, mod_prev, mod_l, nm, w_in, ln_g, ln_b, ws, bs, w_out, nrm, wr, br)


def _expert_kernel(nu_ref, tg_ref, tb_ref, hs_ref, gs_ref, wg_ref, wu_ref, wd_ref, y_ref):
    t = pl.program_id(0)

    @pl.when(t < nu_ref[0])
    def _():
        xb = _load_rows(hs_ref, 0, TMG).astype(bf16)
        gates = gs_ref[...]
        acc = jnp.zeros((TMG, D), f32)
        gate_up = lambda e: (_dot(xb, wg_ref[e]), _dot(xb, wu_ref[e]))
        nxt = gate_up(0)
        for e in range(E_PER_GROUP):
            a, b = nxt
            if e + 1 < E_PER_GROUP:
                nxt = gate_up(e + 1)
            acc = acc + _dot(_silu(a) * b * gates[:, e:e + 1], wd_ref[e])
        _store_rows(y_ref, 0, acc)


def _experts(layer, n_used, tile_g, tile_blk, hs, gs, wg, wu, wd):
    wspec = lambda a, b: pl.BlockSpec((None, None, E_PER_GROUP, a, b), lambda t, nu, tg, tb: (layer, tg[t], 0, 0, 0))
    grid_spec = pltpu.PrefetchScalarGridSpec(
        num_scalar_prefetch=3,
        grid=(MAXT_G,),
        in_specs=[
            pl.BlockSpec((TMG * ROW_SUB, 128), lambda t, nu, tg, tb: (tb[t], 0)),
            pl.BlockSpec((TMG, 128), lambda t, nu, tg, tb: (tb[t], 0)),
            wspec(D, D_EXPERT), wspec(D, D_EXPERT), wspec(D_EXPERT, D),
        ],
        out_specs=pl.BlockSpec((TMG * ROW_SUB, 128), lambda t, nu, tg, tb: (tb[t], 0)),
    )
    return pl.pallas_call(
        _expert_kernel,
        grid_spec=grid_spec,
        out_shape=jax.ShapeDtypeStruct((N_GROUPS * RG * ROW_SUB, 128), f32),
        compiler_params=_cp(("arbitrary",), VMEM_LIMIT_EXPERTS),
        name="experts",
    )(n_used, tile_g, tile_blk, hs, gs, wg, wu, wd)


def _dispatch_plan(cnt_tiles):
    cnt = cnt_tiles[:, 0, 0:N_GROUPS].astype(i32)
    ends = jnp.cumsum(cnt, axis=0)
    run_src = (ends - cnt + jnp.arange(N_GROUPS, dtype=i32)[None, :] * RG).reshape(-1)
    totals = ends[NT - 1]
    tiles_g = (totals + TMG - 1) // TMG
    tile_end = jnp.cumsum(tiles_g)
    n_used = tile_end[N_GROUPS - 1:]
    t = jnp.minimum(jnp.arange(MAXT_G, dtype=i32), n_used[0] - 1)
    tile_g = jnp.sum((tile_end[None, :] <= t[:, None]).astype(i32), axis=1)
    tile_start = tile_end - tiles_g
    first = jnp.sum(jnp.where(tile_g[:, None] == jnp.arange(N_GROUPS, dtype=i32)[None, :], tile_start[None, :], 0), axis=1)
    tile_blk = tile_g * (RG // TMG) + t - first
    return cnt.reshape(-1), run_src, n_used, tile_g, tile_blk


def _final_kernel(n_ref, s_ref, x_ref, route_ref, y_hbm, mod_ref, fn_ref, o_ref, ybuf, ysem, *, n_steps, tile_off):
    i = pl.program_id(0)
    mod = mod_ref[0]
    ybase = _combine_fetch(i, n_steps, SUB, tile_off, n_ref, s_ref, y_hbm, ybuf, ysem)

    def chain(s):
        rows = slice(s * TM, (s + 1) * TM)
        y = _unsort(ybuf, ybase[s], route_ref[rows, 0:1])
        yield
        x = x_ref[rows, :] + mod[:, 5 * D:6 * D] * y
        o_ref[rows, :] = _rms(x, fn_ref[...])

    _lockstep([chain(s) for s in range(SUB)])


def _final(run_n, run_src, x, route, y, mod_l, fnorm, *, n_tiles, tile_off):
    n_steps, step_off = n_tiles // SUB, tile_off // SUB
    grid_spec = pltpu.PrefetchScalarGridSpec(
        num_scalar_prefetch=2,
        grid=(n_steps,),
        in_specs=[
            pl.BlockSpec((SUB * TM, D), lambda i, *_: (step_off + i, 0)),
            pl.BlockSpec((SUB * TM, 128), lambda i, *_: (step_off + i, 0)),
            pl.BlockSpec(memory_space=pl.ANY),
            pl.BlockSpec((1, 1, N_MOD * D), lambda i, *_: (_mod_row_step(step_off + i), 0, 0)),
            pl.BlockSpec((1, D), lambda i, *_: (0, 0)),
        ],
        out_specs=pl.BlockSpec((SUB * TM, D), lambda i, *_: (i, 0)),
        scratch_shapes=_combine_scratch(SUB),
    )
    return pl.pallas_call(
        functools.partial(_final_kernel, n_steps=n_steps, tile_off=tile_off),
        grid_spec=grid_spec,
        out_shape=jax.ShapeDtypeStruct((n_tiles * TM, D), f32),
        compiler_params=_cp(("arbitrary",)),
        name="final_norm",
    )(run_n, run_src, x, route, y, mod_l, fnorm)


def _rope_tables():
    rows = DEC_SEQ // GRID_W
    r, c = jnp.meshgrid(jnp.arange(rows), jnp.arange(GRID_W), indexing="ij")
    pos = jnp.stack([r.reshape(-1), c.reshape(-1)], axis=-1).astype(f32)
    inv = ROPE_BASE ** (-jnp.arange(AXIS_FREQS, dtype=f32) / AXIS_FREQS)
    ang = pos[:, :, None] * inv
    cos, sin = jnp.cos(ang), jnp.sin(ang)
    cos_h = jnp.concatenate([cos, cos], axis=-1).reshape(DEC_SEQ, DH_A)
    sin_h = jnp.concatenate([-sin, sin], axis=-1).reshape(DEC_SEQ, DH_A)
    cos_t = jnp.concatenate([jnp.ones((SUB * TM, 2 * DH_A), f32), jnp.tile(cos_h, (1, 2))], axis=0)
    sin_t = jnp.concatenate([jnp.zeros((SUB * TM, 2 * DH_A), f32), jnp.tile(sin_h, (1, 2))], axis=0)
    return cos_t, sin_t


def _gate_lanes(p):
    out = jnp.zeros((128,), f32)
    for d in range(2):
        out = out.at[d * 2 * H_B:d * 2 * H_B + H_B].set(p[d])
    return out.reshape(1, 128)


def _router_weights(w_rg, b_rg, w_re, b_re):
    wr = jnp.zeros((D, 256), f32).at[:, 0:N_GROUPS].set(w_rg).at[:, 128:128 + N_EXPERTS].set(w_re)
    br = jnp.zeros((1, 256), f32).at[0, 0:N_GROUPS].set(b_rg).at[0, 128:128 + N_EXPERTS].set(b_re)
    return jnp.stack(_hi_lo(wr)), br


def kernel(x_prompt, x_sample, cache_k, cache_v, state_fwd, state_bwd, c, c_ctx, norm_mix, norm_ffn, w_mod, b_mod,
           w_in_even, conv_w, a_log, dt_bias, delta_norm, lam_q, lam_k, subln, w_out_even, w_in_odd, sgu_ln_g,
           sgu_ln_b, w_spatial, b_spatial, w_out_odd, w_router_group, b_router_group, w_router_expert,
           b_router_expert, w_exp_gate, w_exp_up, w_exp_down, final_norm):
    xp = x_prompt.reshape(N_P, D)
    xs = x_sample.reshape(N_S, D)
    cond = jnp.concatenate([c, c_ctx[None, :], jnp.zeros((COND_ROWS - DEC_BATCH - 1, D), f32)], axis=0)
    mod = _modulation(cond, w_mod, b_mod)
    mod0 = mod[0].reshape(COND_ROWS, 1, N_MOD * D)
    mod1 = mod[1].reshape(COND_ROWS, 1, N_MOD * D)

    lam_init = 0.8 - 0.6 * math.exp(-0.3 * 0)
    w_in_pad = jnp.pad(w_in_even[0], ((0, 0), (0, IN_EVEN_PAD - IN_EVEN))).astype(bf16)
    cos_t, sin_t = _rope_tables()
    qa, ka, va, qkvb, gate_b, ab, ka_p, va_p = _even_in(xp, xs, mod0, norm_mix[0:1], w_in_pad, cos_t, sin_t)

    ctx_k = cache_k[:, 0].reshape(DEC_BATCH * PAST, W_A)
    ctx_v = cache_v[:, 0].reshape(DEC_BATCH * PAST, W_A)
    attn = functools.partial(_attention, qa, ka, va, lam_q[0], lam_k[0], subln[0:1], lam_init)
    oa_p = attn(seq=SEQ, tq=SEQ, tok_off=0)
    oa_s = attn(seq=DEC_SEQ, tq=512, tok_off=N_P, ctx=(ctx_k, ctx_v))

    conv_pad = jnp.pad(conv_w[0], ((0, 8 - CONV_K), (0, 0)))
    dl = functools.partial(_delta, qkvb, gate_b, ab, conv_pad, _gate_lanes(a_log[0]), _gate_lanes(dt_bias[0]),
                           delta_norm[0:1])
    ob_p, s_f, s_b = dl(seq=SEQ, tok_off=0)
    ob_s, _, _ = dl(seq=DEC_SEQ, tok_off=N_P, init=(state_fwd[:, 0], state_bwd[:, 0]))

    wr0, br0 = _router_weights(w_router_group[0], b_router_group[0], w_router_expert[0], b_router_expert[0])
    x1, route0, cnt0, hs, gs = _even_out(xp, xs, oa_p, oa_s, ob_p, ob_s, mod0, w_out_even[0].astype(bf16),
                                         norm_ffn[0:1], wr0, br0)

    wshape = (DEPTH, N_GROUPS, E_PER_GROUP)
    wg = w_exp_gate.reshape(wshape + (D, D_EXPERT))
    wu = w_exp_up.reshape(wshape + (D, D_EXPERT))
    wd = w_exp_down.reshape(wshape + (D_EXPERT, D))
    run_n0, run_src0, n_used, tile_g, tile_blk = _dispatch_plan(cnt0)
    y0 = _experts(0, n_used, tile_g, tile_blk, hs, gs, wg, wu, wd)

    bs = jnp.broadcast_to(b_spatial[0][:, :, None], (C_GROUPS, C_CHUNK, W_C // C_GROUPS))
    wr1, br1 = _router_weights(w_router_group[1], b_router_group[1], w_router_expert[1], b_router_expert[1])
    x2, route1, cnt1, hs, gs = _odd_mixer(run_n0, run_src0, x1, route0, y0, mod0, mod1, norm_mix[1:2],
                                          w_in_odd[0].astype(bf16), sgu_ln_g[0:1], sgu_ln_b[0:1],
                                          w_spatial[0].astype(bf16), bs, w_out_odd[0].astype(bf16), norm_ffn[1:2],
                                          wr1, br1)
    run_n1, run_src1, n_used, tile_g, tile_blk = _dispatch_plan(cnt1)
    y1 = _experts(1, n_used, tile_g, tile_blk, hs, gs, wg, wu, wd)

    fin = functools.partial(_final, run_n1, run_src1, x2, route1, y1, mod1, final_norm[None, :])
    y_prompt = fin(n_tiles=NPT, tile_off=0)
    y_sample = fin(n_tiles=NT - NPT, tile_off=NPT)

    new_cache_k = ka_p.reshape(BATCH, 1, SEQ, H_A, 2 * DH_A)
    new_cache_v = va_p.reshape(BATCH, 1, SEQ, H_A, 2 * DH_A)
    return (y_prompt.reshape(BATCH, SEQ, D), y_sample.reshape(DEC_BATCH, DEC_SEQ, D), new_cache_k, new_cache_v,
            s_f[:, None], s_b[:, None])
```

```python
import functools
import math

import jax
import jax.numpy as jnp
from jax import lax
from jax.experimental import pallas as pl
from jax.experimental.pallas import tpu as pltpu

f32 = jnp.float32
bf16 = jnp.bfloat16
i32 = jnp.int32

D = 1024
BATCH, SEQ = 16, 256
DEC_BATCH, DEC_SEQ, PAST = 8, 1024, 512
DEPTH = 2
GRID_W = 64
N_MOD = 6
EPS = 1e-6
H_A, DH_A = 4, 64
W_A = H_A * 2 * DH_A
ROPE_BASE = 10000.0
AXIS_FREQS = DH_A // 4
H_B, DK_B, DV_B = 4, 128, 128
W_B = H_B * DV_B
CONV_K = 5
CH = 64
IN_EVEN = 3 * W_A + 4 * W_B + 4 * H_B
IN_EVEN_PAD = 3712
W_C = D
C_GROUPS, C_CHUNK = 4, 128
N_GROUPS, E_PER_GROUP, D_EXPERT = 4, 8, 256
N_EXPERTS = N_GROUPS * E_PER_GROUP

N_P = BATCH * SEQ
N_S = DEC_BATCH * DEC_SEQ
N_TOK = N_P + N_S
TM = 256
NPT = N_P // TM
NT = N_TOK // TM
COND_ROWS = 16

VMEM_LIMIT = 56 * 1024 * 1024
VMEM_LIMIT_EXPERTS = 62 * 1024 * 1024


def _cp(sem, vmem_limit=VMEM_LIMIT):
    return pltpu.CompilerParams(dimension_semantics=sem, vmem_limit_bytes=vmem_limit)


SUB = 2
NS = NT // SUB
NPS = NPT // SUB


def _mod_row_step(i):
    return jnp.where(i < NPS, DEC_BATCH, (i - NPS) // (DEC_SEQ // (SUB * TM)))


def _lockstep(chains):
    live = list(chains)
    while live:
        nxt = []
        for ch in live:
            try:
                next(ch)
                nxt.append(ch)
            except StopIteration:
                pass
        live = nxt


def _rms(x, g):
    return x * lax.rsqrt(jnp.mean(x * x, axis=-1, keepdims=True) + EPS) * g


def _silu(x):
    return x * jax.nn.sigmoid(x)


def _dot(a, b):
    return jnp.dot(a.astype(bf16), b.astype(bf16), preferred_element_type=f32)


ROW_SUB = D // 128


def _store_rows(ref, base, x):
    rows = x.shape[0]
    for j in range(ROW_SUB):
        ref[pl.ds(base + j, rows, stride=ROW_SUB), :] = x[:, j * 128:(j + 1) * 128]


def _load_rows(ref, base, rows):
    return jnp.concatenate([ref[pl.ds(base + j, rows, stride=ROW_SUB), :] for j in range(ROW_SUB)], axis=1)


TN_MOD = 1536


def _mod_kernel(c_ref, w_ref, b_ref, o_ref):
    a = _silu(c_ref[...])
    o_ref[0] = _dot(a, w_ref[0]) + b_ref[0]


def _modulation(cond, w_mod, b_mod):
    return pl.pallas_call(
        _mod_kernel,
        grid=(DEPTH, N_MOD * D // TN_MOD),
        in_specs=[
            pl.BlockSpec((COND_ROWS, D), lambda l, j: (0, 0)),
            pl.BlockSpec((1, D, TN_MOD), lambda l, j: (l, 0, j)),
            pl.BlockSpec((1, 1, TN_MOD), lambda l, j: (l, 0, j)),
        ],
        out_specs=pl.BlockSpec((1, COND_ROWS, TN_MOD), lambda l, j: (l, 0, j)),
        out_shape=jax.ShapeDtypeStruct((DEPTH, COND_ROWS, N_MOD * D), f32),
        compiler_params=_cp(("arbitrary", "arbitrary")),
        name="modulation",
    )(cond, w_mod, b_mod.reshape(DEPTH, 1, N_MOD * D))


Q_SCALE = DH_A ** -0.5 * math.log2(math.e)


def _even_in_kernel(xp_ref, xs_ref, mod_ref, nrm_ref, w_ref, cos_ref, sin_ref,
                    q_ref, k_ref, v_ref, qkvb_ref, gate_ref, ab_ref, kp_ref, vp_ref):
    i = pl.program_id(0)
    is_p = i < NPS
    mod = mod_ref[0]
    lane = lax.broadcasted_iota(i32, (TM, 2 * DH_A), 1)
    first = (lane % (2 * AXIS_FREQS)) < AXIS_FREQS

    def chain(s):
        rows = slice(s * TM, (s + 1) * TM)
        x = jnp.where(is_p, xp_ref[rows, :], xs_ref[rows, :])
        h = _rms(x, nrm_ref[...]) * (1.0 + mod[:, D:2 * D]) + mod[:, 0:D]
        proj = _dot(h, w_ref[...])
        yield
        cos = cos_ref[rows, :]
        sin = sin_ref[rows, :]

        def rope(xh):
            partner = jnp.where(first, pltpu.roll(xh, 2 * DH_A - AXIS_FREQS, 1), pltpu.roll(xh, AXIS_FREQS, 1))
            return xh * cos + partner * sin

        for hh in range(H_A):
            sl = slice(hh * 2 * DH_A, (hh + 1) * 2 * DH_A)
            q_ref[rows, sl] = (rope(proj[:, sl]) * Q_SCALE).astype(bf16)
            k_ref[rows, sl] = rope(proj[:, W_A + hh * 2 * DH_A:W_A + (hh + 1) * 2 * DH_A])
        v_ref[rows, :] = proj[:, 2 * W_A:3 * W_A]
        qkvb_ref[rows, :] = proj[:, 3 * W_A:3 * W_A + 3 * W_B]
        gate_ref[rows, :] = proj[:, 3 * W_A + 3 * W_B:3 * W_A + 4 * W_B]
        ab_ref[rows, :] = proj[:, 3 * W_A + 4 * W_B:IN_EVEN_PAD]

    _lockstep([chain(s) for s in range(SUB)])

    @pl.when(is_p)
    def _():
        kp_ref[...] = k_ref[...]
        vp_ref[...] = v_ref[...]


def _even_in(xp, xs, mod_l, nrm, w_in_pad, cos_t, sin_t):
    tile = lambda w: pl.BlockSpec((SUB * TM, w), lambda i: (i, 0))
    ptile = lambda w: pl.BlockSpec((SUB * TM, w), lambda i: (jnp.minimum(i, NPS - 1), 0))
    rope_idx = lambda i: (jnp.where(i < NPS, 0, 1 + (i - NPS) % (DEC_SEQ // (SUB * TM))), 0)
    return pl.pallas_call(
        _even_in_kernel,
        grid=(NS,),
        in_specs=[
            ptile(D),
            pl.BlockSpec((SUB * TM, D), lambda i: (jnp.maximum(i - NPS, 0), 0)),
            pl.BlockSpec((1, 1, N_MOD * D), lambda i: (_mod_row_step(i), 0, 0)),
            pl.BlockSpec((1, D), lambda i: (0, 0)),
            pl.BlockSpec((D, IN_EVEN_PAD), lambda i: (0, 0)),
            pl.BlockSpec((SUB * TM, 2 * DH_A), rope_idx),
            pl.BlockSpec((SUB * TM, 2 * DH_A), rope_idx),
        ],
        out_specs=[tile(W_A), tile(W_A), tile(W_A), tile(3 * W_B), tile(W_B), tile(128), ptile(W_A), ptile(W_A)],
        out_shape=[
            jax.ShapeDtypeStruct((N_TOK, W_A), bf16),
            jax.ShapeDtypeStruct((N_TOK, W_A), f32),
            jax.ShapeDtypeStruct((N_TOK, W_A), f32),
            jax.ShapeDtypeStruct((N_TOK, 3 * W_B), f32),
            jax.ShapeDtypeStruct((N_TOK, W_B), f32),
            jax.ShapeDtypeStruct((N_TOK, 128), f32),
            jax.ShapeDtypeStruct((N_P, W_A), f32),
            jax.ShapeDtypeStruct((N_P, W_A), f32),
        ],
        compiler_params=_cp(("arbitrary",)),
        name="even_in",
    )(xp, xs, mod_l, nrm, w_in_pad, cos_t, sin_t)


ATTN_HEADS_IN_LOCKSTEP = 2


def _attn_kernel(*refs, lam_init, has_ctx):
    if has_ctx:
        q_ref, kn_ref, vn_ref, kc_ref, vc_ref, lq_ref, lk_ref, sub_ref, o_ref = refs
    else:
        q_ref, kn_ref, vn_ref, lq_ref, lk_ref, sub_ref, o_ref = refs
    tq = q_ref.shape[0]
    lq = lq_ref[...]
    lk = lk_ref[...]
    prod = lq * lk
    lam = (jnp.exp(jnp.sum(prod[0:1], axis=-1, keepdims=True))
           - jnp.exp(jnp.sum(prod[1:2], axis=-1, keepdims=True)) + lam_init)
    lane = lax.broadcasted_iota(i32, (tq, 2 * DH_A), 1)
    nt = (((1,), (1,)), ((), ()))
    tn = (((0,), (0,)), ((), ()))
    sub_col = jnp.broadcast_to(sub_ref[...], (8, 2 * DH_A)).T[:, 0:1]
    outs = {}

    def softmax_part(hh, part, srcs):
        q = q_ref[:, hh * 2 * DH_A:(hh + 1) * 2 * DH_A]
        qm = jnp.where((lane < DH_A) if part == 0 else (lane >= DH_A), q, jnp.zeros_like(q))
        ss = [lax.dot_general(k, qm, nt, preferred_element_type=f32) for k, _ in srcs]
        yield
        m = ss[0].max(axis=0, keepdims=True)
        for s in ss[1:]:
            m = jnp.maximum(m, s.max(axis=0, keepdims=True))
        l = jnp.zeros((1, tq), f32)
        o = jnp.zeros((2 * DH_A, tq), f32)
        for s, (_, v) in zip(ss, srcs):
            e = jnp.exp2(s - m)
            l = l + jnp.sum(e, axis=0, keepdims=True)
            o = o + lax.dot_general(v, e.astype(bf16), tn, preferred_element_type=f32)
        yield
        outs[(hh, part)] = o / l

    for h0 in range(0, H_A, ATTN_HEADS_IN_LOCKSTEP):
        chains = []
        for hh in range(h0, h0 + ATTN_HEADS_IN_LOCKSTEP):
            sl = slice(hh * 2 * DH_A, (hh + 1) * 2 * DH_A)
            srcs = [(kn_ref[:, sl].astype(bf16), vn_ref[:, sl].astype(bf16))]
            if has_ctx:
                srcs.append((kc_ref[:, sl].astype(bf16), vc_ref[:, sl].astype(bf16)))
            chains += [softmax_part(hh, part, srcs) for part in range(2)]
        _lockstep(chains)
        for hh in range(h0, h0 + ATTN_HEADS_IN_LOCKSTEP):
            o = outs[(hh, 0)] - lam * outs[(hh, 1)]
            o = o * lax.rsqrt(jnp.mean(o * o, axis=0, keepdims=True) + EPS) * sub_col * (1.0 - lam_init)
            o_ref[:, hh * 2 * DH_A:(hh + 1) * 2 * DH_A] = o.T.astype(bf16)


def _attention(q, ka, va, lam_q, lam_k, subln, lam_init, *, seq, tq, tok_off, ctx=None):
    nq = seq // tq
    nb = (N_P if ctx is None else N_S) // seq
    qo, ko = tok_off // tq, tok_off // seq
    in_specs = [
        pl.BlockSpec((tq, W_A), lambda b, j: (qo + b * nq + j, 0)),
        pl.BlockSpec((seq, W_A), lambda b, j: (ko + b, 0)),
        pl.BlockSpec((seq, W_A), lambda b, j: (ko + b, 0)),
    ]
    args = [q, ka, va]
    if ctx is not None:
        in_specs += [pl.BlockSpec((PAST, W_A), lambda b, j: (b, 0))] * 2
        args += list(ctx)
    in_specs += [pl.BlockSpec((2, DH_A), lambda b, j: (0, 0))] * 2 + [pl.BlockSpec((1, 2 * DH_A), lambda b, j: (0, 0))]
    args += [lam_q, lam_k, subln]
    return pl.pallas_call(
        functools.partial(_attn_kernel, lam_init=lam_init, has_ctx=ctx is not None),
        grid=(nb, nq),
        in_specs=in_specs,
        out_specs=pl.BlockSpec((tq, W_A), lambda b, j: (b * nq + j, 0)),
        out_shape=jax.ShapeDtypeStruct((nb * seq, W_A), bf16),
        compiler_params=_cp(("arbitrary", "arbitrary")),
        name="diff_attn_ctx" if ctx is not None else "diff_attn",
    )(*args)


def _softplus(x):
    return jnp.maximum(x, 0.0) + jnp.log1p(jnp.exp(-jnp.abs(x)))


RW = H_B * CH
CPI = 4


def _rep_rows(x, times):
    return jnp.concatenate([x] * times, axis=0)


def _tri_inv(lmats, bd, ii, jj):
    b16 = (ii // 16) == (jj // 16)
    b32 = (ii // 32) == (jj // 32)
    eye = (ii == jj).astype(f32)
    n = len(lmats)
    mm = lambda a, b: jnp.dot(a.astype(bf16), bd(b), preferred_element_type=f32)
    l0 = [jnp.where(b16, l, 0.0) for l in lmats]
    x = [eye - a for a in l0]
    p = [mm(a, a) for a in l0]
    for level in range(3):
        xp = [mm(x[i], p[i]) for i in range(n)]
        if level < 2:
            p = [mm(a, a) for a in p]
        x = [x[i] + xp[i] for i in range(n)]
    for off_diag in (jnp.logical_and(b32, jnp.logical_not(b16)), jnp.logical_not(b32)):
        t = [mm(jnp.where(off_diag, lmats[i], 0.0), x[i]) for i in range(n)]
        t = [mm(x[i], t[i]) for i in range(n)]
        x = [x[i] - t[i] for i in range(n)]
    return x


def _delta_kernel(*refs, seq, has_init):
    if has_init:
        (qkv_ref, gate_ref, ab_ref, cw_ref, al_ref, dt_ref, dn_ref, sf0_ref, sb0_ref,
         ob_ref, sf_ref, sb_ref, xpad, xq, xk, xv, gacc, bacc, o_f, o_b, st, pu, pw, pqe, pkd, patt, ptot) = refs
    else:
        (qkv_ref, gate_ref, ab_ref, cw_ref, al_ref, dt_ref, dn_ref,
         ob_ref, sf_ref, sb_ref, xpad, xq, xk, xv, gacc, bacc, o_f, o_b, st, pu, pw, pqe, pkd, patt, ptot) = refs
    n = seq // CH
    win = CH + 16

    xpad[0:8, :] = jnp.zeros((8, 3 * W_B), f32)
    xpad[seq + 8:seq + 16, :] = jnp.zeros((8, 3 * W_B), f32)
    xpad[8:seq + 8, :] = qkv_ref[...]
    neg_a = -jnp.exp(al_ref[...])
    dtb = dt_ref[...]

    def pre(c, carry):
        r0 = pl.multiple_of(c * CH, CH)
        a = xpad[pl.ds(r0, win), :]
        y = jnp.zeros((CH, 3 * W_B), f32)
        for j in range(CONV_K):
            sh = ((CONV_K - 1) // 2 - j) % win
            y = y + pltpu.roll(a, sh, 0)[8:8 + CH] * cw_ref[j:j + 1, :]
        y = _silu(y)
        for hh in range(H_B):
            qh = y[:, hh * DK_B:(hh + 1) * DK_B]
            kh = y[:, W_B + hh * DK_B:W_B + (hh + 1) * DK_B]
            qn = qh * lax.rsqrt(jnp.sum(qh * qh, axis=-1, keepdims=True) + EPS) * (DK_B ** -0.5)
            kn = kh * lax.rsqrt(jnp.sum(kh * kh, axis=-1, keepdims=True) + EPS)
            xq[pl.ds(r0, CH), hh * DK_B:(hh + 1) * DK_B] = qn
            xk[pl.ds(r0, CH), hh * DK_B:(hh + 1) * DK_B] = kn
        xv[pl.ds(r0, CH), :] = y[:, 2 * W_B:3 * W_B]
        ab = ab_ref[pl.ds(r0, CH), :]
        gacc[pl.ds(r0, CH), :] = neg_a * _softplus(ab + dtb)
        bacc[pl.ds(r0, CH), :] = jax.nn.sigmoid(ab)
        return carry

    lax.fori_loop(0, n, pre, 0)

    ii = lax.broadcasted_iota(i32, (CH, RW), 0)
    lane_r = lax.broadcasted_iota(i32, (CH, RW), 1)
    jj = lane_r % CH
    hb = lane_r // CH
    rowl = lax.broadcasted_iota(i32, (CH, 128), 0)
    bd_mask = (lax.broadcasted_iota(i32, (RW, RW), 0) // CH) == (lax.broadcasted_iota(i32, (RW, RW), 1) // CH)
    wide_mask = ((lax.broadcasted_iota(i32, (RW, W_B), 0) // CH)
                 == (lax.broadcasted_iota(i32, (RW, W_B), 1) // DK_B))
    pair_mask = ((lax.broadcasted_iota(i32, (2 * DK_B, 2 * DV_B), 0) // DK_B)
                 == (lax.broadcasted_iota(i32, (2 * DK_B, 2 * DV_B), 1) // DV_B))
    zero_b = jnp.zeros((), bf16)

    def bd(x):
        return jnp.where(bd_mask, _rep_rows(x.astype(bf16), H_B), zero_b)

    def bd_wide(x):
        return jnp.where(wide_mask, _rep_rows(x.astype(bf16), H_B), zero_b)

    def per_head(src, lanes):
        return jnp.concatenate([jnp.broadcast_to(src[:, ln:ln + 1], (src.shape[0], DK_B)) for ln in lanes], axis=1)

    nt = (((1,), (1,)), ((), ()))
    tn = (((0,), (0,)), ((), ()))

    def local(it, carry):
        chains = []
        for j in range(CPI):
            c = it * CPI + j
            r0 = pl.multiple_of(c * CH, CH)
            g = gacc[pl.ds(r0, CH), :]
            beta = bacc[pl.ds(r0, CH), :]
            pre_g = g
            for s in (1, 2, 4, 8, 16, 32):
                pre_g = pre_g + jnp.where(rowl >= s, pltpu.roll(pre_g, s, 0), 0.0)
            tot = pre_g[CH - 1:CH, :]
            q_all = xq[pl.ds(r0, CH), :]
            k_all = xk[pl.ds(r0, CH), :]
            v_all = xv[pl.ds(r0, CH), :]
            k_bd = bd_wide(k_all)
            for d in range(2):
                gc = pre_g if d == 0 else tot - pre_g + g
                lanes = [d * 2 * H_B + hh for hh in range(H_B)]
                gct = _rep_rows(gc, H_B).T
                g_col = jnp.zeros((CH, RW), f32)
                g_row = jnp.zeros((1, RW), f32)
                for hh, ln in enumerate(lanes):
                    g_col = jnp.where(hb == hh, gc[:, ln:ln + 1], g_col)
                    g_row = jnp.where(hb[0:1] == hh, gct[ln:ln + 1, :], g_row)
                g_wide = per_head(gc, lanes)
                b_wide = per_head(beta, [ln + H_B for ln in lanes])
                incl = (ii >= jj) if d == 0 else (ii <= jj)
                strict = (ii > jj) if d == 0 else (ii < jj)
                eg = jnp.exp(g_wide)
                kb = k_all * b_wide
                chains.append(dict(
                    ci=c * 2 + d, incl=incl, strict=strict, k_bd=k_bd, kb=kb, q=q_all, tot=tot,
                    dec=jnp.where(incl, jnp.exp(jnp.where(incl, g_col - g_row, 0.0)), 0.0),
                    vb=v_all * b_wide, kbe=kb * eg, qe=q_all * eg,
                    kd=k_all * jnp.exp(per_head(tot, lanes) - g_wide)))
        kk = [lax.dot_general(e["kb"].astype(bf16), e["k_bd"], nt, preferred_element_type=f32) for e in chains]
        qk = [lax.dot_general(e["q"].astype(bf16), e["k_bd"], nt, preferred_element_type=f32) for e in chains]
        lmat = [jnp.where(e["strict"], kk[i] * e["dec"], 0.0) for i, e in enumerate(chains)]
        tinv = _tri_inv(lmat, bd, ii, jj)
        u = [jnp.dot(tinv[i].astype(bf16), bd_wide(e["vb"]), preferred_element_type=f32) for i, e in enumerate(chains)]
        w = [jnp.dot(tinv[i].astype(bf16), bd_wide(e["kbe"]), preferred_element_type=f32) for i, e in enumerate(chains)]
        for i, e in enumerate(chains):
            ci = e["ci"]
            pu[ci] = u[i]
            pw[ci] = w[i].astype(bf16)
            pqe[ci] = e["qe"].astype(bf16)
            pkd[ci] = e["kd"].astype(bf16)
            patt[ci] = jnp.where(e["incl"], qk[i] * e["dec"], 0.0).astype(bf16)
            ptot[ci] = jnp.broadcast_to(e["tot"], (8, 128))
        return carry

    lax.fori_loop(0, n // CPI, local, 0)

    for hh in range(H_B):
        rows = slice(hh * DK_B, (hh + 1) * DK_B)
        if has_init:
            st[0, rows, :] = sf0_ref[0, hh]
            st[1, rows, :] = sb0_ref[0, hh]
        else:
            st[0, rows, :] = jnp.zeros((DK_B, DV_B), f32)
            st[1, rows, :] = jnp.zeros((DK_B, DV_B), f32)

    def scan(it, carry):
        chains = []
        for d in range(2):
            c = it if d == 0 else n - 1 - it
            ci = c * 2 + d
            s_old = st[d]
            s_b = s_old.astype(bf16)
            chains.append(dict(
                d=d, r0=pl.multiple_of(c * CH, CH), s_old=s_old, u=pu[ci], att=patt[ci], kd=pkd[ci], tot=ptot[ci],
                lhs=jnp.concatenate([pw[ci], pqe[ci]], axis=0),
                rhs=[jnp.where(pair_mask, jnp.concatenate([s_b[p * 2 * DK_B:(p + 1) * 2 * DK_B]] * 2, axis=1), zero_b)
                     for p in range(2)]))
        ws_qs = [jnp.concatenate(
            [jnp.dot(e["lhs"][:, p * 2 * DK_B:(p + 1) * 2 * DK_B], e["rhs"][p], preferred_element_type=f32)
             for p in range(2)], axis=1) for e in chains]
        v_new = [e["u"] - ws_qs[i][0:CH] for i, e in enumerate(chains)]
        av = [jnp.dot(e["att"], bd_wide(v_new[i]), preferred_element_type=f32) for i, e in enumerate(chains)]
        su = [lax.dot_general(jnp.where(wide_mask, _rep_rows(e["kd"], H_B), zero_b),
                              jnp.concatenate([v_new[i][:, hh * DV_B:(hh + 1) * DV_B] for hh in range(H_B)],
                                              axis=0).astype(bf16),
                              tn, preferred_element_type=f32) for i, e in enumerate(chains)]
        for i, e in enumerate(chains):
            d = e["d"]
            if d == 0:
                o_f[pl.ds(e["r0"], CH), :] = ws_qs[i][CH:2 * CH] + av[i]
            else:
                o_b[pl.ds(e["r0"], CH), :] = ws_qs[i][CH:2 * CH] + av[i]
            for hh in range(H_B):
                rows = slice(hh * DK_B, (hh + 1) * DK_B)
                ln = d * 2 * H_B + hh
                st[d, rows, :] = e["s_old"][rows] * jnp.exp(e["tot"][0:1, ln:ln + 1]) + su[i][rows]
        return carry

    lax.fori_loop(0, n, scan, 0)

    def post(c, carry):
        r0 = pl.multiple_of(c * CH, CH)
        o = o_f[pl.ds(r0, CH), :] + o_b[pl.ds(r0, CH), :]
        gt = gate_ref[pl.ds(r0, CH), :]
        for hh in range(H_B):
            sl = slice(hh * DV_B, (hh + 1) * DV_B)
            ob_ref[pl.ds(r0, CH), sl] = (_rms(o[:, sl], dn_ref[...]) * _silu(gt[:, sl])).astype(bf16)
        return carry

    lax.fori_loop(0, n, post, 0)
    for hh in range(H_B):
        sf_ref[0, hh] = st[0, hh * DK_B:(hh + 1) * DK_B, :]
        sb_ref[0, hh] = st[1, hh * DK_B:(hh + 1) * DK_B, :]


def _delta(qkvb, gate, ab, conv_pad, al_lane, dt_lane, dnorm, *, seq, tok_off, init=None):
    nb = (N_P if init is None else N_S) // seq
    bo = tok_off // seq
    rows = lambda w: pl.BlockSpec((seq, w), lambda b: (bo + b, 0))
    const = lambda shp: pl.BlockSpec(shp, lambda b: tuple(0 for _ in shp))
    st_spec = pl.BlockSpec((1, H_B, DK_B, DV_B), lambda b: (b, 0, 0, 0))
    in_specs = [rows(3 * W_B), rows(W_B), rows(128), const((8, 3 * W_B)), const((1, 128)), const((1, 128)),
                const((1, DV_B))]
    args = [qkvb, gate, ab, conv_pad, al_lane, dt_lane, dnorm]
    if init is not None:
        in_specs += [st_spec, st_spec]
        args += [init[0], init[1]]
    return pl.pallas_call(
        functools.partial(_delta_kernel, seq=seq, has_init=init is not None),
        grid=(nb,),
        in_specs=in_specs,
        out_specs=[pl.BlockSpec((seq, W_B), lambda b: (b, 0)), st_spec, st_spec],
        out_shape=[
            jax.ShapeDtypeStruct((nb * seq, W_B), bf16),
            jax.ShapeDtypeStruct((nb, H_B, DK_B, DV_B), f32),
            jax.ShapeDtypeStruct((nb, H_B, DK_B, DV_B), f32),
        ],
        scratch_shapes=[
            pltpu.VMEM((seq + 16, 3 * W_B), f32),
            pltpu.VMEM((seq, W_B), f32),
            pltpu.VMEM((seq, W_B), f32),
            pltpu.VMEM((seq, W_B), f32),
            pltpu.VMEM((seq, 128), f32),
            pltpu.VMEM((seq, 128), f32),
            pltpu.VMEM((seq, W_B), f32),
            pltpu.VMEM((seq, W_B), f32),
            pltpu.VMEM((2, H_B * DK_B, DV_B), f32),
            pltpu.VMEM((2 * seq // CH, CH, W_B), f32),
            pltpu.VMEM((2 * seq // CH, CH, W_B), bf16),
            pltpu.VMEM((2 * seq // CH, CH, W_B), bf16),
            pltpu.VMEM((2 * seq // CH, CH, W_B), bf16),
            pltpu.VMEM((2 * seq // CH, CH, RW), bf16),
            pltpu.VMEM((2 * seq // CH, 8, 128), f32),
        ],
        compiler_params=_cp(("arbitrary",)),
        name="delta_ctx" if init is not None else "delta",
    )(*args)


RG = N_TOK
TMG = 256
MAXT_G = N_TOK // TMG + N_GROUPS
TILE_ROWS = TM * ROW_SUB
RUN_BITS = tuple(range(TM.bit_length() - 1, -1, -1))


def _hi_lo(x):
    hi = x.astype(bf16)
    return hi, (x - hi.astype(f32)).astype(bf16)


def _run_pieces(n):
    return [(1 << b, (n >> (b + 1)) << (b + 1), ((n >> b) & 1) == 1) for b in RUN_BITS]


def _route_chain(s, x1, mod, nrm_ref, wr_ref, br_ref, route_ref, cnt_ref, hbuf, gbuf, counts):
    rows = slice(s * TM, (s + 1) * TM)
    h2 = _rms(x1, nrm_ref[...]) * (1.0 + mod[:, 4 * D:5 * D]) + mod[:, 3 * D:4 * D]
    h_hi, h_lo = _hi_lo(h2)
    w_hi, w_lo = wr_ref[0], wr_ref[1]
    logits = (jnp.dot(h_hi, w_hi, preferred_element_type=f32) + jnp.dot(h_hi, w_lo, preferred_element_type=f32)
              + jnp.dot(h_lo, w_hi, preferred_element_type=f32)) + br_ref[...]
    yield
    lane = lax.broadcasted_iota(i32, (TM, 128), 1)
    lane_f = lane.astype(f32)
    neg = jnp.float32(-1e30)
    lg = jnp.where(lane < N_GROUPS, logits[:, 0:128], neg)
    mg = lg.max(axis=-1, keepdims=True)
    gidx = jnp.min(jnp.where(lg == mg, lane_f, 128.0), axis=-1, keepdims=True)
    pg = 1.0 / jnp.sum(jnp.exp(lg - mg), axis=-1, keepdims=True)
    le = jnp.where((lane // E_PER_GROUP).astype(f32) == gidx, logits[:, 128:256], neg)
    m1 = le.max(axis=-1, keepdims=True)
    i1 = jnp.min(jnp.where(le == m1, lane_f, 128.0), axis=-1, keepdims=True)
    le2 = jnp.where(lane_f == i1, neg, le)
    m2 = le2.max(axis=-1, keepdims=True)
    i2 = jnp.min(jnp.where(le2 == m2, lane_f, 128.0), axis=-1, keepdims=True)
    e2 = jnp.exp(m2 - m1)
    w1 = pg / (1.0 + e2)
    w2 = w1 * e2
    local = lane_f + gidx * E_PER_GROUP
    gates = jnp.where(local == i1, w1, jnp.where(local == i2, w2, 0.0))

    onehot = (lane_f == gidx).astype(f32)
    r = lax.broadcasted_iota(i32, (TM, TM), 0)
    c = lax.broadcasted_iota(i32, (TM, TM), 1)
    before = jnp.dot((r > c).astype(bf16), onehot.astype(bf16), preferred_element_type=f32)
    yield
    cnt = jnp.sum(onehot, axis=0, keepdims=True)
    pos = (jnp.sum(onehot * before, axis=-1, keepdims=True)
           + jnp.sum(jnp.where(lane_f < gidx, cnt, 0.0), axis=-1, keepdims=True))
    route_ref[rows, :] = jnp.where(lane == 0, pos, 0.0)
    cnt_ref[s] = cnt
    counts[s] = cnt.astype(i32)

    perm_t = (c.astype(f32) == pos).astype(bf16)
    tn = (((0,), (0,)), ((), ()))
    h_sorted = lax.dot_general(perm_t, h_hi, tn, preferred_element_type=f32)
    g_hi, g_lo = _hi_lo(gates)
    g_sorted = (lax.dot_general(perm_t, g_hi, tn, preferred_element_type=f32)
                + lax.dot_general(perm_t, g_lo, tn, preferred_element_type=f32))
    yield
    _store_rows(hbuf, s * TILE_ROWS, h_sorted)
    gbuf[s * TM:(s + 1) * TM, :] = g_sorted


def _dispatch_done(s, hs_hbm, gs_hbm, hbuf, gbuf, sem):
    pltpu.make_async_copy(hbuf.at[pl.ds(s * TILE_ROWS, TILE_ROWS), :], hs_hbm.at[pl.ds(0, TILE_ROWS), :],
                          sem.at[s]).wait()
    pltpu.make_async_copy(gbuf.at[pl.ds(s * TM, TM), :], gs_hbm.at[pl.ds(0, TM), :], sem.at[s]).wait()


def _dispatch_runs(s, cnt_i, hs_hbm, gs_hbm, hbuf, gbuf, run, sem):
    off = 0
    for g in range(N_GROUPS):
        n = cnt_i[0, g]
        dst = g * RG + run[g]
        for rows, o, pred in _run_pieces(n):
            @pl.when(pred)
            def _():
                pltpu.make_async_copy(
                    hbuf.at[pl.ds(pl.multiple_of(s * TILE_ROWS + (off + o) * ROW_SUB, ROW_SUB), rows * ROW_SUB), :],
                    hs_hbm.at[pl.ds(pl.multiple_of((dst + o) * ROW_SUB, ROW_SUB), rows * ROW_SUB), :],
                    sem.at[s]).start()
                pltpu.make_async_copy(gbuf.at[pl.ds(s * TM + off + o, rows), :], gs_hbm.at[pl.ds(dst + o, rows), :],
                                      sem.at[s]).start()
        run[g] = run[g] + n
        off = off + n


def _dispatch_tail_fill(hs_hbm, gs_hbm, hbuf, gbuf, run, sem):
    hbuf[pl.ds(0, TMG * ROW_SUB), :] = jnp.zeros((TMG * ROW_SUB, 128), f32)
    gbuf[pl.ds(0, TMG), :] = jnp.zeros((TMG, 128), f32)
    for g in range(N_GROUPS):
        tot = run[g]
        pad = (-tot) & (TMG - 1)
        for rows, o, pred in _run_pieces(pad):
            @pl.when(pred)
            def _():
                dst = g * RG + tot + o
                ch = pltpu.make_async_copy(
                    hbuf.at[pl.ds(0, rows * ROW_SUB), :],
                    hs_hbm.at[pl.ds(pl.multiple_of(dst * ROW_SUB, ROW_SUB), rows * ROW_SUB), :], sem.at[0])
                cg = pltpu.make_async_copy(gbuf.at[pl.ds(0, rows), :], gs_hbm.at[pl.ds(dst, rows), :], sem.at[1])
                ch.start()
                cg.start()
                ch.wait()
                cg.wait()


def _mixer_out_tail(i, chains, counts, hs_hbm, gs_hbm, hbuf, gbuf, run, sem):
    @pl.when(i == 0)
    def _():
        for g in range(N_GROUPS):
            run[g] = 0

    @pl.when(i >= 1)
    def _():
        for s in range(SUB):
            _dispatch_done(s, hs_hbm, gs_hbm, hbuf, gbuf, sem)

    _lockstep(chains)
    for s in range(SUB):
        _dispatch_runs(s, counts[s], hs_hbm, gs_hbm, hbuf, gbuf, run, sem)

    @pl.when(i == pl.num_programs(0) - 1)
    def _():
        for s in range(SUB):
            _dispatch_done(s, hs_hbm, gs_hbm, hbuf, gbuf, sem)
        _dispatch_tail_fill(hs_hbm, gs_hbm, hbuf, gbuf, run, sem)


def _dispatch_out_specs():
    return [
        pl.BlockSpec((SUB * TM, D), lambda i, *_: (i, 0)),
        pl.BlockSpec((SUB * TM, 128), lambda i, *_: (i, 0)),
        pl.BlockSpec((SUB, 1, 128), lambda i, *_: (i, 0, 0)),
        pl.BlockSpec(memory_space=pl.ANY),
        pl.BlockSpec(memory_space=pl.ANY),
    ]


_DISPATCH_OUT_SHAPES = [
    jax.ShapeDtypeStruct((N_TOK, D), f32),
    jax.ShapeDtypeStruct((N_TOK, 128), f32),
    jax.ShapeDtypeStruct((NT, 1, 128), f32),
    jax.ShapeDtypeStruct((N_GROUPS * RG * ROW_SUB, 128), f32),
    jax.ShapeDtypeStruct((N_GROUPS * RG, 128), f32),
]
_DISPATCH_SCRATCH = [
    pltpu.VMEM((SUB * TILE_ROWS, 128), f32),
    pltpu.VMEM((SUB * TM, 128), f32),
    pltpu.SMEM((N_GROUPS,), i32),
    pltpu.SemaphoreType.DMA((SUB,)),
]


def _router_in_specs():
    return [pl.BlockSpec((1, D), lambda i, *_: (0, 0)),
            pl.BlockSpec((2, D, 256), lambda i, *_: (0, 0, 0)),
            pl.BlockSpec((1, 256), lambda i, *_: (0, 0))]


def _fetch_runs(n_ref, s_ref, y_hbm, ybuf, sem, tile, slot):
    base = slot * TILE_ROWS
    off = 0
    for g in range(N_GROUPS):
        n = n_ref[tile * N_GROUPS + g]
        src = s_ref[tile * N_GROUPS + g]
        for rows, o, pred in _run_pieces(n):
            @pl.when(pred)
            def _():
                pltpu.make_async_copy(
                    y_hbm.at[pl.ds(pl.multiple_of((src + o) * ROW_SUB, ROW_SUB), rows * ROW_SUB), :],
                    ybuf.at[pl.ds(pl.multiple_of(base + (off + o) * ROW_SUB, ROW_SUB), rows * ROW_SUB), :],
                    sem.at[slot]).start()
        off = off + n


def _combine_fetch(i, n_steps, sub, tile0, n_ref, s_ref, y_hbm, ybuf, sem):
    half = (i % 2) * sub

    @pl.when(i == 0)
    def _():
        for s in range(sub):
            _fetch_runs(n_ref, s_ref, y_hbm, ybuf, sem, tile0 + s, s)

    @pl.when(i + 1 < n_steps)
    def _():
        for s in range(sub):
            _fetch_runs(n_ref, s_ref, y_hbm, ybuf, sem, tile0 + (i + 1) * sub + s, sub - half + s)

    bases = []
    for s in range(sub):
        base = pl.multiple_of((half + s) * TILE_ROWS, TILE_ROWS)
        pltpu.make_async_copy(y_hbm.at[pl.ds(0, TILE_ROWS), :], ybuf.at[pl.ds(base, TILE_ROWS), :],
                              sem.at[half + s]).wait()
        bases.append(base)
    return bases


def _unsort(ybuf, base, pos):
    y_hi, y_lo = _hi_lo(_load_rows(ybuf, base, TM))
    perm_t = (lax.broadcasted_iota(i32, (TM, TM), 1).astype(f32) == pos).astype(bf16)
    return jnp.dot(perm_t, y_hi, preferred_element_type=f32) + jnp.dot(perm_t, y_lo, preferred_element_type=f32)


def _combine_scratch(sub):
    return [pltpu.VMEM((2 * sub * TILE_ROWS, 128), f32), pltpu.SemaphoreType.DMA((2 * sub,))]


def _even_out_kernel(xp_ref, xs_ref, oap_ref, oas_ref, obp_ref, obs_ref, mod_ref, w_ref, nrm_ref, wr_ref, br_ref,
                     x1_ref, route_ref, cnt_ref, hs_hbm, gs_hbm, hbuf, gbuf, run, sem):
    i = pl.program_id(0)
    is_p = i < NPS
    mod = mod_ref[0]
    counts = [None] * SUB

    def chain(s):
        rows = slice(s * TM, (s + 1) * TM)
        x = jnp.where(is_p, xp_ref[rows, :], xs_ref[rows, :])
        oa = jnp.where(is_p, oap_ref[rows, :], oas_ref[rows, :])
        ob = jnp.where(is_p, obp_ref[rows, :], obs_ref[rows, :])
        out = (jnp.dot(oa, w_ref[0:W_A, :], preferred_element_type=f32)
               + jnp.dot(ob, w_ref[W_A:W_A + W_B, :], preferred_element_type=f32))
        yield
        x1 = x + mod[:, 2 * D:3 * D] * out
        x1_ref[rows, :] = x1
        yield from _route_chain(s, x1, mod, nrm_ref, wr_ref, br_ref, route_ref, cnt_ref, hbuf, gbuf, counts)

    _mixer_out_tail(i, [chain(s) for s in range(SUB)], counts, hs_hbm, gs_hbm, hbuf, gbuf, run, sem)


def _even_out(xp, xs, oa_p, oa_s, ob_p, ob_s, mod_l, w_out, nrm, wr, br):
    ptile = lambda w: pl.BlockSpec((SUB * TM, w), lambda i: (jnp.minimum(i, NPS - 1), 0))
    stile = lambda w: pl.BlockSpec((SUB * TM, w), lambda i: (jnp.maximum(i - NPS, 0), 0))
    return pl.pallas_call(
        _even_out_kernel,
        grid=(NS,),
        in_specs=[
            ptile(D), stile(D), ptile(W_A), stile(W_A), ptile(W_B), stile(W_B),
            pl.BlockSpec((1, 1, N_MOD * D), lambda i: (_mod_row_step(i), 0, 0)),
            pl.BlockSpec((W_A + W_B, D), lambda i: (0, 0)),
        ] + _router_in_specs(),
        out_specs=_dispatch_out_specs(),
        out_shape=_DISPATCH_OUT_SHAPES,
        scratch_shapes=_DISPATCH_SCRATCH,
        compiler_params=_cp(("arbitrary",)),
        name="even_out_router",
    )(xp, xs, oa_p, oa_s, ob_p, ob_s, mod_l, w_out, nrm, wr, br)


def _gelu_tanh(x):
    return 0.5 * x * (1.0 + jnp.tanh(math.sqrt(2.0 / math.pi) * (x + 0.044715 * (x * x * x))))


def _odd_kernel(n_ref, s_ref, x_ref, proute_ref, y_hbm, modp_ref, mod_ref, nm_ref, win_ref, lng_ref, lnb_ref,
                ws_ref, bs_ref, wout_ref, nrm_ref, wr_ref, br_ref,
                x1_ref, route_ref, cnt_ref, hs_hbm, gs_hbm, gated, ybuf, ysem, hbuf, gbuf, run, sem):
    i = pl.program_id(0)
    modp = modp_ref[0]
    mod = mod_ref[0]
    counts = [None] * SUB
    ybase = _combine_fetch(i, NS, SUB, 0, n_ref, s_ref, y_hbm, ybuf, ysem)
    gw = W_C // C_GROUPS

    def chain(s):
        rows = slice(s * TM, (s + 1) * TM)
        y = _unsort(ybuf, ybase[s], proute_ref[rows, 0:1])
        yield
        x = x_ref[rows, :] + modp[:, 5 * D:6 * D] * y
        h = _rms(x, nm_ref[...]) * (1.0 + mod[:, D:2 * D]) + mod[:, 0:D]
        z = _dot(h, win_ref[...])
        yield
        z = _gelu_tanh(z)
        u = z[:, 0:W_C]
        v = z[:, W_C:2 * W_C]
        mu = jnp.mean(v, axis=-1, keepdims=True)
        vc = v - mu
        var = jnp.mean(vc * vc, axis=-1, keepdims=True)
        vn = (vc * lax.rsqrt(var + EPS) * lng_ref[...] + lnb_ref[...]).astype(bf16)
        for ck in range(TM // C_CHUNK):
            rs = slice(ck * C_CHUNK, (ck + 1) * C_CHUNK)
            for g in range(C_GROUPS):
                cs = slice(g * gw, (g + 1) * gw)
                mixed = jnp.dot(ws_ref[g], vn[rs, cs], preferred_element_type=f32) + bs_ref[g]
                gated[s * TM + ck * C_CHUNK:s * TM + (ck + 1) * C_CHUNK, cs] = (u[rs, cs] * mixed).astype(bf16)
        yield
        out = jnp.dot(gated[rows, :], wout_ref[...], preferred_element_type=f32)
        yield
        x1 = x + mod[:, 2 * D:3 * D] * out
        x1_ref[rows, :] = x1
        yield from _route_chain(s, x1, mod, nrm_ref, wr_ref, br_ref, route_ref, cnt_ref, hbuf, gbuf, counts)

    _mixer_out_tail(i, [chain(s) for s in range(SUB)], counts, hs_hbm, gs_hbm, hbuf, gbuf, run, sem)


def _odd_mixer(run_n, run_src, x, prev_route, y, mod_prev, mod_l, nm, w_in, ln_g, ln_b, ws, bs, w_out, nrm, wr, br):
    tile = lambda w: pl.BlockSpec((SUB * TM, w), lambda i, *_: (i, 0))
    modspec = pl.BlockSpec((1, 1, N_MOD * D), lambda i, *_: (_mod_row_step(i), 0, 0))
    const = lambda shp: pl.BlockSpec(shp, lambda i, *_: tuple(0 for _ in shp))
    grid_spec = pltpu.PrefetchScalarGridSpec(
        num_scalar_prefetch=2,
        grid=(NS,),
        in_specs=[tile(D), tile(128), pl.BlockSpec(memory_space=pl.ANY), modspec, modspec, const((1, D)),
                  const((D, 2 * W_C)), const((1, W_C)), const((1, W_C)), const((C_GROUPS, C_CHUNK, C_CHUNK)),
                  const((C_GROUPS, C_CHUNK, W_C // C_GROUPS)), const((W_C, D))] + _router_in_specs(),
        out_specs=_dispatch_out_specs(),
        scratch_shapes=[pltpu.VMEM((SUB * TM, W_C), bf16)] + _combine_scratch(SUB) + _DISPATCH_SCRATCH,
    )
    return pl.pallas_call(
        _odd_kernel,
        grid_spec=grid_spec,
        out_shape=_DISPATCH_OUT_SHAPES,
        compiler_params=_cp(("arbitrary",)),
        name="odd_mixer_router",
    )(run_n, run_src, x, prev_route, y, mod_prev, mod_l, nm, w_in, ln_g, ln_b, ws, bs, w_out, nrm, wr, br)


def _expert_kernel(nu_ref, tg_ref, tb_ref, hs_ref, gs_ref, wg_ref, wu_ref, wd_ref, y_ref):
    t = pl.program_id(0)

    @pl.when(t < nu_ref[0])
    def _():
        xb = _load_rows(hs_ref, 0, TMG).astype(bf16)
        gates = gs_ref[...]
        acc = jnp.zeros((TMG, D), f32)
        gate_up = lambda e: (_dot(xb, wg_ref[e]), _dot(xb, wu_ref[e]))
        nxt = gate_up(0)
        for e in range(E_PER_GROUP):
            a, b = nxt
            if e + 1 < E_PER_GROUP:
                nxt = gate_up(e + 1)
            acc = acc + _dot(_silu(a) * b * gates[:, e:e + 1], wd_ref[e])
        _store_rows(y_ref, 0, acc)


def _experts(layer, n_used, tile_g, tile_blk, hs, gs, wg, wu, wd):
    wspec = lambda a, b: pl.BlockSpec((None, None, E_PER_GROUP, a, b), lambda t, nu, tg, tb: (layer, tg[t], 0, 0, 0))
    grid_spec = pltpu.PrefetchScalarGridSpec(
        num_scalar_prefetch=3,
        grid=(MAXT_G,),
        in_specs=[
            pl.BlockSpec((TMG * ROW_SUB, 128), lambda t, nu, tg, tb: (tb[t], 0)),
            pl.BlockSpec((TMG, 128), lambda t, nu, tg, tb: (tb[t], 0)),
            wspec(D, D_EXPERT), wspec(D, D_EXPERT), wspec(D_EXPERT, D),
        ],
        out_specs=pl.BlockSpec((TMG * ROW_SUB, 128), lambda t, nu, tg, tb: (tb[t], 0)),
    )
    return pl.pallas_call(
        _expert_kernel,
        grid_spec=grid_spec,
        out_shape=jax.ShapeDtypeStruct((N_GROUPS * RG * ROW_SUB, 128), f32),
        compiler_params=_cp(("arbitrary",), VMEM_LIMIT_EXPERTS),
        name="experts",
    )(n_used, tile_g, tile_blk, hs, gs, wg, wu, wd)


def _dispatch_plan(cnt_tiles):
    cnt = cnt_tiles[:, 0, 0:N_GROUPS].astype(i32)
    ends = jnp.cumsum(cnt, axis=0)
    run_src = (ends - cnt + jnp.arange(N_GROUPS, dtype=i32)[None, :] * RG).reshape(-1)
    totals = ends[NT - 1]
    tiles_g = (totals + TMG - 1) // TMG
    tile_end = jnp.cumsum(tiles_g)
    n_used = tile_end[N_GROUPS - 1:]
    t = jnp.minimum(jnp.arange(MAXT_G, dtype=i32), n_used[0] - 1)
    tile_g = jnp.sum((tile_end[None, :] <= t[:, None]).astype(i32), axis=1)
    tile_start = tile_end - tiles_g
    first = jnp.sum(jnp.where(tile_g[:, None] == jnp.arange(N_GROUPS, dtype=i32)[None, :], tile_start[None, :], 0), axis=1)
    tile_blk = tile_g * (RG // TMG) + t - first
    return cnt.reshape(-1), run_src, n_used, tile_g, tile_blk


def _final_kernel(n_ref, s_ref, x_ref, route_ref, y_hbm, mod_ref, fn_ref, o_ref, ybuf, ysem, *, n_steps, tile_off):
    i = pl.program_id(0)
    mod = mod_ref[0]
    ybase = _combine_fetch(i, n_steps, SUB, tile_off, n_ref, s_ref, y_hbm, ybuf, ysem)

    def chain(s):
        rows = slice(s * TM, (s + 1) * TM)
        y = _unsort(ybuf, ybase[s], route_ref[rows, 0:1])
        yield
        x = x_ref[rows, :] + mod[:, 5 * D:6 * D] * y
        o_ref[rows, :] = _rms(x, fn_ref[...])

    _lockstep([chain(s) for s in range(SUB)])


def _final(run_n, run_src, x, route, y, mod_l, fnorm, *, n_tiles, tile_off):
    n_steps, step_off = n_tiles // SUB, tile_off // SUB
    grid_spec = pltpu.PrefetchScalarGridSpec(
        num_scalar_prefetch=2,
        grid=(n_steps,),
        in_specs=[
            pl.BlockSpec((SUB * TM, D), lambda i, *_: (step_off + i, 0)),
            pl.BlockSpec((SUB * TM, 128), lambda i, *_: (step_off + i, 0)),
            pl.BlockSpec(memory_space=pl.ANY),
            pl.BlockSpec((1, 1, N_MOD * D), lambda i, *_: (_mod_row_step(step_off + i), 0, 0)),
            pl.BlockSpec((1, D), lambda i, *_: (0, 0)),
        ],
        out_specs=pl.BlockSpec((SUB * TM, D), lambda i, *_: (i, 0)),
        scratch_shapes=_combine_scratch(SUB),
    )
    return pl.pallas_call(
        functools.partial(_final_kernel, n_steps=n_steps, tile_off=tile_off),
        grid_spec=grid_spec,
        out_shape=jax.ShapeDtypeStruct((n_tiles * TM, D), f32),
        compiler_params=_cp(("arbitrary",)),
        name="final_norm",
    )(run_n, run_src, x, route, y, mod_l, fnorm)


def _rope_tables():
    rows = DEC_SEQ // GRID_W
    r, c = jnp.meshgrid(jnp.arange(rows), jnp.arange(GRID_W), indexing="ij")
    pos = jnp.stack([r.reshape(-1), c.reshape(-1)], axis=-1).astype(f32)
    inv = ROPE_BASE ** (-jnp.arange(AXIS_FREQS, dtype=f32) / AXIS_FREQS)
    ang = pos[:, :, None] * inv
    cos, sin = jnp.cos(ang), jnp.sin(ang)
    cos_h = jnp.concatenate([cos, cos], axis=-1).reshape(DEC_SEQ, DH_A)
    sin_h = jnp.concatenate([-sin, sin], axis=-1).reshape(DEC_SEQ, DH_A)
    cos_t = jnp.concatenate([jnp.ones((SUB * TM, 2 * DH_A), f32), jnp.tile(cos_h, (1, 2))], axis=0)
    sin_t = jnp.concatenate([jnp.zeros((SUB * TM, 2 * DH_A), f32), jnp.tile(sin_h, (1, 2))], axis=0)
    return cos_t, sin_t


def _gate_lanes(p):
    out = jnp.zeros((128,), f32)
    for d in range(2):
        out = out.at[d * 2 * H_B:d * 2 * H_B + H_B].set(p[d])
    return out.reshape(1, 128)


def _router_weights(w_rg, b_rg, w_re, b_re):
    wr = jnp.zeros((D, 256), f32).at[:, 0:N_GROUPS].set(w_rg).at[:, 128:128 + N_EXPERTS].set(w_re)
    br = jnp.zeros((1, 256), f32).at[0, 0:N_GROUPS].set(b_rg).at[0, 128:128 + N_EXPERTS].set(b_re)
    return jnp.stack(_hi_lo(wr)), br


def kernel(x_prompt, x_sample, cache_k, cache_v, state_fwd, state_bwd, c, c_ctx, norm_mix, norm_ffn, w_mod, b_mod,
           w_in_even, conv_w, a_log, dt_bias, delta_norm, lam_q, lam_k, subln, w_out_even, w_in_odd, sgu_ln_g,
           sgu_ln_b, w_spatial, b_spatial, w_out_odd, w_router_group, b_router_group, w_router_expert,
           b_router_expert, w_exp_gate, w_exp_up, w_exp_down, final_norm):
    xp = x_prompt.reshape(N_P, D)
    xs = x_sample.reshape(N_S, D)
    cond = jnp.concatenate([c, c_ctx[None, :], jnp.zeros((COND_ROWS - DEC_BATCH - 1, D), f32)], axis=0)
    mod = _modulation(cond, w_mod, b_mod)
    mod0 = mod[0].reshape(COND_ROWS, 1, N_MOD * D)
    mod1 = mod[1].reshape(COND_ROWS, 1, N_MOD * D)

    lam_init = 0.8 - 0.6 * math.exp(-0.3 * 0)
    w_in_pad = jnp.pad(w_in_even[0], ((0, 0), (0, IN_EVEN_PAD - IN_EVEN))).astype(bf16)
    cos_t, sin_t = _rope_tables()
    qa, ka, va, qkvb, gate_b, ab, ka_p, va_p = _even_in(xp, xs, mod0, norm_mix[0:1], w_in_pad, cos_t, sin_t)

    ctx_k = cache_k[:, 0].reshape(DEC_BATCH * PAST, W_A)
    ctx_v = cache_v[:, 0].reshape(DEC_BATCH * PAST, W_A)
    attn = functools.partial(_attention, qa, ka, va, lam_q[0], lam_k[0], subln[0:1], lam_init)
    oa_p = attn(seq=SEQ, tq=SEQ, tok_off=0)
    oa_s = attn(seq=DEC_SEQ, tq=512, tok_off=N_P, ctx=(ctx_k, ctx_v))

    conv_pad = jnp.pad(conv_w[0], ((0, 8 - CONV_K), (0, 0)))
    dl = functools.partial(_delta, qkvb, gate_b, ab, conv_pad, _gate_lanes(a_log[0]), _gate_lanes(dt_bias[0]),
                           delta_norm[0:1])
    ob_p, s_f, s_b = dl(seq=SEQ, tok_off=0)
    ob_s, _, _ = dl(seq=DEC_SEQ, tok_off=N_P, init=(state_fwd[:, 0], state_bwd[:, 0]))

    wr0, br0 = _router_weights(w_router_group[0], b_router_group[0], w_router_expert[0], b_router_expert[0])
    x1, route0, cnt0, hs, gs = _even_out(xp, xs, oa_p, oa_s, ob_p, ob_s, mod0, w_out_even[0].astype(bf16),
                                         norm_ffn[0:1], wr0, br0)

    wshape = (DEPTH, N_GROUPS, E_PER_GROUP)
    wg = w_exp_gate.reshape(wshape + (D, D_EXPERT))
    wu = w_exp_up.reshape(wshape + (D, D_EXPERT))
    wd = w_exp_down.reshape(wshape + (D_EXPERT, D))
    run_n0, run_src0, n_used, tile_g, tile_blk = _dispatch_plan(cnt0)
    y0 = _experts(0, n_used, tile_g, tile_blk, hs, gs, wg, wu, wd)

    bs = jnp.broadcast_to(b_spatial[0][:, :, None], (C_GROUPS, C_CHUNK, W_C // C_GROUPS))
    wr1, br1 = _router_weights(w_router_group[1], b_router_group[1], w_router_expert[1], b_router_expert[1])
    x2, route1, cnt1, hs, gs = _odd_mixer(run_n0, run_src0, x1, route0, y0, mod0, mod1, norm_mix[1:2],
                                          w_in_odd[0].astype(bf16), sgu_ln_g[0:1], sgu_ln_b[0:1],
                                          w_spatial[0].astype(bf16), bs, w_out_odd[0].astype(bf16), norm_ffn[1:2],
                                          wr1, br1)
    run_n1, run_src1, n_used, tile_g, tile_blk = _dispatch_plan(cnt1)
    y1 = _experts(1, n_used, tile_g, tile_blk, hs, gs, wg, wu, wd)

    fin = functools.partial(_final, run_n1, run_src1, x2, route1, y1, mod1, final_norm[None, :])
    y_prompt = fin(n_tiles=NPT, tile_off=0)
    y_sample = fin(n_tiles=NT - NPT, tile_off=NPT)

    new_cache_k = ka_p.reshape(BATCH, 1, SEQ, H_A, 2 * DH_A)
    new_cache_v = va_p.reshape(BATCH, 1, SEQ, H_A, 2 * DH_A)
    return (y_prompt.reshape(BATCH, SEQ, D), y_sample.reshape(DEC_BATCH, DEC_SEQ, D), new_cache_k, new_cache_v,
            s_f[:, None], s_b[:, None])
```

```python
import functools
import math

import jax
import jax.numpy as jnp
from jax import lax
from jax.experimental import pallas as pl
from jax.experimental.pallas import tpu as pltpu

f32 = jnp.float32
bf16 = jnp.bfloat16
i32 = jnp.int32

D = 1024
BATCH, SEQ = 16, 256
DEC_BATCH, DEC_SEQ, PAST = 8, 1024, 512
DEPTH = 2
GRID_W = 64
N_MOD = 6
EPS = 1e-6
H_A, DH_A = 4, 64
W_A = H_A * 2 * DH_A
ROPE_BASE = 10000.0
AXIS_FREQS = DH_A // 4
H_B, DK_B, DV_B = 4, 128, 128
W_B = H_B * DV_B
CONV_K = 5
CH = 64
IN_EVEN = 3 * W_A + 4 * W_B + 4 * H_B
IN_EVEN_PAD = 3712
W_C = D
C_GROUPS, C_CHUNK = 4, 128
N_GROUPS, E_PER_GROUP, D_EXPERT = 4, 8, 256
N_EXPERTS = N_GROUPS * E_PER_GROUP

N_P = BATCH * SEQ
N_S = DEC_BATCH * DEC_SEQ
N_TOK = N_P + N_S
TM = 256
NPT = N_P // TM
NT = N_TOK // TM
COND_ROWS = 16

VMEM_LIMIT = 56 * 1024 * 1024
VMEM_LIMIT_EXPERTS = 62 * 1024 * 1024


def _cp(sem, vmem_limit=VMEM_LIMIT):
    return pltpu.CompilerParams(dimension_semantics=sem, vmem_limit_bytes=vmem_limit)


SUB = 2
NS = NT // SUB
NPS = NPT // SUB


def _mod_row_step(i, sub=SUB):
    return jnp.where(i < NPT // sub, DEC_BATCH, (i - NPT // sub) // (DEC_SEQ // (sub * TM)))


def _lockstep(chains):
    live = list(chains)
    while live:
        nxt = []
        for ch in live:
            try:
                next(ch)
                nxt.append(ch)
            except StopIteration:
                pass
        live = nxt


def _rms(x, g):
    return x * lax.rsqrt(jnp.mean(x * x, axis=-1, keepdims=True) + EPS) * g


def _silu(x):
    return x * jax.nn.sigmoid(x)


def _dot(a, b):
    return jnp.dot(a.astype(bf16), b.astype(bf16), preferred_element_type=f32)


ROW_SUB = D // 128


def _store_rows(ref, base, x):
    rows = x.shape[0]
    for j in range(ROW_SUB):
        ref[pl.ds(base + j, rows, stride=ROW_SUB), :] = x[:, j * 128:(j + 1) * 128]


def _load_rows(ref, base, rows):
    return jnp.concatenate([ref[pl.ds(base + j, rows, stride=ROW_SUB), :] for j in range(ROW_SUB)], axis=1)


TN_MOD = 1536


def _mod_kernel(c_ref, w_ref, b_ref, o_ref):
    a = _silu(c_ref[...])
    o_ref[0] = _dot(a, w_ref[0]) + b_ref[0]


def _modulation(cond, w_mod, b_mod):
    return pl.pallas_call(
        _mod_kernel,
        grid=(DEPTH, N_MOD * D // TN_MOD),
        in_specs=[
            pl.BlockSpec((COND_ROWS, D), lambda l, j: (0, 0)),
            pl.BlockSpec((1, D, TN_MOD), lambda l, j: (l, 0, j)),
            pl.BlockSpec((1, 1, TN_MOD), lambda l, j: (l, 0, j)),
        ],
        out_specs=pl.BlockSpec((1, COND_ROWS, TN_MOD), lambda l, j: (l, 0, j)),
        out_shape=jax.ShapeDtypeStruct((DEPTH, COND_ROWS, N_MOD * D), f32),
        compiler_params=_cp(("arbitrary", "arbitrary")),
        name="modulation",
    )(cond, w_mod, b_mod.reshape(DEPTH, 1, N_MOD * D))


Q_SCALE = DH_A ** -0.5 * math.log2(math.e)


def _even_in_kernel(xp_ref, xs_ref, mod_ref, nrm_ref, w_ref, cos_ref, sin_ref,
                    q_ref, k_ref, v_ref, qkvb_ref, gate_ref, ab_ref, kp_ref, vp_ref):
    i = pl.program_id(0)
    is_p = i < NPS
    mod = mod_ref[0]
    lane = lax.broadcasted_iota(i32, (TM, 2 * DH_A), 1)
    first = (lane % (2 * AXIS_FREQS)) < AXIS_FREQS

    def chain(s):
        rows = slice(s * TM, (s + 1) * TM)
        x = jnp.where(is_p, xp_ref[rows, :], xs_ref[rows, :])
        h = _rms(x, nrm_ref[...]) * (1.0 + mod[:, D:2 * D]) + mod[:, 0:D]
        proj = _dot(h, w_ref[...])
        yield
        cos = cos_ref[rows, :]
        sin = sin_ref[rows, :]

        def rope(xh):
            partner = jnp.where(first, pltpu.roll(xh, 2 * DH_A - AXIS_FREQS, 1), pltpu.roll(xh, AXIS_FREQS, 1))
            return xh * cos + partner * sin

        for hh in range(H_A):
            sl = slice(hh * 2 * DH_A, (hh + 1) * 2 * DH_A)
            q_ref[rows, sl] = (rope(proj[:, sl]) * Q_SCALE).astype(bf16)
            k_ref[rows, sl] = rope(proj[:, W_A + hh * 2 * DH_A:W_A + (hh + 1) * 2 * DH_A])
        v_ref[rows, :] = proj[:, 2 * W_A:3 * W_A]
        qkvb_ref[rows, :] = proj[:, 3 * W_A:3 * W_A + 3 * W_B]
        gate_ref[rows, :] = proj[:, 3 * W_A + 3 * W_B:3 * W_A + 4 * W_B]
        ab_ref[rows, :] = proj[:, 3 * W_A + 4 * W_B:IN_EVEN_PAD]

    _lockstep([chain(s) for s in range(SUB)])

    @pl.when(is_p)
    def _():
        kp_ref[...] = k_ref[...]
        vp_ref[...] = v_ref[...]


def _even_in(xp, xs, mod_l, nrm, w_in_pad, cos_t, sin_t):
    tile = lambda w: pl.BlockSpec((SUB * TM, w), lambda i: (i, 0))
    ptile = lambda w: pl.BlockSpec((SUB * TM, w), lambda i: (jnp.minimum(i, NPS - 1), 0))
    rope_idx = lambda i: (jnp.where(i < NPS, 0, 1 + (i - NPS) % (DEC_SEQ // (SUB * TM))), 0)
    return pl.pallas_call(
        _even_in_kernel,
        grid=(NS,),
        in_specs=[
            ptile(D),
            pl.BlockSpec((SUB * TM, D), lambda i: (jnp.maximum(i - NPS, 0), 0)),
            pl.BlockSpec((1, 1, N_MOD * D), lambda i: (_mod_row_step(i), 0, 0)),
            pl.BlockSpec((1, D), lambda i: (0, 0)),
            pl.BlockSpec((D, IN_EVEN_PAD), lambda i: (0, 0)),
            pl.BlockSpec((SUB * TM, 2 * DH_A), rope_idx),
            pl.BlockSpec((SUB * TM, 2 * DH_A), rope_idx),
        ],
        out_specs=[tile(W_A), tile(W_A), tile(W_A), tile(3 * W_B), tile(W_B), tile(128), ptile(W_A), ptile(W_A)],
        out_shape=[
            jax.ShapeDtypeStruct((N_TOK, W_A), bf16),
            jax.ShapeDtypeStruct((N_TOK, W_A), f32),
            jax.ShapeDtypeStruct((N_TOK, W_A), f32),
            jax.ShapeDtypeStruct((N_TOK, 3 * W_B), f32),
            jax.ShapeDtypeStruct((N_TOK, W_B), f32),
            jax.ShapeDtypeStruct((N_TOK, 128), f32),
            jax.ShapeDtypeStruct((N_P, W_A), f32),
            jax.ShapeDtypeStruct((N_P, W_A), f32),
        ],
        compiler_params=_cp(("arbitrary",)),
        name="even_in",
    )(xp, xs, mod_l, nrm, w_in_pad, cos_t, sin_t)


ATTN_HEADS_IN_LOCKSTEP = 2


def _attn_kernel(*refs, lam_init, has_ctx):
    if has_ctx:
        q_ref, kn_ref, vn_ref, kc_ref, vc_ref, lq_ref, lk_ref, sub_ref, o_ref = refs
    else:
        q_ref, kn_ref, vn_ref, lq_ref, lk_ref, sub_ref, o_ref = refs
    tq = q_ref.shape[0]
    lq = lq_ref[...]
    lk = lk_ref[...]
    prod = lq * lk
    lam = (jnp.exp(jnp.sum(prod[0:1], axis=-1, keepdims=True))
           - jnp.exp(jnp.sum(prod[1:2], axis=-1, keepdims=True)) + lam_init)
    lane = lax.broadcasted_iota(i32, (tq, 2 * DH_A), 1)
    nt = (((1,), (1,)), ((), ()))
    tn = (((0,), (0,)), ((), ()))
    sub_col = jnp.broadcast_to(sub_ref[...], (8, 2 * DH_A)).T[:, 0:1]
    outs = {}

    def softmax_part(hh, part, srcs):
        q = q_ref[:, hh * 2 * DH_A:(hh + 1) * 2 * DH_A]
        qm = jnp.where((lane < DH_A) if part == 0 else (lane >= DH_A), q, jnp.zeros_like(q))
        ss = [lax.dot_general(k, qm, nt, preferred_element_type=f32) for k, _ in srcs]
        yield
        m = ss[0].max(axis=0, keepdims=True)
        for s in ss[1:]:
            m = jnp.maximum(m, s.max(axis=0, keepdims=True))
        l = jnp.zeros((1, tq), f32)
        o = jnp.zeros((2 * DH_A, tq), f32)
        for s, (_, v) in zip(ss, srcs):
            e = jnp.exp2(s - m)
            l = l + jnp.sum(e, axis=0, keepdims=True)
            o = o + lax.dot_general(v, e.astype(bf16), tn, preferred_element_type=f32)
        yield
        outs[(hh, part)] = o / l

    for h0 in range(0, H_A, ATTN_HEADS_IN_LOCKSTEP):
        chains = []
        for hh in range(h0, h0 + ATTN_HEADS_IN_LOCKSTEP):
            sl = slice(hh * 2 * DH_A, (hh + 1) * 2 * DH_A)
            srcs = [(kn_ref[:, sl].astype(bf16), vn_ref[:, sl].astype(bf16))]
            if has_ctx:
                srcs.append((kc_ref[:, sl].astype(bf16), vc_ref[:, sl].astype(bf16)))
            chains += [softmax_part(hh, part, srcs) for part in range(2)]
        _lockstep(chains)
        for hh in range(h0, h0 + ATTN_HEADS_IN_LOCKSTEP):
            o = outs[(hh, 0)] - lam * outs[(hh, 1)]
            o = o * lax.rsqrt(jnp.mean(o * o, axis=0, keepdims=True) + EPS) * sub_col * (1.0 - lam_init)
            o_ref[:, hh * 2 * DH_A:(hh + 1) * 2 * DH_A] = o.T.astype(bf16)


def _attention(q, ka, va, lam_q, lam_k, subln, lam_init, *, seq, tq, tok_off, ctx=None):
    nq = seq // tq
    nb = (N_P if ctx is None else N_S) // seq
    qo, ko = tok_off // tq, tok_off // seq
    in_specs = [
        pl.BlockSpec((tq, W_A), lambda b, j: (qo + b * nq + j, 0)),
        pl.BlockSpec((seq, W_A), lambda b, j: (ko + b, 0)),
        pl.BlockSpec((seq, W_A), lambda b, j: (ko + b, 0)),
    ]
    args = [q, ka, va]
    if ctx is not None:
        in_specs += [pl.BlockSpec((PAST, W_A), lambda b, j: (b, 0))] * 2
        args += list(ctx)
    in_specs += [pl.BlockSpec((2, DH_A), lambda b, j: (0, 0))] * 2 + [pl.BlockSpec((1, 2 * DH_A), lambda b, j: (0, 0))]
    args += [lam_q, lam_k, subln]
    return pl.pallas_call(
        functools.partial(_attn_kernel, lam_init=lam_init, has_ctx=ctx is not None),
        grid=(nb, nq),
        in_specs=in_specs,
        out_specs=pl.BlockSpec((tq, W_A), lambda b, j: (b * nq + j, 0)),
        out_shape=jax.ShapeDtypeStruct((nb * seq, W_A), bf16),
        compiler_params=_cp(("arbitrary", "arbitrary")),
        name="diff_attn_ctx" if ctx is not None else "diff_attn",
    )(*args)


def _softplus(x):
    return jnp.maximum(x, 0.0) + jnp.log1p(jnp.exp(-jnp.abs(x)))


RW = H_B * CH
CPI = 4


def _rep_rows(x, times):
    return jnp.concatenate([x] * times, axis=0)


def _tri_inv(lmats, bd, ii, jj):
    b16 = (ii // 16) == (jj // 16)
    b32 = (ii // 32) == (jj // 32)
    eye = (ii == jj).astype(f32)
    n = len(lmats)
    mm = lambda a, b: jnp.dot(a.astype(bf16), bd(b), preferred_element_type=f32)
    l0 = [jnp.where(b16, l, 0.0) for l in lmats]
    x = [eye - a for a in l0]
    p = [mm(a, a) for a in l0]
    for level in range(3):
        xp = [mm(x[i], p[i]) for i in range(n)]
        if level < 2:
            p = [mm(a, a) for a in p]
        x = [x[i] + xp[i] for i in range(n)]
    for off_diag in (jnp.logical_and(b32, jnp.logical_not(b16)), jnp.logical_not(b32)):
        t = [mm(jnp.where(off_diag, lmats[i], 0.0), x[i]) for i in range(n)]
        t = [mm(x[i], t[i]) for i in range(n)]
        x = [x[i] - t[i] for i in range(n)]
    return x


def _delta_kernel(*refs, seq, has_init):
    if has_init:
        (qkv_ref, gate_ref, ab_ref, cw_ref, al_ref, dt_ref, dn_ref, sf0_ref, sb0_ref,
         ob_ref, sf_ref, sb_ref, xpad, xq, xk, xv, gacc, bacc, o_f, o_b, st, pu, pw, pqe, pkd, patt, ptot) = refs
    else:
        (qkv_ref, gate_ref, ab_ref, cw_ref, al_ref, dt_ref, dn_ref,
         ob_ref, sf_ref, sb_ref, xpad, xq, xk, xv, gacc, bacc, o_f, o_b, st, pu, pw, pqe, pkd, patt, ptot) = refs
    n = seq // CH
    win = CH + 16

    xpad[0:8, :] = jnp.zeros((8, 3 * W_B), f32)
    xpad[seq + 8:seq + 16, :] = jnp.zeros((8, 3 * W_B), f32)
    xpad[8:seq + 8, :] = qkv_ref[...]
    neg_a = -jnp.exp(al_ref[...])
    dtb = dt_ref[...]

    def pre(c, carry):
        r0 = pl.multiple_of(c * CH, CH)
        a = xpad[pl.ds(r0, win), :]
        y = jnp.zeros((CH, 3 * W_B), f32)
        for j in range(CONV_K):
            sh = ((CONV_K - 1) // 2 - j) % win
            y = y + pltpu.roll(a, sh, 0)[8:8 + CH] * cw_ref[j:j + 1, :]
        y = _silu(y)
        for hh in range(H_B):
            qh = y[:, hh * DK_B:(hh + 1) * DK_B]
            kh = y[:, W_B + hh * DK_B:W_B + (hh + 1) * DK_B]
            qn = qh * lax.rsqrt(jnp.sum(qh * qh, axis=-1, keepdims=True) + EPS) * (DK_B ** -0.5)
            kn = kh * lax.rsqrt(jnp.sum(kh * kh, axis=-1, keepdims=True) + EPS)
            xq[pl.ds(r0, CH), hh * DK_B:(hh + 1) * DK_B] = qn
            xk[pl.ds(r0, CH), hh * DK_B:(hh + 1) * DK_B] = kn
        xv[pl.ds(r0, CH), :] = y[:, 2 * W_B:3 * W_B]
        ab = ab_ref[pl.ds(r0, CH), :]
        gacc[pl.ds(r0, CH), :] = neg_a * _softplus(ab + dtb)
        bacc[pl.ds(r0, CH), :] = jax.nn.sigmoid(ab)
        return carry

    lax.fori_loop(0, n, pre, 0)

    ii = lax.broadcasted_iota(i32, (CH, RW), 0)
    lane_r = lax.broadcasted_iota(i32, (CH, RW), 1)
    jj = lane_r % CH
    hb = lane_r // CH
    rowl = lax.broadcasted_iota(i32, (CH, 128), 0)
    bd_mask = (lax.broadcasted_iota(i32, (RW, RW), 0) // CH) == (lax.broadcasted_iota(i32, (RW, RW), 1) // CH)
    wide_mask = ((lax.broadcasted_iota(i32, (RW, W_B), 0) // CH)
                 == (lax.broadcasted_iota(i32, (RW, W_B), 1) // DK_B))
    pair_mask = ((lax.broadcasted_iota(i32, (2 * DK_B, 2 * DV_B), 0) // DK_B)
                 == (lax.broadcasted_iota(i32, (2 * DK_B, 2 * DV_B), 1) // DV_B))
    zero_b = jnp.zeros((), bf16)

    def bd(x):
        return jnp.where(bd_mask, _rep_rows(x.astype(bf16), H_B), zero_b)

    def bd_wide(x):
        return jnp.where(wide_mask, _rep_rows(x.astype(bf16), H_B), zero_b)

    def per_head(src, lanes):
        return jnp.concatenate([jnp.broadcast_to(src[:, ln:ln + 1], (src.shape[0], DK_B)) for ln in lanes], axis=1)

    nt = (((1,), (1,)), ((), ()))
    tn = (((0,), (0,)), ((), ()))

    def local(it, carry):
        chains = []
        for j in range(CPI):
            c = it * CPI + j
            r0 = pl.multiple_of(c * CH, CH)
            g = gacc[pl.ds(r0, CH), :]
            beta = bacc[pl.ds(r0, CH), :]
            pre_g = g
            for s in (1, 2, 4, 8, 16, 32):
                pre_g = pre_g + jnp.where(rowl >= s, pltpu.roll(pre_g, s, 0), 0.0)
            tot = pre_g[CH - 1:CH, :]
            q_all = xq[pl.ds(r0, CH), :]
            k_all = xk[pl.ds(r0, CH), :]
            v_all = xv[pl.ds(r0, CH), :]
            k_bd = bd_wide(k_all)
            for d in range(2):
                gc = pre_g if d == 0 else tot - pre_g + g
                lanes = [d * 2 * H_B + hh for hh in range(H_B)]
                gct = _rep_rows(gc, H_B).T
                g_col = jnp.zeros((CH, RW), f32)
                g_row = jnp.zeros((1, RW), f32)
                for hh, ln in enumerate(lanes):
                    g_col = jnp.where(hb == hh, gc[:, ln:ln + 1], g_col)
                    g_row = jnp.where(hb[0:1] == hh, gct[ln:ln + 1, :], g_row)
                g_wide = per_head(gc, lanes)
                b_wide = per_head(beta, [ln + H_B for ln in lanes])
                incl = (ii >= jj) if d == 0 else (ii <= jj)
                strict = (ii > jj) if d == 0 else (ii < jj)
                eg = jnp.exp(g_wide)
                kb = k_all * b_wide
                chains.append(dict(
                    ci=c * 2 + d, incl=incl, strict=strict, k_bd=k_bd, kb=kb, q=q_all, tot=tot,
                    dec=jnp.where(incl, jnp.exp(jnp.where(incl, g_col - g_row, 0.0)), 0.0),
                    vb=v_all * b_wide, kbe=kb * eg, qe=q_all * eg,
                    kd=k_all * jnp.exp(per_head(tot, lanes) - g_wide)))
        kk = [lax.dot_general(e["kb"].astype(bf16), e["k_bd"], nt, preferred_element_type=f32) for e in chains]
        qk = [lax.dot_general(e["q"].astype(bf16), e["k_bd"], nt, preferred_element_type=f32) for e in chains]
        lmat = [jnp.where(e["strict"], kk[i] * e["dec"], 0.0) for i, e in enumerate(chains)]
        tinv = _tri_inv(lmat, bd, ii, jj)
        u = [jnp.dot(tinv[i].astype(bf16), bd_wide(e["vb"]), preferred_element_type=f32) for i, e in enumerate(chains)]
        w = [jnp.dot(tinv[i].astype(bf16), bd_wide(e["kbe"]), preferred_element_type=f32) for i, e in enumerate(chains)]
        for i, e in enumerate(chains):
            ci = e["ci"]
            pu[ci] = u[i]
            pw[ci] = w[i].astype(bf16)
            pqe[ci] = e["qe"].astype(bf16)
            pkd[ci] = e["kd"].astype(bf16)
            patt[ci] = jnp.where(e["incl"], qk[i] * e["dec"], 0.0).astype(bf16)
            ptot[ci] = jnp.broadcast_to(e["tot"], (8, 128))
        return carry

    lax.fori_loop(0, n // CPI, local, 0)

    for hh in range(H_B):
        rows = slice(hh * DK_B, (hh + 1) * DK_B)
        if has_init:
            st[0, rows, :] = sf0_ref[0, hh]
            st[1, rows, :] = sb0_ref[0, hh]
        else:
            st[0, rows, :] = jnp.zeros((DK_B, DV_B), f32)
            st[1, rows, :] = jnp.zeros((DK_B, DV_B), f32)

    def scan(it, carry):
        chains = []
        for d in range(2):
            c = it if d == 0 else n - 1 - it
            ci = c * 2 + d
            s_old = st[d]
            s_b = s_old.astype(bf16)
            chains.append(dict(
                d=d, r0=pl.multiple_of(c * CH, CH), s_old=s_old, u=pu[ci], att=patt[ci], kd=pkd[ci], tot=ptot[ci],
                lhs=jnp.concatenate([pw[ci], pqe[ci]], axis=0),
                rhs=[jnp.where(pair_mask, jnp.concatenate([s_b[p * 2 * DK_B:(p + 1) * 2 * DK_B]] * 2, axis=1), zero_b)
                     for p in range(2)]))
        ws_qs = [jnp.concatenate(
            [jnp.dot(e["lhs"][:, p * 2 * DK_B:(p + 1) * 2 * DK_B], e["rhs"][p], preferred_element_type=f32)
             for p in range(2)], axis=1) for e in chains]
        v_new = [e["u"] - ws_qs[i][0:CH] for i, e in enumerate(chains)]
        av = [jnp.dot(e["att"], bd_wide(v_new[i]), preferred_element_type=f32) for i, e in enumerate(chains)]
        su = [lax.dot_general(jnp.where(wide_mask, _rep_rows(e["kd"], H_B), zero_b),
                              jnp.concatenate([v_new[i][:, hh * DV_B:(hh + 1) * DV_B] for hh in range(H_B)],
                                              axis=0).astype(bf16),
                              tn, preferred_element_type=f32) for i, e in enumerate(chains)]
        for i, e in enumerate(chains):
            d = e["d"]
            if d == 0:
                o_f[pl.ds(e["r0"], CH), :] = ws_qs[i][CH:2 * CH] + av[i]
            else:
                o_b[pl.ds(e["r0"], CH), :] = ws_qs[i][CH:2 * CH] + av[i]
            for hh in range(H_B):
                rows = slice(hh * DK_B, (hh + 1) * DK_B)
                ln = d * 2 * H_B + hh
                st[d, rows, :] = e["s_old"][rows] * jnp.exp(e["tot"][0:1, ln:ln + 1]) + su[i][rows]
        return carry

    lax.fori_loop(0, n, scan, 0)

    def post(c, carry):
        r0 = pl.multiple_of(c * CH, CH)
        o = o_f[pl.ds(r0, CH), :] + o_b[pl.ds(r0, CH), :]
        gt = gate_ref[pl.ds(r0, CH), :]
        for hh in range(H_B):
            sl = slice(hh * DV_B, (hh + 1) * DV_B)
            ob_ref[pl.ds(r0, CH), sl] = (_rms(o[:, sl], dn_ref[...]) * _silu(gt[:, sl])).astype(bf16)
        return carry

    lax.fori_loop(0, n, post, 0)
    for hh in range(H_B):
        sf_ref[0, hh] = st[0, hh * DK_B:(hh + 1) * DK_B, :]
        sb_ref[0, hh] = st[1, hh * DK_B:(hh + 1) * DK_B, :]


def _delta(qkvb, gate, ab, conv_pad, al_lane, dt_lane, dnorm, *, seq, tok_off, init=None):
    nb = (N_P if init is None else N_S) // seq
    bo = tok_off // seq
    rows = lambda w: pl.BlockSpec((seq, w), lambda b: (bo + b, 0))
    const = lambda shp: pl.BlockSpec(shp, lambda b: tuple(0 for _ in shp))
    st_spec = pl.BlockSpec((1, H_B, DK_B, DV_B), lambda b: (b, 0, 0, 0))
    in_specs = [rows(3 * W_B), rows(W_B), rows(128), const((8, 3 * W_B)), const((1, 128)), const((1, 128)),
                const((1, DV_B))]
    args = [qkvb, gate, ab, conv_pad, al_lane, dt_lane, dnorm]
    if init is not None:
        in_specs += [st_spec, st_spec]
        args += [init[0], init[1]]
    return pl.pallas_call(
        functools.partial(_delta_kernel, seq=seq, has_init=init is not None),
        grid=(nb,),
        in_specs=in_specs,
        out_specs=[pl.BlockSpec((seq, W_B), lambda b: (b, 0)), st_spec, st_spec],
        out_shape=[
            jax.ShapeDtypeStruct((nb * seq, W_B), bf16),
            jax.ShapeDtypeStruct((nb, H_B, DK_B, DV_B), f32),
            jax.ShapeDtypeStruct((nb, H_B, DK_B, DV_B), f32),
        ],
        scratch_shapes=[
            pltpu.VMEM((seq + 16, 3 * W_B), f32),
            pltpu.VMEM((seq, W_B), f32),
            pltpu.VMEM((seq, W_B), f32),
            pltpu.VMEM((seq, W_B), f32),
            pltpu.VMEM((seq, 128), f32),
            pltpu.VMEM((seq, 128), f32),
            pltpu.VMEM((seq, W_B), f32),
            pltpu.VMEM((seq, W_B), f32),
            pltpu.VMEM((2, H_B * DK_B, DV_B), f32),
            pltpu.VMEM((2 * seq // CH, CH, W_B), f32),
            pltpu.VMEM((2 * seq // CH, CH, W_B), bf16),
            pltpu.VMEM((2 * seq // CH, CH, W_B), bf16),
            pltpu.VMEM((2 * seq // CH, CH, W_B), bf16),
            pltpu.VMEM((2 * seq // CH, CH, RW), bf16),
            pltpu.VMEM((2 * seq // CH, 8, 128), f32),
        ],
        compiler_params=_cp(("arbitrary",)),
        name="delta_ctx" if init is not None else "delta",
    )(*args)


RG = N_TOK
TMG = 256
MAXT_G = N_TOK // TMG + N_GROUPS
TILE_ROWS = TM * ROW_SUB
RUN_BITS = tuple(range(TM.bit_length() - 1, -1, -1))


def _hi_lo(x):
    hi = x.astype(bf16)
    return hi, (x - hi.astype(f32)).astype(bf16)


def _run_pieces(n):
    return [(1 << b, (n >> (b + 1)) << (b + 1), ((n >> b) & 1) == 1) for b in RUN_BITS]


def _route_chain(s, x1, mod, nrm_ref, wr_ref, br_ref, route_ref, cnt_ref, hbuf, gbuf, counts):
    rows = slice(s * TM, (s + 1) * TM)
    h2 = _rms(x1, nrm_ref[...]) * (1.0 + mod[:, 4 * D:5 * D]) + mod[:, 3 * D:4 * D]
    h_hi, h_lo = _hi_lo(h2)
    w_hi, w_lo = wr_ref[0], wr_ref[1]
    logits = (jnp.dot(h_hi, w_hi, preferred_element_type=f32) + jnp.dot(h_hi, w_lo, preferred_element_type=f32)
              + jnp.dot(h_lo, w_hi, preferred_element_type=f32)) + br_ref[...]
    yield
    lane = lax.broadcasted_iota(i32, (TM, 128), 1)
    lane_f = lane.astype(f32)
    neg = jnp.float32(-1e30)
    lg = jnp.where(lane < N_GROUPS, logits[:, 0:128], neg)
    mg = lg.max(axis=-1, keepdims=True)
    gidx = jnp.min(jnp.where(lg == mg, lane_f, 128.0), axis=-1, keepdims=True)
    pg = 1.0 / jnp.sum(jnp.exp(lg - mg), axis=-1, keepdims=True)
    le = jnp.where((lane // E_PER_GROUP).astype(f32) == gidx, logits[:, 128:256], neg)
    m1 = le.max(axis=-1, keepdims=True)
    i1 = jnp.min(jnp.where(le == m1, lane_f, 128.0), axis=-1, keepdims=True)
    le2 = jnp.where(lane_f == i1, neg, le)
    m2 = le2.max(axis=-1, keepdims=True)
    i2 = jnp.min(jnp.where(le2 == m2, lane_f, 128.0), axis=-1, keepdims=True)
    e2 = jnp.exp(m2 - m1)
    w1 = pg / (1.0 + e2)
    w2 = w1 * e2
    local = lane_f + gidx * E_PER_GROUP
    gates = jnp.where(local == i1, w1, jnp.where(local == i2, w2, 0.0))

    onehot = (lane_f == gidx).astype(f32)
    r = lax.broadcasted_iota(i32, (TM, TM), 0)
    c = lax.broadcasted_iota(i32, (TM, TM), 1)
    before = jnp.dot((r > c).astype(bf16), onehot.astype(bf16), preferred_element_type=f32)
    yield
    cnt = jnp.sum(onehot, axis=0, keepdims=True)
    pos = (jnp.sum(onehot * before, axis=-1, keepdims=True)
           + jnp.sum(jnp.where(lane_f < gidx, cnt, 0.0), axis=-1, keepdims=True))
    route_ref[rows, :] = jnp.where(lane == 0, pos, 0.0)
    cnt_ref[s] = cnt
    counts[s] = cnt.astype(i32)

    perm_t = (c.astype(f32) == pos).astype(bf16)
    tn = (((0,), (0,)), ((), ()))
    h_sorted = lax.dot_general(perm_t, h_hi, tn, preferred_element_type=f32)
    g_hi, g_lo = _hi_lo(gates)
    g_sorted = (lax.dot_general(perm_t, g_hi, tn, preferred_element_type=f32)
                + lax.dot_general(perm_t, g_lo, tn, preferred_element_type=f32))
    yield
    _store_rows(hbuf, s * TILE_ROWS, h_sorted)
    gbuf[s * TM:(s + 1) * TM, :] = g_sorted


def _dispatch_done(s, hs_hbm, gs_hbm, hbuf, gbuf, sem):
    pltpu.make_async_copy(hbuf.at[pl.ds(s * TILE_ROWS, TILE_ROWS), :], hs_hbm.at[pl.ds(0, TILE_ROWS), :],
                          sem.at[s]).wait()
    pltpu.make_async_copy(gbuf.at[pl.ds(s * TM, TM), :], gs_hbm.at[pl.ds(0, TM), :], sem.at[s]).wait()


def _dispatch_runs(s, cnt_i, hs_hbm, gs_hbm, hbuf, gbuf, run, sem):
    off = 0
    for g in range(N_GROUPS):
        n = cnt_i[0, g]
        dst = g * RG + run[g]
        for rows, o, pred in _run_pieces(n):
            @pl.when(pred)
            def _():
                pltpu.make_async_copy(
                    hbuf.at[pl.ds(pl.multiple_of(s * TILE_ROWS + (off + o) * ROW_SUB, ROW_SUB), rows * ROW_SUB), :],
                    hs_hbm.at[pl.ds(pl.multiple_of((dst + o) * ROW_SUB, ROW_SUB), rows * ROW_SUB), :],
                    sem.at[s]).start()
                pltpu.make_async_copy(gbuf.at[pl.ds(s * TM + off + o, rows), :], gs_hbm.at[pl.ds(dst + o, rows), :],
                                      sem.at[s]).start()
        run[g] = run[g] + n
        off = off + n


def _dispatch_tail_fill(hs_hbm, gs_hbm, hbuf, gbuf, run, sem):
    hbuf[pl.ds(0, TMG * ROW_SUB), :] = jnp.zeros((TMG * ROW_SUB, 128), f32)
    gbuf[pl.ds(0, TMG), :] = jnp.zeros((TMG, 128), f32)
    for g in range(N_GROUPS):
        tot = run[g]
        pad = (-tot) & (TMG - 1)
        for rows, o, pred in _run_pieces(pad):
            @pl.when(pred)
            def _():
                dst = g * RG + tot + o
                ch = pltpu.make_async_copy(
                    hbuf.at[pl.ds(0, rows * ROW_SUB), :],
                    hs_hbm.at[pl.ds(pl.multiple_of(dst * ROW_SUB, ROW_SUB), rows * ROW_SUB), :], sem.at[0])
                cg = pltpu.make_async_copy(gbuf.at[pl.ds(0, rows), :], gs_hbm.at[pl.ds(dst, rows), :], sem.at[1])
                ch.start()
                cg.start()
                ch.wait()
                cg.wait()


def _mixer_out_tail(i, chains, counts, hs_hbm, gs_hbm, hbuf, gbuf, run, sem):
    @pl.when(i == 0)
    def _():
        for g in range(N_GROUPS):
            run[g] = 0

    sub = len(chains)

    @pl.when(i >= 1)
    def _():
        for s in range(sub):
            _dispatch_done(s, hs_hbm, gs_hbm, hbuf, gbuf, sem)

    _lockstep(chains)
    for s in range(sub):
        _dispatch_runs(s, counts[s], hs_hbm, gs_hbm, hbuf, gbuf, run, sem)

    @pl.when(i == pl.num_programs(0) - 1)
    def _():
        for s in range(sub):
            _dispatch_done(s, hs_hbm, gs_hbm, hbuf, gbuf, sem)
        _dispatch_tail_fill(hs_hbm, gs_hbm, hbuf, gbuf, run, sem)


def _dispatch_out_specs(sub=SUB):
    return [
        pl.BlockSpec((sub * TM, D), lambda i, *_: (i, 0)),
        pl.BlockSpec((sub * TM, 128), lambda i, *_: (i, 0)),
        pl.BlockSpec((sub, 1, 128), lambda i, *_: (i, 0, 0)),
        pl.BlockSpec(memory_space=pl.ANY),
        pl.BlockSpec(memory_space=pl.ANY),
    ]


_DISPATCH_OUT_SHAPES = [
    jax.ShapeDtypeStruct((N_TOK, D), f32),
    jax.ShapeDtypeStruct((N_TOK, 128), f32),
    jax.ShapeDtypeStruct((NT, 1, 128), f32),
    jax.ShapeDtypeStruct((N_GROUPS * RG * ROW_SUB, 128), f32),
    jax.ShapeDtypeStruct((N_GROUPS * RG, 128), f32),
]


def _dispatch_scratch(sub=SUB):
    return [
        pltpu.VMEM((sub * TILE_ROWS, 128), f32),
        pltpu.VMEM((sub * TM, 128), f32),
        pltpu.SMEM((N_GROUPS,), i32),
        pltpu.SemaphoreType.DMA((sub,)),
    ]


def _router_in_specs():
    return [pl.BlockSpec((1, D), lambda i, *_: (0, 0)),
            pl.BlockSpec((2, D, 256), lambda i, *_: (0, 0, 0)),
            pl.BlockSpec((1, 256), lambda i, *_: (0, 0))]


def _fetch_runs(n_ref, s_ref, y_hbm, ybuf, sem, tile, slot):
    base = slot * TILE_ROWS
    off = 0
    for g in range(N_GROUPS):
        n = n_ref[tile * N_GROUPS + g]
        src = s_ref[tile * N_GROUPS + g]
        for rows, o, pred in _run_pieces(n):
            @pl.when(pred)
            def _():
                pltpu.make_async_copy(
                    y_hbm.at[pl.ds(pl.multiple_of((src + o) * ROW_SUB, ROW_SUB), rows * ROW_SUB), :],
                    ybuf.at[pl.ds(pl.multiple_of(base + (off + o) * ROW_SUB, ROW_SUB), rows * ROW_SUB), :],
                    sem.at[slot]).start()
        off = off + n


def _combine_fetch(i, n_steps, sub, tile0, n_ref, s_ref, y_hbm, ybuf, sem):
    half = (i % 2) * sub

    @pl.when(i == 0)
    def _():
        for s in range(sub):
            _fetch_runs(n_ref, s_ref, y_hbm, ybuf, sem, tile0 + s, s)

    @pl.when(i + 1 < n_steps)
    def _():
        for s in range(sub):
            _fetch_runs(n_ref, s_ref, y_hbm, ybuf, sem, tile0 + (i + 1) * sub + s, sub - half + s)

    bases = []
    for s in range(sub):
        base = pl.multiple_of((half + s) * TILE_ROWS, TILE_ROWS)
        pltpu.make_async_copy(y_hbm.at[pl.ds(0, TILE_ROWS), :], ybuf.at[pl.ds(base, TILE_ROWS), :],
                              sem.at[half + s]).wait()
        bases.append(base)
    return bases


def _unsort(ybuf, base, pos):
    y_hi, y_lo = _hi_lo(_load_rows(ybuf, base, TM))
    perm_t = (lax.broadcasted_iota(i32, (TM, TM), 1).astype(f32) == pos).astype(bf16)
    return jnp.dot(perm_t, y_hi, preferred_element_type=f32) + jnp.dot(perm_t, y_lo, preferred_element_type=f32)


def _combine_scratch(sub):
    return [pltpu.VMEM((2 * sub * TILE_ROWS, 128), f32), pltpu.SemaphoreType.DMA((2 * sub,))]


SUB_EVEN_OUT = 4


def _even_out_kernel(xp_ref, xs_ref, oap_ref, oas_ref, obp_ref, obs_ref, mod_ref, w_ref, nrm_ref, wr_ref, br_ref,
                     x1_ref, route_ref, cnt_ref, hs_hbm, gs_hbm, hbuf, gbuf, run, sem):
    i = pl.program_id(0)
    is_p = i < NPT // SUB_EVEN_OUT
    mod = mod_ref[0]
    counts = [None] * SUB_EVEN_OUT

    def chain(s):
        rows = slice(s * TM, (s + 1) * TM)
        x = jnp.where(is_p, xp_ref[rows, :], xs_ref[rows, :])
        oa = jnp.where(is_p, oap_ref[rows, :], oas_ref[rows, :])
        ob = jnp.where(is_p, obp_ref[rows, :], obs_ref[rows, :])
        out = (jnp.dot(oa, w_ref[0:W_A, :], preferred_element_type=f32)
               + jnp.dot(ob, w_ref[W_A:W_A + W_B, :], preferred_element_type=f32))
        yield
        x1 = x + mod[:, 2 * D:3 * D] * out
        x1_ref[rows, :] = x1
        yield from _route_chain(s, x1, mod, nrm_ref, wr_ref, br_ref, route_ref, cnt_ref, hbuf, gbuf, counts)

    _mixer_out_tail(i, [chain(s) for s in range(SUB_EVEN_OUT)], counts, hs_hbm, gs_hbm, hbuf, gbuf, run, sem)


def _even_out(xp, xs, oa_p, oa_s, ob_p, ob_s, mod_l, w_out, nrm, wr, br):
    sub = SUB_EVEN_OUT
    ptile = lambda w: pl.BlockSpec((sub * TM, w), lambda i: (jnp.minimum(i, NPT // sub - 1), 0))
    stile = lambda w: pl.BlockSpec((sub * TM, w), lambda i: (jnp.maximum(i - NPT // sub, 0), 0))
    return pl.pallas_call(
        _even_out_kernel,
        grid=(NT // sub,),
        in_specs=[
            ptile(D), stile(D), ptile(W_A), stile(W_A), ptile(W_B), stile(W_B),
            pl.BlockSpec((1, 1, N_MOD * D), lambda i: (_mod_row_step(i, sub), 0, 0)),
            pl.BlockSpec((W_A + W_B, D), lambda i: (0, 0)),
        ] + _router_in_specs(),
        out_specs=_dispatch_out_specs(sub),
        out_shape=_DISPATCH_OUT_SHAPES,
        scratch_shapes=_dispatch_scratch(sub),
        compiler_params=_cp(("arbitrary",)),
        name="even_out_router",
    )(xp, xs, oa_p, oa_s, ob_p, ob_s, mod_l, w_out, nrm, wr, br)


def _gelu_tanh(x):
    return 0.5 * x * (1.0 + jnp.tanh(math.sqrt(2.0 / math.pi) * (x + 0.044715 * (x * x * x))))


def _odd_kernel(n_ref, s_ref, x_ref, proute_ref, y_hbm, modp_ref, mod_ref, nm_ref, win_ref, lng_ref, lnb_ref,
                ws_ref, bs_ref, wout_ref, nrm_ref, wr_ref, br_ref,
                x1_ref, route_ref, cnt_ref, hs_hbm, gs_hbm, gated, ybuf, ysem, hbuf, gbuf, run, sem):
    i = pl.program_id(0)
    modp = modp_ref[0]
    mod = mod_ref[0]
    counts = [None] * SUB
    ybase = _combine_fetch(i, NS, SUB, 0, n_ref, s_ref, y_hbm, ybuf, ysem)
    gw = W_C // C_GROUPS

    def chain(s):
        rows = slice(s * TM, (s + 1) * TM)
        y = _unsort(ybuf, ybase[s], proute_ref[rows, 0:1])
        yield
        x = x_ref[rows, :] + modp[:, 5 * D:6 * D] * y
        h = _rms(x, nm_ref[...]) * (1.0 + mod[:, D:2 * D]) + mod[:, 0:D]
        z = _dot(h, win_ref[...])
        yield
        z = _gelu_tanh(z)
        u = z[:, 0:W_C]
        v = z[:, W_C:2 * W_C]
        mu = jnp.mean(v, axis=-1, keepdims=True)
        vc = v - mu
        var = jnp.mean(vc * vc, axis=-1, keepdims=True)
        vn = (vc * lax.rsqrt(var + EPS) * lng_ref[...] + lnb_ref[...]).astype(bf16)
        for ck in range(TM // C_CHUNK):
            rs = slice(ck * C_CHUNK, (ck + 1) * C_CHUNK)
            for g in range(C_GROUPS):
                cs = slice(g * gw, (g + 1) * gw)
                mixed = jnp.dot(ws_ref[g], vn[rs, cs], preferred_element_type=f32) + bs_ref[g]
                gated[s * TM + ck * C_CHUNK:s * TM + (ck + 1) * C_CHUNK, cs] = (u[rs, cs] * mixed).astype(bf16)
        yield
        out = jnp.dot(gated[rows, :], wout_ref[...], preferred_element_type=f32)
        yield
        x1 = x + mod[:, 2 * D:3 * D] * out
        x1_ref[rows, :] = x1
        yield from _route_chain(s, x1, mod, nrm_ref, wr_ref, br_ref, route_ref, cnt_ref, hbuf, gbuf, counts)

    _mixer_out_tail(i, [chain(s) for s in range(SUB)], counts, hs_hbm, gs_hbm, hbuf, gbuf, run, sem)


def _odd_mixer(run_n, run_src, x, prev_route, y, mod_prev, mod_l, nm, w_in, ln_g, ln_b, ws, bs, w_out, nrm, wr, br):
    tile = lambda w: pl.BlockSpec((SUB * TM, w), lambda i, *_: (i, 0))
    modspec = pl.BlockSpec((1, 1, N_MOD * D), lambda i, *_: (_mod_row_step(i), 0, 0))
    const = lambda shp: pl.BlockSpec(shp, lambda i, *_: tuple(0 for _ in shp))
    grid_spec = pltpu.PrefetchScalarGridSpec(
        num_scalar_prefetch=2,
        grid=(NS,),
        in_specs=[tile(D), tile(128), pl.BlockSpec(memory_space=pl.ANY), modspec, modspec, const((1, D)),
                  const((D, 2 * W_C)), const((1, W_C)), const((1, W_C)), const((C_GROUPS, C_CHUNK, C_CHUNK)),
                  const((C_GROUPS, C_CHUNK, W_C // C_GROUPS)), const((W_C, D))] + _router_in_specs(),
        out_specs=_dispatch_out_specs(),
        scratch_shapes=[pltpu.VMEM((SUB * TM, W_C), bf16)] + _combine_scratch(SUB) + _dispatch_scratch(),
    )
    return pl.pallas_call(
        _odd_kernel,
        grid_spec=grid_spec,
        out_shape=_DISPATCH_OUT_SHAPES,
        compiler_params=_cp(("arbitrary",)),
        name="odd_mixer_router",
    )(run_n, run_src, x, prev_route, y, mod_prev, mod_l, nm, w_in, ln_g, ln_b, ws, bs, w_out, nrm, wr, br)


def _expert_kernel(nu_ref, tg_ref, tb_ref, hs_ref, gs_ref, wg_ref, wu_ref, wd_ref, y_ref):
    t = pl.program_id(0)

    @pl.when(t < nu_ref[0])
    def _():
        xb = _load_rows(hs_ref, 0, TMG).astype(bf16)
        gates = gs_ref[...]
        acc = jnp.zeros((TMG, D), f32)
        gate_up = lambda e: (_dot(xb, wg_ref[e]), _dot(xb, wu_ref[e]))
        nxt = gate_up(0)
        for e in range(E_PER_GROUP):
            a, b = nxt
            if e + 1 < E_PER_GROUP:
                nxt = gate_up(e + 1)
            acc = acc + _dot(_silu(a) * b * gates[:, e:e + 1], wd_ref[e])
        _store_rows(y_ref, 0, acc)


def _experts(layer, n_used, tile_g, tile_blk, hs, gs, wg, wu, wd):
    wspec = lambda a, b: pl.BlockSpec((None, None, E_PER_GROUP, a, b), lambda t, nu, tg, tb: (layer, tg[t], 0, 0, 0))
    grid_spec = pltpu.PrefetchScalarGridSpec(
        num_scalar_prefetch=3,
        grid=(MAXT_G,),
        in_specs=[
            pl.BlockSpec((TMG * ROW_SUB, 128), lambda t, nu, tg, tb: (tb[t], 0)),
            pl.BlockSpec((TMG, 128), lambda t, nu, tg, tb: (tb[t], 0)),
            wspec(D, D_EXPERT), wspec(D, D_EXPERT), wspec(D_EXPERT, D),
        ],
        out_specs=pl.BlockSpec((TMG * ROW_SUB, 128), lambda t, nu, tg, tb: (tb[t], 0)),
    )
    return pl.pallas_call(
        _expert_kernel,
        grid_spec=grid_spec,
        out_shape=jax.ShapeDtypeStruct((N_GROUPS * RG * ROW_SUB, 128), f32),
        compiler_params=_cp(("arbitrary",), VMEM_LIMIT_EXPERTS),
        name="experts",
    )(n_used, tile_g, tile_blk, hs, gs, wg, wu, wd)


def _dispatch_plan(cnt_tiles):
    cnt = cnt_tiles[:, 0, 0:N_GROUPS].astype(i32)
    ends = jnp.cumsum(cnt, axis=0)
    run_src = (ends - cnt + jnp.arange(N_GROUPS, dtype=i32)[None, :] * RG).reshape(-1)
    totals = ends[NT - 1]
    tiles_g = (totals + TMG - 1) // TMG
    tile_end = jnp.cumsum(tiles_g)
    n_used = tile_end[N_GROUPS - 1:]
    t = jnp.minimum(jnp.arange(MAXT_G, dtype=i32), n_used[0] - 1)
    tile_g = jnp.sum((tile_end[None, :] <= t[:, None]).astype(i32), axis=1)
    tile_start = tile_end - tiles_g
    first = jnp.sum(jnp.where(tile_g[:, None] == jnp.arange(N_GROUPS, dtype=i32)[None, :], tile_start[None, :], 0), axis=1)
    tile_blk = tile_g * (RG // TMG) + t - first
    return cnt.reshape(-1), run_src, n_used, tile_g, tile_blk


def _final_kernel(n_ref, s_ref, x_ref, route_ref, y_hbm, mod_ref, fn_ref, o_ref, ybuf, ysem, *, n_steps, tile_off):
    i = pl.program_id(0)
    mod = mod_ref[0]
    ybase = _combine_fetch(i, n_steps, SUB, tile_off, n_ref, s_ref, y_hbm, ybuf, ysem)

    def chain(s):
        rows = slice(s * TM, (s + 1) * TM)
        y = _unsort(ybuf, ybase[s], route_ref[rows, 0:1])
        yield
        x = x_ref[rows, :] + mod[:, 5 * D:6 * D] * y
        o_ref[rows, :] = _rms(x, fn_ref[...])

    _lockstep([chain(s) for s in range(SUB)])


def _final(run_n, run_src, x, route, y, mod_l, fnorm, *, n_tiles, tile_off):
    n_steps, step_off = n_tiles // SUB, tile_off // SUB
    grid_spec = pltpu.PrefetchScalarGridSpec(
        num_scalar_prefetch=2,
        grid=(n_steps,),
        in_specs=[
            pl.BlockSpec((SUB * TM, D), lambda i, *_: (step_off + i, 0)),
            pl.BlockSpec((SUB * TM, 128), lambda i, *_: (step_off + i, 0)),
            pl.BlockSpec(memory_space=pl.ANY),
            pl.BlockSpec((1, 1, N_MOD * D), lambda i, *_: (_mod_row_step(step_off + i), 0, 0)),
            pl.BlockSpec((1, D), lambda i, *_: (0, 0)),
        ],
        out_specs=pl.BlockSpec((SUB * TM, D), lambda i, *_: (i, 0)),
        scratch_shapes=_combine_scratch(SUB),
    )
    return pl.pallas_call(
        functools.partial(_final_kernel, n_steps=n_steps, tile_off=tile_off),
        grid_spec=grid_spec,
        out_shape=jax.ShapeDtypeStruct((n_tiles * TM, D), f32),
        compiler_params=_cp(("arbitrary",)),
        name="final_norm",
    )(run_n, run_src, x, route, y, mod_l, fnorm)


def _rope_tables():
    rows = DEC_SEQ // GRID_W
    r, c = jnp.meshgrid(jnp.arange(rows), jnp.arange(GRID_W), indexing="ij")
    pos = jnp.stack([r.reshape(-1), c.reshape(-1)], axis=-1).astype(f32)
    inv = ROPE_BASE ** (-jnp.arange(AXIS_FREQS, dtype=f32) / AXIS_FREQS)
    ang = pos[:, :, None] * inv
    cos, sin = jnp.cos(ang), jnp.sin(ang)
    cos_h = jnp.concatenate([cos, cos], axis=-1).reshape(DEC_SEQ, DH_A)
    sin_h = jnp.concatenate([-sin, sin], axis=-1).reshape(DEC_SEQ, DH_A)
    cos_t = jnp.concatenate([jnp.ones((SUB * TM, 2 * DH_A), f32), jnp.tile(cos_h, (1, 2))], axis=0)
    sin_t = jnp.concatenate([jnp.zeros((SUB * TM, 2 * DH_A), f32), jnp.tile(sin_h, (1, 2))], axis=0)
    return cos_t, sin_t


def _gate_lanes(p):
    out = jnp.zeros((128,), f32)
    for d in range(2):
        out = out.at[d * 2 * H_B:d * 2 * H_B + H_B].set(p[d])
    return out.reshape(1, 128)


def _router_weights(w_rg, b_rg, w_re, b_re):
    wr = jnp.zeros((D, 256), f32).at[:, 0:N_GROUPS].set(w_rg).at[:, 128:128 + N_EXPERTS].set(w_re)
    br = jnp.zeros((1, 256), f32).at[0, 0:N_GROUPS].set(b_rg).at[0, 128:128 + N_EXPERTS].set(b_re)
    return jnp.stack(_hi_lo(wr)), br


def kernel(x_prompt, x_sample, cache_k, cache_v, state_fwd, state_bwd, c, c_ctx, norm_mix, norm_ffn, w_mod, b_mod,
           w_in_even, conv_w, a_log, dt_bias, delta_norm, lam_q, lam_k, subln, w_out_even, w_in_odd, sgu_ln_g,
           sgu_ln_b, w_spatial, b_spatial, w_out_odd, w_router_group, b_router_group, w_router_expert,
           b_router_expert, w_exp_gate, w_exp_up, w_exp_down, final_norm):
    xp = x_prompt.reshape(N_P, D)
    xs = x_sample.reshape(N_S, D)
    cond = jnp.concatenate([c, c_ctx[None, :], jnp.zeros((COND_ROWS - DEC_BATCH - 1, D), f32)], axis=0)
    mod = _modulation(cond, w_mod, b_mod)
    mod0 = mod[0].reshape(COND_ROWS, 1, N_MOD * D)
    mod1 = mod[1].reshape(COND_ROWS, 1, N_MOD * D)

    lam_init = 0.8 - 0.6 * math.exp(-0.3 * 0)
    w_in_pad = jnp.pad(w_in_even[0], ((0, 0), (0, IN_EVEN_PAD - IN_EVEN))).astype(bf16)
    cos_t, sin_t = _rope_tables()
    qa, ka, va, qkvb, gate_b, ab, ka_p, va_p = _even_in(xp, xs, mod0, norm_mix[0:1], w_in_pad, cos_t, sin_t)

    ctx_k = cache_k[:, 0].reshape(DEC_BATCH * PAST, W_A)
    ctx_v = cache_v[:, 0].reshape(DEC_BATCH * PAST, W_A)
    attn = functools.partial(_attention, qa, ka, va, lam_q[0], lam_k[0], subln[0:1], lam_init)
    oa_p = attn(seq=SEQ, tq=SEQ, tok_off=0)
    oa_s = attn(seq=DEC_SEQ, tq=512, tok_off=N_P, ctx=(ctx_k, ctx_v))

    conv_pad = jnp.pad(conv_w[0], ((0, 8 - CONV_K), (0, 0)))
    dl = functools.partial(_delta, qkvb, gate_b, ab, conv_pad, _gate_lanes(a_log[0]), _gate_lanes(dt_bias[0]),
                           delta_norm[0:1])
    ob_p, s_f, s_b = dl(seq=SEQ, tok_off=0)
    ob_s, _, _ = dl(seq=DEC_SEQ, tok_off=N_P, init=(state_fwd[:, 0], state_bwd[:, 0]))

    wr0, br0 = _router_weights(w_router_group[0], b_router_group[0], w_router_expert[0], b_router_expert[0])
    x1, route0, cnt0, hs, gs = _even_out(xp, xs, oa_p, oa_s, ob_p, ob_s, mod0, w_out_even[0].astype(bf16),
                                         norm_ffn[0:1], wr0, br0)

    wshape = (DEPTH, N_GROUPS, E_PER_GROUP)
    wg = w_exp_gate.reshape(wshape + (D, D_EXPERT))
    wu = w_exp_up.reshape(wshape + (D, D_EXPERT))
    wd = w_exp_down.reshape(wshape + (D_EXPERT, D))
    run_n0, run_src0, n_used, tile_g, tile_blk = _dispatch_plan(cnt0)
    y0 = _experts(0, n_used, tile_g, tile_blk, hs, gs, wg, wu, wd)

    bs = jnp.broadcast_to(b_spatial[0][:, :, None], (C_GROUPS, C_CHUNK, W_C // C_GROUPS))
    wr1, br1 = _router_weights(w_router_group[1], b_router_group[1], w_router_expert[1], b_router_expert[1])
    x2, route1, cnt1, hs, gs = _odd_mixer(run_n0, run_src0, x1, route0, y0, mod0, mod1, norm_mix[1:2],
                                          w_in_odd[0].astype(bf16), sgu_ln_g[0:1], sgu_ln_b[0:1],
                                          w_spatial[0].astype(bf16), bs, w_out_odd[0].astype(bf16), norm_ffn[1:2],
                                          wr1, br1)
    run_n1, run_src1, n_used, tile_g, tile_blk = _dispatch_plan(cnt1)
    y1 = _experts(1, n_used, tile_g, tile_blk, hs, gs, wg, wu, wd)

    fin = functools.partial(_final, run_n1, run_src1, x2, route1, y1, mod1, final_norm[None, :])
    y_prompt = fin(n_tiles=NPT, tile_off=0)
    y_sample = fin(n_tiles=NT - NPT, tile_off=NPT)

    new_cache_k = ka_p.reshape(BATCH, 1, SEQ, H_A, 2 * DH_A)
    new_cache_v = va_p.reshape(BATCH, 1, SEQ, H_A, 2 * DH_A)
    return (y_prompt.reshape(BATCH, SEQ, D), y_sample.reshape(DEC_BATCH, DEC_SEQ, D), new_cache_k, new_cache_v,
            s_f[:, None], s_b[:, None])
```

```python
import functools
import math

import jax
import jax.numpy as jnp
from jax import lax
from jax.experimental import pallas as pl
from jax.experimental.pallas import tpu as pltpu

f32 = jnp.float32
bf16 = jnp.bfloat16
i32 = jnp.int32

D = 1024
BATCH, SEQ = 16, 256
DEC_BATCH, DEC_SEQ, PAST = 8, 1024, 512
DEPTH = 2
GRID_W = 64
N_MOD = 6
EPS = 1e-6
H_A, DH_A = 4, 64
W_A = H_A * 2 * DH_A
ROPE_BASE = 10000.0
AXIS_FREQS = DH_A // 4
H_B, DK_B, DV_B = 4, 128, 128
W_B = H_B * DV_B
CONV_K = 5
CH = 64
IN_EVEN = 3 * W_A + 4 * W_B + 4 * H_B
IN_EVEN_PAD = 3712
W_C = D
C_GROUPS, C_CHUNK = 4, 128
N_GROUPS, E_PER_GROUP, D_EXPERT = 4, 8, 256
N_EXPERTS = N_GROUPS * E_PER_GROUP

N_P = BATCH * SEQ
N_S = DEC_BATCH * DEC_SEQ
N_TOK = N_P + N_S
TM = 256
NPT = N_P // TM
NT = N_TOK // TM
COND_ROWS = 16

VMEM_LIMIT = 56 * 1024 * 1024
VMEM_LIMIT_EXPERTS = 62 * 1024 * 1024


def _cp(sem, vmem_limit=VMEM_LIMIT):
    return pltpu.CompilerParams(dimension_semantics=sem, vmem_limit_bytes=vmem_limit)


SUB = 2
NS = NT // SUB
NPS = NPT // SUB
SUB_OUT = 4


def _mod_row_step(i, sub=SUB):
    return jnp.where(i < NPT // sub, DEC_BATCH, (i - NPT // sub) // (DEC_SEQ // (sub * TM)))


def _lockstep(chains):
    live = list(chains)
    while live:
        nxt = []
        for ch in live:
            try:
                next(ch)
                nxt.append(ch)
            except StopIteration:
                pass
        live = nxt


def _rms(x, g):
    return x * lax.rsqrt(jnp.mean(x * x, axis=-1, keepdims=True) + EPS) * g


def _silu(x):
    return x * jax.nn.sigmoid(x)


def _dot(a, b):
    return jnp.dot(a.astype(bf16), b.astype(bf16), preferred_element_type=f32)


ROW_SUB = D // 128


def _store_rows(ref, base, x):
    rows = x.shape[0]
    for j in range(ROW_SUB):
        ref[pl.ds(base + j, rows, stride=ROW_SUB), :] = x[:, j * 128:(j + 1) * 128]


def _load_rows(ref, base, rows):
    return jnp.concatenate([ref[pl.ds(base + j, rows, stride=ROW_SUB), :] for j in range(ROW_SUB)], axis=1)


TN_MOD = 1536


def _mod_kernel(c_ref, w_ref, b_ref, o_ref):
    a = _silu(c_ref[...])
    o_ref[0] = _dot(a, w_ref[0]) + b_ref[0]


def _modulation(cond, w_mod, b_mod):
    return pl.pallas_call(
        _mod_kernel,
        grid=(DEPTH, N_MOD * D // TN_MOD),
        in_specs=[
            pl.BlockSpec((COND_ROWS, D), lambda l, j: (0, 0)),
            pl.BlockSpec((1, D, TN_MOD), lambda l, j: (l, 0, j)),
            pl.BlockSpec((1, 1, TN_MOD), lambda l, j: (l, 0, j)),
        ],
        out_specs=pl.BlockSpec((1, COND_ROWS, TN_MOD), lambda l, j: (l, 0, j)),
        out_shape=jax.ShapeDtypeStruct((DEPTH, COND_ROWS, N_MOD * D), f32),
        compiler_params=_cp(("arbitrary", "arbitrary")),
        name="modulation",
    )(cond, w_mod, b_mod.reshape(DEPTH, 1, N_MOD * D))


Q_SCALE = DH_A ** -0.5 * math.log2(math.e)


def _even_in_kernel(xp_ref, xs_ref, mod_ref, nrm_ref, w_ref, cos_ref, sin_ref,
                    q_ref, k_ref, v_ref, qkvb_ref, gate_ref, ab_ref, kp_ref, vp_ref):
    i = pl.program_id(0)
    is_p = i < NPS
    mod = mod_ref[0]
    lane = lax.broadcasted_iota(i32, (TM, 2 * DH_A), 1)
    first = (lane % (2 * AXIS_FREQS)) < AXIS_FREQS

    def chain(s):
        rows = slice(s * TM, (s + 1) * TM)
        x = jnp.where(is_p, xp_ref[rows, :], xs_ref[rows, :])
        h = _rms(x, nrm_ref[...]) * (1.0 + mod[:, D:2 * D]) + mod[:, 0:D]
        proj = _dot(h, w_ref[...])
        yield
        cos = cos_ref[rows, :]
        sin = sin_ref[rows, :]

        def rope(xh):
            partner = jnp.where(first, pltpu.roll(xh, 2 * DH_A - AXIS_FREQS, 1), pltpu.roll(xh, AXIS_FREQS, 1))
            return xh * cos + partner * sin

        for hh in range(H_A):
            sl = slice(hh * 2 * DH_A, (hh + 1) * 2 * DH_A)
            q_ref[rows, sl] = (rope(proj[:, sl]) * Q_SCALE).astype(bf16)
            k_ref[rows, sl] = rope(proj[:, W_A + hh * 2 * DH_A:W_A + (hh + 1) * 2 * DH_A])
        v_ref[rows, :] = proj[:, 2 * W_A:3 * W_A]
        qkvb_ref[rows, :] = proj[:, 3 * W_A:3 * W_A + 3 * W_B]
        gate_ref[rows, :] = proj[:, 3 * W_A + 3 * W_B:3 * W_A + 4 * W_B]
        ab_ref[rows, :] = proj[:, 3 * W_A + 4 * W_B:IN_EVEN_PAD]

    _lockstep([chain(s) for s in range(SUB)])

    @pl.when(is_p)
    def _():
        kp_ref[...] = k_ref[...]
        vp_ref[...] = v_ref[...]


def _even_in(xp, xs, mod_l, nrm, w_in_pad, cos_t, sin_t):
    tile = lambda w: pl.BlockSpec((SUB * TM, w), lambda i: (i, 0))
    ptile = lambda w: pl.BlockSpec((SUB * TM, w), lambda i: (jnp.minimum(i, NPS - 1), 0))
    rope_idx = lambda i: (jnp.where(i < NPS, 0, 1 + (i - NPS) % (DEC_SEQ // (SUB * TM))), 0)
    return pl.pallas_call(
        _even_in_kernel,
        grid=(NS,),
        in_specs=[
            ptile(D),
            pl.BlockSpec((SUB * TM, D), lambda i: (jnp.maximum(i - NPS, 0), 0)),
            pl.BlockSpec((1, 1, N_MOD * D), lambda i: (_mod_row_step(i), 0, 0)),
            pl.BlockSpec((1, D), lambda i: (0, 0)),
            pl.BlockSpec((D, IN_EVEN_PAD), lambda i: (0, 0)),
            pl.BlockSpec((SUB * TM, 2 * DH_A), rope_idx),
            pl.BlockSpec((SUB * TM, 2 * DH_A), rope_idx),
        ],
        out_specs=[tile(W_A), tile(W_A), tile(W_A), tile(3 * W_B), tile(W_B), tile(128), ptile(W_A), ptile(W_A)],
        out_shape=[
            jax.ShapeDtypeStruct((N_TOK, W_A), bf16),
            jax.ShapeDtypeStruct((N_TOK, W_A), f32),
            jax.ShapeDtypeStruct((N_TOK, W_A), f32),
            jax.ShapeDtypeStruct((N_TOK, 3 * W_B), f32),
            jax.ShapeDtypeStruct((N_TOK, W_B), f32),
            jax.ShapeDtypeStruct((N_TOK, 128), f32),
            jax.ShapeDtypeStruct((N_P, W_A), f32),
            jax.ShapeDtypeStruct((N_P, W_A), f32),
        ],
        compiler_params=_cp(("arbitrary",)),
        name="even_in",
    )(xp, xs, mod_l, nrm, w_in_pad, cos_t, sin_t)


ATTN_HEADS_IN_LOCKSTEP = 2


def _attn_kernel(*refs, lam_init, has_ctx):
    if has_ctx:
        q_ref, kn_ref, vn_ref, kc_ref, vc_ref, lq_ref, lk_ref, sub_ref, o_ref = refs
    else:
        q_ref, kn_ref, vn_ref, lq_ref, lk_ref, sub_ref, o_ref = refs
    tq = q_ref.shape[0]
    lq = lq_ref[...]
    lk = lk_ref[...]
    prod = lq * lk
    lam = (jnp.exp(jnp.sum(prod[0:1], axis=-1, keepdims=True))
           - jnp.exp(jnp.sum(prod[1:2], axis=-1, keepdims=True)) + lam_init)
    lane = lax.broadcasted_iota(i32, (tq, 2 * DH_A), 1)
    nt = (((1,), (1,)), ((), ()))
    tn = (((0,), (0,)), ((), ()))
    sub_col = jnp.broadcast_to(sub_ref[...], (8, 2 * DH_A)).T[:, 0:1]
    outs = {}

    def softmax_part(hh, part, srcs):
        q = q_ref[:, hh * 2 * DH_A:(hh + 1) * 2 * DH_A]
        qm = jnp.where((lane < DH_A) if part == 0 else (lane >= DH_A), q, jnp.zeros_like(q))
        ss = [lax.dot_general(k, qm, nt, preferred_element_type=f32) for k, _ in srcs]
        yield
        m = ss[0].max(axis=0, keepdims=True)
        for s in ss[1:]:
            m = jnp.maximum(m, s.max(axis=0, keepdims=True))
        l = jnp.zeros((1, tq), f32)
        o = jnp.zeros((2 * DH_A, tq), f32)
        for s, (_, v) in zip(ss, srcs):
            e = jnp.exp2(s - m)
            l = l + jnp.sum(e, axis=0, keepdims=True)
            o = o + lax.dot_general(v, e.astype(bf16), tn, preferred_element_type=f32)
        yield
        outs[(hh, part)] = o / l

    for h0 in range(0, H_A, ATTN_HEADS_IN_LOCKSTEP):
        chains = []
        for hh in range(h0, h0 + ATTN_HEADS_IN_LOCKSTEP):
            sl = slice(hh * 2 * DH_A, (hh + 1) * 2 * DH_A)
            srcs = [(kn_ref[:, sl].astype(bf16), vn_ref[:, sl].astype(bf16))]
            if has_ctx:
                srcs.append((kc_ref[:, sl].astype(bf16), vc_ref[:, sl].astype(bf16)))
            chains += [softmax_part(hh, part, srcs) for part in range(2)]
        _lockstep(chains)
        for hh in range(h0, h0 + ATTN_HEADS_IN_LOCKSTEP):
            o = outs[(hh, 0)] - lam * outs[(hh, 1)]
            o = o * lax.rsqrt(jnp.mean(o * o, axis=0, keepdims=True) + EPS) * sub_col * (1.0 - lam_init)
            o_ref[:, hh * 2 * DH_A:(hh + 1) * 2 * DH_A] = o.T.astype(bf16)


def _attention(q, ka, va, lam_q, lam_k, subln, lam_init, *, seq, tq, tok_off, ctx=None):
    nq = seq // tq
    nb = (N_P if ctx is None else N_S) // seq
    qo, ko = tok_off // tq, tok_off // seq
    in_specs = [
        pl.BlockSpec((tq, W_A), lambda b, j: (qo + b * nq + j, 0)),
        pl.BlockSpec((seq, W_A), lambda b, j: (ko + b, 0)),
        pl.BlockSpec((seq, W_A), lambda b, j: (ko + b, 0)),
    ]
    args = [q, ka, va]
    if ctx is not None:
        in_specs += [pl.BlockSpec((PAST, W_A), lambda b, j: (b, 0))] * 2
        args += list(ctx)
    in_specs += [pl.BlockSpec((2, DH_A), lambda b, j: (0, 0))] * 2 + [pl.BlockSpec((1, 2 * DH_A), lambda b, j: (0, 0))]
    args += [lam_q, lam_k, subln]
    return pl.pallas_call(
        functools.partial(_attn_kernel, lam_init=lam_init, has_ctx=ctx is not None),
        grid=(nb, nq),
        in_specs=in_specs,
        out_specs=pl.BlockSpec((tq, W_A), lambda b, j: (b * nq + j, 0)),
        out_shape=jax.ShapeDtypeStruct((nb * seq, W_A), bf16),
        compiler_params=_cp(("arbitrary", "arbitrary")),
        name="diff_attn_ctx" if ctx is not None else "diff_attn",
    )(*args)


def _softplus(x):
    return jnp.maximum(x, 0.0) + jnp.log1p(jnp.exp(-jnp.abs(x)))


RW = H_B * CH
CPI = 4


def _rep_rows(x, times):
    return jnp.concatenate([x] * times, axis=0)


def _tri_inv(lmats, bd, ii, jj):
    b16 = (ii // 16) == (jj // 16)
    b32 = (ii // 32) == (jj // 32)
    eye = (ii == jj).astype(f32)
    n = len(lmats)
    mm = lambda a, b: jnp.dot(a.astype(bf16), bd(b), preferred_element_type=f32)
    l0 = [jnp.where(b16, l, 0.0) for l in lmats]
    x = [eye - a for a in l0]
    p = [mm(a, a) for a in l0]
    for level in range(3):
        xp = [mm(x[i], p[i]) for i in range(n)]
        if level < 2:
            p = [mm(a, a) for a in p]
        x = [x[i] + xp[i] for i in range(n)]
    for off_diag in (jnp.logical_and(b32, jnp.logical_not(b16)), jnp.logical_not(b32)):
        t = [mm(jnp.where(off_diag, lmats[i], 0.0), x[i]) for i in range(n)]
        t = [mm(x[i], t[i]) for i in range(n)]
        x = [x[i] - t[i] for i in range(n)]
    return x


def _delta_kernel(*refs, seq, has_init):
    if has_init:
        (qkv_ref, gate_ref, ab_ref, cw_ref, al_ref, dt_ref, dn_ref, sf0_ref, sb0_ref,
         ob_ref, sf_ref, sb_ref, xpad, xq, xk, xv, gacc, bacc, o_f, o_b, st, pu, pw, pqe, pkd, patt, ptot) = refs
    else:
        (qkv_ref, gate_ref, ab_ref, cw_ref, al_ref, dt_ref, dn_ref,
         ob_ref, sf_ref, sb_ref, xpad, xq, xk, xv, gacc, bacc, o_f, o_b, st, pu, pw, pqe, pkd, patt, ptot) = refs
    n = seq // CH
    win = CH + 16

    xpad[0:8, :] = jnp.zeros((8, 3 * W_B), f32)
    xpad[seq + 8:seq + 16, :] = jnp.zeros((8, 3 * W_B), f32)
    xpad[8:seq + 8, :] = qkv_ref[...]
    neg_a = -jnp.exp(al_ref[...])
    dtb = dt_ref[...]

    def pre(c, carry):
        r0 = pl.multiple_of(c * CH, CH)
        a = xpad[pl.ds(r0, win), :]
        y = jnp.zeros((CH, 3 * W_B), f32)
        for j in range(CONV_K):
            sh = ((CONV_K - 1) // 2 - j) % win
            y = y + pltpu.roll(a, sh, 0)[8:8 + CH] * cw_ref[j:j + 1, :]
        y = _silu(y)
        for hh in range(H_B):
            qh = y[:, hh * DK_B:(hh + 1) * DK_B]
            kh = y[:, W_B + hh * DK_B:W_B + (hh + 1) * DK_B]
            qn = qh * lax.rsqrt(jnp.sum(qh * qh, axis=-1, keepdims=True) + EPS) * (DK_B ** -0.5)
            kn = kh * lax.rsqrt(jnp.sum(kh * kh, axis=-1, keepdims=True) + EPS)
            xq[pl.ds(r0, CH), hh * DK_B:(hh + 1) * DK_B] = qn
            xk[pl.ds(r0, CH), hh * DK_B:(hh + 1) * DK_B] = kn
        xv[pl.ds(r0, CH), :] = y[:, 2 * W_B:3 * W_B]
        ab = ab_ref[pl.ds(r0, CH), :]
        gacc[pl.ds(r0, CH), :] = neg_a * _softplus(ab + dtb)
        bacc[pl.ds(r0, CH), :] = jax.nn.sigmoid(ab)
        return carry

    lax.fori_loop(0, n, pre, 0)

    ii = lax.broadcasted_iota(i32, (CH, RW), 0)
    lane_r = lax.broadcasted_iota(i32, (CH, RW), 1)
    jj = lane_r % CH
    hb = lane_r // CH
    rowl = lax.broadcasted_iota(i32, (CH, 128), 0)
    bd_mask = (lax.broadcasted_iota(i32, (RW, RW), 0) // CH) == (lax.broadcasted_iota(i32, (RW, RW), 1) // CH)
    wide_mask = ((lax.broadcasted_iota(i32, (RW, W_B), 0) // CH)
                 == (lax.broadcasted_iota(i32, (RW, W_B), 1) // DK_B))
    pair_mask = ((lax.broadcasted_iota(i32, (2 * DK_B, 2 * DV_B), 0) // DK_B)
                 == (lax.broadcasted_iota(i32, (2 * DK_B, 2 * DV_B), 1) // DV_B))
    zero_b = jnp.zeros((), bf16)

    def bd(x):
        return jnp.where(bd_mask, _rep_rows(x.astype(bf16), H_B), zero_b)

    def bd_wide(x):
        return jnp.where(wide_mask, _rep_rows(x.astype(bf16), H_B), zero_b)

    def per_head(src, lanes):
        return jnp.concatenate([jnp.broadcast_to(src[:, ln:ln + 1], (src.shape[0], DK_B)) for ln in lanes], axis=1)

    nt = (((1,), (1,)), ((), ()))
    tn = (((0,), (0,)), ((), ()))

    def local(it, carry):
        chains = []
        for j in range(CPI):
            c = it * CPI + j
            r0 = pl.multiple_of(c * CH, CH)
            g = gacc[pl.ds(r0, CH), :]
            beta = bacc[pl.ds(r0, CH), :]
            pre_g = g
            for s in (1, 2, 4, 8, 16, 32):
                pre_g = pre_g + jnp.where(rowl >= s, pltpu.roll(pre_g, s, 0), 0.0)
            tot = pre_g[CH - 1:CH, :]
            q_all = xq[pl.ds(r0, CH), :]
            k_all = xk[pl.ds(r0, CH), :]
            v_all = xv[pl.ds(r0, CH), :]
            k_bd = bd_wide(k_all)
            for d in range(2):
                gc = pre_g if d == 0 else tot - pre_g + g
                lanes = [d * 2 * H_B + hh for hh in range(H_B)]
                gct = _rep_rows(gc, H_B).T
                g_col = jnp.zeros((CH, RW), f32)
                g_row = jnp.zeros((1, RW), f32)
                for hh, ln in enumerate(lanes):
                    g_col = jnp.where(hb == hh, gc[:, ln:ln + 1], g_col)
                    g_row = jnp.where(hb[0:1] == hh, gct[ln:ln + 1, :], g_row)
                g_wide = per_head(gc, lanes)
                b_wide = per_head(beta, [ln + H_B for ln in lanes])
                incl = (ii >= jj) if d == 0 else (ii <= jj)
                strict = (ii > jj) if d == 0 else (ii < jj)
                eg = jnp.exp(g_wide)
                kb = k_all * b_wide
                chains.append(dict(
                    ci=c * 2 + d, incl=incl, strict=strict, k_bd=k_bd, kb=kb, q=q_all, tot=tot,
                    dec=jnp.where(incl, jnp.exp(jnp.where(incl, g_col - g_row, 0.0)), 0.0),
                    vb=v_all * b_wide, kbe=kb * eg, qe=q_all * eg,
                    kd=k_all * jnp.exp(per_head(tot, lanes) - g_wide)))
        kk = [lax.dot_general(e["kb"].astype(bf16), e["k_bd"], nt, preferred_element_type=f32) for e in chains]
        qk = [lax.dot_general(e["q"].astype(bf16), e["k_bd"], nt, preferred_element_type=f32) for e in chains]
        lmat = [jnp.where(e["strict"], kk[i] * e["dec"], 0.0) for i, e in enumerate(chains)]
        tinv = _tri_inv(lmat, bd, ii, jj)
        u = [jnp.dot(tinv[i].astype(bf16), bd_wide(e["vb"]), preferred_element_type=f32) for i, e in enumerate(chains)]
        w = [jnp.dot(tinv[i].astype(bf16), bd_wide(e["kbe"]), preferred_element_type=f32) for i, e in enumerate(chains)]
        for i, e in enumerate(chains):
            ci = e["ci"]
            pu[ci] = u[i]
            pw[ci] = w[i].astype(bf16)
            pqe[ci] = e["qe"].astype(bf16)
            pkd[ci] = e["kd"].astype(bf16)
            patt[ci] = jnp.where(e["incl"], qk[i] * e["dec"], 0.0).astype(bf16)
            ptot[ci] = jnp.broadcast_to(e["tot"], (8, 128))
        return carry

    lax.fori_loop(0, n // CPI, local, 0)

    for hh in range(H_B):
        rows = slice(hh * DK_B, (hh + 1) * DK_B)
        if has_init:
            st[0, rows, :] = sf0_ref[0, hh]
            st[1, rows, :] = sb0_ref[0, hh]
        else:
            st[0, rows, :] = jnp.zeros((DK_B, DV_B), f32)
            st[1, rows, :] = jnp.zeros((DK_B, DV_B), f32)

    def scan(it, carry):
        chains = []
        for d in range(2):
            c = it if d == 0 else n - 1 - it
            ci = c * 2 + d
            s_old = st[d]
            s_b = s_old.astype(bf16)
            chains.append(dict(
                d=d, r0=pl.multiple_of(c * CH, CH), s_old=s_old, u=pu[ci], att=patt[ci], kd=pkd[ci], tot=ptot[ci],
                lhs=jnp.concatenate([pw[ci], pqe[ci]], axis=0),
                rhs=[jnp.where(pair_mask, jnp.concatenate([s_b[p * 2 * DK_B:(p + 1) * 2 * DK_B]] * 2, axis=1), zero_b)
                     for p in range(2)]))
        ws_qs = [jnp.concatenate(
            [jnp.dot(e["lhs"][:, p * 2 * DK_B:(p + 1) * 2 * DK_B], e["rhs"][p], preferred_element_type=f32)
             for p in range(2)], axis=1) for e in chains]
        v_new = [e["u"] - ws_qs[i][0:CH] for i, e in enumerate(chains)]
        av = [jnp.dot(e["att"], bd_wide(v_new[i]), preferred_element_type=f32) for i, e in enumerate(chains)]
        su = [lax.dot_general(jnp.where(wide_mask, _rep_rows(e["kd"], H_B), zero_b),
                              jnp.concatenate([v_new[i][:, hh * DV_B:(hh + 1) * DV_B] for hh in range(H_B)],
                                              axis=0).astype(bf16),
                              tn, preferred_element_type=f32) for i, e in enumerate(chains)]
        for i, e in enumerate(chains):
            d = e["d"]
            if d == 0:
                o_f[pl.ds(e["r0"], CH), :] = ws_qs[i][CH:2 * CH] + av[i]
            else:
                o_b[pl.ds(e["r0"], CH), :] = ws_qs[i][CH:2 * CH] + av[i]
            for hh in range(H_B):
                rows = slice(hh * DK_B, (hh + 1) * DK_B)
                ln = d * 2 * H_B + hh
                st[d, rows, :] = e["s_old"][rows] * jnp.exp(e["tot"][0:1, ln:ln + 1]) + su[i][rows]
        return carry

    lax.fori_loop(0, n, scan, 0)

    def post(c, carry):
        r0 = pl.multiple_of(c * CH, CH)
        o = o_f[pl.ds(r0, CH), :] + o_b[pl.ds(r0, CH), :]
        gt = gate_ref[pl.ds(r0, CH), :]
        for hh in range(H_B):
            sl = slice(hh * DV_B, (hh + 1) * DV_B)
            ob_ref[pl.ds(r0, CH), sl] = (_rms(o[:, sl], dn_ref[...]) * _silu(gt[:, sl])).astype(bf16)
        return carry

    lax.fori_loop(0, n, post, 0)
    for hh in range(H_B):
        sf_ref[0, hh] = st[0, hh * DK_B:(hh + 1) * DK_B, :]
        sb_ref[0, hh] = st[1, hh * DK_B:(hh + 1) * DK_B, :]


def _delta(qkvb, gate, ab, conv_pad, al_lane, dt_lane, dnorm, *, seq, tok_off, init=None):
    nb = (N_P if init is None else N_S) // seq
    bo = tok_off // seq
    rows = lambda w: pl.BlockSpec((seq, w), lambda b: (bo + b, 0))
    const = lambda shp: pl.BlockSpec(shp, lambda b: tuple(0 for _ in shp))
    st_spec = pl.BlockSpec((1, H_B, DK_B, DV_B), lambda b: (b, 0, 0, 0))
    in_specs = [rows(3 * W_B), rows(W_B), rows(128), const((8, 3 * W_B)), const((1, 128)), const((1, 128)),
                const((1, DV_B))]
    args = [qkvb, gate, ab, conv_pad, al_lane, dt_lane, dnorm]
    if init is not None:
        in_specs += [st_spec, st_spec]
        args += [init[0], init[1]]
    return pl.pallas_call(
        functools.partial(_delta_kernel, seq=seq, has_init=init is not None),
        grid=(nb,),
        in_specs=in_specs,
        out_specs=[pl.BlockSpec((seq, W_B), lambda b: (b, 0)), st_spec, st_spec],
        out_shape=[
            jax.ShapeDtypeStruct((nb * seq, W_B), bf16),
            jax.ShapeDtypeStruct((nb, H_B, DK_B, DV_B), f32),
            jax.ShapeDtypeStruct((nb, H_B, DK_B, DV_B), f32),
        ],
        scratch_shapes=[
            pltpu.VMEM((seq + 16, 3 * W_B), f32),
            pltpu.VMEM((seq, W_B), f32),
            pltpu.VMEM((seq, W_B), f32),
            pltpu.VMEM((seq, W_B), f32),
            pltpu.VMEM((seq, 128), f32),
            pltpu.VMEM((seq, 128), f32),
            pltpu.VMEM((seq, W_B), f32),
            pltpu.VMEM((seq, W_B), f32),
            pltpu.VMEM((2, H_B * DK_B, DV_B), f32),
            pltpu.VMEM((2 * seq // CH, CH, W_B), f32),
            pltpu.VMEM((2 * seq // CH, CH, W_B), bf16),
            pltpu.VMEM((2 * seq // CH, CH, W_B), bf16),
            pltpu.VMEM((2 * seq // CH, CH, W_B), bf16),
            pltpu.VMEM((2 * seq // CH, CH, RW), bf16),
            pltpu.VMEM((2 * seq // CH, 8, 128), f32),
        ],
        compiler_params=_cp(("arbitrary",)),
        name="delta_ctx" if init is not None else "delta",
    )(*args)


RG = N_TOK
TMG = 256
MAXT_G = N_TOK // TMG + N_GROUPS
TILE_ROWS = TM * ROW_SUB
RUN_BITS = tuple(range(TM.bit_length() - 1, -1, -1))


def _hi_lo(x):
    hi = x.astype(bf16)
    return hi, (x - hi.astype(f32)).astype(bf16)


def _run_pieces(n):
    return [(1 << b, (n >> (b + 1)) << (b + 1), ((n >> b) & 1) == 1) for b in RUN_BITS]


def _route_chain(s, x1, mod, nrm_ref, wr_ref, br_ref, route_ref, cnt_ref, hbuf, gbuf, counts):
    rows = slice(s * TM, (s + 1) * TM)
    h2 = _rms(x1, nrm_ref[...]) * (1.0 + mod[:, 4 * D:5 * D]) + mod[:, 3 * D:4 * D]
    h_hi, h_lo = _hi_lo(h2)
    w_hi, w_lo = wr_ref[0], wr_ref[1]
    logits = (jnp.dot(h_hi, w_hi, preferred_element_type=f32) + jnp.dot(h_hi, w_lo, preferred_element_type=f32)
              + jnp.dot(h_lo, w_hi, preferred_element_type=f32)) + br_ref[...]
    yield
    lane = lax.broadcasted_iota(i32, (TM, 128), 1)
    lane_f = lane.astype(f32)
    neg = jnp.float32(-1e30)
    lg = jnp.where(lane < N_GROUPS, logits[:, 0:128], neg)
    mg = lg.max(axis=-1, keepdims=True)
    gidx = jnp.min(jnp.where(lg == mg, lane_f, 128.0), axis=-1, keepdims=True)
    pg = 1.0 / jnp.sum(jnp.exp(lg - mg), axis=-1, keepdims=True)
    le = jnp.where((lane // E_PER_GROUP).astype(f32) == gidx, logits[:, 128:256], neg)
    m1 = le.max(axis=-1, keepdims=True)
    i1 = jnp.min(jnp.where(le == m1, lane_f, 128.0), axis=-1, keepdims=True)
    le2 = jnp.where(lane_f == i1, neg, le)
    m2 = le2.max(axis=-1, keepdims=True)
    i2 = jnp.min(jnp.where(le2 == m2, lane_f, 128.0), axis=-1, keepdims=True)
    e2 = jnp.exp(m2 - m1)
    w1 = pg / (1.0 + e2)
    w2 = w1 * e2
    local = lane_f + gidx * E_PER_GROUP
    gates = jnp.where(local == i1, w1, jnp.where(local == i2, w2, 0.0))

    onehot = (lane_f == gidx).astype(f32)
    r = lax.broadcasted_iota(i32, (TM, TM), 0)
    c = lax.broadcasted_iota(i32, (TM, TM), 1)
    before = jnp.dot((r > c).astype(bf16), onehot.astype(bf16), preferred_element_type=f32)
    yield
    cnt = jnp.sum(onehot, axis=0, keepdims=True)
    pos = (jnp.sum(onehot * before, axis=-1, keepdims=True)
           + jnp.sum(jnp.where(lane_f < gidx, cnt, 0.0), axis=-1, keepdims=True))
    route_ref[rows, :] = jnp.where(lane == 0, pos, 0.0)
    cnt_ref[s] = cnt
    counts[s] = cnt.astype(i32)

    perm_t = (c.astype(f32) == pos).astype(bf16)
    tn = (((0,), (0,)), ((), ()))
    h_sorted = lax.dot_general(perm_t, h_hi, tn, preferred_element_type=f32)
    g_hi, g_lo = _hi_lo(gates)
    g_sorted = (lax.dot_general(perm_t, g_hi, tn, preferred_element_type=f32)
                + lax.dot_general(perm_t, g_lo, tn, preferred_element_type=f32))
    yield
    _store_rows(hbuf, s * TILE_ROWS, h_sorted)
    gbuf[s * TM:(s + 1) * TM, :] = g_sorted


def _dispatch_done(s, hs_hbm, gs_hbm, hbuf, gbuf, sem):
    pltpu.make_async_copy(hbuf.at[pl.ds(s * TILE_ROWS, TILE_ROWS), :], hs_hbm.at[pl.ds(0, TILE_ROWS), :],
                          sem.at[s]).wait()
    pltpu.make_async_copy(gbuf.at[pl.ds(s * TM, TM), :], gs_hbm.at[pl.ds(0, TM), :], sem.at[s]).wait()


def _dispatch_runs(s, cnt_i, hs_hbm, gs_hbm, hbuf, gbuf, run, sem):
    off = 0
    for g in range(N_GROUPS):
        n = cnt_i[0, g]
        dst = g * RG + run[g]
        for rows, o, pred in _run_pieces(n):
            @pl.when(pred)
            def _():
                pltpu.make_async_copy(
                    hbuf.at[pl.ds(pl.multiple_of(s * TILE_ROWS + (off + o) * ROW_SUB, ROW_SUB), rows * ROW_SUB), :],
                    hs_hbm.at[pl.ds(pl.multiple_of((dst + o) * ROW_SUB, ROW_SUB), rows * ROW_SUB), :],
                    sem.at[s]).start()
                pltpu.make_async_copy(gbuf.at[pl.ds(s * TM + off + o, rows), :], gs_hbm.at[pl.ds(dst + o, rows), :],
                                      sem.at[s]).start()
        run[g] = run[g] + n
        off = off + n


def _dispatch_tail_fill(hs_hbm, gs_hbm, hbuf, gbuf, run, sem):
    hbuf[pl.ds(0, TMG * ROW_SUB), :] = jnp.zeros((TMG * ROW_SUB, 128), f32)
    gbuf[pl.ds(0, TMG), :] = jnp.zeros((TMG, 128), f32)
    for g in range(N_GROUPS):
        tot = run[g]
        pad = (-tot) & (TMG - 1)
        for rows, o, pred in _run_pieces(pad):
            @pl.when(pred)
            def _():
                dst = g * RG + tot + o
                ch = pltpu.make_async_copy(
                    hbuf.at[pl.ds(0, rows * ROW_SUB), :],
                    hs_hbm.at[pl.ds(pl.multiple_of(dst * ROW_SUB, ROW_SUB), rows * ROW_SUB), :], sem.at[0])
                cg = pltpu.make_async_copy(gbuf.at[pl.ds(0, rows), :], gs_hbm.at[pl.ds(dst, rows), :], sem.at[1])
                ch.start()
                cg.start()
                ch.wait()
                cg.wait()


def _mixer_out_tail(i, chains, counts, hs_hbm, gs_hbm, hbuf, gbuf, run, sem):
    @pl.when(i == 0)
    def _():
        for g in range(N_GROUPS):
            run[g] = 0

    sub = len(chains)

    @pl.when(i >= 1)
    def _():
        for s in range(sub):
            _dispatch_done(s, hs_hbm, gs_hbm, hbuf, gbuf, sem)

    _lockstep(chains)
    for s in range(sub):
        _dispatch_runs(s, counts[s], hs_hbm, gs_hbm, hbuf, gbuf, run, sem)

    @pl.when(i == pl.num_programs(0) - 1)
    def _():
        for s in range(sub):
            _dispatch_done(s, hs_hbm, gs_hbm, hbuf, gbuf, sem)
        _dispatch_tail_fill(hs_hbm, gs_hbm, hbuf, gbuf, run, sem)


def _dispatch_out_specs(sub=SUB):
    return [
        pl.BlockSpec((sub * TM, D), lambda i, *_: (i, 0)),
        pl.BlockSpec((sub * TM, 128), lambda i, *_: (i, 0)),
        pl.BlockSpec((sub, 1, 128), lambda i, *_: (i, 0, 0)),
        pl.BlockSpec(memory_space=pl.ANY),
        pl.BlockSpec(memory_space=pl.ANY),
    ]


_DISPATCH_OUT_SHAPES = [
    jax.ShapeDtypeStruct((N_TOK, D), f32),
    jax.ShapeDtypeStruct((N_TOK, 128), f32),
    jax.ShapeDtypeStruct((NT, 1, 128), f32),
    jax.ShapeDtypeStruct((N_GROUPS * RG * ROW_SUB, 128), f32),
    jax.ShapeDtypeStruct((N_GROUPS * RG, 128), f32),
]


def _dispatch_scratch(sub=SUB):
    return [
        pltpu.VMEM((sub * TILE_ROWS, 128), f32),
        pltpu.VMEM((sub * TM, 128), f32),
        pltpu.SMEM((N_GROUPS,), i32),
        pltpu.SemaphoreType.DMA((sub,)),
    ]


def _router_in_specs():
    return [pl.BlockSpec((1, D), lambda i, *_: (0, 0)),
            pl.BlockSpec((2, D, 256), lambda i, *_: (0, 0, 0)),
            pl.BlockSpec((1, 256), lambda i, *_: (0, 0))]


def _fetch_runs(n_ref, s_ref, y_hbm, ybuf, sem, tile, slot):
    base = slot * TILE_ROWS
    off = 0
    for g in range(N_GROUPS):
        n = n_ref[tile * N_GROUPS + g]
        src = s_ref[tile * N_GROUPS + g]
        for rows, o, pred in _run_pieces(n):
            @pl.when(pred)
            def _():
                pltpu.make_async_copy(
                    y_hbm.at[pl.ds(pl.multiple_of((src + o) * ROW_SUB, ROW_SUB), rows * ROW_SUB), :],
                    ybuf.at[pl.ds(pl.multiple_of(base + (off + o) * ROW_SUB, ROW_SUB), rows * ROW_SUB), :],
                    sem.at[slot]).start()
        off = off + n


def _combine_fetch(i, n_steps, sub, tile0, n_ref, s_ref, y_hbm, ybuf, sem):
    half = (i % 2) * sub

    @pl.when(i == 0)
    def _():
        for s in range(sub):
            _fetch_runs(n_ref, s_ref, y_hbm, ybuf, sem, tile0 + s, s)

    @pl.when(i + 1 < n_steps)
    def _():
        for s in range(sub):
            _fetch_runs(n_ref, s_ref, y_hbm, ybuf, sem, tile0 + (i + 1) * sub + s, sub - half + s)

    bases = []
    for s in range(sub):
        base = pl.multiple_of((half + s) * TILE_ROWS, TILE_ROWS)
        pltpu.make_async_copy(y_hbm.at[pl.ds(0, TILE_ROWS), :], ybuf.at[pl.ds(base, TILE_ROWS), :],
                              sem.at[half + s]).wait()
        bases.append(base)
    return bases


def _unsort(ybuf, base, pos):
    y_hi, y_lo = _hi_lo(_load_rows(ybuf, base, TM))
    perm_t = (lax.broadcasted_iota(i32, (TM, TM), 1).astype(f32) == pos).astype(bf16)
    return jnp.dot(perm_t, y_hi, preferred_element_type=f32) + jnp.dot(perm_t, y_lo, preferred_element_type=f32)


def _combine_scratch(sub):
    return [pltpu.VMEM((2 * sub * TILE_ROWS, 128), f32), pltpu.SemaphoreType.DMA((2 * sub,))]


def _even_out_kernel(xp_ref, xs_ref, oap_ref, oas_ref, obp_ref, obs_ref, mod_ref, w_ref, nrm_ref, wr_ref, br_ref,
                     x1_ref, route_ref, cnt_ref, hs_hbm, gs_hbm, hbuf, gbuf, run, sem):
    i = pl.program_id(0)
    is_p = i < NPT // SUB_OUT
    mod = mod_ref[0]
    counts = [None] * SUB_OUT

    def chain(s):
        rows = slice(s * TM, (s + 1) * TM)
        x = jnp.where(is_p, xp_ref[rows, :], xs_ref[rows, :])
        oa = jnp.where(is_p, oap_ref[rows, :], oas_ref[rows, :])
        ob = jnp.where(is_p, obp_ref[rows, :], obs_ref[rows, :])
        out = (jnp.dot(oa, w_ref[0:W_A, :], preferred_element_type=f32)
               + jnp.dot(ob, w_ref[W_A:W_A + W_B, :], preferred_element_type=f32))
        yield
        x1 = x + mod[:, 2 * D:3 * D] * out
        x1_ref[rows, :] = x1
        yield from _route_chain(s, x1, mod, nrm_ref, wr_ref, br_ref, route_ref, cnt_ref, hbuf, gbuf, counts)

    _mixer_out_tail(i, [chain(s) for s in range(SUB_OUT)], counts, hs_hbm, gs_hbm, hbuf, gbuf, run, sem)


def _even_out(xp, xs, oa_p, oa_s, ob_p, ob_s, mod_l, w_out, nrm, wr, br):
    sub = SUB_OUT
    ptile = lambda w: pl.BlockSpec((sub * TM, w), lambda i: (jnp.minimum(i, NPT // sub - 1), 0))
    stile = lambda w: pl.BlockSpec((sub * TM, w), lambda i: (jnp.maximum(i - NPT // sub, 0), 0))
    return pl.pallas_call(
        _even_out_kernel,
        grid=(NT // sub,),
        in_specs=[
            ptile(D), stile(D), ptile(W_A), stile(W_A), ptile(W_B), stile(W_B),
            pl.BlockSpec((1, 1, N_MOD * D), lambda i: (_mod_row_step(i, sub), 0, 0)),
            pl.BlockSpec((W_A + W_B, D), lambda i: (0, 0)),
        ] + _router_in_specs(),
        out_specs=_dispatch_out_specs(sub),
        out_shape=_DISPATCH_OUT_SHAPES,
        scratch_shapes=_dispatch_scratch(sub),
        compiler_params=_cp(("arbitrary",)),
        name="even_out_router",
    )(xp, xs, oa_p, oa_s, ob_p, ob_s, mod_l, w_out, nrm, wr, br)


def _gelu_tanh(x):
    return 0.5 * x * (1.0 + jnp.tanh(math.sqrt(2.0 / math.pi) * (x + 0.044715 * (x * x * x))))


def _odd_kernel(n_ref, s_ref, x_ref, proute_ref, y_hbm, modp_ref, mod_ref, nm_ref, win_ref, lng_ref, lnb_ref,
                ws_ref, bs_ref, wout_ref, nrm_ref, wr_ref, br_ref,
                x1_ref, route_ref, cnt_ref, hs_hbm, gs_hbm, gated, ybuf, ysem, hbuf, gbuf, run, sem):
    i = pl.program_id(0)
    modp = modp_ref[0]
    mod = mod_ref[0]
    counts = [None] * SUB_OUT
    ybase = _combine_fetch(i, NT // SUB_OUT, SUB_OUT, 0, n_ref, s_ref, y_hbm, ybuf, ysem)
    gw = W_C // C_GROUPS

    def chain(s):
        rows = slice(s * TM, (s + 1) * TM)
        y = _unsort(ybuf, ybase[s], proute_ref[rows, 0:1])
        yield
        x = x_ref[rows, :] + modp[:, 5 * D:6 * D] * y
        h = _rms(x, nm_ref[...]) * (1.0 + mod[:, D:2 * D]) + mod[:, 0:D]
        z = _dot(h, win_ref[...])
        yield
        z = _gelu_tanh(z)
        u = z[:, 0:W_C]
        v = z[:, W_C:2 * W_C]
        mu = jnp.mean(v, axis=-1, keepdims=True)
        vc = v - mu
        var = jnp.mean(vc * vc, axis=-1, keepdims=True)
        vn = (vc * lax.rsqrt(var + EPS) * lng_ref[...] + lnb_ref[...]).astype(bf16)
        for ck in range(TM // C_CHUNK):
            rs = slice(ck * C_CHUNK, (ck + 1) * C_CHUNK)
            for g in range(C_GROUPS):
                cs = slice(g * gw, (g + 1) * gw)
                mixed = jnp.dot(ws_ref[g], vn[rs, cs], preferred_element_type=f32) + bs_ref[g]
                gated[s * TM + ck * C_CHUNK:s * TM + (ck + 1) * C_CHUNK, cs] = (u[rs, cs] * mixed).astype(bf16)
        yield
        out = jnp.dot(gated[rows, :], wout_ref[...], preferred_element_type=f32)
        yield
        x1 = x + mod[:, 2 * D:3 * D] * out
        x1_ref[rows, :] = x1
        yield from _route_chain(s, x1, mod, nrm_ref, wr_ref, br_ref, route_ref, cnt_ref, hbuf, gbuf, counts)

    _mixer_out_tail(i, [chain(s) for s in range(SUB_OUT)], counts, hs_hbm, gs_hbm, hbuf, gbuf, run, sem)


def _odd_mixer(run_n, run_src, x, prev_route, y, mod_prev, mod_l, nm, w_in, ln_g, ln_b, ws, bs, w_out, nrm, wr, br):
    sub = SUB_OUT
    tile = lambda w: pl.BlockSpec((sub * TM, w), lambda i, *_: (i, 0))
    modspec = pl.BlockSpec((1, 1, N_MOD * D), lambda i, *_: (_mod_row_step(i, sub), 0, 0))
    const = lambda shp: pl.BlockSpec(shp, lambda i, *_: tuple(0 for _ in shp))
    grid_spec = pltpu.PrefetchScalarGridSpec(
        num_scalar_prefetch=2,
        grid=(NT // sub,),
        in_specs=[tile(D), tile(128), pl.BlockSpec(memory_space=pl.ANY), modspec, modspec, const((1, D)),
                  const((D, 2 * W_C)), const((1, W_C)), const((1, W_C)), const((C_GROUPS, C_CHUNK, C_CHUNK)),
                  const((C_GROUPS, C_CHUNK, W_C // C_GROUPS)), const((W_C, D))] + _router_in_specs(),
        out_specs=_dispatch_out_specs(sub),
        scratch_shapes=[pltpu.VMEM((sub * TM, W_C), bf16)] + _combine_scratch(sub) + _dispatch_scratch(sub),
    )
    return pl.pallas_call(
        _odd_kernel,
        grid_spec=grid_spec,
        out_shape=_DISPATCH_OUT_SHAPES,
        compiler_params=_cp(("arbitrary",)),
        name="odd_mixer_router",
    )(run_n, run_src, x, prev_route, y, mod_prev, mod_l, nm, w_in, ln_g, ln_b, ws, bs, w_out, nrm, wr, br)


def _expert_kernel(nu_ref, tg_ref, tb_ref, hs_ref, gs_ref, wg_ref, wu_ref, wd_ref, y_ref):
    t = pl.program_id(0)

    @pl.when(t < nu_ref[0])
    def _():
        xb = _load_rows(hs_ref, 0, TMG).astype(bf16)
        gates = gs_ref[...]
        acc = jnp.zeros((TMG, D), f32)
        gate_up = lambda e: (_dot(xb, wg_ref[e]), _dot(xb, wu_ref[e]))
        nxt = gate_up(0)
        for e in range(E_PER_GROUP):
            a, b = nxt
            if e + 1 < E_PER_GROUP:
                nxt = gate_up(e + 1)
            acc = acc + _dot(_silu(a) * b * gates[:, e:e + 1], wd_ref[e])
        _store_rows(y_ref, 0, acc)


def _experts(layer, n_used, tile_g, tile_blk, hs, gs, wg, wu, wd):
    wspec = lambda a, b: pl.BlockSpec((None, None, E_PER_GROUP, a, b), lambda t, nu, tg, tb: (layer, tg[t], 0, 0, 0))
    grid_spec = pltpu.PrefetchScalarGridSpec(
        num_scalar_prefetch=3,
        grid=(MAXT_G,),
        in_specs=[
            pl.BlockSpec((TMG * ROW_SUB, 128), lambda t, nu, tg, tb: (tb[t], 0)),
            pl.BlockSpec((TMG, 128), lambda t, nu, tg, tb: (tb[t], 0)),
            wspec(D, D_EXPERT), wspec(D, D_EXPERT), wspec(D_EXPERT, D),
        ],
        out_specs=pl.BlockSpec((TMG * ROW_SUB, 128), lambda t, nu, tg, tb: (tb[t], 0)),
    )
    return pl.pallas_call(
        _expert_kernel,
        grid_spec=grid_spec,
        out_shape=jax.ShapeDtypeStruct((N_GROUPS * RG * ROW_SUB, 128), f32),
        compiler_params=_cp(("arbitrary",), VMEM_LIMIT_EXPERTS),
        name="experts",
    )(n_used, tile_g, tile_blk, hs, gs, wg, wu, wd)


def _dispatch_plan(cnt_tiles):
    cnt = cnt_tiles[:, 0, 0:N_GROUPS].astype(i32)
    ends = jnp.cumsum(cnt, axis=0)
    run_src = (ends - cnt + jnp.arange(N_GROUPS, dtype=i32)[None, :] * RG).reshape(-1)
    totals = ends[NT - 1]
    tiles_g = (totals + TMG - 1) // TMG
    tile_end = jnp.cumsum(tiles_g)
    n_used = tile_end[N_GROUPS - 1:]
    t = jnp.minimum(jnp.arange(MAXT_G, dtype=i32), n_used[0] - 1)
    tile_g = jnp.sum((tile_end[None, :] <= t[:, None]).astype(i32), axis=1)
    tile_start = tile_end - tiles_g
    first = jnp.sum(jnp.where(tile_g[:, None] == jnp.arange(N_GROUPS, dtype=i32)[None, :], tile_start[None, :], 0), axis=1)
    tile_blk = tile_g * (RG // TMG) + t - first
    return cnt.reshape(-1), run_src, n_used, tile_g, tile_blk


def _final_kernel(n_ref, s_ref, x_ref, route_ref, y_hbm, mod_ref, fn_ref, o_ref, ybuf, ysem, *, n_steps, tile_off):
    i = pl.program_id(0)
    mod = mod_ref[0]
    ybase = _combine_fetch(i, n_steps, SUB, tile_off, n_ref, s_ref, y_hbm, ybuf, ysem)

    def chain(s):
        rows = slice(s * TM, (s + 1) * TM)
        y = _unsort(ybuf, ybase[s], route_ref[rows, 0:1])
        yield
        x = x_ref[rows, :] + mod[:, 5 * D:6 * D] * y
        o_ref[rows, :] = _rms(x, fn_ref[...])

    _lockstep([chain(s) for s in range(SUB)])


def _final(run_n, run_src, x, route, y, mod_l, fnorm, *, n_tiles, tile_off):
    n_steps, step_off = n_tiles // SUB, tile_off // SUB
    grid_spec = pltpu.PrefetchScalarGridSpec(
        num_scalar_prefetch=2,
        grid=(n_steps,),
        in_specs=[
            pl.BlockSpec((SUB * TM, D), lambda i, *_: (step_off + i, 0)),
            pl.BlockSpec((SUB * TM, 128), lambda i, *_: (step_off + i, 0)),
            pl.BlockSpec(memory_space=pl.ANY),
            pl.BlockSpec((1, 1, N_MOD * D), lambda i, *_: (_mod_row_step(step_off + i), 0, 0)),
            pl.BlockSpec((1, D), lambda i, *_: (0, 0)),
        ],
        out_specs=pl.BlockSpec((SUB * TM, D), lambda i, *_: (i, 0)),
        scratch_shapes=_combine_scratch(SUB),
    )
    return pl.pallas_call(
        functools.partial(_final_kernel, n_steps=n_steps, tile_off=tile_off),
        grid_spec=grid_spec,
        out_shape=jax.ShapeDtypeStruct((n_tiles * TM, D), f32),
        compiler_params=_cp(("arbitrary",)),
        name="final_norm",
    )(run_n, run_src, x, route, y, mod_l, fnorm)


def _rope_tables():
    rows = DEC_SEQ // GRID_W
    r, c = jnp.meshgrid(jnp.arange(rows), jnp.arange(GRID_W), indexing="ij")
    pos = jnp.stack([r.reshape(-1), c.reshape(-1)], axis=-1).astype(f32)
    inv = ROPE_BASE ** (-jnp.arange(AXIS_FREQS, dtype=f32) / AXIS_FREQS)
    ang = pos[:, :, None] * inv
    cos, sin = jnp.cos(ang), jnp.sin(ang)
    cos_h = jnp.concatenate([cos, cos], axis=-1).reshape(DEC_SEQ, DH_A)
    sin_h = jnp.concatenate([-sin, sin], axis=-1).reshape(DEC_SEQ, DH_A)
    cos_t = jnp.concatenate([jnp.ones((SUB * TM, 2 * DH_A), f32), jnp.tile(cos_h, (1, 2))], axis=0)
    sin_t = jnp.concatenate([jnp.zeros((SUB * TM, 2 * DH_A), f32), jnp.tile(sin_h, (1, 2))], axis=0)
    return cos_t, sin_t


def _gate_lanes(p):
    out = jnp.zeros((128,), f32)
    for d in range(2):
        out = out.at[d * 2 * H_B:d * 2 * H_B + H_B].set(p[d])
    return out.reshape(1, 128)


def _router_weights(w_rg, b_rg, w_re, b_re):
    wr = jnp.zeros((D, 256), f32).at[:, 0:N_GROUPS].set(w_rg).at[:, 128:128 + N_EXPERTS].set(w_re)
    br = jnp.zeros((1, 256), f32).at[0, 0:N_GROUPS].set(b_rg).at[0, 128:128 + N_EXPERTS].set(b_re)
    return jnp.stack(_hi_lo(wr)), br


def kernel(x_prompt, x_sample, cache_k, cache_v, state_fwd, state_bwd, c, c_ctx, norm_mix, norm_ffn, w_mod, b_mod,
           w_in_even, conv_w, a_log, dt_bias, delta_norm, lam_q, lam_k, subln, w_out_even, w_in_odd, sgu_ln_g,
           sgu_ln_b, w_spatial, b_spatial, w_out_odd, w_router_group, b_router_group, w_router_expert,
           b_router_expert, w_exp_gate, w_exp_up, w_exp_down, final_norm):
    xp = x_prompt.reshape(N_P, D)
    xs = x_sample.reshape(N_S, D)
    cond = jnp.concatenate([c, c_ctx[None, :], jnp.zeros((COND_ROWS - DEC_BATCH - 1, D), f32)], axis=0)
    mod = _modulation(cond, w_mod, b_mod)
    mod0 = mod[0].reshape(COND_ROWS, 1, N_MOD * D)
    mod1 = mod[1].reshape(COND_ROWS, 1, N_MOD * D)

    lam_init = 0.8 - 0.6 * math.exp(-0.3 * 0)
    w_in_pad = jnp.pad(w_in_even[0], ((0, 0), (0, IN_EVEN_PAD - IN_EVEN))).astype(bf16)
    cos_t, sin_t = _rope_tables()
    qa, ka, va, qkvb, gate_b, ab, ka_p, va_p = _even_in(xp, xs, mod0, norm_mix[0:1], w_in_pad, cos_t, sin_t)

    ctx_k = cache_k[:, 0].reshape(DEC_BATCH * PAST, W_A)
    ctx_v = cache_v[:, 0].reshape(DEC_BATCH * PAST, W_A)
    attn = functools.partial(_attention, qa, ka, va, lam_q[0], lam_k[0], subln[0:1], lam_init)
    oa_p = attn(seq=SEQ, tq=SEQ, tok_off=0)
    oa_s = attn(seq=DEC_SEQ, tq=512, tok_off=N_P, ctx=(ctx_k, ctx_v))

    conv_pad = jnp.pad(conv_w[0], ((0, 8 - CONV_K), (0, 0)))
    dl = functools.partial(_delta, qkvb, gate_b, ab, conv_pad, _gate_lanes(a_log[0]), _gate_lanes(dt_bias[0]),
                           delta_norm[0:1])
    ob_p, s_f, s_b = dl(seq=SEQ, tok_off=0)
    ob_s, _, _ = dl(seq=DEC_SEQ, tok_off=N_P, init=(state_fwd[:, 0], state_bwd[:, 0]))

    wr0, br0 = _router_weights(w_router_group[0], b_router_group[0], w_router_expert[0], b_router_expert[0])
    x1, route0, cnt0, hs, gs = _even_out(xp, xs, oa_p, oa_s, ob_p, ob_s, mod0, w_out_even[0].astype(bf16),
                                         norm_ffn[0:1], wr0, br0)

    wshape = (DEPTH, N_GROUPS, E_PER_GROUP)
    wg = w_exp_gate.reshape(wshape + (D, D_EXPERT))
    wu = w_exp_up.reshape(wshape + (D, D_EXPERT))
    wd = w_exp_down.reshape(wshape + (D_EXPERT, D))
    run_n0, run_src0, n_used, tile_g, tile_blk = _dispatch_plan(cnt0)
    y0 = _experts(0, n_used, tile_g, tile_blk, hs, gs, wg, wu, wd)

    bs = jnp.broadcast_to(b_spatial[0][:, :, None], (C_GROUPS, C_CHUNK, W_C // C_GROUPS))
    wr1, br1 = _router_weights(w_router_group[1], b_router_group[1], w_router_expert[1], b_router_expert[1])
    x2, route1, cnt1, hs, gs = _odd_mixer(run_n0, run_src0, x1, route0, y0, mod0, mod1, norm_mix[1:2],
                                          w_in_odd[0].astype(bf16), sgu_ln_g[0:1], sgu_ln_b[0:1],
                                          w_spatial[0].astype(bf16), bs, w_out_odd[0].astype(bf16), norm_ffn[1:2],
                                          wr1, br1)
    run_n1, run_src1, n_used, tile_g, tile_blk = _dispatch_plan(cnt1)
    y1 = _experts(1, n_used, tile_g, tile_blk, hs, gs, wg, wu, wd)

    fin = functools.partial(_final, run_n1, run_src1, x2, route1, y1, mod1, final_norm[None, :])
    y_prompt = fin(n_tiles=NPT, tile_off=0)
    y_sample = fin(n_tiles=NT - NPT, tile_off=NPT)

    new_cache_k = ka_p.reshape(BATCH, 1, SEQ, H_A, 2 * DH_A)
    new_cache_v = va_p.reshape(BATCH, 1, SEQ, H_A, 2 * DH_A)
    return (y_prompt.reshape(BATCH, SEQ, D), y_sample.reshape(DEC_BATCH, DEC_SEQ, D), new_cache_k, new_cache_v,
            s_f[:, None], s_b[:, None])
```

```python
import functools
import math

import jax
import jax.numpy as jnp
from jax import lax
from jax.experimental import pallas as pl
from jax.experimental.pallas import tpu as pltpu

f32 = jnp.float32
bf16 = jnp.bfloat16
i32 = jnp.int32

D = 1024
BATCH, SEQ = 16, 256
DEC_BATCH, DEC_SEQ, PAST = 8, 1024, 512
DEPTH = 2
GRID_W = 64
N_MOD = 6
EPS = 1e-6
H_A, DH_A = 4, 64
W_A = H_A * 2 * DH_A
ROPE_BASE = 10000.0
AXIS_FREQS = DH_A // 4
H_B, DK_B, DV_B = 4, 128, 128
W_B = H_B * DV_B
CONV_K = 5
CH = 64
IN_EVEN = 3 * W_A + 4 * W_B + 4 * H_B
IN_EVEN_PAD = 3712
W_C = D
C_GROUPS, C_CHUNK = 4, 128
N_GROUPS, E_PER_GROUP, D_EXPERT = 4, 8, 256
N_EXPERTS = N_GROUPS * E_PER_GROUP

N_P = BATCH * SEQ
N_S = DEC_BATCH * DEC_SEQ
N_TOK = N_P + N_S
TM = 256
NPT = N_P // TM
NT = N_TOK // TM
COND_ROWS = 16

VMEM_LIMIT = 56 * 1024 * 1024
VMEM_LIMIT_EXPERTS = 62 * 1024 * 1024


def _cp(sem, vmem_limit=VMEM_LIMIT):
    return pltpu.CompilerParams(dimension_semantics=sem, vmem_limit_bytes=vmem_limit)


SUB = 2
NS = NT // SUB
NPS = NPT // SUB
SUB_OUT = 4


def _mod_row_step(i, sub=SUB):
    return jnp.where(i < NPT // sub, DEC_BATCH, (i - NPT // sub) // (DEC_SEQ // (sub * TM)))


def _lockstep(chains):
    live = list(chains)
    while live:
        nxt = []
        for ch in live:
            try:
                next(ch)
                nxt.append(ch)
            except StopIteration:
                pass
        live = nxt


def _rms(x, g):
    return x * lax.rsqrt(jnp.mean(x * x, axis=-1, keepdims=True) + EPS) * g


def _silu(x):
    return x * jax.nn.sigmoid(x)


def _dot(a, b):
    return jnp.dot(a.astype(bf16), b.astype(bf16), preferred_element_type=f32)


ROW_SUB = D // 128


def _store_rows(ref, base, x):
    rows = x.shape[0]
    for j in range(ROW_SUB):
        ref[pl.ds(base + j, rows, stride=ROW_SUB), :] = x[:, j * 128:(j + 1) * 128]


def _load_rows(ref, base, rows):
    return jnp.concatenate([ref[pl.ds(base + j, rows, stride=ROW_SUB), :] for j in range(ROW_SUB)], axis=1)


TN_MOD = 1536


def _mod_kernel(c_ref, w_ref, b_ref, o_ref):
    a = _silu(c_ref[...])
    o_ref[0] = _dot(a, w_ref[0]) + b_ref[0]


def _modulation(cond, w_mod, b_mod):
    return pl.pallas_call(
        _mod_kernel,
        grid=(DEPTH, N_MOD * D // TN_MOD),
        in_specs=[
            pl.BlockSpec((COND_ROWS, D), lambda l, j: (0, 0)),
            pl.BlockSpec((1, D, TN_MOD), lambda l, j: (l, 0, j)),
            pl.BlockSpec((1, 1, TN_MOD), lambda l, j: (l, 0, j)),
        ],
        out_specs=pl.BlockSpec((1, COND_ROWS, TN_MOD), lambda l, j: (l, 0, j)),
        out_shape=jax.ShapeDtypeStruct((DEPTH, COND_ROWS, N_MOD * D), f32),
        compiler_params=_cp(("arbitrary", "arbitrary")),
        name="modulation",
    )(cond, w_mod, b_mod.reshape(DEPTH, 1, N_MOD * D))


Q_SCALE = DH_A ** -0.5 * math.log2(math.e)


def _even_in_kernel(xp_ref, xs_ref, mod_ref, nrm_ref, w_ref, cos_ref, sin_ref,
                    q_ref, k_ref, v_ref, qkvb_ref, gate_ref, ab_ref, kp_ref, vp_ref):
    i = pl.program_id(0)
    is_p = i < NPS
    mod = mod_ref[0]
    lane = lax.broadcasted_iota(i32, (TM, 2 * DH_A), 1)
    first = (lane % (2 * AXIS_FREQS)) < AXIS_FREQS

    def chain(s):
        rows = slice(s * TM, (s + 1) * TM)
        x = jnp.where(is_p, xp_ref[rows, :], xs_ref[rows, :])
        h = _rms(x, nrm_ref[...]) * (1.0 + mod[:, D:2 * D]) + mod[:, 0:D]
        proj = _dot(h, w_ref[...])
        yield
        cos = cos_ref[rows, :]
        sin = sin_ref[rows, :]

        def rope(xh):
            partner = jnp.where(first, pltpu.roll(xh, 2 * DH_A - AXIS_FREQS, 1), pltpu.roll(xh, AXIS_FREQS, 1))
            return xh * cos + partner * sin

        for hh in range(H_A):
            sl = slice(hh * 2 * DH_A, (hh + 1) * 2 * DH_A)
            q_ref[rows, sl] = (rope(proj[:, sl]) * Q_SCALE).astype(bf16)
            k_ref[rows, sl] = rope(proj[:, W_A + hh * 2 * DH_A:W_A + (hh + 1) * 2 * DH_A])
        v_ref[rows, :] = proj[:, 2 * W_A:3 * W_A]
        qkvb_ref[rows, :] = proj[:, 3 * W_A:3 * W_A + 3 * W_B]
        gate_ref[rows, :] = proj[:, 3 * W_A + 3 * W_B:3 * W_A + 4 * W_B]
        ab_ref[rows, :] = proj[:, 3 * W_A + 4 * W_B:IN_EVEN_PAD]

    _lockstep([chain(s) for s in range(SUB)])

    @pl.when(is_p)
    def _():
        for hh in range(H_A):
            sl = slice(hh * 2 * DH_A, (hh + 1) * 2 * DH_A)
            kp_ref[pl.ds(hh, SUB * TM, stride=H_A), :] = k_ref[:, sl]
            vp_ref[pl.ds(hh, SUB * TM, stride=H_A), :] = v_ref[:, sl]


def _even_in(xp, xs, mod_l, nrm, w_in_pad, cos_t, sin_t):
    tile = lambda w: pl.BlockSpec((SUB * TM, w), lambda i: (i, 0))
    ptile = lambda w: pl.BlockSpec((SUB * TM, w), lambda i: (jnp.minimum(i, NPS - 1), 0))
    rope_idx = lambda i: (jnp.where(i < NPS, 0, 1 + (i - NPS) % (DEC_SEQ // (SUB * TM))), 0)
    return pl.pallas_call(
        _even_in_kernel,
        grid=(NS,),
        in_specs=[
            ptile(D),
            pl.BlockSpec((SUB * TM, D), lambda i: (jnp.maximum(i - NPS, 0), 0)),
            pl.BlockSpec((1, 1, N_MOD * D), lambda i: (_mod_row_step(i), 0, 0)),
            pl.BlockSpec((1, D), lambda i: (0, 0)),
            pl.BlockSpec((D, IN_EVEN_PAD), lambda i: (0, 0)),
            pl.BlockSpec((SUB * TM, 2 * DH_A), rope_idx),
            pl.BlockSpec((SUB * TM, 2 * DH_A), rope_idx),
        ],
        out_specs=[tile(W_A), tile(W_A), tile(W_A), tile(3 * W_B), tile(W_B), tile(128)]
        + [pl.BlockSpec((SUB * TM * H_A, 2 * DH_A), lambda i: (jnp.minimum(i, NPS - 1), 0))] * 2,
        out_shape=[
            jax.ShapeDtypeStruct((N_TOK, W_A), bf16),
            jax.ShapeDtypeStruct((N_TOK, W_A), f32),
            jax.ShapeDtypeStruct((N_TOK, W_A), f32),
            jax.ShapeDtypeStruct((N_TOK, 3 * W_B), f32),
            jax.ShapeDtypeStruct((N_TOK, W_B), f32),
            jax.ShapeDtypeStruct((N_TOK, 128), f32),
            jax.ShapeDtypeStruct((N_P * H_A, 2 * DH_A), f32),
            jax.ShapeDtypeStruct((N_P * H_A, 2 * DH_A), f32),
        ],
        compiler_params=_cp(("arbitrary",)),
        name="even_in",
    )(xp, xs, mod_l, nrm, w_in_pad, cos_t, sin_t)


ATTN_HEADS_IN_LOCKSTEP = 2


def _attn_kernel(*refs, lam_init, has_ctx):
    if has_ctx:
        q_ref, kn_ref, vn_ref, kc_ref, vc_ref, lq_ref, lk_ref, sub_ref, o_ref = refs
    else:
        q_ref, kn_ref, vn_ref, lq_ref, lk_ref, sub_ref, o_ref = refs
    tq = q_ref.shape[0]
    lq = lq_ref[...]
    lk = lk_ref[...]
    prod = lq * lk
    lam = (jnp.exp(jnp.sum(prod[0:1], axis=-1, keepdims=True))
           - jnp.exp(jnp.sum(prod[1:2], axis=-1, keepdims=True)) + lam_init)
    lane = lax.broadcasted_iota(i32, (tq, 2 * DH_A), 1)
    nt = (((1,), (1,)), ((), ()))
    tn = (((0,), (0,)), ((), ()))
    sub_col = jnp.broadcast_to(sub_ref[...], (8, 2 * DH_A)).T[:, 0:1]
    outs = {}

    def softmax_part(hh, part, srcs):
        q = q_ref[:, hh * 2 * DH_A:(hh + 1) * 2 * DH_A]
        qm = jnp.where((lane < DH_A) if part == 0 else (lane >= DH_A), q, jnp.zeros_like(q))
        ss = [lax.dot_general(k, qm, nt, preferred_element_type=f32) for k, _ in srcs]
        yield
        m = ss[0].max(axis=0, keepdims=True)
        for s in ss[1:]:
            m = jnp.maximum(m, s.max(axis=0, keepdims=True))
        l = jnp.zeros((1, tq), f32)
        o = jnp.zeros((2 * DH_A, tq), f32)
        for s, (_, v) in zip(ss, srcs):
            e = jnp.exp2(s - m)
            l = l + jnp.sum(e, axis=0, keepdims=True)
            o = o + lax.dot_general(v, e.astype(bf16), tn, preferred_element_type=f32)
        yield
        outs[(hh, part)] = o / l

    for h0 in range(0, H_A, ATTN_HEADS_IN_LOCKSTEP):
        chains = []
        for hh in range(h0, h0 + ATTN_HEADS_IN_LOCKSTEP):
            sl = slice(hh * 2 * DH_A, (hh + 1) * 2 * DH_A)
            srcs = [(kn_ref[:, sl].astype(bf16), vn_ref[:, sl].astype(bf16))]
            if has_ctx:
                srcs.append((kc_ref[pl.ds(hh, PAST, stride=H_A), :].astype(bf16),
                             vc_ref[pl.ds(hh, PAST, stride=H_A), :].astype(bf16)))
            chains += [softmax_part(hh, part, srcs) for part in range(2)]
        _lockstep(chains)
        for hh in range(h0, h0 + ATTN_HEADS_IN_LOCKSTEP):
            o = outs[(hh, 0)] - lam * outs[(hh, 1)]
            o = o * lax.rsqrt(jnp.mean(o * o, axis=0, keepdims=True) + EPS) * sub_col * (1.0 - lam_init)
            o_ref[:, hh * 2 * DH_A:(hh + 1) * 2 * DH_A] = o.T.astype(bf16)


def _attention(q, ka, va, lam_q, lam_k, subln, lam_init, *, seq, tq, tok_off, ctx=None):
    nq = seq // tq
    nb = (N_P if ctx is None else N_S) // seq
    qo, ko = tok_off // tq, tok_off // seq
    in_specs = [
        pl.BlockSpec((tq, W_A), lambda b, j: (qo + b * nq + j, 0)),
        pl.BlockSpec((seq, W_A), lambda b, j: (ko + b, 0)),
        pl.BlockSpec((seq, W_A), lambda b, j: (ko + b, 0)),
    ]
    args = [q, ka, va]
    if ctx is not None:
        in_specs += [pl.BlockSpec((PAST * H_A, 2 * DH_A), lambda b, j: (b, 0))] * 2
        args += list(ctx)
    in_specs += [pl.BlockSpec((2, DH_A), lambda b, j: (0, 0))] * 2 + [pl.BlockSpec((1, 2 * DH_A), lambda b, j: (0, 0))]
    args += [lam_q, lam_k, subln]
    return pl.pallas_call(
        functools.partial(_attn_kernel, lam_init=lam_init, has_ctx=ctx is not None),
        grid=(nb, nq),
        in_specs=in_specs,
        out_specs=pl.BlockSpec((tq, W_A), lambda b, j: (b * nq + j, 0)),
        out_shape=jax.ShapeDtypeStruct((nb * seq, W_A), bf16),
        compiler_params=_cp(("arbitrary", "arbitrary")),
        name="diff_attn_ctx" if ctx is not None else "diff_attn",
    )(*args)


def _softplus(x):
    return jnp.maximum(x, 0.0) + jnp.log1p(jnp.exp(-jnp.abs(x)))


RW = H_B * CH
CPI = 4


def _rep_rows(x, times):
    return jnp.concatenate([x] * times, axis=0)


def _tri_inv(lmats, bd, ii, jj):
    b16 = (ii // 16) == (jj // 16)
    b32 = (ii // 32) == (jj // 32)
    eye = (ii == jj).astype(f32)
    n = len(lmats)
    mm = lambda a, b: jnp.dot(a.astype(bf16), bd(b), preferred_element_type=f32)
    l0 = [jnp.where(b16, l, 0.0) for l in lmats]
    x = [eye - a for a in l0]
    p = [mm(a, a) for a in l0]
    for level in range(3):
        xp = [mm(x[i], p[i]) for i in range(n)]
        if level < 2:
            p = [mm(a, a) for a in p]
        x = [x[i] + xp[i] for i in range(n)]
    for off_diag in (jnp.logical_and(b32, jnp.logical_not(b16)), jnp.logical_not(b32)):
        t = [mm(jnp.where(off_diag, lmats[i], 0.0), x[i]) for i in range(n)]
        t = [mm(x[i], t[i]) for i in range(n)]
        x = [x[i] - t[i] for i in range(n)]
    return x


def _delta_kernel(*refs, seq, has_init):
    if has_init:
        (qkv_ref, gate_ref, ab_ref, cw_ref, al_ref, dt_ref, dn_ref, sf0_ref, sb0_ref,
         ob_ref, sf_ref, sb_ref, xpad, xq, xk, xv, gacc, bacc, o_f, o_b, st, pu, pw, pqe, pkd, patt, ptot) = refs
    else:
        (qkv_ref, gate_ref, ab_ref, cw_ref, al_ref, dt_ref, dn_ref,
         ob_ref, sf_ref, sb_ref, xpad, xq, xk, xv, gacc, bacc, o_f, o_b, st, pu, pw, pqe, pkd, patt, ptot) = refs
    n = seq // CH
    win = CH + 16

    xpad[0:8, :] = jnp.zeros((8, 3 * W_B), f32)
    xpad[seq + 8:seq + 16, :] = jnp.zeros((8, 3 * W_B), f32)
    xpad[8:seq + 8, :] = qkv_ref[...]
    neg_a = -jnp.exp(al_ref[...])
    dtb = dt_ref[...]

    def pre(c, carry):
        r0 = pl.multiple_of(c * CH, CH)
        a = xpad[pl.ds(r0, win), :]
        y = jnp.zeros((CH, 3 * W_B), f32)
        for j in range(CONV_K):
            sh = ((CONV_K - 1) // 2 - j) % win
            y = y + pltpu.roll(a, sh, 0)[8:8 + CH] * cw_ref[j:j + 1, :]
        y = _silu(y)
        for hh in range(H_B):
            qh = y[:, hh * DK_B:(hh + 1) * DK_B]
            kh = y[:, W_B + hh * DK_B:W_B + (hh + 1) * DK_B]
            qn = qh * lax.rsqrt(jnp.sum(qh * qh, axis=-1, keepdims=True) + EPS) * (DK_B ** -0.5)
            kn = kh * lax.rsqrt(jnp.sum(kh * kh, axis=-1, keepdims=True) + EPS)
            xq[pl.ds(r0, CH), hh * DK_B:(hh + 1) * DK_B] = qn
            xk[pl.ds(r0, CH), hh * DK_B:(hh + 1) * DK_B] = kn
        xv[pl.ds(r0, CH), :] = y[:, 2 * W_B:3 * W_B]
        ab = ab_ref[pl.ds(r0, CH), :]
        gacc[pl.ds(r0, CH), :] = neg_a * _softplus(ab + dtb)
        bacc[pl.ds(r0, CH), :] = jax.nn.sigmoid(ab)
        return carry

    lax.fori_loop(0, n, pre, 0)

    ii = lax.broadcasted_iota(i32, (CH, RW), 0)
    lane_r = lax.broadcasted_iota(i32, (CH, RW), 1)
    jj = lane_r % CH
    hb = lane_r // CH
    rowl = lax.broadcasted_iota(i32, (CH, 128), 0)
    bd_mask = (lax.broadcasted_iota(i32, (RW, RW), 0) // CH) == (lax.broadcasted_iota(i32, (RW, RW), 1) // CH)
    wide_mask = ((lax.broadcasted_iota(i32, (RW, W_B), 0) // CH)
                 == (lax.broadcasted_iota(i32, (RW, W_B), 1) // DK_B))
    pair_mask = ((lax.broadcasted_iota(i32, (2 * DK_B, 2 * DV_B), 0) // DK_B)
                 == (lax.broadcasted_iota(i32, (2 * DK_B, 2 * DV_B), 1) // DV_B))
    zero_b = jnp.zeros((), bf16)

    def bd(x):
        return jnp.where(bd_mask, _rep_rows(x.astype(bf16), H_B), zero_b)

    def bd_wide(x):
        return jnp.where(wide_mask, _rep_rows(x.astype(bf16), H_B), zero_b)

    def per_head(src, lanes):
        return jnp.concatenate([jnp.broadcast_to(src[:, ln:ln + 1], (src.shape[0], DK_B)) for ln in lanes], axis=1)

    nt = (((1,), (1,)), ((), ()))
    tn = (((0,), (0,)), ((), ()))

    def local(it, carry):
        chains = []
        for j in range(CPI):
            c = it * CPI + j
            r0 = pl.multiple_of(c * CH, CH)
            g = gacc[pl.ds(r0, CH), :]
            beta = bacc[pl.ds(r0, CH), :]
            pre_g = g
            for s in (1, 2, 4, 8, 16, 32):
                pre_g = pre_g + jnp.where(rowl >= s, pltpu.roll(pre_g, s, 0), 0.0)
            tot = pre_g[CH - 1:CH, :]
            q_all = xq[pl.ds(r0, CH), :]
            k_all = xk[pl.ds(r0, CH), :]
            v_all = xv[pl.ds(r0, CH), :]
            k_bd = bd_wide(k_all)
            for d in range(2):
                gc = pre_g if d == 0 else tot - pre_g + g
                lanes = [d * 2 * H_B + hh for hh in range(H_B)]
                gct = _rep_rows(gc, H_B).T
                g_col = jnp.zeros((CH, RW), f32)
                g_row = jnp.zeros((1, RW), f32)
                for hh, ln in enumerate(lanes):
                    g_col = jnp.where(hb == hh, gc[:, ln:ln + 1], g_col)
                    g_row = jnp.where(hb[0:1] == hh, gct[ln:ln + 1, :], g_row)
                g_wide = per_head(gc, lanes)
                b_wide = per_head(beta, [ln + H_B for ln in lanes])
                incl = (ii >= jj) if d == 0 else (ii <= jj)
                strict = (ii > jj) if d == 0 else (ii < jj)
                eg = jnp.exp(g_wide)
                kb = k_all * b_wide
                chains.append(dict(
                    ci=c * 2 + d, incl=incl, strict=strict, k_bd=k_bd, kb=kb, q=q_all, tot=tot,
                    dec=jnp.where(incl, jnp.exp(jnp.where(incl, g_col - g_row, 0.0)), 0.0),
                    vb=v_all * b_wide, kbe=kb * eg, qe=q_all * eg,
                    kd=k_all * jnp.exp(per_head(tot, lanes) - g_wide)))
        kk = [lax.dot_general(e["kb"].astype(bf16), e["k_bd"], nt, preferred_element_type=f32) for e in chains]
        qk = [lax.dot_general(e["q"].astype(bf16), e["k_bd"], nt, preferred_element_type=f32) for e in chains]
        lmat = [jnp.where(e["strict"], kk[i] * e["dec"], 0.0) for i, e in enumerate(chains)]
        tinv = _tri_inv(lmat, bd, ii, jj)
        u = [jnp.dot(tinv[i].astype(bf16), bd_wide(e["vb"]), preferred_element_type=f32) for i, e in enumerate(chains)]
        w = [jnp.dot(tinv[i].astype(bf16), bd_wide(e["kbe"]), preferred_element_type=f32) for i, e in enumerate(chains)]
        for i, e in enumerate(chains):
            ci = e["ci"]
            pu[ci] = u[i]
            pw[ci] = w[i].astype(bf16)
            pqe[ci] = e["qe"].astype(bf16)
            pkd[ci] = e["kd"].astype(bf16)
            patt[ci] = jnp.where(e["incl"], qk[i] * e["dec"], 0.0).astype(bf16)
            ptot[ci] = jnp.broadcast_to(e["tot"], (8, 128))
        return carry

    lax.fori_loop(0, n // CPI, local, 0)

    for hh in range(H_B):
        rows = slice(hh * DK_B, (hh + 1) * DK_B)
        if has_init:
            st[0, rows, :] = sf0_ref[0, hh]
            st[1, rows, :] = sb0_ref[0, hh]
        else:
            st[0, rows, :] = jnp.zeros((DK_B, DV_B), f32)
            st[1, rows, :] = jnp.zeros((DK_B, DV_B), f32)

    def scan(it, carry):
        chains = []
        for d in range(2):
            c = it if d == 0 else n - 1 - it
            ci = c * 2 + d
            s_old = st[d]
            s_b = s_old.astype(bf16)
            chains.append(dict(
                d=d, r0=pl.multiple_of(c * CH, CH), s_old=s_old, u=pu[ci], att=patt[ci], kd=pkd[ci], tot=ptot[ci],
                lhs=jnp.concatenate([pw[ci], pqe[ci]], axis=0),
                rhs=[jnp.where(pair_mask, jnp.concatenate([s_b[p * 2 * DK_B:(p + 1) * 2 * DK_B]] * 2, axis=1), zero_b)
                     for p in range(2)]))
        ws_qs = [jnp.concatenate(
            [jnp.dot(e["lhs"][:, p * 2 * DK_B:(p + 1) * 2 * DK_B], e["rhs"][p], preferred_element_type=f32)
             for p in range(2)], axis=1) for e in chains]
        v_new = [e["u"] - ws_qs[i][0:CH] for i, e in enumerate(chains)]
        av = [jnp.dot(e["att"], bd_wide(v_new[i]), preferred_element_type=f32) for i, e in enumerate(chains)]
        su = [lax.dot_general(jnp.where(wide_mask, _rep_rows(e["kd"], H_B), zero_b),
                              jnp.concatenate([v_new[i][:, hh * DV_B:(hh + 1) * DV_B] for hh in range(H_B)],
                                              axis=0).astype(bf16),
                              tn, preferred_element_type=f32) for i, e in enumerate(chains)]
        for i, e in enumerate(chains):
            d = e["d"]
            if d == 0:
                o_f[pl.ds(e["r0"], CH), :] = ws_qs[i][CH:2 * CH] + av[i]
            else:
                o_b[pl.ds(e["r0"], CH), :] = ws_qs[i][CH:2 * CH] + av[i]
            for hh in range(H_B):
                rows = slice(hh * DK_B, (hh + 1) * DK_B)
                ln = d * 2 * H_B + hh
                st[d, rows, :] = e["s_old"][rows] * jnp.exp(e["tot"][0:1, ln:ln + 1]) + su[i][rows]
        return carry

    lax.fori_loop(0, n, scan, 0)

    def post(c, carry):
        r0 = pl.multiple_of(c * CH, CH)
        o = o_f[pl.ds(r0, CH), :] + o_b[pl.ds(r0, CH), :]
        gt = gate_ref[pl.ds(r0, CH), :]
        for hh in range(H_B):
            sl = slice(hh * DV_B, (hh + 1) * DV_B)
            ob_ref[pl.ds(r0, CH), sl] = (_rms(o[:, sl], dn_ref[...]) * _silu(gt[:, sl])).astype(bf16)
        return carry

    lax.fori_loop(0, n, post, 0)
    for hh in range(H_B):
        sf_ref[0, hh] = st[0, hh * DK_B:(hh + 1) * DK_B, :]
        sb_ref[0, hh] = st[1, hh * DK_B:(hh + 1) * DK_B, :]


def _delta(qkvb, gate, ab, conv_pad, al_lane, dt_lane, dnorm, *, seq, tok_off, init=None):
    nb = (N_P if init is None else N_S) // seq
    bo = tok_off // seq
    rows = lambda w: pl.BlockSpec((seq, w), lambda b: (bo + b, 0))
    const = lambda shp: pl.BlockSpec(shp, lambda b: tuple(0 for _ in shp))
    st_spec = pl.BlockSpec((1, H_B, DK_B, DV_B), lambda b: (b, 0, 0, 0))
    in_specs = [rows(3 * W_B), rows(W_B), rows(128), const((8, 3 * W_B)), const((1, 128)), const((1, 128)),
                const((1, DV_B))]
    args = [qkvb, gate, ab, conv_pad, al_lane, dt_lane, dnorm]
    if init is not None:
        in_specs += [st_spec, st_spec]
        args += [init[0], init[1]]
    return pl.pallas_call(
        functools.partial(_delta_kernel, seq=seq, has_init=init is not None),
        grid=(nb,),
        in_specs=in_specs,
        out_specs=[pl.BlockSpec((seq, W_B), lambda b: (b, 0)), st_spec, st_spec],
        out_shape=[
            jax.ShapeDtypeStruct((nb * seq, W_B), bf16),
            jax.ShapeDtypeStruct((nb, H_B, DK_B, DV_B), f32),
            jax.ShapeDtypeStruct((nb, H_B, DK_B, DV_B), f32),
        ],
        scratch_shapes=[
            pltpu.VMEM((seq + 16, 3 * W_B), f32),
            pltpu.VMEM((seq, W_B), f32),
            pltpu.VMEM((seq, W_B), f32),
            pltpu.VMEM((seq, W_B), f32),
            pltpu.VMEM((seq, 128), f32),
            pltpu.VMEM((seq, 128), f32),
            pltpu.VMEM((seq, W_B), f32),
            pltpu.VMEM((seq, W_B), f32),
            pltpu.VMEM((2, H_B * DK_B, DV_B), f32),
            pltpu.VMEM((2 * seq // CH, CH, W_B), f32),
            pltpu.VMEM((2 * seq // CH, CH, W_B), bf16),
            pltpu.VMEM((2 * seq // CH, CH, W_B), bf16),
            pltpu.VMEM((2 * seq // CH, CH, W_B), bf16),
            pltpu.VMEM((2 * seq // CH, CH, RW), bf16),
            pltpu.VMEM((2 * seq // CH, 8, 128), f32),
        ],
        compiler_params=_cp(("arbitrary",)),
        name="delta_ctx" if init is not None else "delta",
    )(*args)


RG = N_TOK
TMG = 256
MAXT_G = N_TOK // TMG + N_GROUPS
TILE_ROWS = TM * ROW_SUB
RUN_BITS = tuple(range(TM.bit_length() - 1, -1, -1))


def _hi_lo(x):
    hi = x.astype(bf16)
    return hi, (x - hi.astype(f32)).astype(bf16)


def _run_pieces(n):
    return [(1 << b, (n >> (b + 1)) << (b + 1), ((n >> b) & 1) == 1) for b in RUN_BITS]


def _route_chain(s, x1, mod, nrm_ref, wr_ref, br_ref, route_ref, cnt_ref, hbuf, gbuf, counts):
    rows = slice(s * TM, (s + 1) * TM)
    h2 = _rms(x1, nrm_ref[...]) * (1.0 + mod[:, 4 * D:5 * D]) + mod[:, 3 * D:4 * D]
    h_hi, h_lo = _hi_lo(h2)
    w_hi, w_lo = wr_ref[0], wr_ref[1]
    logits = (jnp.dot(h_hi, w_hi, preferred_element_type=f32) + jnp.dot(h_hi, w_lo, preferred_element_type=f32)
              + jnp.dot(h_lo, w_hi, preferred_element_type=f32)) + br_ref[...]
    yield
    lane = lax.broadcasted_iota(i32, (TM, 128), 1)
    lane_f = lane.astype(f32)
    neg = jnp.float32(-1e30)
    lg = jnp.where(lane < N_GROUPS, logits[:, 0:128], neg)
    mg = lg.max(axis=-1, keepdims=True)
    gidx = jnp.min(jnp.where(lg == mg, lane_f, 128.0), axis=-1, keepdims=True)
    pg = 1.0 / jnp.sum(jnp.exp(lg - mg), axis=-1, keepdims=True)
    le = jnp.where((lane // E_PER_GROUP).astype(f32) == gidx, logits[:, 128:256], neg)
    m1 = le.max(axis=-1, keepdims=True)
    i1 = jnp.min(jnp.where(le == m1, lane_f, 128.0), axis=-1, keepdims=True)
    le2 = jnp.where(lane_f == i1, neg, le)
    m2 = le2.max(axis=-1, keepdims=True)
    i2 = jnp.min(jnp.where(le2 == m2, lane_f, 128.0), axis=-1, keepdims=True)
    e2 = jnp.exp(m2 - m1)
    w1 = pg / (1.0 + e2)
    w2 = w1 * e2
    local = lane_f + gidx * E_PER_GROUP
    gates = jnp.where(local == i1, w1, jnp.where(local == i2, w2, 0.0))

    onehot = (lane_f == gidx).astype(f32)
    r = lax.broadcasted_iota(i32, (TM, TM), 0)
    c = lax.broadcasted_iota(i32, (TM, TM), 1)
    before = jnp.dot((r > c).astype(bf16), onehot.astype(bf16), preferred_element_type=f32)
    yield
    cnt = jnp.sum(onehot, axis=0, keepdims=True)
    pos = (jnp.sum(onehot * before, axis=-1, keepdims=True)
           + jnp.sum(jnp.where(lane_f < gidx, cnt, 0.0), axis=-1, keepdims=True))
    route_ref[rows, :] = jnp.where(lane == 0, pos, 0.0)
    cnt_ref[s] = cnt
    counts[s] = cnt.astype(i32)

    perm_t = (c.astype(f32) == pos).astype(bf16)
    tn = (((0,), (0,)), ((), ()))
    h_sorted = lax.dot_general(perm_t, h_hi, tn, preferred_element_type=f32)
    g_hi, g_lo = _hi_lo(gates)
    g_sorted = (lax.dot_general(perm_t, g_hi, tn, preferred_element_type=f32)
                + lax.dot_general(perm_t, g_lo, tn, preferred_element_type=f32))
    yield
    _store_rows(hbuf, s * TILE_ROWS, h_sorted)
    gbuf[s * TM:(s + 1) * TM, :] = g_sorted


def _dispatch_done(s, hs_hbm, gs_hbm, hbuf, gbuf, sem):
    pltpu.make_async_copy(hbuf.at[pl.ds(s * TILE_ROWS, TILE_ROWS), :], hs_hbm.at[pl.ds(0, TILE_ROWS), :],
                          sem.at[s]).wait()
    pltpu.make_async_copy(gbuf.at[pl.ds(s * TM, TM), :], gs_hbm.at[pl.ds(0, TM), :], sem.at[s]).wait()


def _dispatch_runs(s, cnt_i, hs_hbm, gs_hbm, hbuf, gbuf, run, sem):
    off = 0
    for g in range(N_GROUPS):
        n = cnt_i[0, g]
        dst = g * RG + run[g]
        for rows, o, pred in _run_pieces(n):
            @pl.when(pred)
            def _():
                pltpu.make_async_copy(
                    hbuf.at[pl.ds(pl.multiple_of(s * TILE_ROWS + (off + o) * ROW_SUB, ROW_SUB), rows * ROW_SUB), :],
                    hs_hbm.at[pl.ds(pl.multiple_of((dst + o) * ROW_SUB, ROW_SUB), rows * ROW_SUB), :],
                    sem.at[s]).start()
                pltpu.make_async_copy(gbuf.at[pl.ds(s * TM + off + o, rows), :], gs_hbm.at[pl.ds(dst + o, rows), :],
                                      sem.at[s]).start()
        run[g] = run[g] + n
        off = off + n


def _dispatch_tail_fill(hs_hbm, gs_hbm, hbuf, gbuf, run, sem):
    hbuf[pl.ds(0, TMG * ROW_SUB), :] = jnp.zeros((TMG * ROW_SUB, 128), f32)
    gbuf[pl.ds(0, TMG), :] = jnp.zeros((TMG, 128), f32)
    for g in range(N_GROUPS):
        tot = run[g]
        pad = (-tot) & (TMG - 1)
        for rows, o, pred in _run_pieces(pad):
            @pl.when(pred)
            def _():
                dst = g * RG + tot + o
                ch = pltpu.make_async_copy(
                    hbuf.at[pl.ds(0, rows * ROW_SUB), :],
                    hs_hbm.at[pl.ds(pl.multiple_of(dst * ROW_SUB, ROW_SUB), rows * ROW_SUB), :], sem.at[0])
                cg = pltpu.make_async_copy(gbuf.at[pl.ds(0, rows), :], gs_hbm.at[pl.ds(dst, rows), :], sem.at[1])
                ch.start()
                cg.start()
                ch.wait()
                cg.wait()


def _mixer_out_tail(i, chains, counts, hs_hbm, gs_hbm, hbuf, gbuf, run, sem):
    @pl.when(i == 0)
    def _():
        for g in range(N_GROUPS):
            run[g] = 0

    sub = len(chains)

    @pl.when(i >= 1)
    def _():
        for s in range(sub):
            _dispatch_done(s, hs_hbm, gs_hbm, hbuf, gbuf, sem)

    _lockstep(chains)
    for s in range(sub):
        _dispatch_runs(s, counts[s], hs_hbm, gs_hbm, hbuf, gbuf, run, sem)

    @pl.when(i == pl.num_programs(0) - 1)
    def _():
        for s in range(sub):
            _dispatch_done(s, hs_hbm, gs_hbm, hbuf, gbuf, sem)
        _dispatch_tail_fill(hs_hbm, gs_hbm, hbuf, gbuf, run, sem)


def _dispatch_out_specs(sub=SUB):
    return [
        pl.BlockSpec((sub * TM, D), lambda i, *_: (i, 0)),
        pl.BlockSpec((sub * TM, 128), lambda i, *_: (i, 0)),
        pl.BlockSpec((sub, 1, 128), lambda i, *_: (i, 0, 0)),
        pl.BlockSpec(memory_space=pl.ANY),
        pl.BlockSpec(memory_space=pl.ANY),
    ]


_DISPATCH_OUT_SHAPES = [
    jax.ShapeDtypeStruct((N_TOK, D), f32),
    jax.ShapeDtypeStruct((N_TOK, 128), f32),
    jax.ShapeDtypeStruct((NT, 1, 128), f32),
    jax.ShapeDtypeStruct((N_GROUPS * RG * ROW_SUB, 128), f32),
    jax.ShapeDtypeStruct((N_GROUPS * RG, 128), f32),
]


def _dispatch_scratch(sub=SUB):
    return [
        pltpu.VMEM((sub * TILE_ROWS, 128), f32),
        pltpu.VMEM((sub * TM, 128), f32),
        pltpu.SMEM((N_GROUPS,), i32),
        pltpu.SemaphoreType.DMA((sub,)),
    ]


def _router_in_specs():
    return [pl.BlockSpec((1, D), lambda i, *_: (0, 0)),
            pl.BlockSpec((2, D, 256), lambda i, *_: (0, 0, 0)),
            pl.BlockSpec((1, 256), lambda i, *_: (0, 0))]


def _fetch_runs(n_ref, s_ref, y_hbm, ybuf, sem, tile, slot):
    base = slot * TILE_ROWS
    off = 0
    for g in range(N_GROUPS):
        n = n_ref[tile * N_GROUPS + g]
        src = s_ref[tile * N_GROUPS + g]
        for rows, o, pred in _run_pieces(n):
            @pl.when(pred)
            def _():
                pltpu.make_async_copy(
                    y_hbm.at[pl.ds(pl.multiple_of((src + o) * ROW_SUB, ROW_SUB), rows * ROW_SUB), :],
                    ybuf.at[pl.ds(pl.multiple_of(base + (off + o) * ROW_SUB, ROW_SUB), rows * ROW_SUB), :],
                    sem.at[slot]).start()
        off = off + n


def _combine_fetch(i, n_steps, sub, tile0, n_ref, s_ref, y_hbm, ybuf, sem):
    half = (i % 2) * sub

    @pl.when(i == 0)
    def _():
        for s in range(sub):
            _fetch_runs(n_ref, s_ref, y_hbm, ybuf, sem, tile0 + s, s)

    @pl.when(i + 1 < n_steps)
    def _():
        for s in range(sub):
            _fetch_runs(n_ref, s_ref, y_hbm, ybuf, sem, tile0 + (i + 1) * sub + s, sub - half + s)

    bases = []
    for s in range(sub):
        base = pl.multiple_of((half + s) * TILE_ROWS, TILE_ROWS)
        pltpu.make_async_copy(y_hbm.at[pl.ds(0, TILE_ROWS), :], ybuf.at[pl.ds(base, TILE_ROWS), :],
                              sem.at[half + s]).wait()
        bases.append(base)
    return bases


def _unsort(ybuf, base, pos):
    y_hi, y_lo = _hi_lo(_load_rows(ybuf, base, TM))
    perm_t = (lax.broadcasted_iota(i32, (TM, TM), 1).astype(f32) == pos).astype(bf16)
    return jnp.dot(perm_t, y_hi, preferred_element_type=f32) + jnp.dot(perm_t, y_lo, preferred_element_type=f32)


def _combine_scratch(sub):
    return [pltpu.VMEM((2 * sub * TILE_ROWS, 128), f32), pltpu.SemaphoreType.DMA((2 * sub,))]


def _even_out_kernel(xp_ref, xs_ref, oap_ref, oas_ref, obp_ref, obs_ref, mod_ref, w_ref, nrm_ref, wr_ref, br_ref,
                     x1_ref, route_ref, cnt_ref, hs_hbm, gs_hbm, hbuf, gbuf, run, sem):
    i = pl.program_id(0)
    is_p = i < NPT // SUB_OUT
    mod = mod_ref[0]
    counts = [None] * SUB_OUT

    def chain(s):
        rows = slice(s * TM, (s + 1) * TM)
        x = jnp.where(is_p, xp_ref[rows, :], xs_ref[rows, :])
        oa = jnp.where(is_p, oap_ref[rows, :], oas_ref[rows, :])
        ob = jnp.where(is_p, obp_ref[rows, :], obs_ref[rows, :])
        out = (jnp.dot(oa, w_ref[0:W_A, :], preferred_element_type=f32)
               + jnp.dot(ob, w_ref[W_A:W_A + W_B, :], preferred_element_type=f32))
        yield
        x1 = x + mod[:, 2 * D:3 * D] * out
        x1_ref[rows, :] = x1
        yield from _route_chain(s, x1, mod, nrm_ref, wr_ref, br_ref, route_ref, cnt_ref, hbuf, gbuf, counts)

    _mixer_out_tail(i, [chain(s) for s in range(SUB_OUT)], counts, hs_hbm, gs_hbm, hbuf, gbuf, run, sem)


def _even_out(xp, xs, oa_p, oa_s, ob_p, ob_s, mod_l, w_out, nrm, wr, br):
    sub = SUB_OUT
    ptile = lambda w: pl.BlockSpec((sub * TM, w), lambda i: (jnp.minimum(i, NPT // sub - 1), 0))
    stile = lambda w: pl.BlockSpec((sub * TM, w), lambda i: (jnp.maximum(i - NPT // sub, 0), 0))
    return pl.pallas_call(
        _even_out_kernel,
        grid=(NT // sub,),
        in_specs=[
            ptile(D), stile(D), ptile(W_A), stile(W_A), ptile(W_B), stile(W_B),
            pl.BlockSpec((1, 1, N_MOD * D), lambda i: (_mod_row_step(i, sub), 0, 0)),
            pl.BlockSpec((W_A + W_B, D), lambda i: (0, 0)),
        ] + _router_in_specs(),
        out_specs=_dispatch_out_specs(sub),
        out_shape=_DISPATCH_OUT_SHAPES,
        scratch_shapes=_dispatch_scratch(sub),
        compiler_params=_cp(("arbitrary",)),
        name="even_out_router",
    )(xp, xs, oa_p, oa_s, ob_p, ob_s, mod_l, w_out, nrm, wr, br)


def _gelu_tanh(x):
    return 0.5 * x * (1.0 + jnp.tanh(math.sqrt(2.0 / math.pi) * (x + 0.044715 * (x * x * x))))


def _odd_kernel(n_ref, s_ref, x_ref, proute_ref, y_hbm, modp_ref, mod_ref, nm_ref, win_ref, lng_ref, lnb_ref,
                ws_ref, bs_ref, wout_ref, nrm_ref, wr_ref, br_ref,
                x1_ref, route_ref, cnt_ref, hs_hbm, gs_hbm, gated, ybuf, ysem, hbuf, gbuf, run, sem):
    i = pl.program_id(0)
    modp = modp_ref[0]
    mod = mod_ref[0]
    counts = [None] * SUB_OUT
    ybase = _combine_fetch(i, NT // SUB_OUT, SUB_OUT, 0, n_ref, s_ref, y_hbm, ybuf, ysem)
    gw = W_C // C_GROUPS

    def chain(s):
        rows = slice(s * TM, (s + 1) * TM)
        y = _unsort(ybuf, ybase[s], proute_ref[rows, 0:1])
        yield
        x = x_ref[rows, :] + modp[:, 5 * D:6 * D] * y
        h = _rms(x, nm_ref[...]) * (1.0 + mod[:, D:2 * D]) + mod[:, 0:D]
        z = _dot(h, win_ref[...])
        yield
        z = _gelu_tanh(z)
        u = z[:, 0:W_C]
        v = z[:, W_C:2 * W_C]
        mu = jnp.mean(v, axis=-1, keepdims=True)
        vc = v - mu
        var = jnp.mean(vc * vc, axis=-1, keepdims=True)
        vn = (vc * lax.rsqrt(var + EPS) * lng_ref[...] + lnb_ref[...]).astype(bf16)
        for ck in range(TM // C_CHUNK):
            rs = slice(ck * C_CHUNK, (ck + 1) * C_CHUNK)
            for g in range(C_GROUPS):
                cs = slice(g * gw, (g + 1) * gw)
                mixed = jnp.dot(ws_ref[g], vn[rs, cs], preferred_element_type=f32) + bs_ref[g]
                gated[s * TM + ck * C_CHUNK:s * TM + (ck + 1) * C_CHUNK, cs] = (u[rs, cs] * mixed).astype(bf16)
        yield
        out = jnp.dot(gated[rows, :], wout_ref[...], preferred_element_type=f32)
        yield
        x1 = x + mod[:, 2 * D:3 * D] * out
        x1_ref[rows, :] = x1
        yield from _route_chain(s, x1, mod, nrm_ref, wr_ref, br_ref, route_ref, cnt_ref, hbuf, gbuf, counts)

    _mixer_out_tail(i, [chain(s) for s in range(SUB_OUT)], counts, hs_hbm, gs_hbm, hbuf, gbuf, run, sem)


def _odd_mixer(run_n, run_src, x, prev_route, y, mod_prev, mod_l, nm, w_in, ln_g, ln_b, ws, bs, w_out, nrm, wr, br):
    sub = SUB_OUT
    tile = lambda w: pl.BlockSpec((sub * TM, w), lambda i, *_: (i, 0))
    modspec = pl.BlockSpec((1, 1, N_MOD * D), lambda i, *_: (_mod_row_step(i, sub), 0, 0))
    const = lambda shp: pl.BlockSpec(shp, lambda i, *_: tuple(0 for _ in shp))
    grid_spec = pltpu.PrefetchScalarGridSpec(
        num_scalar_prefetch=2,
        grid=(NT // sub,),
        in_specs=[tile(D), tile(128), pl.BlockSpec(memory_space=pl.ANY), modspec, modspec, const((1, D)),
                  const((D, 2 * W_C)), const((1, W_C)), const((1, W_C)), const((C_GROUPS, C_CHUNK, C_CHUNK)),
                  const((C_GROUPS, C_CHUNK, W_C // C_GROUPS)), const((W_C, D))] + _router_in_specs(),
        out_specs=_dispatch_out_specs(sub),
        scratch_shapes=[pltpu.VMEM((sub * TM, W_C), bf16)] + _combine_scratch(sub) + _dispatch_scratch(sub),
    )
    return pl.pallas_call(
        _odd_kernel,
        grid_spec=grid_spec,
        out_shape=_DISPATCH_OUT_SHAPES,
        compiler_params=_cp(("arbitrary",)),
        name="odd_mixer_router",
    )(run_n, run_src, x, prev_route, y, mod_prev, mod_l, nm, w_in, ln_g, ln_b, ws, bs, w_out, nrm, wr, br)


def _expert_kernel(nu_ref, tg_ref, tb_ref, hs_ref, gs_ref, wg_ref, wu_ref, wd_ref, y_ref):
    t = pl.program_id(0)

    @pl.when(t < nu_ref[0])
    def _():
        xb = _load_rows(hs_ref, 0, TMG).astype(bf16)
        gates = gs_ref[...]
        acc = jnp.zeros((TMG, D), f32)
        gate_up = lambda e: (_dot(xb, wg_ref[e]), _dot(xb, wu_ref[e]))
        nxt = gate_up(0)
        for e in range(E_PER_GROUP):
            a, b = nxt
            if e + 1 < E_PER_GROUP:
                nxt = gate_up(e + 1)
            acc = acc + _dot(_silu(a) * b * gates[:, e:e + 1], wd_ref[e])
        _store_rows(y_ref, 0, acc)


def _experts(layer, n_used, tile_g, tile_blk, hs, gs, wg, wu, wd):
    wspec = lambda a, b: pl.BlockSpec((None, None, E_PER_GROUP, a, b), lambda t, nu, tg, tb: (layer, tg[t], 0, 0, 0))
    grid_spec = pltpu.PrefetchScalarGridSpec(
        num_scalar_prefetch=3,
        grid=(MAXT_G,),
        in_specs=[
            pl.BlockSpec((TMG * ROW_SUB, 128), lambda t, nu, tg, tb: (tb[t], 0)),
            pl.BlockSpec((TMG, 128), lambda t, nu, tg, tb: (tb[t], 0)),
            wspec(D, D_EXPERT), wspec(D, D_EXPERT), wspec(D_EXPERT, D),
        ],
        out_specs=pl.BlockSpec((TMG * ROW_SUB, 128), lambda t, nu, tg, tb: (tb[t], 0)),
    )
    return pl.pallas_call(
        _expert_kernel,
        grid_spec=grid_spec,
        out_shape=jax.ShapeDtypeStruct((N_GROUPS * RG * ROW_SUB, 128), f32),
        compiler_params=_cp(("arbitrary",), VMEM_LIMIT_EXPERTS),
        name="experts",
    )(n_used, tile_g, tile_blk, hs, gs, wg, wu, wd)


def _dispatch_plan(cnt_tiles):
    cnt = cnt_tiles[:, 0, 0:N_GROUPS].astype(i32)
    ends = jnp.cumsum(cnt, axis=0)
    run_src = (ends - cnt + jnp.arange(N_GROUPS, dtype=i32)[None, :] * RG).reshape(-1)
    totals = ends[NT - 1]
    tiles_g = (totals + TMG - 1) // TMG
    tile_end = jnp.cumsum(tiles_g)
    n_used = tile_end[N_GROUPS - 1:]
    t = jnp.minimum(jnp.arange(MAXT_G, dtype=i32), n_used[0] - 1)
    tile_g = jnp.sum((tile_end[None, :] <= t[:, None]).astype(i32), axis=1)
    tile_start = tile_end - tiles_g
    first = jnp.sum(jnp.where(tile_g[:, None] == jnp.arange(N_GROUPS, dtype=i32)[None, :], tile_start[None, :], 0), axis=1)
    tile_blk = tile_g * (RG // TMG) + t - first
    return cnt.reshape(-1), run_src, n_used, tile_g, tile_blk


def _final_kernel(n_ref, s_ref, x_ref, route_ref, y_hbm, mod_ref, fn_ref, o_ref, ybuf, ysem, *, n_steps, tile_off):
    i = pl.program_id(0)
    mod = mod_ref[0]
    ybase = _combine_fetch(i, n_steps, SUB, tile_off, n_ref, s_ref, y_hbm, ybuf, ysem)

    def chain(s):
        rows = slice(s * TM, (s + 1) * TM)
        y = _unsort(ybuf, ybase[s], route_ref[rows, 0:1])
        yield
        x = x_ref[rows, :] + mod[:, 5 * D:6 * D] * y
        o_ref[rows, :] = _rms(x, fn_ref[...])

    _lockstep([chain(s) for s in range(SUB)])


def _final(run_n, run_src, x, route, y, mod_l, fnorm, *, n_tiles, tile_off):
    n_steps, step_off = n_tiles // SUB, tile_off // SUB
    grid_spec = pltpu.PrefetchScalarGridSpec(
        num_scalar_prefetch=2,
        grid=(n_steps,),
        in_specs=[
            pl.BlockSpec((SUB * TM, D), lambda i, *_: (step_off + i, 0)),
            pl.BlockSpec((SUB * TM, 128), lambda i, *_: (step_off + i, 0)),
            pl.BlockSpec(memory_space=pl.ANY),
            pl.BlockSpec((1, 1, N_MOD * D), lambda i, *_: (_mod_row_step(step_off + i), 0, 0)),
            pl.BlockSpec((1, D), lambda i, *_: (0, 0)),
        ],
        out_specs=pl.BlockSpec((SUB * TM, D), lambda i, *_: (i, 0)),
        scratch_shapes=_combine_scratch(SUB),
    )
    return pl.pallas_call(
        functools.partial(_final_kernel, n_steps=n_steps, tile_off=tile_off),
        grid_spec=grid_spec,
        out_shape=jax.ShapeDtypeStruct((n_tiles * TM, D), f32),
        compiler_params=_cp(("arbitrary",)),
        name="final_norm",
    )(run_n, run_src, x, route, y, mod_l, fnorm)


def _rope_tables():
    rows = DEC_SEQ // GRID_W
    r, c = jnp.meshgrid(jnp.arange(rows), jnp.arange(GRID_W), indexing="ij")
    pos = jnp.stack([r.reshape(-1), c.reshape(-1)], axis=-1).astype(f32)
    inv = ROPE_BASE ** (-jnp.arange(AXIS_FREQS, dtype=f32) / AXIS_FREQS)
    ang = pos[:, :, None] * inv
    cos, sin = jnp.cos(ang), jnp.sin(ang)
    cos_h = jnp.concatenate([cos, cos], axis=-1).reshape(DEC_SEQ, DH_A)
    sin_h = jnp.concatenate([-sin, sin], axis=-1).reshape(DEC_SEQ, DH_A)
    cos_t = jnp.concatenate([jnp.ones((SUB * TM, 2 * DH_A), f32), jnp.tile(cos_h, (1, 2))], axis=0)
    sin_t = jnp.concatenate([jnp.zeros((SUB * TM, 2 * DH_A), f32), jnp.tile(sin_h, (1, 2))], axis=0)
    return cos_t, sin_t


def _gate_lanes(p):
    out = jnp.zeros((128,), f32)
    for d in range(2):
        out = out.at[d * 2 * H_B:d * 2 * H_B + H_B].set(p[d])
    return out.reshape(1, 128)


def _router_weights(w_rg, b_rg, w_re, b_re):
    wr = jnp.zeros((D, 256), f32).at[:, 0:N_GROUPS].set(w_rg).at[:, 128:128 + N_EXPERTS].set(w_re)
    br = jnp.zeros((1, 256), f32).at[0, 0:N_GROUPS].set(b_rg).at[0, 128:128 + N_EXPERTS].set(b_re)
    return jnp.stack(_hi_lo(wr)), br


def kernel(x_prompt, x_sample, cache_k, cache_v, state_fwd, state_bwd, c, c_ctx, norm_mix, norm_ffn, w_mod, b_mod,
           w_in_even, conv_w, a_log, dt_bias, delta_norm, lam_q, lam_k, subln, w_out_even, w_in_odd, sgu_ln_g,
           sgu_ln_b, w_spatial, b_spatial, w_out_odd, w_router_group, b_router_group, w_router_expert,
           b_router_expert, w_exp_gate, w_exp_up, w_exp_down, final_norm):
    xp = x_prompt.reshape(N_P, D)
    xs = x_sample.reshape(N_S, D)
    cond = jnp.concatenate([c, c_ctx[None, :], jnp.zeros((COND_ROWS - DEC_BATCH - 1, D), f32)], axis=0)
    mod = _modulation(cond, w_mod, b_mod)
    mod0 = mod[0].reshape(COND_ROWS, 1, N_MOD * D)
    mod1 = mod[1].reshape(COND_ROWS, 1, N_MOD * D)

    lam_init = 0.8 - 0.6 * math.exp(-0.3 * 0)
    w_in_pad = jnp.pad(w_in_even[0], ((0, 0), (0, IN_EVEN_PAD - IN_EVEN))).astype(bf16)
    cos_t, sin_t = _rope_tables()
    qa, ka, va, qkvb, gate_b, ab, ka_p, va_p = _even_in(xp, xs, mod0, norm_mix[0:1], w_in_pad, cos_t, sin_t)

    ctx_k = cache_k[:, 0].reshape(DEC_BATCH * PAST * H_A, 2 * DH_A)
    ctx_v = cache_v[:, 0].reshape(DEC_BATCH * PAST * H_A, 2 * DH_A)
    attn = functools.partial(_attention, qa, ka, va, lam_q[0], lam_k[0], subln[0:1], lam_init)
    oa_p = attn(seq=SEQ, tq=SEQ, tok_off=0)
    oa_s = attn(seq=DEC_SEQ, tq=512, tok_off=N_P, ctx=(ctx_k, ctx_v))

    conv_pad = jnp.pad(conv_w[0], ((0, 8 - CONV_K), (0, 0)))
    dl = functools.partial(_delta, qkvb, gate_b, ab, conv_pad, _gate_lanes(a_log[0]), _gate_lanes(dt_bias[0]),
                           delta_norm[0:1])
    ob_p, s_f, s_b = dl(seq=SEQ, tok_off=0)
    ob_s, _, _ = dl(seq=DEC_SEQ, tok_off=N_P, init=(state_fwd[:, 0], state_bwd[:, 0]))

    wr0, br0 = _router_weights(w_router_group[0], b_router_group[0], w_router_expert[0], b_router_expert[0])
    x1, route0, cnt0, hs, gs = _even_out(xp, xs, oa_p, oa_s, ob_p, ob_s, mod0, w_out_even[0].astype(bf16),
                                         norm_ffn[0:1], wr0, br0)

    wshape = (DEPTH, N_GROUPS, E_PER_GROUP)
    wg = w_exp_gate.reshape(wshape + (D, D_EXPERT))
    wu = w_exp_up.reshape(wshape + (D, D_EXPERT))
    wd = w_exp_down.reshape(wshape + (D_EXPERT, D))
    run_n0, run_src0, n_used, tile_g, tile_blk = _dispatch_plan(cnt0)
    y0 = _experts(0, n_used, tile_g, tile_blk, hs, gs, wg, wu, wd)

    bs = jnp.broadcast_to(b_spatial[0][:, :, None], (C_GROUPS, C_CHUNK, W_C // C_GROUPS))
    wr1, br1 = _router_weights(w_router_group[1], b_router_group[1], w_router_expert[1], b_router_expert[1])
    x2, route1, cnt1, hs, gs = _odd_mixer(run_n0, run_src0, x1, route0, y0, mod0, mod1, norm_mix[1:2],
                                          w_in_odd[0].astype(bf16), sgu_ln_g[0:1], sgu_ln_b[0:1],
                                          w_spatial[0].astype(bf16), bs, w_out_odd[0].astype(bf16), norm_ffn[1:2],
                                          wr1, br1)
    run_n1, run_src1, n_used, tile_g, tile_blk = _dispatch_plan(cnt1)
    y1 = _experts(1, n_used, tile_g, tile_blk, hs, gs, wg, wu, wd)

    fin = functools.partial(_final, run_n1, run_src1, x2, route1, y1, mod1, final_norm[None, :])
    y_prompt = fin(n_tiles=NPT, tile_off=0)
    y_sample = fin(n_tiles=NT - NPT, tile_off=NPT)

    new_cache_k = ka_p.reshape(BATCH, 1, SEQ, H_A, 2 * DH_A)
    new_cache_v = va_p.reshape(BATCH, 1, SEQ, H_A, 2 * DH_A)
    return (y_prompt.reshape(BATCH, SEQ, D), y_sample.reshape(DEC_BATCH, DEC_SEQ, D), new_cache_k, new_cache_v,
            s_f[:, None], s_b[:, None])
```

```python
import functools
import math

import jax
import jax.numpy as jnp
from jax import lax
from jax.experimental import pallas as pl
from jax.experimental.pallas import tpu as pltpu

f32 = jnp.float32
bf16 = jnp.bfloat16
i32 = jnp.int32

D = 1024
BATCH, SEQ = 16, 256
DEC_BATCH, DEC_SEQ, PAST = 8, 1024, 512
DEPTH = 2
GRID_W = 64
N_MOD = 6
EPS = 1e-6
H_A, DH_A = 4, 64
W_A = H_A * 2 * DH_A
ROPE_BASE = 10000.0
AXIS_FREQS = DH_A // 4
H_B, DK_B, DV_B = 4, 128, 128
W_B = H_B * DV_B
CONV_K = 5
CH = 64
IN_EVEN = 3 * W_A + 4 * W_B + 4 * H_B
IN_EVEN_PAD = 3712
W_C = D
C_GROUPS, C_CHUNK = 4, 128
N_GROUPS, E_PER_GROUP, D_EXPERT = 4, 8, 256
N_EXPERTS = N_GROUPS * E_PER_GROUP

N_P = BATCH * SEQ
N_S = DEC_BATCH * DEC_SEQ
N_TOK = N_P + N_S
TM = 256
NPT = N_P // TM
NT = N_TOK // TM
COND_ROWS = 16

VMEM_LIMIT = 56 * 1024 * 1024
VMEM_LIMIT_EXPERTS = 62 * 1024 * 1024


def _cp(sem, vmem_limit=VMEM_LIMIT):
    return pltpu.CompilerParams(dimension_semantics=sem, vmem_limit_bytes=vmem_limit)


SUB = 2
NS = NT // SUB
NPS = NPT // SUB
SUB_OUT = 4


def _mod_row_step(i, sub=SUB):
    return jnp.where(i < NPT // sub, DEC_BATCH, (i - NPT // sub) // (DEC_SEQ // (sub * TM)))


def _lockstep(chains):
    live = list(chains)
    while live:
        nxt = []
        for ch in live:
            try:
                next(ch)
                nxt.append(ch)
            except StopIteration:
                pass
        live = nxt


def _rms(x, g):
    return x * lax.rsqrt(jnp.mean(x * x, axis=-1, keepdims=True) + EPS) * g


def _silu(x):
    return x * jax.nn.sigmoid(x)


def _dot(a, b):
    return jnp.dot(a.astype(bf16), b.astype(bf16), preferred_element_type=f32)


ROW_SUB = D // 128


def _store_rows(ref, base, x):
    rows = x.shape[0]
    for j in range(ROW_SUB):
        ref[pl.ds(base + j, rows, stride=ROW_SUB), :] = x[:, j * 128:(j + 1) * 128]


def _load_rows(ref, base, rows):
    return jnp.concatenate([ref[pl.ds(base + j, rows, stride=ROW_SUB), :] for j in range(ROW_SUB)], axis=1)


TN_MOD = 1536


def _mod_kernel(c_ref, w_ref, b_ref, o_ref):
    a = _silu(c_ref[...])
    o_ref[0] = _dot(a, w_ref[0]) + b_ref[0]


def _modulation(cond, w_mod, b_mod):
    return pl.pallas_call(
        _mod_kernel,
        grid=(DEPTH, N_MOD * D // TN_MOD),
        in_specs=[
            pl.BlockSpec((COND_ROWS, D), lambda l, j: (0, 0)),
            pl.BlockSpec((1, D, TN_MOD), lambda l, j: (l, 0, j)),
            pl.BlockSpec((1, 1, TN_MOD), lambda l, j: (l, 0, j)),
        ],
        out_specs=pl.BlockSpec((1, COND_ROWS, TN_MOD), lambda l, j: (l, 0, j)),
        out_shape=jax.ShapeDtypeStruct((DEPTH, COND_ROWS, N_MOD * D), f32),
        compiler_params=_cp(("arbitrary", "arbitrary")),
        name="modulation",
    )(cond, w_mod, b_mod.reshape(DEPTH, 1, N_MOD * D))


Q_SCALE = DH_A ** -0.5 * math.log2(math.e)


def _even_in_kernel(xp_ref, xs_ref, mod_ref, nrm_ref, w_ref, cos_ref, sin_ref,
                    q_ref, k_ref, v_ref, qkvb_ref, gate_ref, ab_ref, kp_ref, vp_ref):
    i = pl.program_id(0)
    is_p = i < NPS
    mod = mod_ref[0]
    lane = lax.broadcasted_iota(i32, (TM, 2 * DH_A), 1)
    first = (lane % (2 * AXIS_FREQS)) < AXIS_FREQS

    def chain(s):
        rows = slice(s * TM, (s + 1) * TM)
        x = jnp.where(is_p, xp_ref[rows, :], xs_ref[rows, :])
        h = _rms(x, nrm_ref[...]) * (1.0 + mod[:, D:2 * D]) + mod[:, 0:D]
        proj = _dot(h, w_ref[...])
        yield
        cos = cos_ref[rows, :]
        sin = sin_ref[rows, :]

        def rope(xh):
            partner = jnp.where(first, pltpu.roll(xh, 2 * DH_A - AXIS_FREQS, 1), pltpu.roll(xh, AXIS_FREQS, 1))
            return xh * cos + partner * sin

        for hh in range(H_A):
            sl = slice(hh * 2 * DH_A, (hh + 1) * 2 * DH_A)
            q_ref[rows, sl] = (rope(proj[:, sl]) * Q_SCALE).astype(bf16)
            k_ref[rows, sl] = rope(proj[:, W_A + hh * 2 * DH_A:W_A + (hh + 1) * 2 * DH_A])
        v_ref[rows, :] = proj[:, 2 * W_A:3 * W_A]
        qkvb_ref[rows, :] = proj[:, 3 * W_A:3 * W_A + 3 * W_B]
        gate_ref[rows, :] = proj[:, 3 * W_A + 3 * W_B:3 * W_A + 4 * W_B]
        ab_ref[rows, :] = proj[:, 3 * W_A + 4 * W_B:IN_EVEN_PAD]

    _lockstep([chain(s) for s in range(SUB)])

    @pl.when(is_p)
    def _():
        for hh in range(H_A):
            sl = slice(hh * 2 * DH_A, (hh + 1) * 2 * DH_A)
            kp_ref[pl.ds(hh, SUB * TM, stride=H_A), :] = k_ref[:, sl]
            vp_ref[pl.ds(hh, SUB * TM, stride=H_A), :] = v_ref[:, sl]


def _even_in(xp, xs, mod_l, nrm, w_in_pad, cos_t, sin_t):
    tile = lambda w: pl.BlockSpec((SUB * TM, w), lambda i: (i, 0))
    ptile = lambda w: pl.BlockSpec((SUB * TM, w), lambda i: (jnp.minimum(i, NPS - 1), 0))
    rope_idx = lambda i: (jnp.where(i < NPS, 0, 1 + (i - NPS) % (DEC_SEQ // (SUB * TM))), 0)
    return pl.pallas_call(
        _even_in_kernel,
        grid=(NS,),
        in_specs=[
            ptile(D),
            pl.BlockSpec((SUB * TM, D), lambda i: (jnp.maximum(i - NPS, 0), 0)),
            pl.BlockSpec((1, 1, N_MOD * D), lambda i: (_mod_row_step(i), 0, 0)),
            pl.BlockSpec((1, D), lambda i: (0, 0)),
            pl.BlockSpec((D, IN_EVEN_PAD), lambda i: (0, 0)),
            pl.BlockSpec((SUB * TM, 2 * DH_A), rope_idx),
            pl.BlockSpec((SUB * TM, 2 * DH_A), rope_idx),
        ],
        out_specs=[tile(W_A), tile(W_A), tile(W_A), tile(3 * W_B), tile(W_B), tile(128)]
        + [pl.BlockSpec((SUB * TM * H_A, 2 * DH_A), lambda i: (jnp.minimum(i, NPS - 1), 0))] * 2,
        out_shape=[
            jax.ShapeDtypeStruct((N_TOK, W_A), bf16),
            jax.ShapeDtypeStruct((N_TOK, W_A), f32),
            jax.ShapeDtypeStruct((N_TOK, W_A), f32),
            jax.ShapeDtypeStruct((N_TOK, 3 * W_B), f32),
            jax.ShapeDtypeStruct((N_TOK, W_B), f32),
            jax.ShapeDtypeStruct((N_TOK, 128), f32),
            jax.ShapeDtypeStruct((N_P * H_A, 2 * DH_A), f32),
            jax.ShapeDtypeStruct((N_P * H_A, 2 * DH_A), f32),
        ],
        compiler_params=_cp(("arbitrary",)),
        name="even_in",
    )(xp, xs, mod_l, nrm, w_in_pad, cos_t, sin_t)


ATTN_HEADS_IN_LOCKSTEP = 2


def _attn_kernel(*refs, lam_init, has_ctx):
    if has_ctx:
        q_ref, kn_ref, vn_ref, kc_ref, vc_ref, lq_ref, lk_ref, sub_ref, o_ref = refs
    else:
        q_ref, kn_ref, vn_ref, lq_ref, lk_ref, sub_ref, o_ref = refs
    tq = q_ref.shape[0]
    lq = lq_ref[...]
    lk = lk_ref[...]
    prod = lq * lk
    lam = (jnp.exp(jnp.sum(prod[0:1], axis=-1, keepdims=True))
           - jnp.exp(jnp.sum(prod[1:2], axis=-1, keepdims=True)) + lam_init)
    lane = lax.broadcasted_iota(i32, (tq, 2 * DH_A), 1)
    nt = (((1,), (1,)), ((), ()))
    tn = (((0,), (0,)), ((), ()))
    sub_col = jnp.broadcast_to(sub_ref[...], (8, 2 * DH_A)).T[:, 0:1]
    outs = {}

    def softmax_part(hh, part, srcs):
        q = q_ref[:, hh * 2 * DH_A:(hh + 1) * 2 * DH_A]
        qm = jnp.where((lane < DH_A) if part == 0 else (lane >= DH_A), q, jnp.zeros_like(q))
        ss = [lax.dot_general(k, qm, nt, preferred_element_type=f32) for k, _ in srcs]
        yield
        m = ss[0].max(axis=0, keepdims=True)
        for s in ss[1:]:
            m = jnp.maximum(m, s.max(axis=0, keepdims=True))
        l = jnp.zeros((1, tq), f32)
        o = jnp.zeros((2 * DH_A, tq), f32)
        for s, (_, v) in zip(ss, srcs):
            e = jnp.exp2(s - m)
            l = l + jnp.sum(e, axis=0, keepdims=True)
            o = o + lax.dot_general(v, e.astype(bf16), tn, preferred_element_type=f32)
        yield
        outs[(hh, part)] = o / l

    for h0 in range(0, H_A, ATTN_HEADS_IN_LOCKSTEP):
        chains = []
        for hh in range(h0, h0 + ATTN_HEADS_IN_LOCKSTEP):
            sl = slice(hh * 2 * DH_A, (hh + 1) * 2 * DH_A)
            srcs = [(kn_ref[:, sl].astype(bf16), vn_ref[:, sl].astype(bf16))]
            if has_ctx:
                srcs.append((kc_ref[pl.ds(hh, PAST, stride=H_A), :].astype(bf16),
                             vc_ref[pl.ds(hh, PAST, stride=H_A), :].astype(bf16)))
            chains += [softmax_part(hh, part, srcs) for part in range(2)]
        _lockstep(chains)
        for hh in range(h0, h0 + ATTN_HEADS_IN_LOCKSTEP):
            o = outs[(hh, 0)] - lam * outs[(hh, 1)]
            o = o * lax.rsqrt(jnp.mean(o * o, axis=0, keepdims=True) + EPS) * sub_col * (1.0 - lam_init)
            o_ref[:, hh * 2 * DH_A:(hh + 1) * 2 * DH_A] = o.T.astype(bf16)


def _attention(q, ka, va, lam_q, lam_k, subln, lam_init, *, seq, tq, tok_off, ctx=None):
    nq = seq // tq
    nb = (N_P if ctx is None else N_S) // seq
    qo, ko = tok_off // tq, tok_off // seq
    in_specs = [
        pl.BlockSpec((tq, W_A), lambda b, j: (qo + b * nq + j, 0)),
        pl.BlockSpec((seq, W_A), lambda b, j: (ko + b, 0)),
        pl.BlockSpec((seq, W_A), lambda b, j: (ko + b, 0)),
    ]
    args = [q, ka, va]
    if ctx is not None:
        in_specs += [pl.BlockSpec((PAST * H_A, 2 * DH_A), lambda b, j: (b, 0))] * 2
        args += list(ctx)
    in_specs += [pl.BlockSpec((2, DH_A), lambda b, j: (0, 0))] * 2 + [pl.BlockSpec((1, 2 * DH_A), lambda b, j: (0, 0))]
    args += [lam_q, lam_k, subln]
    return pl.pallas_call(
        functools.partial(_attn_kernel, lam_init=lam_init, has_ctx=ctx is not None),
        grid=(nb, nq),
        in_specs=in_specs,
        out_specs=pl.BlockSpec((tq, W_A), lambda b, j: (b * nq + j, 0)),
        out_shape=jax.ShapeDtypeStruct((nb * seq, W_A), bf16),
        compiler_params=_cp(("arbitrary", "arbitrary")),
        name="diff_attn_ctx" if ctx is not None else "diff_attn",
    )(*args)


def _softplus(x):
    return jnp.maximum(x, 0.0) + jnp.log1p(jnp.exp(-jnp.abs(x)))


RW = H_B * CH
CPI = 4


def _rep_rows(x, times):
    return jnp.concatenate([x] * times, axis=0)


def _tri_inv(lmats, bd, ii, jj):
    b16 = (ii // 16) == (jj // 16)
    b32 = (ii // 32) == (jj // 32)
    eye = (ii == jj).astype(f32)
    n = len(lmats)
    mm = lambda a, b: jnp.dot(a.astype(bf16), bd(b), preferred_element_type=f32)
    l0 = [jnp.where(b16, l, 0.0) for l in lmats]
    x = [eye - a for a in l0]
    p = [mm(a, a) for a in l0]
    for level in range(3):
        xp = [mm(x[i], p[i]) for i in range(n)]
        if level < 2:
            p = [mm(a, a) for a in p]
        x = [x[i] + xp[i] for i in range(n)]
    for off_diag in (jnp.logical_and(b32, jnp.logical_not(b16)), jnp.logical_not(b32)):
        t = [mm(jnp.where(off_diag, lmats[i], 0.0), x[i]) for i in range(n)]
        t = [mm(x[i], t[i]) for i in range(n)]
        x = [x[i] - t[i] for i in range(n)]
    return x


def _delta_kernel(*refs, seq, has_init):
    if has_init:
        (qkv_ref, gate_ref, ab_ref, cw_ref, al_ref, dt_ref, dn_ref, sf0_ref, sb0_ref,
         ob_ref, sf_ref, sb_ref, xpad, xq, xk, xv, gacc, bacc, o_f, o_b, st, pu, pw, pqe, pkd, patt, ptot) = refs
    else:
        (qkv_ref, gate_ref, ab_ref, cw_ref, al_ref, dt_ref, dn_ref,
         ob_ref, sf_ref, sb_ref, xpad, xq, xk, xv, gacc, bacc, o_f, o_b, st, pu, pw, pqe, pkd, patt, ptot) = refs
    n = seq // CH
    win = CH + 16

    xpad[0:8, :] = jnp.zeros((8, 3 * W_B), f32)
    xpad[seq + 8:seq + 16, :] = jnp.zeros((8, 3 * W_B), f32)
    xpad[8:seq + 8, :] = qkv_ref[...]
    neg_a = -jnp.exp(al_ref[...])
    dtb = dt_ref[...]

    def pre(c, carry):
        r0 = pl.multiple_of(c * CH, CH)
        a = xpad[pl.ds(r0, win), :]
        y = jnp.zeros((CH, 3 * W_B), f32)
        for j in range(CONV_K):
            sh = ((CONV_K - 1) // 2 - j) % win
            y = y + pltpu.roll(a, sh, 0)[8:8 + CH] * cw_ref[j:j + 1, :]
        y = _silu(y)
        for hh in range(H_B):
            qh = y[:, hh * DK_B:(hh + 1) * DK_B]
            kh = y[:, W_B + hh * DK_B:W_B + (hh + 1) * DK_B]
            qn = qh * lax.rsqrt(jnp.sum(qh * qh, axis=-1, keepdims=True) + EPS) * (DK_B ** -0.5)
            kn = kh * lax.rsqrt(jnp.sum(kh * kh, axis=-1, keepdims=True) + EPS)
            xq[pl.ds(r0, CH), hh * DK_B:(hh + 1) * DK_B] = qn
            xk[pl.ds(r0, CH), hh * DK_B:(hh + 1) * DK_B] = kn
        xv[pl.ds(r0, CH), :] = y[:, 2 * W_B:3 * W_B]
        ab = ab_ref[pl.ds(r0, CH), :]
        gacc[pl.ds(r0, CH), :] = neg_a * _softplus(ab + dtb)
        bacc[pl.ds(r0, CH), :] = jax.nn.sigmoid(ab)
        return carry

    lax.fori_loop(0, n, pre, 0)

    ii = lax.broadcasted_iota(i32, (CH, RW), 0)
    lane_r = lax.broadcasted_iota(i32, (CH, RW), 1)
    jj = lane_r % CH
    hb = lane_r // CH
    rowl = lax.broadcasted_iota(i32, (CH, 128), 0)
    bd_mask = (lax.broadcasted_iota(i32, (RW, RW), 0) // CH) == (lax.broadcasted_iota(i32, (RW, RW), 1) // CH)
    wide_mask = ((lax.broadcasted_iota(i32, (RW, W_B), 0) // CH)
                 == (lax.broadcasted_iota(i32, (RW, W_B), 1) // DK_B))
    pair_mask = ((lax.broadcasted_iota(i32, (2 * DK_B, 2 * DV_B), 0) // DK_B)
                 == (lax.broadcasted_iota(i32, (2 * DK_B, 2 * DV_B), 1) // DV_B))
    zero_b = jnp.zeros((), bf16)

    def bd(x):
        return jnp.where(bd_mask, _rep_rows(x.astype(bf16), H_B), zero_b)

    def bd_wide(x):
        return jnp.where(wide_mask, _rep_rows(x.astype(bf16), H_B), zero_b)

    def per_head(src, lanes):
        return jnp.concatenate([jnp.broadcast_to(src[:, ln:ln + 1], (src.shape[0], DK_B)) for ln in lanes], axis=1)

    nt = (((1,), (1,)), ((), ()))
    tn = (((0,), (0,)), ((), ()))

    def local(it, carry):
        chains = []
        for j in range(CPI):
            c = it * CPI + j
            r0 = pl.multiple_of(c * CH, CH)
            g = gacc[pl.ds(r0, CH), :]
            beta = bacc[pl.ds(r0, CH), :]
            pre_g = g
            for s in (1, 2, 4, 8, 16, 32):
                pre_g = pre_g + jnp.where(rowl >= s, pltpu.roll(pre_g, s, 0), 0.0)
            tot = pre_g[CH - 1:CH, :]
            q_all = xq[pl.ds(r0, CH), :]
            k_all = xk[pl.ds(r0, CH), :]
            v_all = xv[pl.ds(r0, CH), :]
            k_bd = bd_wide(k_all)
            for d in range(2):
                gc = pre_g if d == 0 else tot - pre_g + g
                lanes = [d * 2 * H_B + hh for hh in range(H_B)]
                gct = _rep_rows(gc, H_B).T
                g_col = jnp.zeros((CH, RW), f32)
                g_row = jnp.zeros((1, RW), f32)
                for hh, ln in enumerate(lanes):
                    g_col = jnp.where(hb == hh, gc[:, ln:ln + 1], g_col)
                    g_row = jnp.where(hb[0:1] == hh, gct[ln:ln + 1, :], g_row)
                g_wide = per_head(gc, lanes)
                b_wide = per_head(beta, [ln + H_B for ln in lanes])
                incl = (ii >= jj) if d == 0 else (ii <= jj)
                strict = (ii > jj) if d == 0 else (ii < jj)
                eg = jnp.exp(g_wide)
                kb = k_all * b_wide
                chains.append(dict(
                    ci=c * 2 + d, incl=incl, strict=strict, k_bd=k_bd, kb=kb, q=q_all, tot=tot,
                    dec=jnp.where(incl, jnp.exp(jnp.where(incl, g_col - g_row, 0.0)), 0.0),
                    vb=v_all * b_wide, kbe=kb * eg, qe=q_all * eg,
                    kd=k_all * jnp.exp(per_head(tot, lanes) - g_wide)))
        kk = [lax.dot_general(e["kb"].astype(bf16), e["k_bd"], nt, preferred_element_type=f32) for e in chains]
        qk = [lax.dot_general(e["q"].astype(bf16), e["k_bd"], nt, preferred_element_type=f32) for e in chains]
        lmat = [jnp.where(e["strict"], kk[i] * e["dec"], 0.0) for i, e in enumerate(chains)]
        tinv = _tri_inv(lmat, bd, ii, jj)
        u = [jnp.dot(tinv[i].astype(bf16), bd_wide(e["vb"]), preferred_element_type=f32) for i, e in enumerate(chains)]
        w = [jnp.dot(tinv[i].astype(bf16), bd_wide(e["kbe"]), preferred_element_type=f32) for i, e in enumerate(chains)]
        for i, e in enumerate(chains):
            ci = e["ci"]
            pu[ci] = u[i]
            pw[ci] = w[i].astype(bf16)
            pqe[ci] = e["qe"].astype(bf16)
            pkd[ci] = e["kd"].astype(bf16)
            patt[ci] = jnp.where(e["incl"], qk[i] * e["dec"], 0.0).astype(bf16)
            ptot[ci] = jnp.broadcast_to(e["tot"], (8, 128))
        return carry

    lax.fori_loop(0, n // CPI, local, 0)

    for hh in range(H_B):
        rows = slice(hh * DK_B, (hh + 1) * DK_B)
        if has_init:
            st[0, rows, :] = sf0_ref[0, hh]
            st[1, rows, :] = sb0_ref[0, hh]
        else:
            st[0, rows, :] = jnp.zeros((DK_B, DV_B), f32)
            st[1, rows, :] = jnp.zeros((DK_B, DV_B), f32)

    def scan(it, carry):
        chains = []
        for d in range(2):
            c = it if d == 0 else n - 1 - it
            ci = c * 2 + d
            s_old = st[d]
            s_b = s_old.astype(bf16)
            chains.append(dict(
                d=d, r0=pl.multiple_of(c * CH, CH), s_old=s_old, u=pu[ci], att=patt[ci], kd=pkd[ci], tot=ptot[ci],
                lhs=jnp.concatenate([pw[ci], pqe[ci]], axis=0),
                rhs=[jnp.where(pair_mask, jnp.concatenate([s_b[p * 2 * DK_B:(p + 1) * 2 * DK_B]] * 2, axis=1), zero_b)
                     for p in range(2)]))
        ws_qs = [jnp.concatenate(
            [jnp.dot(e["lhs"][:, p * 2 * DK_B:(p + 1) * 2 * DK_B], e["rhs"][p], preferred_element_type=f32)
             for p in range(2)], axis=1) for e in chains]
        v_new = [e["u"] - ws_qs[i][0:CH] for i, e in enumerate(chains)]
        av = [jnp.dot(e["att"], bd_wide(v_new[i]), preferred_element_type=f32) for i, e in enumerate(chains)]
        su = [lax.dot_general(jnp.where(wide_mask, _rep_rows(e["kd"], H_B), zero_b),
                              jnp.concatenate([v_new[i][:, hh * DV_B:(hh + 1) * DV_B] for hh in range(H_B)],
                                              axis=0).astype(bf16),
                              tn, preferred_element_type=f32) for i, e in enumerate(chains)]
        for i, e in enumerate(chains):
            d = e["d"]
            if d == 0:
                o_f[pl.ds(e["r0"], CH), :] = ws_qs[i][CH:2 * CH] + av[i]
            else:
                o_b[pl.ds(e["r0"], CH), :] = ws_qs[i][CH:2 * CH] + av[i]
            for hh in range(H_B):
                rows = slice(hh * DK_B, (hh + 1) * DK_B)
                ln = d * 2 * H_B + hh
                st[d, rows, :] = e["s_old"][rows] * jnp.exp(e["tot"][0:1, ln:ln + 1]) + su[i][rows]
        return carry

    lax.fori_loop(0, n, scan, 0)

    def post(c, carry):
        r0 = pl.multiple_of(c * CH, CH)
        o = o_f[pl.ds(r0, CH), :] + o_b[pl.ds(r0, CH), :]
        gt = gate_ref[pl.ds(r0, CH), :]
        for hh in range(H_B):
            sl = slice(hh * DV_B, (hh + 1) * DV_B)
            ob_ref[pl.ds(r0, CH), sl] = (_rms(o[:, sl], dn_ref[...]) * _silu(gt[:, sl])).astype(bf16)
        return carry

    lax.fori_loop(0, n, post, 0)
    for hh in range(H_B):
        sf_ref[0, hh] = st[0, hh * DK_B:(hh + 1) * DK_B, :]
        sb_ref[0, hh] = st[1, hh * DK_B:(hh + 1) * DK_B, :]


def _delta(qkvb, gate, ab, conv_pad, al_lane, dt_lane, dnorm, *, seq, tok_off, init=None):
    nb = (N_P if init is None else N_S) // seq
    bo = tok_off // seq
    rows = lambda w: pl.BlockSpec((seq, w), lambda b: (bo + b, 0))
    const = lambda shp: pl.BlockSpec(shp, lambda b: tuple(0 for _ in shp))
    st_spec = pl.BlockSpec((1, H_B, DK_B, DV_B), lambda b: (b, 0, 0, 0))
    in_specs = [rows(3 * W_B), rows(W_B), rows(128), const((8, 3 * W_B)), const((1, 128)), const((1, 128)),
                const((1, DV_B))]
    args = [qkvb, gate, ab, conv_pad, al_lane, dt_lane, dnorm]
    if init is not None:
        in_specs += [st_spec, st_spec]
        args += [init[0], init[1]]
    return pl.pallas_call(
        functools.partial(_delta_kernel, seq=seq, has_init=init is not None),
        grid=(nb,),
        in_specs=in_specs,
        out_specs=[pl.BlockSpec((seq, W_B), lambda b: (b, 0)), st_spec, st_spec],
        out_shape=[
            jax.ShapeDtypeStruct((nb * seq, W_B), bf16),
            jax.ShapeDtypeStruct((nb, H_B, DK_B, DV_B), f32),
            jax.ShapeDtypeStruct((nb, H_B, DK_B, DV_B), f32),
        ],
        scratch_shapes=[
            pltpu.VMEM((seq + 16, 3 * W_B), f32),
            pltpu.VMEM((seq, W_B), f32),
            pltpu.VMEM((seq, W_B), f32),
            pltpu.VMEM((seq, W_B), f32),
            pltpu.VMEM((seq, 128), f32),
            pltpu.VMEM((seq, 128), f32),
            pltpu.VMEM((seq, W_B), f32),
            pltpu.VMEM((seq, W_B), f32),
            pltpu.VMEM((2, H_B * DK_B, DV_B), f32),
            pltpu.VMEM((2 * seq // CH, CH, W_B), f32),
            pltpu.VMEM((2 * seq // CH, CH, W_B), bf16),
            pltpu.VMEM((2 * seq // CH, CH, W_B), bf16),
            pltpu.VMEM((2 * seq // CH, CH, W_B), bf16),
            pltpu.VMEM((2 * seq // CH, CH, RW), bf16),
            pltpu.VMEM((2 * seq // CH, 8, 128), f32),
        ],
        compiler_params=_cp(("arbitrary",)),
        name="delta_ctx" if init is not None else "delta",
    )(*args)


RG = N_TOK
TMG = 256
MAXT_G = N_TOK // TMG + N_GROUPS
TILE_ROWS = TM * ROW_SUB
RUN_BITS = tuple(range(TM.bit_length() - 1, -1, -1))


def _hi_lo(x):
    hi = x.astype(bf16)
    return hi, (x - hi.astype(f32)).astype(bf16)


def _run_pieces(n):
    return [(1 << b, (n >> (b + 1)) << (b + 1), ((n >> b) & 1) == 1) for b in RUN_BITS]


def _route_chain(s, x1, mod, nrm_ref, wr_ref, br_ref, route_ref, cnt_ref, hbuf, gbuf, counts):
    rows = slice(s * TM, (s + 1) * TM)
    h2 = _rms(x1, nrm_ref[...]) * (1.0 + mod[:, 4 * D:5 * D]) + mod[:, 3 * D:4 * D]
    h_hi, h_lo = _hi_lo(h2)
    w_hi, w_lo = wr_ref[0], wr_ref[1]
    logits = (jnp.dot(h_hi, w_hi, preferred_element_type=f32) + jnp.dot(h_hi, w_lo, preferred_element_type=f32)
              + jnp.dot(h_lo, w_hi, preferred_element_type=f32)) + br_ref[...]
    yield
    lane = lax.broadcasted_iota(i32, (TM, 128), 1)
    lane_f = lane.astype(f32)
    neg = jnp.float32(-1e30)
    lg = jnp.where(lane < N_GROUPS, logits[:, 0:128], neg)
    mg = lg.max(axis=-1, keepdims=True)
    gidx = jnp.min(jnp.where(lg == mg, lane_f, 128.0), axis=-1, keepdims=True)
    pg = 1.0 / jnp.sum(jnp.exp(lg - mg), axis=-1, keepdims=True)
    le = jnp.where((lane // E_PER_GROUP).astype(f32) == gidx, logits[:, 128:256], neg)
    m1 = le.max(axis=-1, keepdims=True)
    i1 = jnp.min(jnp.where(le == m1, lane_f, 128.0), axis=-1, keepdims=True)
    le2 = jnp.where(lane_f == i1, neg, le)
    m2 = le2.max(axis=-1, keepdims=True)
    i2 = jnp.min(jnp.where(le2 == m2, lane_f, 128.0), axis=-1, keepdims=True)
    e2 = jnp.exp(m2 - m1)
    w1 = pg / (1.0 + e2)
    w2 = w1 * e2
    local = lane_f + gidx * E_PER_GROUP
    gates = jnp.where(local == i1, w1, jnp.where(local == i2, w2, 0.0))

    onehot = (lane_f == gidx).astype(f32)
    r = lax.broadcasted_iota(i32, (TM, TM), 0)
    c = lax.broadcasted_iota(i32, (TM, TM), 1)
    before = jnp.dot((r > c).astype(bf16), onehot.astype(bf16), preferred_element_type=f32)
    yield
    cnt = jnp.sum(onehot, axis=0, keepdims=True)
    pos = (jnp.sum(onehot * before, axis=-1, keepdims=True)
           + jnp.sum(jnp.where(lane_f < gidx, cnt, 0.0), axis=-1, keepdims=True))
    route_ref[rows, :] = jnp.where(lane == 0, pos, 0.0)
    cnt_ref[s] = cnt
    counts[s] = cnt.astype(i32)

    perm_t = (c.astype(f32) == pos).astype(bf16)
    tn = (((0,), (0,)), ((), ()))
    h_sorted = lax.dot_general(perm_t, h_hi, tn, preferred_element_type=f32)
    g_hi, g_lo = _hi_lo(gates)
    g_sorted = (lax.dot_general(perm_t, g_hi, tn, preferred_element_type=f32)
                + lax.dot_general(perm_t, g_lo, tn, preferred_element_type=f32))
    yield
    _store_rows(hbuf, s * TILE_ROWS, h_sorted)
    gbuf[s * TM:(s + 1) * TM, :] = g_sorted


def _dispatch_done(s, hs_hbm, gs_hbm, hbuf, gbuf, sem):
    pltpu.make_async_copy(hbuf.at[pl.ds(s * TILE_ROWS, TILE_ROWS), :], hs_hbm.at[pl.ds(0, TILE_ROWS), :],
                          sem.at[s]).wait()
    pltpu.make_async_copy(gbuf.at[pl.ds(s * TM, TM), :], gs_hbm.at[pl.ds(0, TM), :], sem.at[s]).wait()


def _dispatch_runs(s, cnt_i, hs_hbm, gs_hbm, hbuf, gbuf, run, sem):
    off = 0
    for g in range(N_GROUPS):
        n = cnt_i[0, g]
        dst = g * RG + run[g]
        for rows, o, pred in _run_pieces(n):
            @pl.when(pred)
            def _():
                pltpu.make_async_copy(
                    hbuf.at[pl.ds(pl.multiple_of(s * TILE_ROWS + (off + o) * ROW_SUB, ROW_SUB), rows * ROW_SUB), :],
                    hs_hbm.at[pl.ds(pl.multiple_of((dst + o) * ROW_SUB, ROW_SUB), rows * ROW_SUB), :],
                    sem.at[s]).start()
                pltpu.make_async_copy(gbuf.at[pl.ds(s * TM + off + o, rows), :], gs_hbm.at[pl.ds(dst + o, rows), :],
                                      sem.at[s]).start()
        run[g] = run[g] + n
        off = off + n


def _dispatch_tail_fill(hs_hbm, gs_hbm, hbuf, gbuf, run, sem):
    hbuf[pl.ds(0, TMG * ROW_SUB), :] = jnp.zeros((TMG * ROW_SUB, 128), f32)
    gbuf[pl.ds(0, TMG), :] = jnp.zeros((TMG, 128), f32)
    for g in range(N_GROUPS):
        tot = run[g]
        pad = (-tot) & (TMG - 1)
        for rows, o, pred in _run_pieces(pad):
            @pl.when(pred)
            def _():
                dst = g * RG + tot + o
                ch = pltpu.make_async_copy(
                    hbuf.at[pl.ds(0, rows * ROW_SUB), :],
                    hs_hbm.at[pl.ds(pl.multiple_of(dst * ROW_SUB, ROW_SUB), rows * ROW_SUB), :], sem.at[0])
                cg = pltpu.make_async_copy(gbuf.at[pl.ds(0, rows), :], gs_hbm.at[pl.ds(dst, rows), :], sem.at[1])
                ch.start()
                cg.start()
                ch.wait()
                cg.wait()


def _mixer_out_tail(i, chains, counts, hs_hbm, gs_hbm, hbuf, gbuf, run, sem):
    @pl.when(i == 0)
    def _():
        for g in range(N_GROUPS):
            run[g] = 0

    sub = len(chains)

    @pl.when(i >= 1)
    def _():
        for s in range(sub):
            _dispatch_done(s, hs_hbm, gs_hbm, hbuf, gbuf, sem)

    _lockstep(chains)
    for s in range(sub):
        _dispatch_runs(s, counts[s], hs_hbm, gs_hbm, hbuf, gbuf, run, sem)

    @pl.when(i == pl.num_programs(0) - 1)
    def _():
        for s in range(sub):
            _dispatch_done(s, hs_hbm, gs_hbm, hbuf, gbuf, sem)
        _dispatch_tail_fill(hs_hbm, gs_hbm, hbuf, gbuf, run, sem)


def _dispatch_out_specs(sub=SUB):
    return [
        pl.BlockSpec((sub * TM, D), lambda i, *_: (i, 0)),
        pl.BlockSpec((sub * TM, 128), lambda i, *_: (i, 0)),
        pl.BlockSpec((sub, 1, 128), lambda i, *_: (i, 0, 0)),
        pl.BlockSpec(memory_space=pl.ANY),
        pl.BlockSpec(memory_space=pl.ANY),
    ]


_DISPATCH_OUT_SHAPES = [
    jax.ShapeDtypeStruct((N_TOK, D), f32),
    jax.ShapeDtypeStruct((N_TOK, 128), f32),
    jax.ShapeDtypeStruct((NT, 1, 128), f32),
    jax.ShapeDtypeStruct((N_GROUPS * RG * ROW_SUB, 128), f32),
    jax.ShapeDtypeStruct((N_GROUPS * RG, 128), f32),
]


def _dispatch_scratch(sub=SUB):
    return [
        pltpu.VMEM((sub * TILE_ROWS, 128), f32),
        pltpu.VMEM((sub * TM, 128), f32),
        pltpu.SMEM((N_GROUPS,), i32),
        pltpu.SemaphoreType.DMA((sub,)),
    ]


def _router_in_specs():
    return [pl.BlockSpec((1, D), lambda i, *_: (0, 0)),
            pl.BlockSpec((2, D, 256), lambda i, *_: (0, 0, 0)),
            pl.BlockSpec((1, 256), lambda i, *_: (0, 0))]


def _fetch_runs(n_ref, s_ref, y_hbm, ybuf, sem, tile, slot):
    base = slot * TILE_ROWS
    off = 0
    for g in range(N_GROUPS):
        n = n_ref[tile * N_GROUPS + g]
        src = s_ref[tile * N_GROUPS + g]
        for rows, o, pred in _run_pieces(n):
            @pl.when(pred)
            def _():
                pltpu.make_async_copy(
                    y_hbm.at[pl.ds(pl.multiple_of((src + o) * ROW_SUB, ROW_SUB), rows * ROW_SUB), :],
                    ybuf.at[pl.ds(pl.multiple_of(base + (off + o) * ROW_SUB, ROW_SUB), rows * ROW_SUB), :],
                    sem.at[slot]).start()
        off = off + n


def _combine_fetch(i, n_steps, sub, tile0, n_ref, s_ref, y_hbm, ybuf, sem):
    half = (i % 2) * sub

    @pl.when(i == 0)
    def _():
        for s in range(sub):
            _fetch_runs(n_ref, s_ref, y_hbm, ybuf, sem, tile0 + s, s)

    @pl.when(i + 1 < n_steps)
    def _():
        for s in range(sub):
            _fetch_runs(n_ref, s_ref, y_hbm, ybuf, sem, tile0 + (i + 1) * sub + s, sub - half + s)

    bases = []
    for s in range(sub):
        base = pl.multiple_of((half + s) * TILE_ROWS, TILE_ROWS)
        pltpu.make_async_copy(y_hbm.at[pl.ds(0, TILE_ROWS), :], ybuf.at[pl.ds(base, TILE_ROWS), :],
                              sem.at[half + s]).wait()
        bases.append(base)
    return bases


def _unsort(ybuf, base, pos):
    y_hi, y_lo = _hi_lo(_load_rows(ybuf, base, TM))
    perm_t = (lax.broadcasted_iota(i32, (TM, TM), 1).astype(f32) == pos).astype(bf16)
    return jnp.dot(perm_t, y_hi, preferred_element_type=f32) + jnp.dot(perm_t, y_lo, preferred_element_type=f32)


def _combine_scratch(sub):
    return [pltpu.VMEM((2 * sub * TILE_ROWS, 128), f32), pltpu.SemaphoreType.DMA((2 * sub,))]


def _even_out_kernel(xp_ref, xs_ref, oap_ref, oas_ref, obp_ref, obs_ref, mod_ref, w_ref, nrm_ref, wr_ref, br_ref,
                     x1_ref, route_ref, cnt_ref, hs_hbm, gs_hbm, hbuf, gbuf, run, sem):
    i = pl.program_id(0)
    is_p = i < NPT // SUB_OUT
    mod = mod_ref[0]
    counts = [None] * SUB_OUT

    def chain(s):
        rows = slice(s * TM, (s + 1) * TM)
        x = jnp.where(is_p, xp_ref[rows, :], xs_ref[rows, :])
        oa = jnp.where(is_p, oap_ref[rows, :], oas_ref[rows, :])
        ob = jnp.where(is_p, obp_ref[rows, :], obs_ref[rows, :])
        out = (jnp.dot(oa, w_ref[0:W_A, :], preferred_element_type=f32)
               + jnp.dot(ob, w_ref[W_A:W_A + W_B, :], preferred_element_type=f32))
        yield
        x1 = x + mod[:, 2 * D:3 * D] * out
        x1_ref[rows, :] = x1
        yield from _route_chain(s, x1, mod, nrm_ref, wr_ref, br_ref, route_ref, cnt_ref, hbuf, gbuf, counts)

    _mixer_out_tail(i, [chain(s) for s in range(SUB_OUT)], counts, hs_hbm, gs_hbm, hbuf, gbuf, run, sem)


def _even_out(xp, xs, oa_p, oa_s, ob_p, ob_s, mod_l, w_out, nrm, wr, br):
    sub = SUB_OUT
    ptile = lambda w: pl.BlockSpec((sub * TM, w), lambda i: (jnp.minimum(i, NPT // sub - 1), 0))
    stile = lambda w: pl.BlockSpec((sub * TM, w), lambda i: (jnp.maximum(i - NPT // sub, 0), 0))
    return pl.pallas_call(
        _even_out_kernel,
        grid=(NT // sub,),
        in_specs=[
            ptile(D), stile(D), ptile(W_A), stile(W_A), ptile(W_B), stile(W_B),
            pl.BlockSpec((1, 1, N_MOD * D), lambda i: (_mod_row_step(i, sub), 0, 0)),
            pl.BlockSpec((W_A + W_B, D), lambda i: (0, 0)),
        ] + _router_in_specs(),
        out_specs=_dispatch_out_specs(sub),
        out_shape=_DISPATCH_OUT_SHAPES,
        scratch_shapes=_dispatch_scratch(sub),
        compiler_params=_cp(("arbitrary",)),
        name="even_out_router",
    )(xp, xs, oa_p, oa_s, ob_p, ob_s, mod_l, w_out, nrm, wr, br)


def _gelu_tanh(x):
    return 0.5 * x * (1.0 + jnp.tanh(math.sqrt(2.0 / math.pi) * (x + 0.044715 * (x * x * x))))


def _odd_kernel(n_ref, s_ref, x_ref, proute_ref, y_hbm, modp_ref, mod_ref, nm_ref, win_ref, lng_ref, lnb_ref,
                ws_ref, bs_ref, wout_ref, nrm_ref, wr_ref, br_ref,
                x1_ref, route_ref, cnt_ref, hs_hbm, gs_hbm, gated, ybuf, ysem, hbuf, gbuf, run, sem):
    i = pl.program_id(0)
    modp = modp_ref[0]
    mod = mod_ref[0]
    counts = [None] * SUB_OUT
    ybase = _combine_fetch(i, NT // SUB_OUT, SUB_OUT, 0, n_ref, s_ref, y_hbm, ybuf, ysem)
    gw = W_C // C_GROUPS

    def chain(s):
        rows = slice(s * TM, (s + 1) * TM)
        y = _unsort(ybuf, ybase[s], proute_ref[rows, 0:1])
        yield
        x = x_ref[rows, :] + modp[:, 5 * D:6 * D] * y
        h = _rms(x, nm_ref[...]) * (1.0 + mod[:, D:2 * D]) + mod[:, 0:D]
        z = _dot(h, win_ref[...])
        yield
        z = _gelu_tanh(z)
        u = z[:, 0:W_C]
        v = z[:, W_C:2 * W_C]
        mu = jnp.mean(v, axis=-1, keepdims=True)
        vc = v - mu
        var = jnp.mean(vc * vc, axis=-1, keepdims=True)
        vn = (vc * lax.rsqrt(var + EPS) * lng_ref[...] + lnb_ref[...]).astype(bf16)
        for ck in range(TM // C_CHUNK):
            rs = slice(ck * C_CHUNK, (ck + 1) * C_CHUNK)
            for g in range(C_GROUPS):
                cs = slice(g * gw, (g + 1) * gw)
                mixed = jnp.dot(ws_ref[g], vn[rs, cs], preferred_element_type=f32) + bs_ref[g]
                gated[s * TM + ck * C_CHUNK:s * TM + (ck + 1) * C_CHUNK, cs] = (u[rs, cs] * mixed).astype(bf16)
        yield
        out = jnp.dot(gated[rows, :], wout_ref[...], preferred_element_type=f32)
        yield
        x1 = x + mod[:, 2 * D:3 * D] * out
        x1_ref[rows, :] = x1
        yield from _route_chain(s, x1, mod, nrm_ref, wr_ref, br_ref, route_ref, cnt_ref, hbuf, gbuf, counts)

    _mixer_out_tail(i, [chain(s) for s in range(SUB_OUT)], counts, hs_hbm, gs_hbm, hbuf, gbuf, run, sem)


def _odd_mixer(run_n, run_src, x, prev_route, y, mod_prev, mod_l, nm, w_in, ln_g, ln_b, ws, bs, w_out, nrm, wr, br):
    sub = SUB_OUT
    tile = lambda w: pl.BlockSpec((sub * TM, w), lambda i, *_: (i, 0))
    modspec = pl.BlockSpec((1, 1, N_MOD * D), lambda i, *_: (_mod_row_step(i, sub), 0, 0))
    const = lambda shp: pl.BlockSpec(shp, lambda i, *_: tuple(0 for _ in shp))
    grid_spec = pltpu.PrefetchScalarGridSpec(
        num_scalar_prefetch=2,
        grid=(NT // sub,),
        in_specs=[tile(D), tile(128), pl.BlockSpec(memory_space=pl.ANY), modspec, modspec, const((1, D)),
                  const((D, 2 * W_C)), const((1, W_C)), const((1, W_C)), const((C_GROUPS, C_CHUNK, C_CHUNK)),
                  const((C_GROUPS, C_CHUNK, W_C // C_GROUPS)), const((W_C, D))] + _router_in_specs(),
        out_specs=_dispatch_out_specs(sub),
        scratch_shapes=[pltpu.VMEM((sub * TM, W_C), bf16)] + _combine_scratch(sub) + _dispatch_scratch(sub),
    )
    return pl.pallas_call(
        _odd_kernel,
        grid_spec=grid_spec,
        out_shape=_DISPATCH_OUT_SHAPES,
        compiler_params=_cp(("arbitrary",)),
        name="odd_mixer_router",
    )(run_n, run_src, x, prev_route, y, mod_prev, mod_l, nm, w_in, ln_g, ln_b, ws, bs, w_out, nrm, wr, br)


def _expert_kernel(nu_ref, tg_ref, tb_ref, tfirst_ref, tnext_ref, tslot_ref, hs_ref, gs_ref, wg_hbm, wu_hbm, wd_hbm,
                   y_ref, wg_buf, wu_buf, wd_buf, wsem, *, layer):
    t = pl.program_id(0)

    def group_copies(g, s):
        return [pltpu.make_async_copy(w.at[layer, g], buf.at[s], wsem.at[s])
                for w, buf in ((wg_hbm, wg_buf), (wu_hbm, wu_buf), (wd_hbm, wd_buf))]

    @pl.when(t < nu_ref[0])
    def _():
        slot = tslot_ref[t]

        @pl.when(t == 0)
        def _():
            for cp in group_copies(tg_ref[0], 0):
                cp.start()

        @pl.when(tfirst_ref[t] == 1)
        def _():
            @pl.when(tnext_ref[t] >= 0)
            def _():
                for cp in group_copies(tnext_ref[t], 1 - slot):
                    cp.start()

            for cp in group_copies(tg_ref[t], slot):
                cp.wait()

        wg_ref, wu_ref, wd_ref = wg_buf.at[slot], wu_buf.at[slot], wd_buf.at[slot]
        xb = _load_rows(hs_ref, 0, TMG).astype(bf16)
        gates = gs_ref[...]
        acc = jnp.zeros((TMG, D), f32)
        gate_up = lambda e: (_dot(xb, wg_ref[e]), _dot(xb, wu_ref[e]))
        nxt = gate_up(0)
        for e in range(E_PER_GROUP):
            a, b = nxt
            if e + 1 < E_PER_GROUP:
                nxt = gate_up(e + 1)
            acc = acc + _dot(_silu(a) * b * gates[:, e:e + 1], wd_ref[e])
        _store_rows(y_ref, 0, acc)


def _experts(layer, n_used, tile_g, tile_blk, tile_first, tile_next, tile_slot, hs, gs, wg, wu, wd):
    tile_map = lambda t, nu, tg, tb, *_: (tb[t], 0)
    grid_spec = pltpu.PrefetchScalarGridSpec(
        num_scalar_prefetch=6,
        grid=(MAXT_G,),
        in_specs=[
            pl.BlockSpec((TMG * ROW_SUB, 128), tile_map),
            pl.BlockSpec((TMG, 128), tile_map),
            pl.BlockSpec(memory_space=pl.ANY), pl.BlockSpec(memory_space=pl.ANY), pl.BlockSpec(memory_space=pl.ANY),
        ],
        out_specs=pl.BlockSpec((TMG * ROW_SUB, 128), tile_map),
        scratch_shapes=[
            pltpu.VMEM((2, E_PER_GROUP, D, D_EXPERT), f32),
            pltpu.VMEM((2, E_PER_GROUP, D, D_EXPERT), f32),
            pltpu.VMEM((2, E_PER_GROUP, D_EXPERT, D), f32),
            pltpu.SemaphoreType.DMA((2,)),
        ],
    )
    return pl.pallas_call(
        functools.partial(_expert_kernel, layer=layer),
        grid_spec=grid_spec,
        out_shape=jax.ShapeDtypeStruct((N_GROUPS * RG * ROW_SUB, 128), f32),
        compiler_params=_cp(("arbitrary",), VMEM_LIMIT_EXPERTS),
        name="experts",
    )(n_used, tile_g, tile_blk, tile_first, tile_next, tile_slot, hs, gs, wg, wu, wd)


def _dispatch_plan(cnt_tiles):
    cnt = cnt_tiles[:, 0, 0:N_GROUPS].astype(i32)
    ends = jnp.cumsum(cnt, axis=0)
    run_src = (ends - cnt + jnp.arange(N_GROUPS, dtype=i32)[None, :] * RG).reshape(-1)
    totals = ends[NT - 1]
    tiles_g = (totals + TMG - 1) // TMG
    tile_end = jnp.cumsum(tiles_g)
    n_used = tile_end[N_GROUPS - 1:]
    t = jnp.minimum(jnp.arange(MAXT_G, dtype=i32), n_used[0] - 1)
    tile_g = jnp.sum((tile_end[None, :] <= t[:, None]).astype(i32), axis=1)
    tile_start = tile_end - tiles_g
    first = jnp.sum(jnp.where(tile_g[:, None] == jnp.arange(N_GROUPS, dtype=i32)[None, :], tile_start[None, :], 0), axis=1)
    tile_blk = tile_g * (RG // TMG) + t - first
    groups = jnp.arange(N_GROUPS, dtype=i32)
    owns = tiles_g > 0
    earlier = jnp.logical_and(owns[None, :], groups[None, :] < tile_g[:, None])
    later = jnp.logical_and(owns[None, :], groups[None, :] > tile_g[:, None])
    tile_slot = jnp.sum(earlier.astype(i32), axis=1) % 2
    nxt = jnp.min(jnp.where(later, groups[None, :], N_GROUPS), axis=1)
    tile_next = jnp.where(nxt < N_GROUPS, nxt, -1).astype(i32)
    tile_first = (t == first).astype(i32)
    return cnt.reshape(-1), run_src, n_used, tile_g, tile_blk, tile_first, tile_next, tile_slot


def _final_kernel(n_ref, s_ref, x_ref, route_ref, y_hbm, mod_ref, fn_ref, o_ref, ybuf, ysem, *, n_steps, tile_off):
    i = pl.program_id(0)
    mod = mod_ref[0]
    ybase = _combine_fetch(i, n_steps, SUB, tile_off, n_ref, s_ref, y_hbm, ybuf, ysem)

    def chain(s):
        rows = slice(s * TM, (s + 1) * TM)
        y = _unsort(ybuf, ybase[s], route_ref[rows, 0:1])
        yield
        x = x_ref[rows, :] + mod[:, 5 * D:6 * D] * y
        o_ref[rows, :] = _rms(x, fn_ref[...])

    _lockstep([chain(s) for s in range(SUB)])


def _final(run_n, run_src, x, route, y, mod_l, fnorm, *, n_tiles, tile_off):
    n_steps, step_off = n_tiles // SUB, tile_off // SUB
    grid_spec = pltpu.PrefetchScalarGridSpec(
        num_scalar_prefetch=2,
        grid=(n_steps,),
        in_specs=[
            pl.BlockSpec((SUB * TM, D), lambda i, *_: (step_off + i, 0)),
            pl.BlockSpec((SUB * TM, 128), lambda i, *_: (step_off + i, 0)),
            pl.BlockSpec(memory_space=pl.ANY),
            pl.BlockSpec((1, 1, N_MOD * D), lambda i, *_: (_mod_row_step(step_off + i), 0, 0)),
            pl.BlockSpec((1, D), lambda i, *_: (0, 0)),
        ],
        out_specs=pl.BlockSpec((SUB * TM, D), lambda i, *_: (i, 0)),
        scratch_shapes=_combine_scratch(SUB),
    )
    return pl.pallas_call(
        functools.partial(_final_kernel, n_steps=n_steps, tile_off=tile_off),
        grid_spec=grid_spec,
        out_shape=jax.ShapeDtypeStruct((n_tiles * TM, D), f32),
        compiler_params=_cp(("arbitrary",)),
        name="final_norm",
    )(run_n, run_src, x, route, y, mod_l, fnorm)


def _rope_tables():
    rows = DEC_SEQ // GRID_W
    r, c = jnp.meshgrid(jnp.arange(rows), jnp.arange(GRID_W), indexing="ij")
    pos = jnp.stack([r.reshape(-1), c.reshape(-1)], axis=-1).astype(f32)
    inv = ROPE_BASE ** (-jnp.arange(AXIS_FREQS, dtype=f32) / AXIS_FREQS)
    ang = pos[:, :, None] * inv
    cos, sin = jnp.cos(ang), jnp.sin(ang)
    cos_h = jnp.concatenate([cos, cos], axis=-1).reshape(DEC_SEQ, DH_A)
    sin_h = jnp.concatenate([-sin, sin], axis=-1).reshape(DEC_SEQ, DH_A)
    cos_t = jnp.concatenate([jnp.ones((SUB * TM, 2 * DH_A), f32), jnp.tile(cos_h, (1, 2))], axis=0)
    sin_t = jnp.concatenate([jnp.zeros((SUB * TM, 2 * DH_A), f32), jnp.tile(sin_h, (1, 2))], axis=0)
    return cos_t, sin_t


def _gate_lanes(p):
    out = jnp.zeros((128,), f32)
    for d in range(2):
        out = out.at[d * 2 * H_B:d * 2 * H_B + H_B].set(p[d])
    return out.reshape(1, 128)


def _router_weights(w_rg, b_rg, w_re, b_re):
    wr = jnp.zeros((D, 256), f32).at[:, 0:N_GROUPS].set(w_rg).at[:, 128:128 + N_EXPERTS].set(w_re)
    br = jnp.zeros((1, 256), f32).at[0, 0:N_GROUPS].set(b_rg).at[0, 128:128 + N_EXPERTS].set(b_re)
    return jnp.stack(_hi_lo(wr)), br


def kernel(x_prompt, x_sample, cache_k, cache_v, state_fwd, state_bwd, c, c_ctx, norm_mix, norm_ffn, w_mod, b_mod,
           w_in_even, conv_w, a_log, dt_bias, delta_norm, lam_q, lam_k, subln, w_out_even, w_in_odd, sgu_ln_g,
           sgu_ln_b, w_spatial, b_spatial, w_out_odd, w_router_group, b_router_group, w_router_expert,
           b_router_expert, w_exp_gate, w_exp_up, w_exp_down, final_norm):
    xp = x_prompt.reshape(N_P, D)
    xs = x_sample.reshape(N_S, D)
    cond = jnp.concatenate([c, c_ctx[None, :], jnp.zeros((COND_ROWS - DEC_BATCH - 1, D), f32)], axis=0)
    mod = _modulation(cond, w_mod, b_mod)
    mod0 = mod[0].reshape(COND_ROWS, 1, N_MOD * D)
    mod1 = mod[1].reshape(COND_ROWS, 1, N_MOD * D)

    lam_init = 0.8 - 0.6 * math.exp(-0.3 * 0)
    w_in_pad = jnp.pad(w_in_even[0], ((0, 0), (0, IN_EVEN_PAD - IN_EVEN))).astype(bf16)
    cos_t, sin_t = _rope_tables()
    qa, ka, va, qkvb, gate_b, ab, ka_p, va_p = _even_in(xp, xs, mod0, norm_mix[0:1], w_in_pad, cos_t, sin_t)

    ctx_k = cache_k[:, 0].reshape(DEC_BATCH * PAST * H_A, 2 * DH_A)
    ctx_v = cache_v[:, 0].reshape(DEC_BATCH * PAST * H_A, 2 * DH_A)
    attn = functools.partial(_attention, qa, ka, va, lam_q[0], lam_k[0], subln[0:1], lam_init)
    oa_p = attn(seq=SEQ, tq=SEQ, tok_off=0)
    oa_s = attn(seq=DEC_SEQ, tq=512, tok_off=N_P, ctx=(ctx_k, ctx_v))

    conv_pad = jnp.pad(conv_w[0], ((0, 8 - CONV_K), (0, 0)))
    dl = functools.partial(_delta, qkvb, gate_b, ab, conv_pad, _gate_lanes(a_log[0]), _gate_lanes(dt_bias[0]),
                           delta_norm[0:1])
    ob_p, s_f, s_b = dl(seq=SEQ, tok_off=0)
    ob_s, _, _ = dl(seq=DEC_SEQ, tok_off=N_P, init=(state_fwd[:, 0], state_bwd[:, 0]))

    wr0, br0 = _router_weights(w_router_group[0], b_router_group[0], w_router_expert[0], b_router_expert[0])
    x1, route0, cnt0, hs, gs = _even_out(xp, xs, oa_p, oa_s, ob_p, ob_s, mod0, w_out_even[0].astype(bf16),
                                         norm_ffn[0:1], wr0, br0)

    wshape = (DEPTH, N_GROUPS, E_PER_GROUP)
    wg = w_exp_gate.reshape(wshape + (D, D_EXPERT))
    wu = w_exp_up.reshape(wshape + (D, D_EXPERT))
    wd = w_exp_down.reshape(wshape + (D_EXPERT, D))
    run_n0, run_src0, *tile_plan = _dispatch_plan(cnt0)
    y0 = _experts(0, *tile_plan, hs, gs, wg, wu, wd)

    bs = jnp.broadcast_to(b_spatial[0][:, :, None], (C_GROUPS, C_CHUNK, W_C // C_GROUPS))
    wr1, br1 = _router_weights(w_router_group[1], b_router_group[1], w_router_expert[1], b_router_expert[1])
    x2, route1, cnt1, hs, gs = _odd_mixer(run_n0, run_src0, x1, route0, y0, mod0, mod1, norm_mix[1:2],
                                          w_in_odd[0].astype(bf16), sgu_ln_g[0:1], sgu_ln_b[0:1],
                                          w_spatial[0].astype(bf16), bs, w_out_odd[0].astype(bf16), norm_ffn[1:2],
                                          wr1, br1)
    run_n1, run_src1, *tile_plan = _dispatch_plan(cnt1)
    y1 = _experts(1, *tile_plan, hs, gs, wg, wu, wd)

    fin = functools.partial(_final, run_n1, run_src1, x2, route1, y1, mod1, final_norm[None, :])
    y_prompt = fin(n_tiles=NPT, tile_off=0)
    y_sample = fin(n_tiles=NT - NPT, tile_off=NPT)

    new_cache_k = ka_p.reshape(BATCH, 1, SEQ, H_A, 2 * DH_A)
    new_cache_v = va_p.reshape(BATCH, 1, SEQ, H_A, 2 * DH_A)
    return (y_prompt.reshape(BATCH, SEQ, D), y_sample.reshape(DEC_BATCH, DEC_SEQ, D), new_cache_k, new_cache_v,
            s_f[:, None], s_b[:, None])
```

```python
import functools
import math

import jax
import jax.numpy as jnp
from jax import lax
from jax.experimental import pallas as pl
from jax.experimental.pallas import tpu as pltpu

f32 = jnp.float32
bf16 = jnp.bfloat16
i32 = jnp.int32

D = 1024
BATCH, SEQ = 16, 256
DEC_BATCH, DEC_SEQ, PAST = 8, 1024, 512
DEPTH = 2
GRID_W = 64
N_MOD = 6
EPS = 1e-6
H_A, DH_A = 4, 64
W_A = H_A * 2 * DH_A
ROPE_BASE = 10000.0
AXIS_FREQS = DH_A // 4
H_B, DK_B, DV_B = 4, 128, 128
W_B = H_B * DV_B
CONV_K = 5
CH = 64
IN_EVEN = 3 * W_A + 4 * W_B + 4 * H_B
IN_EVEN_PAD = 3712
W_C = D
C_GROUPS, C_CHUNK = 4, 128
N_GROUPS, E_PER_GROUP, D_EXPERT = 4, 8, 256
N_EXPERTS = N_GROUPS * E_PER_GROUP

N_P = BATCH * SEQ
N_S = DEC_BATCH * DEC_SEQ
N_TOK = N_P + N_S
TM = 256
NPT = N_P // TM
NT = N_TOK // TM
COND_ROWS = 16

VMEM_LIMIT = 56 * 1024 * 1024
VMEM_LIMIT_EXPERTS = 62 * 1024 * 1024


def _cp(sem, vmem_limit=VMEM_LIMIT):
    return pltpu.CompilerParams(dimension_semantics=sem, vmem_limit_bytes=vmem_limit)


SUB = 2
NS = NT // SUB
NPS = NPT // SUB
SUB_OUT = 4


def _mod_row_step(i, sub=SUB):
    return jnp.where(i < NPT // sub, DEC_BATCH, (i - NPT // sub) // (DEC_SEQ // (sub * TM)))


def _lockstep(chains):
    live = list(chains)
    while live:
        nxt = []
        for ch in live:
            try:
                next(ch)
                nxt.append(ch)
            except StopIteration:
                pass
        live = nxt


def _rms(x, g):
    return x * lax.rsqrt(jnp.mean(x * x, axis=-1, keepdims=True) + EPS) * g


def _silu(x):
    return x * jax.nn.sigmoid(x)


def _dot(a, b):
    return jnp.dot(a.astype(bf16), b.astype(bf16), preferred_element_type=f32)


ROW_SUB = D // 128


def _store_rows(ref, base, x):
    rows = x.shape[0]
    for j in range(ROW_SUB):
        ref[pl.ds(base + j, rows, stride=ROW_SUB), :] = x[:, j * 128:(j + 1) * 128]


def _load_rows(ref, base, rows):
    return jnp.concatenate([ref[pl.ds(base + j, rows, stride=ROW_SUB), :] for j in range(ROW_SUB)], axis=1)


TN_MOD = 1536


def _mod_kernel(c_ref, w_ref, b_ref, o_ref):
    a = _silu(c_ref[...])
    o_ref[0] = _dot(a, w_ref[0]) + b_ref[0]


def _modulation(cond, w_mod, b_mod):
    return pl.pallas_call(
        _mod_kernel,
        grid=(DEPTH, N_MOD * D // TN_MOD),
        in_specs=[
            pl.BlockSpec((COND_ROWS, D), lambda l, j: (0, 0)),
            pl.BlockSpec((1, D, TN_MOD), lambda l, j: (l, 0, j)),
            pl.BlockSpec((1, 1, TN_MOD), lambda l, j: (l, 0, j)),
        ],
        out_specs=pl.BlockSpec((1, COND_ROWS, TN_MOD), lambda l, j: (l, 0, j)),
        out_shape=jax.ShapeDtypeStruct((DEPTH, COND_ROWS, N_MOD * D), f32),
        compiler_params=_cp(("arbitrary", "arbitrary")),
        name="modulation",
    )(cond, w_mod, b_mod.reshape(DEPTH, 1, N_MOD * D))


Q_SCALE = DH_A ** -0.5 * math.log2(math.e)


def _even_in_kernel(xp_ref, xs_ref, mod_ref, nrm_ref, w_ref, cos_ref, sin_ref,
                    q_ref, k_ref, v_ref, qkvb_ref, gate_ref, ab_ref, kp_ref, vp_ref):
    i = pl.program_id(0)
    is_p = i < NPS
    mod = mod_ref[0]
    lane = lax.broadcasted_iota(i32, (TM, 2 * DH_A), 1)
    first = (lane % (2 * AXIS_FREQS)) < AXIS_FREQS

    def chain(s):
        rows = slice(s * TM, (s + 1) * TM)
        x = jnp.where(is_p, xp_ref[rows, :], xs_ref[rows, :])
        h = _rms(x, nrm_ref[...]) * (1.0 + mod[:, D:2 * D]) + mod[:, 0:D]
        proj = _dot(h, w_ref[...])
        yield
        cos = cos_ref[rows, :]
        sin = sin_ref[rows, :]

        def rope(xh):
            partner = jnp.where(first, pltpu.roll(xh, 2 * DH_A - AXIS_FREQS, 1), pltpu.roll(xh, AXIS_FREQS, 1))
            return xh * cos + partner * sin

        for hh in range(H_A):
            sl = slice(hh * 2 * DH_A, (hh + 1) * 2 * DH_A)
            q_ref[rows, sl] = (rope(proj[:, sl]) * Q_SCALE).astype(bf16)
            k_ref[rows, sl] = rope(proj[:, W_A + hh * 2 * DH_A:W_A + (hh + 1) * 2 * DH_A])
        v_ref[rows, :] = proj[:, 2 * W_A:3 * W_A]
        qkvb_ref[rows, :] = proj[:, 3 * W_A:3 * W_A + 3 * W_B]
        gate_ref[rows, :] = proj[:, 3 * W_A + 3 * W_B:3 * W_A + 4 * W_B]
        ab_ref[rows, :] = proj[:, 3 * W_A + 4 * W_B:IN_EVEN_PAD]

    _lockstep([chain(s) for s in range(SUB)])

    @pl.when(is_p)
    def _():
        for hh in range(H_A):
            sl = slice(hh * 2 * DH_A, (hh + 1) * 2 * DH_A)
            kp_ref[pl.ds(hh, SUB * TM, stride=H_A), :] = k_ref[:, sl]
            vp_ref[pl.ds(hh, SUB * TM, stride=H_A), :] = v_ref[:, sl]


def _even_in(xp, xs, mod_l, nrm, w_in_pad, cos_t, sin_t):
    tile = lambda w: pl.BlockSpec((SUB * TM, w), lambda i: (i, 0))
    ptile = lambda w: pl.BlockSpec((SUB * TM, w), lambda i: (jnp.minimum(i, NPS - 1), 0))
    rope_idx = lambda i: (jnp.where(i < NPS, 0, 1 + (i - NPS) % (DEC_SEQ // (SUB * TM))), 0)
    return pl.pallas_call(
        _even_in_kernel,
        grid=(NS,),
        in_specs=[
            ptile(D),
            pl.BlockSpec((SUB * TM, D), lambda i: (jnp.maximum(i - NPS, 0), 0)),
            pl.BlockSpec((1, 1, N_MOD * D), lambda i: (_mod_row_step(i), 0, 0)),
            pl.BlockSpec((1, D), lambda i: (0, 0)),
            pl.BlockSpec((D, IN_EVEN_PAD), lambda i: (0, 0)),
            pl.BlockSpec((SUB * TM, 2 * DH_A), rope_idx),
            pl.BlockSpec((SUB * TM, 2 * DH_A), rope_idx),
        ],
        out_specs=[tile(W_A), tile(W_A), tile(W_A), tile(3 * W_B), tile(W_B), tile(128)]
        + [pl.BlockSpec((SUB * TM * H_A, 2 * DH_A), lambda i: (jnp.minimum(i, NPS - 1), 0))] * 2,
        out_shape=[
            jax.ShapeDtypeStruct((N_TOK, W_A), bf16),
            jax.ShapeDtypeStruct((N_TOK, W_A), f32),
            jax.ShapeDtypeStruct((N_TOK, W_A), f32),
            jax.ShapeDtypeStruct((N_TOK, 3 * W_B), f32),
            jax.ShapeDtypeStruct((N_TOK, W_B), f32),
            jax.ShapeDtypeStruct((N_TOK, 128), f32),
            jax.ShapeDtypeStruct((N_P * H_A, 2 * DH_A), f32),
            jax.ShapeDtypeStruct((N_P * H_A, 2 * DH_A), f32),
        ],
        compiler_params=_cp(("arbitrary",)),
        name="even_in",
    )(xp, xs, mod_l, nrm, w_in_pad, cos_t, sin_t)


ATTN_HEADS_IN_LOCKSTEP = 2


def _attn_kernel(*refs, lam_init, has_ctx):
    if has_ctx:
        q_ref, kn_ref, vn_ref, kc_ref, vc_ref, lq_ref, lk_ref, sub_ref, o_ref = refs
    else:
        q_ref, kn_ref, vn_ref, lq_ref, lk_ref, sub_ref, o_ref = refs
    tq = q_ref.shape[0]
    lq = lq_ref[...]
    lk = lk_ref[...]
    prod = lq * lk
    lam = (jnp.exp(jnp.sum(prod[0:1], axis=-1, keepdims=True))
           - jnp.exp(jnp.sum(prod[1:2], axis=-1, keepdims=True)) + lam_init)
    lane = lax.broadcasted_iota(i32, (tq, 2 * DH_A), 1)
    nt = (((1,), (1,)), ((), ()))
    tn = (((0,), (0,)), ((), ()))
    sub_col = jnp.broadcast_to(sub_ref[...], (8, 2 * DH_A)).T[:, 0:1]
    outs = {}

    def softmax_part(hh, part, srcs):
        q = q_ref[:, hh * 2 * DH_A:(hh + 1) * 2 * DH_A]
        qm = jnp.where((lane < DH_A) if part == 0 else (lane >= DH_A), q, jnp.zeros_like(q))
        ss = [lax.dot_general(k, qm, nt, preferred_element_type=f32) for k, _ in srcs]
        yield
        m = ss[0].max(axis=0, keepdims=True)
        for s in ss[1:]:
            m = jnp.maximum(m, s.max(axis=0, keepdims=True))
        l = jnp.zeros((1, tq), f32)
        o = jnp.zeros((2 * DH_A, tq), f32)
        for s, (_, v) in zip(ss, srcs):
            e = jnp.exp2(s - m)
            l = l + jnp.sum(e, axis=0, keepdims=True)
            o = o + lax.dot_general(v, e.astype(bf16), tn, preferred_element_type=f32)
        yield
        outs[(hh, part)] = o / l

    for h0 in range(0, H_A, ATTN_HEADS_IN_LOCKSTEP):
        chains = []
        for hh in range(h0, h0 + ATTN_HEADS_IN_LOCKSTEP):
            sl = slice(hh * 2 * DH_A, (hh + 1) * 2 * DH_A)
            srcs = [(kn_ref[:, sl].astype(bf16), vn_ref[:, sl].astype(bf16))]
            if has_ctx:
                srcs.append((kc_ref[pl.ds(hh, PAST, stride=H_A), :].astype(bf16),
                             vc_ref[pl.ds(hh, PAST, stride=H_A), :].astype(bf16)))
            chains += [softmax_part(hh, part, srcs) for part in range(2)]
        _lockstep(chains)
        for hh in range(h0, h0 + ATTN_HEADS_IN_LOCKSTEP):
            o = outs[(hh, 0)] - lam * outs[(hh, 1)]
            o = o * lax.rsqrt(jnp.mean(o * o, axis=0, keepdims=True) + EPS) * sub_col * (1.0 - lam_init)
            o_ref[:, hh * 2 * DH_A:(hh + 1) * 2 * DH_A] = o.T.astype(bf16)


def _attention(q, ka, va, lam_q, lam_k, subln, lam_init, *, seq, tq, tok_off, ctx=None):
    nq = seq // tq
    nb = (N_P if ctx is None else N_S) // seq
    qo, ko = tok_off // tq, tok_off // seq
    in_specs = [
        pl.BlockSpec((tq, W_A), lambda b, j: (qo + b * nq + j, 0)),
        pl.BlockSpec((seq, W_A), lambda b, j: (ko + b, 0)),
        pl.BlockSpec((seq, W_A), lambda b, j: (ko + b, 0)),
    ]
    args = [q, ka, va]
    if ctx is not None:
        in_specs += [pl.BlockSpec((PAST * H_A, 2 * DH_A), lambda b, j: (b, 0))] * 2
        args += list(ctx)
    in_specs += [pl.BlockSpec((2, DH_A), lambda b, j: (0, 0))] * 2 + [pl.BlockSpec((1, 2 * DH_A), lambda b, j: (0, 0))]
    args += [lam_q, lam_k, subln]
    return pl.pallas_call(
        functools.partial(_attn_kernel, lam_init=lam_init, has_ctx=ctx is not None),
        grid=(nb, nq),
        in_specs=in_specs,
        out_specs=pl.BlockSpec((tq, W_A), lambda b, j: (b * nq + j, 0)),
        out_shape=jax.ShapeDtypeStruct((nb * seq, W_A), bf16),
        compiler_params=_cp(("arbitrary", "arbitrary")),
        name="diff_attn_ctx" if ctx is not None else "diff_attn",
    )(*args)


def _softplus(x):
    return jnp.maximum(x, 0.0) + jnp.log1p(jnp.exp(-jnp.abs(x)))


RW = H_B * CH
CPI = 4


def _rep_rows(x, times):
    return jnp.concatenate([x] * times, axis=0)


def _tri_inv(lmats, bd, ii, jj):
    b16 = (ii // 16) == (jj // 16)
    b32 = (ii // 32) == (jj // 32)
    eye = (ii == jj).astype(f32)
    n = len(lmats)
    mm = lambda a, b: jnp.dot(a.astype(bf16), bd(b), preferred_element_type=f32)
    l0 = [jnp.where(b16, l, 0.0) for l in lmats]
    x = [eye - a for a in l0]
    p = [mm(a, a) for a in l0]
    for level in range(3):
        xp = [mm(x[i], p[i]) for i in range(n)]
        if level < 2:
            p = [mm(a, a) for a in p]
        x = [x[i] + xp[i] for i in range(n)]
    for off_diag in (jnp.logical_and(b32, jnp.logical_not(b16)), jnp.logical_not(b32)):
        t = [mm(jnp.where(off_diag, lmats[i], 0.0), x[i]) for i in range(n)]
        t = [mm(x[i], t[i]) for i in range(n)]
        x = [x[i] - t[i] for i in range(n)]
    return x


def _delta_kernel(*refs, seq, has_init):
    if has_init:
        (qkv_ref, gate_ref, ab_ref, cw_ref, al_ref, dt_ref, dn_ref, sf0_ref, sb0_ref,
         ob_ref, sf_ref, sb_ref, xpad, xq, xk, xv, gacc, bacc, o_f, o_b, st, pu, pw, pqe, pkd, patt, ptot) = refs
    else:
        (qkv_ref, gate_ref, ab_ref, cw_ref, al_ref, dt_ref, dn_ref,
         ob_ref, sf_ref, sb_ref, xpad, xq, xk, xv, gacc, bacc, o_f, o_b, st, pu, pw, pqe, pkd, patt, ptot) = refs
    n = seq // CH
    win = CH + 16

    xpad[0:8, :] = jnp.zeros((8, 3 * W_B), f32)
    xpad[seq + 8:seq + 16, :] = jnp.zeros((8, 3 * W_B), f32)
    xpad[8:seq + 8, :] = qkv_ref[...]
    neg_a = -jnp.exp(al_ref[...])
    dtb = dt_ref[...]

    def pre(c, carry):
        r0 = pl.multiple_of(c * CH, CH)
        a = xpad[pl.ds(r0, win), :]
        y = jnp.zeros((CH, 3 * W_B), f32)
        for j in range(CONV_K):
            sh = ((CONV_K - 1) // 2 - j) % win
            y = y + pltpu.roll(a, sh, 0)[8:8 + CH] * cw_ref[j:j + 1, :]
        y = _silu(y)
        for hh in range(H_B):
            qh = y[:, hh * DK_B:(hh + 1) * DK_B]
            kh = y[:, W_B + hh * DK_B:W_B + (hh + 1) * DK_B]
            qn = qh * lax.rsqrt(jnp.sum(qh * qh, axis=-1, keepdims=True) + EPS) * (DK_B ** -0.5)
            kn = kh * lax.rsqrt(jnp.sum(kh * kh, axis=-1, keepdims=True) + EPS)
            xq[pl.ds(r0, CH), hh * DK_B:(hh + 1) * DK_B] = qn
            xk[pl.ds(r0, CH), hh * DK_B:(hh + 1) * DK_B] = kn
        xv[pl.ds(r0, CH), :] = y[:, 2 * W_B:3 * W_B]
        ab = ab_ref[pl.ds(r0, CH), :]
        gacc[pl.ds(r0, CH), :] = neg_a * _softplus(ab + dtb)
        bacc[pl.ds(r0, CH), :] = jax.nn.sigmoid(ab)
        return carry

    lax.fori_loop(0, n, pre, 0)

    ii = lax.broadcasted_iota(i32, (CH, RW), 0)
    lane_r = lax.broadcasted_iota(i32, (CH, RW), 1)
    jj = lane_r % CH
    hb = lane_r // CH
    rowl = lax.broadcasted_iota(i32, (CH, 128), 0)
    bd_mask = (lax.broadcasted_iota(i32, (RW, RW), 0) // CH) == (lax.broadcasted_iota(i32, (RW, RW), 1) // CH)
    wide_mask = ((lax.broadcasted_iota(i32, (RW, W_B), 0) // CH)
                 == (lax.broadcasted_iota(i32, (RW, W_B), 1) // DK_B))
    pair_mask = ((lax.broadcasted_iota(i32, (2 * DK_B, 2 * DV_B), 0) // DK_B)
                 == (lax.broadcasted_iota(i32, (2 * DK_B, 2 * DV_B), 1) // DV_B))
    zero_b = jnp.zeros((), bf16)

    def bd(x):
        return jnp.where(bd_mask, _rep_rows(x.astype(bf16), H_B), zero_b)

    def bd_wide(x):
        return jnp.where(wide_mask, _rep_rows(x.astype(bf16), H_B), zero_b)

    def per_head(src, lanes):
        return jnp.concatenate([jnp.broadcast_to(src[:, ln:ln + 1], (src.shape[0], DK_B)) for ln in lanes], axis=1)

    nt = (((1,), (1,)), ((), ()))
    tn = (((0,), (0,)), ((), ()))

    def local(it, carry):
        chains = []
        for j in range(CPI):
            c = it * CPI + j
            r0 = pl.multiple_of(c * CH, CH)
            g = gacc[pl.ds(r0, CH), :]
            beta = bacc[pl.ds(r0, CH), :]
            pre_g = g
            for s in (1, 2, 4, 8, 16, 32):
                pre_g = pre_g + jnp.where(rowl >= s, pltpu.roll(pre_g, s, 0), 0.0)
            tot = pre_g[CH - 1:CH, :]
            q_all = xq[pl.ds(r0, CH), :]
            k_all = xk[pl.ds(r0, CH), :]
            v_all = xv[pl.ds(r0, CH), :]
            k_bd = bd_wide(k_all)
            for d in range(2):
                gc = pre_g if d == 0 else tot - pre_g + g
                lanes = [d * 2 * H_B + hh for hh in range(H_B)]
                gct = _rep_rows(gc, H_B).T
                g_col = jnp.zeros((CH, RW), f32)
                g_row = jnp.zeros((1, RW), f32)
                for hh, ln in enumerate(lanes):
                    g_col = jnp.where(hb == hh, gc[:, ln:ln + 1], g_col)
                    g_row = jnp.where(hb[0:1] == hh, gct[ln:ln + 1, :], g_row)
                g_wide = per_head(gc, lanes)
                b_wide = per_head(beta, [ln + H_B for ln in lanes])
                incl = (ii >= jj) if d == 0 else (ii <= jj)
                strict = (ii > jj) if d == 0 else (ii < jj)
                eg = jnp.exp(g_wide)
                kb = k_all * b_wide
                chains.append(dict(
                    ci=c * 2 + d, incl=incl, strict=strict, k_bd=k_bd, kb=kb, q=q_all, tot=tot,
                    dec=jnp.where(incl, jnp.exp(jnp.where(incl, g_col - g_row, 0.0)), 0.0),
                    vb=v_all * b_wide, kbe=kb * eg, qe=q_all * eg,
                    kd=k_all * jnp.exp(per_head(tot, lanes) - g_wide)))
        kk = [lax.dot_general(e["kb"].astype(bf16), e["k_bd"], nt, preferred_element_type=f32) for e in chains]
        qk = [lax.dot_general(e["q"].astype(bf16), e["k_bd"], nt, preferred_element_type=f32) for e in chains]
        lmat = [jnp.where(e["strict"], kk[i] * e["dec"], 0.0) for i, e in enumerate(chains)]
        tinv = _tri_inv(lmat, bd, ii, jj)
        u = [jnp.dot(tinv[i].astype(bf16), bd_wide(e["vb"]), preferred_element_type=f32) for i, e in enumerate(chains)]
        w = [jnp.dot(tinv[i].astype(bf16), bd_wide(e["kbe"]), preferred_element_type=f32) for i, e in enumerate(chains)]
        for i, e in enumerate(chains):
            ci = e["ci"]
            pu[ci] = u[i]
            pw[ci] = w[i].astype(bf16)
            pqe[ci] = e["qe"].astype(bf16)
            pkd[ci] = e["kd"].astype(bf16)
            patt[ci] = jnp.where(e["incl"], qk[i] * e["dec"], 0.0).astype(bf16)
            ptot[ci] = jnp.broadcast_to(e["tot"], (8, 128))
        return carry

    lax.fori_loop(0, n // CPI, local, 0)

    for hh in range(H_B):
        rows = slice(hh * DK_B, (hh + 1) * DK_B)
        if has_init:
            st[0, rows, :] = sf0_ref[0, hh]
            st[1, rows, :] = sb0_ref[0, hh]
        else:
            st[0, rows, :] = jnp.zeros((DK_B, DV_B), f32)
            st[1, rows, :] = jnp.zeros((DK_B, DV_B), f32)

    def scan(it, carry):
        chains = []
        for d in range(2):
            c = it if d == 0 else n - 1 - it
            ci = c * 2 + d
            s_old = st[d]
            s_b = s_old.astype(bf16)
            chains.append(dict(
                d=d, r0=pl.multiple_of(c * CH, CH), s_old=s_old, u=pu[ci], att=patt[ci], kd=pkd[ci], tot=ptot[ci],
                lhs=jnp.concatenate([pw[ci], pqe[ci]], axis=0),
                rhs=[jnp.where(pair_mask, jnp.concatenate([s_b[p * 2 * DK_B:(p + 1) * 2 * DK_B]] * 2, axis=1), zero_b)
                     for p in range(2)]))
        ws_qs = [jnp.concatenate(
            [jnp.dot(e["lhs"][:, p * 2 * DK_B:(p + 1) * 2 * DK_B], e["rhs"][p], preferred_element_type=f32)
             for p in range(2)], axis=1) for e in chains]
        v_new = [e["u"] - ws_qs[i][0:CH] for i, e in enumerate(chains)]
        av = [jnp.dot(e["att"], bd_wide(v_new[i]), preferred_element_type=f32) for i, e in enumerate(chains)]
        su = [lax.dot_general(jnp.where(wide_mask, _rep_rows(e["kd"], H_B), zero_b),
                              jnp.concatenate([v_new[i][:, hh * DV_B:(hh + 1) * DV_B] for hh in range(H_B)],
                                              axis=0).astype(bf16),
                              tn, preferred_element_type=f32) for i, e in enumerate(chains)]
        for i, e in enumerate(chains):
            d = e["d"]
            if d == 0:
                o_f[pl.ds(e["r0"], CH), :] = ws_qs[i][CH:2 * CH] + av[i]
            else:
                o_b[pl.ds(e["r0"], CH), :] = ws_qs[i][CH:2 * CH] + av[i]
            for hh in range(H_B):
                rows = slice(hh * DK_B, (hh + 1) * DK_B)
                ln = d * 2 * H_B + hh
                st[d, rows, :] = e["s_old"][rows] * jnp.exp(e["tot"][0:1, ln:ln + 1]) + su[i][rows]
        return carry

    lax.fori_loop(0, n, scan, 0)

    def post(c, carry):
        r0 = pl.multiple_of(c * CH, CH)
        o = o_f[pl.ds(r0, CH), :] + o_b[pl.ds(r0, CH), :]
        gt = gate_ref[pl.ds(r0, CH), :]
        for hh in range(H_B):
            sl = slice(hh * DV_B, (hh + 1) * DV_B)
            ob_ref[pl.ds(r0, CH), sl] = (_rms(o[:, sl], dn_ref[...]) * _silu(gt[:, sl])).astype(bf16)
        return carry

    lax.fori_loop(0, n, post, 0)
    for hh in range(H_B):
        sf_ref[0, hh] = st[0, hh * DK_B:(hh + 1) * DK_B, :]
        sb_ref[0, hh] = st[1, hh * DK_B:(hh + 1) * DK_B, :]


def _delta(qkvb, gate, ab, conv_pad, al_lane, dt_lane, dnorm, *, seq, tok_off, init=None):
    nb = (N_P if init is None else N_S) // seq
    bo = tok_off // seq
    rows = lambda w: pl.BlockSpec((seq, w), lambda b: (bo + b, 0))
    const = lambda shp: pl.BlockSpec(shp, lambda b: tuple(0 for _ in shp))
    st_spec = pl.BlockSpec((1, H_B, DK_B, DV_B), lambda b: (b, 0, 0, 0))
    in_specs = [rows(3 * W_B), rows(W_B), rows(128), const((8, 3 * W_B)), const((1, 128)), const((1, 128)),
                const((1, DV_B))]
    args = [qkvb, gate, ab, conv_pad, al_lane, dt_lane, dnorm]
    if init is not None:
        in_specs += [st_spec, st_spec]
        args += [init[0], init[1]]
    return pl.pallas_call(
        functools.partial(_delta_kernel, seq=seq, has_init=init is not None),
        grid=(nb,),
        in_specs=in_specs,
        out_specs=[pl.BlockSpec((seq, W_B), lambda b: (b, 0)), st_spec, st_spec],
        out_shape=[
            jax.ShapeDtypeStruct((nb * seq, W_B), bf16),
            jax.ShapeDtypeStruct((nb, H_B, DK_B, DV_B), f32),
            jax.ShapeDtypeStruct((nb, H_B, DK_B, DV_B), f32),
        ],
        scratch_shapes=[
            pltpu.VMEM((seq + 16, 3 * W_B), f32),
            pltpu.VMEM((seq, W_B), f32),
            pltpu.VMEM((seq, W_B), f32),
            pltpu.VMEM((seq, W_B), f32),
            pltpu.VMEM((seq, 128), f32),
            pltpu.VMEM((seq, 128), f32),
            pltpu.VMEM((seq, W_B), f32),
            pltpu.VMEM((seq, W_B), f32),
            pltpu.VMEM((2, H_B * DK_B, DV_B), f32),
            pltpu.VMEM((2 * seq // CH, CH, W_B), f32),
            pltpu.VMEM((2 * seq // CH, CH, W_B), bf16),
            pltpu.VMEM((2 * seq // CH, CH, W_B), bf16),
            pltpu.VMEM((2 * seq // CH, CH, W_B), bf16),
            pltpu.VMEM((2 * seq // CH, CH, RW), bf16),
            pltpu.VMEM((2 * seq // CH, 8, 128), f32),
        ],
        compiler_params=_cp(("arbitrary",)),
        name="delta_ctx" if init is not None else "delta",
    )(*args)


RG = N_TOK
TMG = 512
MAXT_G = N_TOK // TMG + N_GROUPS
TILE_ROWS = TM * ROW_SUB
RUN_BITS = tuple(range(TM.bit_length() - 1, -1, -1))


def _hi_lo(x):
    hi = x.astype(bf16)
    return hi, (x - hi.astype(f32)).astype(bf16)


def _run_pieces(n):
    return [(1 << b, (n >> (b + 1)) << (b + 1), ((n >> b) & 1) == 1) for b in RUN_BITS]


def _route_chain(s, x1, mod, nrm_ref, wr_ref, br_ref, route_ref, cnt_ref, hbuf, gbuf, counts):
    rows = slice(s * TM, (s + 1) * TM)
    h2 = _rms(x1, nrm_ref[...]) * (1.0 + mod[:, 4 * D:5 * D]) + mod[:, 3 * D:4 * D]
    h_hi, h_lo = _hi_lo(h2)
    w_hi, w_lo = wr_ref[0], wr_ref[1]
    logits = (jnp.dot(h_hi, w_hi, preferred_element_type=f32) + jnp.dot(h_hi, w_lo, preferred_element_type=f32)
              + jnp.dot(h_lo, w_hi, preferred_element_type=f32)) + br_ref[...]
    yield
    lane = lax.broadcasted_iota(i32, (TM, 128), 1)
    lane_f = lane.astype(f32)
    neg = jnp.float32(-1e30)
    lg = jnp.where(lane < N_GROUPS, logits[:, 0:128], neg)
    mg = lg.max(axis=-1, keepdims=True)
    gidx = jnp.min(jnp.where(lg == mg, lane_f, 128.0), axis=-1, keepdims=True)
    pg = 1.0 / jnp.sum(jnp.exp(lg - mg), axis=-1, keepdims=True)
    le = jnp.where((lane // E_PER_GROUP).astype(f32) == gidx, logits[:, 128:256], neg)
    m1 = le.max(axis=-1, keepdims=True)
    i1 = jnp.min(jnp.where(le == m1, lane_f, 128.0), axis=-1, keepdims=True)
    le2 = jnp.where(lane_f == i1, neg, le)
    m2 = le2.max(axis=-1, keepdims=True)
    i2 = jnp.min(jnp.where(le2 == m2, lane_f, 128.0), axis=-1, keepdims=True)
    e2 = jnp.exp(m2 - m1)
    w1 = pg / (1.0 + e2)
    w2 = w1 * e2
    local = lane_f + gidx * E_PER_GROUP
    gates = jnp.where(local == i1, w1, jnp.where(local == i2, w2, 0.0))

    onehot = (lane_f == gidx).astype(f32)
    r = lax.broadcasted_iota(i32, (TM, TM), 0)
    c = lax.broadcasted_iota(i32, (TM, TM), 1)
    before = jnp.dot((r > c).astype(bf16), onehot.astype(bf16), preferred_element_type=f32)
    yield
    cnt = jnp.sum(onehot, axis=0, keepdims=True)
    pos = (jnp.sum(onehot * before, axis=-1, keepdims=True)
           + jnp.sum(jnp.where(lane_f < gidx, cnt, 0.0), axis=-1, keepdims=True))
    route_ref[rows, :] = jnp.where(lane == 0, pos, 0.0)
    cnt_ref[s] = cnt
    counts[s] = cnt.astype(i32)

    perm_t = (c.astype(f32) == pos).astype(bf16)
    tn = (((0,), (0,)), ((), ()))
    h_sorted = lax.dot_general(perm_t, h_hi, tn, preferred_element_type=f32)
    g_hi, g_lo = _hi_lo(gates)
    g_sorted = (lax.dot_general(perm_t, g_hi, tn, preferred_element_type=f32)
                + lax.dot_general(perm_t, g_lo, tn, preferred_element_type=f32))
    yield
    _store_rows(hbuf, s * TILE_ROWS, h_sorted)
    gbuf[s * TM:(s + 1) * TM, :] = g_sorted


def _dispatch_done(s, hs_hbm, gs_hbm, hbuf, gbuf, sem):
    pltpu.make_async_copy(hbuf.at[pl.ds(s * TILE_ROWS, TILE_ROWS), :], hs_hbm.at[pl.ds(0, TILE_ROWS), :],
                          sem.at[s]).wait()
    pltpu.make_async_copy(gbuf.at[pl.ds(s * TM, TM), :], gs_hbm.at[pl.ds(0, TM), :], sem.at[s]).wait()


def _dispatch_runs(s, cnt_i, hs_hbm, gs_hbm, hbuf, gbuf, run, sem):
    off = 0
    for g in range(N_GROUPS):
        n = cnt_i[0, g]
        dst = g * RG + run[g]
        for rows, o, pred in _run_pieces(n):
            @pl.when(pred)
            def _():
                pltpu.make_async_copy(
                    hbuf.at[pl.ds(pl.multiple_of(s * TILE_ROWS + (off + o) * ROW_SUB, ROW_SUB), rows * ROW_SUB), :],
                    hs_hbm.at[pl.ds(pl.multiple_of((dst + o) * ROW_SUB, ROW_SUB), rows * ROW_SUB), :],
                    sem.at[s]).start()
                pltpu.make_async_copy(gbuf.at[pl.ds(s * TM + off + o, rows), :], gs_hbm.at[pl.ds(dst + o, rows), :],
                                      sem.at[s]).start()
        run[g] = run[g] + n
        off = off + n


def _dispatch_tail_fill(hs_hbm, gs_hbm, hbuf, gbuf, run, sem):
    hbuf[pl.ds(0, TMG * ROW_SUB), :] = jnp.zeros((TMG * ROW_SUB, 128), f32)
    gbuf[pl.ds(0, TMG), :] = jnp.zeros((TMG, 128), f32)
    for g in range(N_GROUPS):
        tot = run[g]
        pad = (-tot) & (TMG - 1)
        for rows, o, pred in _run_pieces(pad):
            @pl.when(pred)
            def _():
                dst = g * RG + tot + o
                ch = pltpu.make_async_copy(
                    hbuf.at[pl.ds(0, rows * ROW_SUB), :],
                    hs_hbm.at[pl.ds(pl.multiple_of(dst * ROW_SUB, ROW_SUB), rows * ROW_SUB), :], sem.at[0])
                cg = pltpu.make_async_copy(gbuf.at[pl.ds(0, rows), :], gs_hbm.at[pl.ds(dst, rows), :], sem.at[1])
                ch.start()
                cg.start()
                ch.wait()
                cg.wait()


def _mixer_out_tail(i, chains, counts, hs_hbm, gs_hbm, hbuf, gbuf, run, sem):
    @pl.when(i == 0)
    def _():
        for g in range(N_GROUPS):
            run[g] = 0

    sub = len(chains)

    @pl.when(i >= 1)
    def _():
        for s in range(sub):
            _dispatch_done(s, hs_hbm, gs_hbm, hbuf, gbuf, sem)

    _lockstep(chains)
    for s in range(sub):
        _dispatch_runs(s, counts[s], hs_hbm, gs_hbm, hbuf, gbuf, run, sem)

    @pl.when(i == pl.num_programs(0) - 1)
    def _():
        for s in range(sub):
            _dispatch_done(s, hs_hbm, gs_hbm, hbuf, gbuf, sem)
        _dispatch_tail_fill(hs_hbm, gs_hbm, hbuf, gbuf, run, sem)


def _dispatch_out_specs(sub=SUB):
    return [
        pl.BlockSpec((sub * TM, D), lambda i, *_: (i, 0)),
        pl.BlockSpec((sub * TM, 128), lambda i, *_: (i, 0)),
        pl.BlockSpec((sub, 1, 128), lambda i, *_: (i, 0, 0)),
        pl.BlockSpec(memory_space=pl.ANY),
        pl.BlockSpec(memory_space=pl.ANY),
    ]


_DISPATCH_OUT_SHAPES = [
    jax.ShapeDtypeStruct((N_TOK, D), f32),
    jax.ShapeDtypeStruct((N_TOK, 128), f32),
    jax.ShapeDtypeStruct((NT, 1, 128), f32),
    jax.ShapeDtypeStruct((N_GROUPS * RG * ROW_SUB, 128), f32),
    jax.ShapeDtypeStruct((N_GROUPS * RG, 128), f32),
]


def _dispatch_scratch(sub=SUB):
    return [
        pltpu.VMEM((sub * TILE_ROWS, 128), f32),
        pltpu.VMEM((sub * TM, 128), f32),
        pltpu.SMEM((N_GROUPS,), i32),
        pltpu.SemaphoreType.DMA((sub,)),
    ]


def _router_in_specs():
    return [pl.BlockSpec((1, D), lambda i, *_: (0, 0)),
            pl.BlockSpec((2, D, 256), lambda i, *_: (0, 0, 0)),
            pl.BlockSpec((1, 256), lambda i, *_: (0, 0))]


def _fetch_runs(n_ref, s_ref, y_hbm, ybuf, sem, tile, slot):
    base = slot * TILE_ROWS
    off = 0
    for g in range(N_GROUPS):
        n = n_ref[tile * N_GROUPS + g]
        src = s_ref[tile * N_GROUPS + g]
        for rows, o, pred in _run_pieces(n):
            @pl.when(pred)
            def _():
                pltpu.make_async_copy(
                    y_hbm.at[pl.ds(pl.multiple_of((src + o) * ROW_SUB, ROW_SUB), rows * ROW_SUB), :],
                    ybuf.at[pl.ds(pl.multiple_of(base + (off + o) * ROW_SUB, ROW_SUB), rows * ROW_SUB), :],
                    sem.at[slot]).start()
        off = off + n


def _combine_fetch(i, n_steps, sub, tile0, n_ref, s_ref, y_hbm, ybuf, sem):
    half = (i % 2) * sub

    @pl.when(i == 0)
    def _():
        for s in range(sub):
            _fetch_runs(n_ref, s_ref, y_hbm, ybuf, sem, tile0 + s, s)

    @pl.when(i + 1 < n_steps)
    def _():
        for s in range(sub):
            _fetch_runs(n_ref, s_ref, y_hbm, ybuf, sem, tile0 + (i + 1) * sub + s, sub - half + s)

    bases = []
    for s in range(sub):
        base = pl.multiple_of((half + s) * TILE_ROWS, TILE_ROWS)
        pltpu.make_async_copy(y_hbm.at[pl.ds(0, TILE_ROWS), :], ybuf.at[pl.ds(base, TILE_ROWS), :],
                              sem.at[half + s]).wait()
        bases.append(base)
    return bases


def _unsort(ybuf, base, pos):
    y_hi, y_lo = _hi_lo(_load_rows(ybuf, base, TM))
    perm_t = (lax.broadcasted_iota(i32, (TM, TM), 1).astype(f32) == pos).astype(bf16)
    return jnp.dot(perm_t, y_hi, preferred_element_type=f32) + jnp.dot(perm_t, y_lo, preferred_element_type=f32)


def _combine_scratch(sub):
    return [pltpu.VMEM((2 * sub * TILE_ROWS, 128), f32), pltpu.SemaphoreType.DMA((2 * sub,))]


def _even_out_kernel(xp_ref, xs_ref, oap_ref, oas_ref, obp_ref, obs_ref, mod_ref, w_ref, nrm_ref, wr_ref, br_ref,
                     x1_ref, route_ref, cnt_ref, hs_hbm, gs_hbm, hbuf, gbuf, run, sem):
    i = pl.program_id(0)
    is_p = i < NPT // SUB_OUT
    mod = mod_ref[0]
    counts = [None] * SUB_OUT

    def chain(s):
        rows = slice(s * TM, (s + 1) * TM)
        x = jnp.where(is_p, xp_ref[rows, :], xs_ref[rows, :])
        oa = jnp.where(is_p, oap_ref[rows, :], oas_ref[rows, :])
        ob = jnp.where(is_p, obp_ref[rows, :], obs_ref[rows, :])
        out = (jnp.dot(oa, w_ref[0:W_A, :], preferred_element_type=f32)
               + jnp.dot(ob, w_ref[W_A:W_A + W_B, :], preferred_element_type=f32))
        yield
        x1 = x + mod[:, 2 * D:3 * D] * out
        x1_ref[rows, :] = x1
        yield from _route_chain(s, x1, mod, nrm_ref, wr_ref, br_ref, route_ref, cnt_ref, hbuf, gbuf, counts)

    _mixer_out_tail(i, [chain(s) for s in range(SUB_OUT)], counts, hs_hbm, gs_hbm, hbuf, gbuf, run, sem)


def _even_out(xp, xs, oa_p, oa_s, ob_p, ob_s, mod_l, w_out, nrm, wr, br):
    sub = SUB_OUT
    ptile = lambda w: pl.BlockSpec((sub * TM, w), lambda i: (jnp.minimum(i, NPT // sub - 1), 0))
    stile = lambda w: pl.BlockSpec((sub * TM, w), lambda i: (jnp.maximum(i - NPT // sub, 0), 0))
    return pl.pallas_call(
        _even_out_kernel,
        grid=(NT // sub,),
        in_specs=[
            ptile(D), stile(D), ptile(W_A), stile(W_A), ptile(W_B), stile(W_B),
            pl.BlockSpec((1, 1, N_MOD * D), lambda i: (_mod_row_step(i, sub), 0, 0)),
            pl.BlockSpec((W_A + W_B, D), lambda i: (0, 0)),
        ] + _router_in_specs(),
        out_specs=_dispatch_out_specs(sub),
        out_shape=_DISPATCH_OUT_SHAPES,
        scratch_shapes=_dispatch_scratch(sub),
        compiler_params=_cp(("arbitrary",)),
        name="even_out_router",
    )(xp, xs, oa_p, oa_s, ob_p, ob_s, mod_l, w_out, nrm, wr, br)


def _gelu_tanh(x):
    return 0.5 * x * (1.0 + jnp.tanh(math.sqrt(2.0 / math.pi) * (x + 0.044715 * (x * x * x))))


def _odd_kernel(n_ref, s_ref, x_ref, proute_ref, y_hbm, modp_ref, mod_ref, nm_ref, win_ref, lng_ref, lnb_ref,
                ws_ref, bs_ref, wout_ref, nrm_ref, wr_ref, br_ref,
                x1_ref, route_ref, cnt_ref, hs_hbm, gs_hbm, gated, ybuf, ysem, hbuf, gbuf, run, sem):
    i = pl.program_id(0)
    modp = modp_ref[0]
    mod = mod_ref[0]
    counts = [None] * SUB_OUT
    ybase = _combine_fetch(i, NT // SUB_OUT, SUB_OUT, 0, n_ref, s_ref, y_hbm, ybuf, ysem)
    gw = W_C // C_GROUPS

    def chain(s):
        rows = slice(s * TM, (s + 1) * TM)
        y = _unsort(ybuf, ybase[s], proute_ref[rows, 0:1])
        yield
        x = x_ref[rows, :] + modp[:, 5 * D:6 * D] * y
        h = _rms(x, nm_ref[...]) * (1.0 + mod[:, D:2 * D]) + mod[:, 0:D]
        z = _dot(h, win_ref[...])
        yield
        z = _gelu_tanh(z)
        u = z[:, 0:W_C]
        v = z[:, W_C:2 * W_C]
        mu = jnp.mean(v, axis=-1, keepdims=True)
        vc = v - mu
        var = jnp.mean(vc * vc, axis=-1, keepdims=True)
        vn = (vc * lax.rsqrt(var + EPS) * lng_ref[...] + lnb_ref[...]).astype(bf16)
        for ck in range(TM // C_CHUNK):
            rs = slice(ck * C_CHUNK, (ck + 1) * C_CHUNK)
            for g in range(C_GROUPS):
                cs = slice(g * gw, (g + 1) * gw)
                mixed = jnp.dot(ws_ref[g], vn[rs, cs], preferred_element_type=f32) + bs_ref[g]
                gated[s * TM + ck * C_CHUNK:s * TM + (ck + 1) * C_CHUNK, cs] = (u[rs, cs] * mixed).astype(bf16)
        yield
        out = jnp.dot(gated[rows, :], wout_ref[...], preferred_element_type=f32)
        yield
        x1 = x + mod[:, 2 * D:3 * D] * out
        x1_ref[rows, :] = x1
        yield from _route_chain(s, x1, mod, nrm_ref, wr_ref, br_ref, route_ref, cnt_ref, hbuf, gbuf, counts)

    _mixer_out_tail(i, [chain(s) for s in range(SUB_OUT)], counts, hs_hbm, gs_hbm, hbuf, gbuf, run, sem)


def _odd_mixer(run_n, run_src, x, prev_route, y, mod_prev, mod_l, nm, w_in, ln_g, ln_b, ws, bs, w_out, nrm, wr, br):
    sub = SUB_OUT
    tile = lambda w: pl.BlockSpec((sub * TM, w), lambda i, *_: (i, 0))
    modspec = pl.BlockSpec((1, 1, N_MOD * D), lambda i, *_: (_mod_row_step(i, sub), 0, 0))
    const = lambda shp: pl.BlockSpec(shp, lambda i, *_: tuple(0 for _ in shp))
    grid_spec = pltpu.PrefetchScalarGridSpec(
        num_scalar_prefetch=2,
        grid=(NT // sub,),
        in_specs=[tile(D), tile(128), pl.BlockSpec(memory_space=pl.ANY), modspec, modspec, const((1, D)),
                  const((D, 2 * W_C)), const((1, W_C)), const((1, W_C)), const((C_GROUPS, C_CHUNK, C_CHUNK)),
                  const((C_GROUPS, C_CHUNK, W_C // C_GROUPS)), const((W_C, D))] + _router_in_specs(),
        out_specs=_dispatch_out_specs(sub),
        scratch_shapes=[pltpu.VMEM((sub * TM, W_C), bf16)] + _combine_scratch(sub) + _dispatch_scratch(sub),
    )
    return pl.pallas_call(
        _odd_kernel,
        grid_spec=grid_spec,
        out_shape=_DISPATCH_OUT_SHAPES,
        compiler_params=_cp(("arbitrary",)),
        name="odd_mixer_router",
    )(run_n, run_src, x, prev_route, y, mod_prev, mod_l, nm, w_in, ln_g, ln_b, ws, bs, w_out, nrm, wr, br)


def _expert_kernel(nu_ref, tg_ref, tb_ref, tfirst_ref, tnext_ref, tslot_ref, hs_ref, gs_ref, wg_hbm, wu_hbm, wd_hbm,
                   y_ref, wg_buf, wu_buf, wd_buf, wsem, *, layer):
    t = pl.program_id(0)

    def group_copies(g, s):
        return [pltpu.make_async_copy(w.at[layer, g], buf.at[s], wsem.at[s])
                for w, buf in ((wg_hbm, wg_buf), (wu_hbm, wu_buf), (wd_hbm, wd_buf))]

    @pl.when(t < nu_ref[0])
    def _():
        slot = tslot_ref[t]

        @pl.when(t == 0)
        def _():
            for cp in group_copies(tg_ref[0], 0):
                cp.start()

        @pl.when(tfirst_ref[t] == 1)
        def _():
            @pl.when(tnext_ref[t] >= 0)
            def _():
                for cp in group_copies(tnext_ref[t], 1 - slot):
                    cp.start()

            for cp in group_copies(tg_ref[t], slot):
                cp.wait()

        wg_ref, wu_ref, wd_ref = wg_buf.at[slot], wu_buf.at[slot], wd_buf.at[slot]
        xb = _load_rows(hs_ref, 0, TMG).astype(bf16)
        gates = gs_ref[...]
        acc = jnp.zeros((TMG, D), f32)
        gate_up = lambda e: (_dot(xb, wg_ref[e]), _dot(xb, wu_ref[e]))
        nxt = gate_up(0)
        for e in range(E_PER_GROUP):
            a, b = nxt
            if e + 1 < E_PER_GROUP:
                nxt = gate_up(e + 1)
            acc = acc + _dot(_silu(a) * b * gates[:, e:e + 1], wd_ref[e])
        _store_rows(y_ref, 0, acc)


def _experts(layer, n_used, tile_g, tile_blk, tile_first, tile_next, tile_slot, hs, gs, wg, wu, wd):
    tile_map = lambda t, nu, tg, tb, *_: (tb[t], 0)
    grid_spec = pltpu.PrefetchScalarGridSpec(
        num_scalar_prefetch=6,
        grid=(MAXT_G,),
        in_specs=[
            pl.BlockSpec((TMG * ROW_SUB, 128), tile_map),
            pl.BlockSpec((TMG, 128), tile_map),
            pl.BlockSpec(memory_space=pl.ANY), pl.BlockSpec(memory_space=pl.ANY), pl.BlockSpec(memory_space=pl.ANY),
        ],
        out_specs=pl.BlockSpec((TMG * ROW_SUB, 128), tile_map),
        scratch_shapes=[
            pltpu.VMEM((2, E_PER_GROUP, D, D_EXPERT), f32),
            pltpu.VMEM((2, E_PER_GROUP, D, D_EXPERT), f32),
            pltpu.VMEM((2, E_PER_GROUP, D_EXPERT, D), f32),
            pltpu.SemaphoreType.DMA((2,)),
        ],
    )
    return pl.pallas_call(
        functools.partial(_expert_kernel, layer=layer),
        grid_spec=grid_spec,
        out_shape=jax.ShapeDtypeStruct((N_GROUPS * RG * ROW_SUB, 128), f32),
        compiler_params=_cp(("arbitrary",), VMEM_LIMIT_EXPERTS),
        name="experts",
    )(n_used, tile_g, tile_blk, tile_first, tile_next, tile_slot, hs, gs, wg, wu, wd)


def _dispatch_plan(cnt_tiles):
    cnt = cnt_tiles[:, 0, 0:N_GROUPS].astype(i32)
    ends = jnp.cumsum(cnt, axis=0)
    run_src = (ends - cnt + jnp.arange(N_GROUPS, dtype=i32)[None, :] * RG).reshape(-1)
    totals = ends[NT - 1]
    tiles_g = (totals + TMG - 1) // TMG
    tile_end = jnp.cumsum(tiles_g)
    n_used = tile_end[N_GROUPS - 1:]
    t = jnp.minimum(jnp.arange(MAXT_G, dtype=i32), n_used[0] - 1)
    tile_g = jnp.sum((tile_end[None, :] <= t[:, None]).astype(i32), axis=1)
    tile_start = tile_end - tiles_g
    first = jnp.sum(jnp.where(tile_g[:, None] == jnp.arange(N_GROUPS, dtype=i32)[None, :], tile_start[None, :], 0), axis=1)
    tile_blk = tile_g * (RG // TMG) + t - first
    groups = jnp.arange(N_GROUPS, dtype=i32)
    owns = tiles_g > 0
    earlier = jnp.logical_and(owns[None, :], groups[None, :] < tile_g[:, None])
    later = jnp.logical_and(owns[None, :], groups[None, :] > tile_g[:, None])
    tile_slot = jnp.sum(earlier.astype(i32), axis=1) % 2
    nxt = jnp.min(jnp.where(later, groups[None, :], N_GROUPS), axis=1)
    tile_next = jnp.where(nxt < N_GROUPS, nxt, -1).astype(i32)
    tile_first = (t == first).astype(i32)
    return cnt.reshape(-1), run_src, n_used, tile_g, tile_blk, tile_first, tile_next, tile_slot


def _final_kernel(n_ref, s_ref, x_ref, route_ref, y_hbm, mod_ref, fn_ref, o_ref, ybuf, ysem, *, n_steps, tile_off):
    i = pl.program_id(0)
    mod = mod_ref[0]
    ybase = _combine_fetch(i, n_steps, SUB, tile_off, n_ref, s_ref, y_hbm, ybuf, ysem)

    def chain(s):
        rows = slice(s * TM, (s + 1) * TM)
        y = _unsort(ybuf, ybase[s], route_ref[rows, 0:1])
        yield
        x = x_ref[rows, :] + mod[:, 5 * D:6 * D] * y
        o_ref[rows, :] = _rms(x, fn_ref[...])

    _lockstep([chain(s) for s in range(SUB)])


def _final(run_n, run_src, x, route, y, mod_l, fnorm, *, n_tiles, tile_off):
    n_steps, step_off = n_tiles // SUB, tile_off // SUB
    grid_spec = pltpu.PrefetchScalarGridSpec(
        num_scalar_prefetch=2,
        grid=(n_steps,),
        in_specs=[
            pl.BlockSpec((SUB * TM, D), lambda i, *_: (step_off + i, 0)),
            pl.BlockSpec((SUB * TM, 128), lambda i, *_: (step_off + i, 0)),
            pl.BlockSpec(memory_space=pl.ANY),
            pl.BlockSpec((1, 1, N_MOD * D), lambda i, *_: (_mod_row_step(step_off + i), 0, 0)),
            pl.BlockSpec((1, D), lambda i, *_: (0, 0)),
        ],
        out_specs=pl.BlockSpec((SUB * TM, D), lambda i, *_: (i, 0)),
        scratch_shapes=_combine_scratch(SUB),
    )
    return pl.pallas_call(
        functools.partial(_final_kernel, n_steps=n_steps, tile_off=tile_off),
        grid_spec=grid_spec,
        out_shape=jax.ShapeDtypeStruct((n_tiles * TM, D), f32),
        compiler_params=_cp(("arbitrary",)),
        name="final_norm",
    )(run_n, run_src, x, route, y, mod_l, fnorm)


def _rope_tables():
    rows = DEC_SEQ // GRID_W
    r, c = jnp.meshgrid(jnp.arange(rows), jnp.arange(GRID_W), indexing="ij")
    pos = jnp.stack([r.reshape(-1), c.reshape(-1)], axis=-1).astype(f32)
    inv = ROPE_BASE ** (-jnp.arange(AXIS_FREQS, dtype=f32) / AXIS_FREQS)
    ang = pos[:, :, None] * inv
    cos, sin = jnp.cos(ang), jnp.sin(ang)
    cos_h = jnp.concatenate([cos, cos], axis=-1).reshape(DEC_SEQ, DH_A)
    sin_h = jnp.concatenate([-sin, sin], axis=-1).reshape(DEC_SEQ, DH_A)
    cos_t = jnp.concatenate([jnp.ones((SUB * TM, 2 * DH_A), f32), jnp.tile(cos_h, (1, 2))], axis=0)
    sin_t = jnp.concatenate([jnp.zeros((SUB * TM, 2 * DH_A), f32), jnp.tile(sin_h, (1, 2))], axis=0)
    return cos_t, sin_t


def _gate_lanes(p):
    out = jnp.zeros((128,), f32)
    for d in range(2):
        out = out.at[d * 2 * H_B:d * 2 * H_B + H_B].set(p[d])
    return out.reshape(1, 128)


def _router_weights(w_rg, b_rg, w_re, b_re):
    wr = jnp.zeros((D, 256), f32).at[:, 0:N_GROUPS].set(w_rg).at[:, 128:128 + N_EXPERTS].set(w_re)
    br = jnp.zeros((1, 256), f32).at[0, 0:N_GROUPS].set(b_rg).at[0, 128:128 + N_EXPERTS].set(b_re)
    return jnp.stack(_hi_lo(wr)), br


def kernel(x_prompt, x_sample, cache_k, cache_v, state_fwd, state_bwd, c, c_ctx, norm_mix, norm_ffn, w_mod, b_mod,
           w_in_even, conv_w, a_log, dt_bias, delta_norm, lam_q, lam_k, subln, w_out_even, w_in_odd, sgu_ln_g,
           sgu_ln_b, w_spatial, b_spatial, w_out_odd, w_router_group, b_router_group, w_router_expert,
           b_router_expert, w_exp_gate, w_exp_up, w_exp_down, final_norm):
    xp = x_prompt.reshape(N_P, D)
    xs = x_sample.reshape(N_S, D)
    cond = jnp.concatenate([c, c_ctx[None, :], jnp.zeros((COND_ROWS - DEC_BATCH - 1, D), f32)], axis=0)
    mod = _modulation(cond, w_mod, b_mod)
    mod0 = mod[0].reshape(COND_ROWS, 1, N_MOD * D)
    mod1 = mod[1].reshape(COND_ROWS, 1, N_MOD * D)

    lam_init = 0.8 - 0.6 * math.exp(-0.3 * 0)
    w_in_pad = jnp.pad(w_in_even[0], ((0, 0), (0, IN_EVEN_PAD - IN_EVEN))).astype(bf16)
    cos_t, sin_t = _rope_tables()
    qa, ka, va, qkvb, gate_b, ab, ka_p, va_p = _even_in(xp, xs, mod0, norm_mix[0:1], w_in_pad, cos_t, sin_t)

    ctx_k = cache_k[:, 0].reshape(DEC_BATCH * PAST * H_A, 2 * DH_A)
    ctx_v = cache_v[:, 0].reshape(DEC_BATCH * PAST * H_A, 2 * DH_A)
    attn = functools.partial(_attention, qa, ka, va, lam_q[0], lam_k[0], subln[0:1], lam_init)
    oa_p = attn(seq=SEQ, tq=SEQ, tok_off=0)
    oa_s = attn(seq=DEC_SEQ, tq=512, tok_off=N_P, ctx=(ctx_k, ctx_v))

    conv_pad = jnp.pad(conv_w[0], ((0, 8 - CONV_K), (0, 0)))
    dl = functools.partial(_delta, qkvb, gate_b, ab, conv_pad, _gate_lanes(a_log[0]), _gate_lanes(dt_bias[0]),
                           delta_norm[0:1])
    ob_p, s_f, s_b = dl(seq=SEQ, tok_off=0)
    ob_s, _, _ = dl(seq=DEC_SEQ, tok_off=N_P, init=(state_fwd[:, 0], state_bwd[:, 0]))

    wr0, br0 = _router_weights(w_router_group[0], b_router_group[0], w_router_expert[0], b_router_expert[0])
    x1, route0, cnt0, hs, gs = _even_out(xp, xs, oa_p, oa_s, ob_p, ob_s, mod0, w_out_even[0].astype(bf16),
                                         norm_ffn[0:1], wr0, br0)

    wshape = (DEPTH, N_GROUPS, E_PER_GROUP)
    wg = w_exp_gate.reshape(wshape + (D, D_EXPERT))
    wu = w_exp_up.reshape(wshape + (D, D_EXPERT))
    wd = w_exp_down.reshape(wshape + (D_EXPERT, D))
    run_n0, run_src0, *tile_plan = _dispatch_plan(cnt0)
    y0 = _experts(0, *tile_plan, hs, gs, wg, wu, wd)

    bs = jnp.broadcast_to(b_spatial[0][:, :, None], (C_GROUPS, C_CHUNK, W_C // C_GROUPS))
    wr1, br1 = _router_weights(w_router_group[1], b_router_group[1], w_router_expert[1], b_router_expert[1])
    x2, route1, cnt1, hs, gs = _odd_mixer(run_n0, run_src0, x1, route0, y0, mod0, mod1, norm_mix[1:2],
                                          w_in_odd[0].astype(bf16), sgu_ln_g[0:1], sgu_ln_b[0:1],
                                          w_spatial[0].astype(bf16), bs, w_out_odd[0].astype(bf16), norm_ffn[1:2],
                                          wr1, br1)
    run_n1, run_src1, *tile_plan = _dispatch_plan(cnt1)
    y1 = _experts(1, *tile_plan, hs, gs, wg, wu, wd)

    fin = functools.partial(_final, run_n1, run_src1, x2, route1, y1, mod1, final_norm[None, :])
    y_prompt = fin(n_tiles=NPT, tile_off=0)
    y_sample = fin(n_tiles=NT - NPT, tile_off=NPT)

    new_cache_k = ka_p.reshape(BATCH, 1, SEQ, H_A, 2 * DH_A)
    new_cache_v = va_p.reshape(BATCH, 1, SEQ, H_A, 2 * DH_A)
    return (y_prompt.reshape(BATCH, SEQ, D), y_sample.reshape(DEC_BATCH, DEC_SEQ, D), new_cache_k, new_cache_v,
            s_f[:, None], s_b[:, None])
```
